```python
import math
import jax, jax.numpy as jnp
from jax import lax
import numpy as np

D_MODEL = 1024
BATCH = 8
SEQ = 4096
DEPTH = 2

N_META = 16
GRID_W = 64
N_MIXERS = 2
N_LRU_LAYERS = (DEPTH + 1) // 2
N_NA_LAYERS = DEPTH // 2

D_RNN = D_MODEL
LRU_BLOCKS = 4
LRU_BLOCK = D_RNN // LRU_BLOCKS
CONV_W = 4
LRU_C = 8.0

NA_HEADS = 16
NA_HEAD_DIM = D_MODEL // NA_HEADS
NA_MAX_KH = 8
NA_KW = 16

N_EXPERTS = 32
TOP_K = 4
D_EXPERT = D_MODEL
SWIGLU_LIMIT = 7.0
SWIGLU_ALPHA = 1.702
MOE_BLOCK = 512

DN_ALPHA = (2.0 * DEPTH) ** 0.25
DN_BETA = (8.0 * DEPTH) ** -0.25
LN_EPS = 1e-5

kernel_name = "hybrid_rglru_natten_moe_encoder"


def layer_norm(x, g, b):
    xf = x.astype(jnp.float32)
    mu = jnp.mean(xf, axis=-1, keepdims=True)
    var = jnp.mean(jnp.square(xf - mu), axis=-1, keepdims=True)
    y = (xf - mu) * lax.rsqrt(var + LN_EPS)
    return (y * g.astype(jnp.float32) + b.astype(jnp.float32)).astype(x.dtype)


def _lin_combine(c1, c2):
    a1, b1 = c1
    a2, b2 = c2
    return a1 * a2, a2 * b1 + b2


def linear_scan(a, b, reverse):
    _, h = lax.associative_scan(_lin_combine, (a, b), axis=1, reverse=reverse)
    return h


def rg_lru_direction(xc, wa, ba, wx, bx, lam, reverse):
    B_, L, _ = xc.shape
    xb = xc.reshape(B_, L, LRU_BLOCKS, LRU_BLOCK)
    gate_a = jax.nn.sigmoid(jnp.einsum('blni,nij->blnj', xb, wa).reshape(B_, L, D_RNN) + ba)
    gate_x = jax.nn.sigmoid(jnp.einsum('blni,nij->blnj', xb, wx).reshape(B_, L, D_RNN) + bx)
    log_a = (-LRU_C * gate_a.astype(jnp.float32)) * jax.nn.softplus(-lam.astype(jnp.float32))
    a = jnp.exp(log_a)
    mult = jnp.sqrt(-jnp.expm1(2.0 * log_a))
    start = L - 1 if reverse else 0
    is_start = (jnp.arange(L) == start)[None, :, None]
    mult = jnp.where(is_start, 1.0, mult)
    b = mult * (gate_x * xc).astype(jnp.float32)
    return linear_scan(a, b, reverse)


def rglru_mixer(x, w_in, conv_w, conv_b, wa, ba, wx, bx, lam, w_out):
    u = x @ w_in
    xr, y = u[..., :D_RNN], u[..., D_RNN:]
    xc = lax.conv_general_dilated(
        xr, conv_w[:, None, :], window_strides=(1,),
        padding=[(CONV_W // 2, CONV_W - 1 - CONV_W // 2)],
        dimension_numbers=('NWC', 'WIO', 'NWC'),
        feature_group_count=D_RNN) + conv_b
    h = (rg_lru_direction(xc, wa[0], ba[0], wx[0], bx[0], lam[0], False)
         + rg_lru_direction(xc, wa[1], ba[1], wx[1], bx[1], lam[1], True))
    return (h.astype(x.dtype) * jax.nn.gelu(y)) @ w_out


def na_mixer(x, w_qkv, rpb, meta_bias, w_out):
    B_, L, D = x.shape
    n_tok = L - N_META
    rows = n_tok // GRID_W
    kh = min(NA_MAX_KH, rows)
    kw = min(NA_KW, GRID_W)
    qkv = (x @ w_qkv).reshape(B_, L, 3, NA_HEADS, NA_HEAD_DIM)
    q = qkv[:, :, 0] * (NA_HEAD_DIM ** -0.5)
    k = qkv[:, :, 1]
    v = qkv[:, :, 2]
    qm, km, vm = q[:, :N_META], k[:, :N_META], v[:, :N_META]
    grid_shape = (B_, rows, GRID_W, NA_HEADS, NA_HEAD_DIM)
    qg = q[:, N_META:].reshape(grid_shape)
    kg = k[:, N_META:].reshape(grid_shape)
    vg = v[:, N_META:].reshape(grid_shape)

    row_start = jnp.clip(jnp.arange(rows) - kh // 2, 0, rows - kh)
    col_start = jnp.clip(jnp.arange(GRID_W) - kw // 2, 0, GRID_W - kw)
    col_idx = col_start[:, None] + jnp.arange(kw)[None, :]
    col_off = col_idx - jnp.arange(GRID_W)[:, None] + (NA_KW - 1)
    rpb_f = rpb.astype(jnp.float32)
    meta_b = meta_bias.astype(jnp.float32)

    def row_block(r):
        rs = row_start[r]
        q_r = lax.dynamic_index_in_dim(qg, r, axis=1, keepdims=False)
        k_rows = lax.dynamic_slice_in_dim(kg, rs, kh, axis=1)
        v_rows = lax.dynamic_slice_in_dim(vg, rs, kh, axis=1)
        k_win = k_rows[:, :, col_idx]
        v_win = v_rows[:, :, col_idx]
        row_off = rs + jnp.arange(kh) - r + (NA_MAX_KH - 1)
        bias = rpb_f[:, row_off[None, :, None], col_off[:, None, :]]
        s_win = jnp.einsum('bqhd,bjqkhd->bhqjk', q_r, k_win).astype(jnp.float32) + bias
        s_win = s_win.reshape(B_, NA_HEADS, GRID_W, kh * kw)
        s_meta = jnp.einsum('bqhd,bmhd->bhqm', q_r, km).astype(jnp.float32) + meta_b[:, None, :]
        p = jax.nn.softmax(jnp.concatenate([s_win, s_meta], axis=-1), axis=-1).astype(x.dtype)
        p_win = p[..., :kh * kw].reshape(B_, NA_HEADS, GRID_W, kh, kw)
        p_meta = p[..., kh * kw:]
        return (jnp.einsum('bhqjk,bjqkhd->bqhd', p_win, v_win)
                + jnp.einsum('bhqm,bmhd->bqhd', p_meta, vm))

    og = lax.map(row_block, jnp.arange(rows))
    og = jnp.moveaxis(og, 0, 1).reshape(B_, n_tok, D)
    s_mm = jnp.einsum('bqhd,bmhd->bhqm', qm, km).astype(jnp.float32) + meta_b[:, None, :]
    p_mm = jax.nn.softmax(s_mm, axis=-1).astype(x.dtype)
    om = jnp.einsum('bhqm,bmhd->bqhd', p_mm, vm).reshape(B_, N_META, D)
    return jnp.concatenate([om, og], axis=1) @ w_out


def moe_ffn(x, router_w, router_b, w_gu, b_gu, w_down, b_down):
    B_, L, D = x.shape
    xt = x.reshape(-1, D)
    n = xt.shape[0]
    logits = (xt @ router_w).astype(jnp.float32) + router_b.astype(jnp.float32)
    top_val, top_idx = lax.top_k(logits, TOP_K)
    gates = jax.nn.softmax(top_val, axis=-1).astype(x.dtype)
    n_slots = n * TOP_K
    flat_e = top_idx.reshape(-1).astype(jnp.int32)
    flat_tok = (jnp.arange(n_slots) // TOP_K).astype(jnp.int32)
    flat_gate = gates.reshape(-1)
    order = jnp.argsort(flat_e)
    sorted_e = flat_e[order]
    counts = jax.ops.segment_sum(jnp.ones((n_slots,), jnp.int32), flat_e, num_segments=N_EXPERTS)
    padded = (counts + MOE_BLOCK - 1) // MOE_BLOCK * MOE_BLOCK
    pad_end = jnp.cumsum(padded)
    pad_start = pad_end - padded
    start = jnp.cumsum(counts) - counts
    dest = pad_start[sorted_e] + (jnp.arange(n_slots, dtype=jnp.int32) - start[sorted_e])
    n_blocks = -(-(n_slots + N_EXPERTS * (MOE_BLOCK - 1)) // MOE_BLOCK)
    cap = n_blocks * MOE_BLOCK
    buf_tok = jnp.full((cap,), n, jnp.int32).at[dest].set(flat_tok[order])
    buf_gate = jnp.zeros((cap,), x.dtype).at[dest].set(flat_gate[order])
    block_e = jnp.minimum(
        jnp.searchsorted(pad_end, jnp.arange(n_blocks, dtype=jnp.int32) * MOE_BLOCK, side='right'),
        N_EXPERTS - 1)
    x_pad = jnp.concatenate([xt, jnp.zeros((1, D), x.dtype)], axis=0)

    def expert_block(args):
        tok, g, e = args
        xb = x_pad[tok]
        h = xb @ w_gu[e] + b_gu[e]
        glu = jnp.minimum(h[:, :D_EXPERT], SWIGLU_LIMIT)
        lin = jnp.clip(h[:, D_EXPERT:], -SWIGLU_LIMIT, SWIGLU_LIMIT)
        act = glu * jax.nn.sigmoid(SWIGLU_ALPHA * glu) * (lin + 1.0)
        return (act @ w_down[e] + b_down[e]) * g[:, None]

    y_blocks = lax.map(expert_block, (buf_tok.reshape(n_blocks, MOE_BLOCK),
                                      buf_gate.reshape(n_blocks, MOE_BLOCK), block_e))
    y = jnp.zeros((n + 1, D), x.dtype).at[buf_tok].add(y_blocks.reshape(cap, D))
    return y[:n].reshape(B_, L, D)


def setup_inputs(seed: int = 0) -> dict:
    key = jax.random.key(seed)
    ks = jax.random.split(key, 32)
    f32 = jnp.float32
    nrm = lambda k, shape, s: (jax.random.normal(k, shape, f32) * s).astype(f32)
    D = D_MODEL
    u = jax.random.uniform(ks[9], (N_LRU_LAYERS, 2, D_RNN), f32, 0.9, 0.999)
    sig = u ** (1.0 / LRU_C)
    lam = jnp.log(sig) - jnp.log1p(-sig)
    return {
        "x": nrm(ks[0], (BATCH, SEQ, D), 1.0),
        "meta_tokens": nrm(ks[1], (N_META, D), 1.0),
        "lru_w_in": nrm(ks[2], (N_LRU_LAYERS, D, 2 * D_RNN), D ** -0.5),
        "lru_conv_w": nrm(ks[3], (N_LRU_LAYERS, CONV_W, D_RNN), CONV_W ** -0.5),
        "lru_conv_b": nrm(ks[4], (N_LRU_LAYERS, D_RNN), 0.01),
        "lru_wa": nrm(ks[5], (N_LRU_LAYERS, 2, LRU_BLOCKS, LRU_BLOCK, LRU_BLOCK), LRU_BLOCK ** -0.5),
        "lru_ba": nrm(ks[6], (N_LRU_LAYERS, 2, D_RNN), 0.01),
        "lru_wx": nrm(ks[7], (N_LRU_LAYERS, 2, LRU_BLOCKS, LRU_BLOCK, LRU_BLOCK), LRU_BLOCK ** -0.5),
        "lru_bx": nrm(ks[8], (N_LRU_LAYERS, 2, D_RNN), 0.01),
        "lru_lambda": lam.astype(f32),
        "lru_w_out": nrm(ks[10], (N_LRU_LAYERS, D_RNN, D), DN_BETA * D_RNN ** -0.5),
        "na_w_qkv": nrm(ks[11], (N_NA_LAYERS, D, 3 * D), D ** -0.5),
        "na_rpb": nrm(ks[12], (N_NA_LAYERS, NA_HEADS, 2 * NA_MAX_KH - 1, 2 * NA_KW - 1), 0.02),
        "na_meta_bias": nrm(ks[13], (N_NA_LAYERS, NA_HEADS, N_META), 0.02),
        "na_w_out": nrm(ks[14], (N_NA_LAYERS, D, D), DN_BETA * D ** -0.5),
        "ln_mix_g": 1.0 + nrm(ks[15], (DEPTH, D), 0.01),
        "ln_mix_b": nrm(ks[16], (DEPTH, D), 0.01),
        "router_w": nrm(ks[17], (DEPTH, D, N_EXPERTS), D ** -0.5),
        "router_b": nrm(ks[18], (DEPTH, N_EXPERTS), 0.01),
        "moe_w_gu": nrm(ks[19], (DEPTH, N_EXPERTS, D, 2 * D_EXPERT), D ** -0.5),
        "moe_b_gu": nrm(ks[20], (DEPTH, N_EXPERTS, 2 * D_EXPERT), 0.01),
        "moe_w_down": nrm(ks[21], (DEPTH, N_EXPERTS, D_EXPERT, D), DN_BETA * D_EXPERT ** -0.5),
        "moe_b_down": nrm(ks[22], (DEPTH, N_EXPERTS, D), 0.01),
        "ln_ffn_g": 1.0 + nrm(ks[23], (DEPTH, D), 0.01),
        "ln_ffn_b": nrm(ks[24], (DEPTH, D), 0.01),
    }


def reference(x, meta_tokens, lru_w_in, lru_conv_w, lru_conv_b, lru_wa, lru_ba, lru_wx, lru_bx,
              lru_lambda, lru_w_out, na_w_qkv, na_rpb, na_meta_bias, na_w_out, ln_mix_g, ln_mix_b,
              router_w, router_b, moe_w_gu, moe_b_gu, moe_w_down, moe_b_down, ln_ffn_g, ln_ffn_b):
    B_ = x.shape[0]
    meta = jnp.broadcast_to(meta_tokens[None].astype(x.dtype), (B_, N_META, D_MODEL))
    h = jnp.concatenate([meta, x], axis=1)
    for i in range(DEPTH):
        j = i // N_MIXERS
        if i % N_MIXERS == 0:
            mix = rglru_mixer(h, lru_w_in[j], lru_conv_w[j], lru_conv_b[j], lru_wa[j], lru_ba[j],
                              lru_wx[j], lru_bx[j], lru_lambda[j], lru_w_out[j])
        else:
            mix = na_mixer(h, na_w_qkv[j], na_rpb[j], na_meta_bias[j], na_w_out[j])
        h = layer_norm(DN_ALPHA * h + mix, ln_mix_g[i], ln_mix_b[i])
        ffn = moe_ffn(h, router_w[i], router_b[i], moe_w_gu[i], moe_b_gu[i], moe_w_down[i], moe_b_down[i])
        h = layer_norm(DN_ALPHA * h + ffn, ln_ffn_g[i], ln_ffn_b[i])
    return h[:, N_META:]
```

```python
import functools

import jax
import jax.numpy as jnp
from jax import lax
from jax.experimental import pallas as pl
from jax.experimental.pallas import tpu as pltpu

F32 = jnp.float32
BF16 = jnp.bfloat16
I32 = jnp.int32
U32 = jnp.uint32

D_MODEL = 1024
N_META = 16
GRID_W = 64
LRU_BLOCKS = 4
LRU_BLOCK = D_MODEL // LRU_BLOCKS
CONV_W = 4
LRU_C = 8.0
NA_HEADS = 16
NA_HEAD_DIM = D_MODEL // NA_HEADS
NA_KH = 8
NA_KW = 16
N_EXPERTS = 32
TOP_K = 4
SWIGLU_LIMIT = 7.0
SWIGLU_ALPHA = 1.702
DEPTH = 2
DN_ALPHA = (2.0 * DEPTH) ** 0.25
LN_EPS = 1e-5

LANES = 128
SUBLANES = 8
FRONT_PAD = LANES - N_META
HEAD_PAIRS = D_MODEL // LANES
NEG_BIG = -1e30

ROW_TILE = 512
SCAN_CHUNK = 384
ROUTE_TILE = 512
MOVE_TILE = 256
EXPERT_TILE = 256
VMEM_LIMIT = 56 << 20


def _params(*sem):
    return pltpu.CompilerParams(dimension_semantics=sem, vmem_limit_bytes=VMEM_LIMIT)


def _layer_norm(x, g, b):
    mu = jnp.mean(x, axis=-1, keepdims=True)
    xc = x - mu
    var = jnp.mean(xc * xc, axis=-1, keepdims=True)
    return xc * lax.rsqrt(var + LN_EPS) * g + b


def _matmul_kernel(x_ref, w_ref, o_ref):
    o_ref[...] = jnp.dot(x_ref[...].astype(BF16), w_ref[...], preferred_element_type=F32)


def _matmul(x, w_bf16):
    n, k = x.shape
    m = w_bf16.shape[1]
    return pl.pallas_call(
        _matmul_kernel,
        out_shape=jax.ShapeDtypeStruct((n, m), F32),
        grid=(n // ROW_TILE,),
        in_specs=[pl.BlockSpec((ROW_TILE, k), lambda i: (i, 0)),
                  pl.BlockSpec((k, m), lambda i: (0, 0))],
        out_specs=pl.BlockSpec((ROW_TILE, m), lambda i: (i, 0)),
        compiler_params=_params("parallel"),
        name="in_proj",
    )(x, w_bf16)


def _lru_scan_kernel(xr_ref, xp_ref, xn_ref, cw_ref, cb_ref, wa_ref, ba_ref, wx_ref, bx_ref,
                     lam_ref, h_ref, a_sc, b_sc, carry_sc, *, reverse, chunks_per_batch, lp):
    t_rows = xr_ref.shape[0]
    step = pl.program_id(0)
    n_steps = pl.num_programs(0)
    chunk = (n_steps - 1 - step) if reverse else step
    cib = chunk % chunks_per_batch
    pos = cib * t_rows + lax.broadcasted_iota(I32, (t_rows, 1), 0)
    row = lax.broadcasted_iota(I32, (t_rows, 1), 0)
    real = pos >= FRONT_PAD

    first_chunk = cib == 0
    last_chunk = cib == chunks_per_batch - 1

    xr = jnp.where(real, xr_ref[...], 0.0)
    prev_pos = cib * t_rows - SUBLANES + lax.broadcasted_iota(I32, (SUBLANES, 1), 0)
    prev = jnp.where(prev_pos >= FRONT_PAD, xp_ref[...], 0.0)
    nxt = jnp.where(last_chunk, 0.0, xn_ref[...])

    xm1 = jnp.where(row == 0, prev[7:8, :], pltpu.roll(xr, 1, 0))
    xm2 = pltpu.roll(xr, 2, 0)
    xm2 = jnp.where(row == 0, prev[6:7, :], xm2)
    xm2 = jnp.where(row == 1, prev[7:8, :], xm2)
    xp1 = jnp.where(row == t_rows - 1, nxt[0:1, :], pltpu.roll(xr, t_rows - 1, 0))
    cw = cw_ref[...]
    xc = cw[0:1, :] * xm2 + cw[1:2, :] * xm1 + cw[2:3, :] * xr + cw[3:4, :] * xp1 + cb_ref[...]

    xcb = xc.astype(BF16)
    ga = []
    gx = []
    for blk in range(LRU_BLOCKS):
        xblk = xcb[:, blk * LRU_BLOCK:(blk + 1) * LRU_BLOCK]
        ga.append(jnp.dot(xblk, wa_ref[blk], preferred_element_type=F32))
        gx.append(jnp.dot(xblk, wx_ref[blk], preferred_element_type=F32))
    gate_a = jax.nn.sigmoid(jnp.concatenate(ga, axis=-1) + ba_ref[...])
    gate_x = jax.nn.sigmoid(jnp.concatenate(gx, axis=-1) + bx_ref[...])

    z = -lam_ref[...]
    softplus = jnp.maximum(z, 0.0) + jnp.log1p(jnp.exp(-jnp.abs(z)))
    log_a = (-LRU_C * gate_a) * softplus
    a = jnp.exp(log_a)
    mult = jnp.sqrt(-jnp.tanh(log_a) * (a * a + 1.0))
    start = lp - 1 if reverse else FRONT_PAD
    mult = jnp.where(pos == start, 1.0, mult)
    b = jnp.where(real, mult * (gate_x * xc), 0.0)
    a_sc[...] = a
    b_sc[...] = b

    @pl.when(last_chunk if reverse else first_chunk)
    def _():
        carry_sc[...] = jnp.zeros_like(carry_sc)

    row8 = lax.broadcasted_iota(I32, (SUBLANES, 1), 0)
    n_groups = t_rows // SUBLANES

    def group(i, carry):
        g = (n_groups - 1 - i) if reverse else i
        r0 = pl.multiple_of(g * SUBLANES, SUBLANES)
        av = a_sc[pl.ds(r0, SUBLANES), :]
        bv = b_sc[pl.ds(r0, SUBLANES), :]
        for s in (1, 2, 4):
            shift = SUBLANES - s if reverse else s
            keep = (row8 < SUBLANES - s) if reverse else (row8 >= s)
            a_sh = pltpu.roll(av, shift, 0)
            b_sh = pltpu.roll(bv, shift, 0)
            bv = jnp.where(keep, av * b_sh + bv, bv)
            av = jnp.where(keep, av * a_sh, av)
        hv = av * carry + bv
        h_ref[pl.ds(r0, SUBLANES), :] = hv
        return hv[0:1, :] if reverse else hv[SUBLANES - 1:SUBLANES, :]

    carry_sc[...] = lax.fori_loop(0, n_groups, group, carry_sc[...])


def _lru_scan(u, cw, cb, wa_bf16, ba, wx_bf16, bx, lam, *, reverse, lp):
    n = u.shape[0]
    d = D_MODEL
    t = SCAN_CHUNK
    n_chunks = n // t
    cpb = lp // t
    t8 = t // SUBLANES
    n8 = n // SUBLANES

    def chunk_of(i):
        return (n_chunks - 1 - i) if reverse else i

    kern = functools.partial(_lru_scan_kernel, reverse=reverse, chunks_per_batch=cpb, lp=lp)
    full2 = lambda shape: pl.BlockSpec(shape, lambda i: (0, 0))
    full3 = lambda shape: pl.BlockSpec(shape, lambda i: (0, 0, 0))
    return pl.pallas_call(
        kern,
        out_shape=jax.ShapeDtypeStruct((n, d), F32),
        grid=(n_chunks,),
        in_specs=[
            pl.BlockSpec((t, d), lambda i: (chunk_of(i), 0)),
            pl.BlockSpec((SUBLANES, d), lambda i: (jnp.maximum(chunk_of(i) * t8 - 1, 0), 0)),
            pl.BlockSpec((SUBLANES, d), lambda i: (jnp.minimum((chunk_of(i) + 1) * t8, n8 - 1), 0)),
            full2((CONV_W, d)), full2((1, d)),
            full3((LRU_BLOCKS, LRU_BLOCK, LRU_BLOCK)), full2((1, d)),
            full3((LRU_BLOCKS, LRU_BLOCK, LRU_BLOCK)), full2((1, d)),
            full2((1, d)),
        ],
        out_specs=pl.BlockSpec((t, d), lambda i: (chunk_of(i), 0)),
        scratch_shapes=[pltpu.VMEM((t, d), F32), pltpu.VMEM((t, d), F32), pltpu.VMEM((1, d), F32)],
        compiler_params=_params("arbitrary"),
        name="lru_scan_bwd" if reverse else "lru_scan_fwd",
    )(u, u, u, cw, cb, wa_bf16, ba, wx_bf16, bx, lam)


def _gelu_tanh(y):
    c = 0.7978845608028654
    return y * (0.5 * (1.0 + jnp.tanh(c * (y + 0.044715 * (y * y * y)))))


def _lru_out_kernel(hf_ref, hb_ref, y_ref, res_ref, w_ref, g_ref, b_ref, o_ref):
    gated = (hf_ref[...] + hb_ref[...]) * _gelu_tanh(y_ref[...])
    mix = jnp.dot(gated.astype(BF16), w_ref[...], preferred_element_type=F32)
    o_ref[...] = _layer_norm(DN_ALPHA * res_ref[...] + mix, g_ref[...], b_ref[...])


def _lru_out(hf, hb, u, res, w_bf16, g, b):
    n, d = res.shape
    row = lambda j: pl.BlockSpec((ROW_TILE, d), lambda i, j=j: (i, j))
    vec = pl.BlockSpec((1, d), lambda i: (0, 0))
    return pl.pallas_call(
        _lru_out_kernel,
        out_shape=jax.ShapeDtypeStruct((n, d), F32),
        grid=(n // ROW_TILE,),
        in_specs=[row(0), row(0), row(1), row(0), pl.BlockSpec((d, d), lambda i: (0, 0)), vec, vec],
        out_specs=row(0),
        compiler_params=_params("parallel"),
        name="lru_out",
    )(hf, hb, u, res, w_bf16, g, b)


def _router_kernel(h_ref, w_ref, b_ref, idx_ref, gate_ref, rank_ref, cnt_ref, base_sc):
    tm = h_ref.shape[0]

    @pl.when(pl.program_id(0) == 0)
    def _():
        base_sc[...] = jnp.zeros_like(base_sc)

    logits = jnp.dot(h_ref[...], w_ref[...], preferred_element_type=F32,
                     precision=lax.Precision.HIGHEST) + b_ref[...]
    lane = lax.broadcasted_iota(I32, (tm, N_EXPERTS), 1).astype(F32)
    lane4 = lax.broadcasted_iota(I32, (tm, TOP_K), 1)
    r_i = lax.broadcasted_iota(I32, (tm, tm), 0)
    c_i = lax.broadcasted_iota(I32, (tm, tm), 1)
    lower = jnp.where(r_i > c_i, 1.0, 0.0).astype(BF16)

    vals = logits
    base = base_sc[...]
    idx_out = jnp.zeros((tm, TOP_K), I32)
    val_out = jnp.zeros((tm, TOP_K), F32)
    rank_out = jnp.zeros((tm, TOP_K), I32)
    for k in range(TOP_K):
        m = jnp.max(vals, axis=-1, keepdims=True)
        idx = jnp.min(jnp.where(vals == m, lane, float(N_EXPERTS)), axis=-1, keepdims=True)
        hit = lane == idx
        onehot = jnp.where(hit, 1.0, 0.0)
        before = jnp.dot(lower, onehot.astype(BF16), preferred_element_type=F32)
        rank = jnp.sum(onehot * (base + before), axis=-1, keepdims=True)
        base = base + jnp.sum(onehot, axis=0, keepdims=True)
        idx_out = jnp.where(lane4 == k, idx.astype(I32), idx_out)
        val_out = jnp.where(lane4 == k, m, val_out)
        rank_out = jnp.where(lane4 == k, rank.astype(I32), rank_out)
        vals = jnp.where(hit, -jnp.inf, vals)

    e = jnp.exp(val_out - val_out[:, 0:1])
    idx_ref[...] = idx_out
    gate_ref[...] = e / jnp.sum(e, axis=-1, keepdims=True)
    rank_ref[...] = rank_out
    base_sc[...] = base
    cnt_ref[...] = base.astype(I32)


def _router(h, w, b):
    n, d = h.shape
    tm = ROUTE_TILE
    out4 = lambda dt: jax.ShapeDtypeStruct((n, TOP_K), dt)
    spec4 = pl.BlockSpec((tm, TOP_K), lambda i: (i, 0))
    return pl.pallas_call(
        _router_kernel,
        out_shape=(out4(I32), out4(F32), out4(I32), jax.ShapeDtypeStruct((1, N_EXPERTS), I32)),
        grid=(n // tm,),
        in_specs=[pl.BlockSpec((tm, d), lambda i: (i, 0)),
                  pl.BlockSpec((d, N_EXPERTS), lambda i: (0, 0)),
                  pl.BlockSpec((1, N_EXPERTS), lambda i: (0, 0))],
        out_specs=(spec4, spec4, spec4, pl.BlockSpec((1, N_EXPERTS), lambda i: (0, 0))),
        scratch_shapes=[pltpu.VMEM((1, N_EXPERTS), F32)],
        compiler_params=_params("arbitrary"),
        name="router",
    )(h, w, b)


def _pack_bf16_pairs(x):
    half = x.shape[1] // 2
    lo = pltpu.bitcast(x[:, :half].astype(BF16).astype(F32), U32)
    hi = pltpu.bitcast(x[:, half:].astype(BF16).astype(F32), U32)
    return (lo >> 16) | (hi & jnp.uint32(0xFFFF0000))


def _unpack_bf16_pairs(u):
    lo = pltpu.bitcast(u << 16, F32).astype(BF16)
    hi = pltpu.bitcast(u & jnp.uint32(0xFFFF0000), F32).astype(BF16)
    return jnp.concatenate([lo, hi], axis=-1)


def _dispatch_kernel(dest_ref, h_ref, xs_in_ref, xs_ref, buf, sem):
    del xs_in_ref
    tt = h_ref.shape[0]
    buf[...] = _pack_bf16_pairs(h_ref[...])

    def row_copy(t, k):
        d = dest_ref[0, 0, t * TOP_K + k]
        return pltpu.make_async_copy(buf.at[pl.ds(t, 1), :], xs_ref.at[pl.ds(d, 1), :], sem)

    def issue(t, c):
        for k in range(TOP_K):
            row_copy(t, k).start()
        return c

    def drain(t, c):
        for k in range(TOP_K):
            row_copy(t, k).wait()
        return c

    lax.fori_loop(0, tt, issue, 0)
    lax.fori_loop(0, tt, drain, 0)


def _dispatch(dest_tiles, h, xs_zero):
    n, d = h.shape
    tt = MOVE_TILE
    return pl.pallas_call(
        _dispatch_kernel,
        out_shape=jax.ShapeDtypeStruct(xs_zero.shape, U32),
        grid=(n // tt,),
        in_specs=[pl.BlockSpec((1, 1, tt * TOP_K), lambda i: (i, 0, 0), memory_space=pltpu.SMEM),
                  pl.BlockSpec((tt, d), lambda i: (i, 0)),
                  pl.BlockSpec(memory_space=pl.ANY)],
        out_specs=pl.BlockSpec(memory_space=pl.ANY),
        scratch_shapes=[pltpu.VMEM((tt, d // 2), U32), pltpu.SemaphoreType.DMA],
        input_output_aliases={2: 0},
        compiler_params=_params("arbitrary"),
        name="moe_dispatch",
    )(dest_tiles, h, xs_zero)


def _expert_kernel(be_ref, nu_ref, xs_ref, wgu_ref, bgu_ref, wd_ref, bd_ref, y_ref, wgu_sc, wd_sc):
    i = pl.program_id(0)
    e = be_ref[i]
    e_prev = be_ref[jnp.maximum(i - 1, 0)]
    d = wd_sc.shape[0]

    @pl.when((i == 0) | (e != e_prev))
    def _():
        for r in range(0, d, LANES):
            wgu_sc[r:r + LANES, :] = wgu_ref[0, r:r + LANES, :].astype(BF16)
            wd_sc[r:r + LANES, :] = wd_ref[0, r:r + LANES, :].astype(BF16)

    @pl.when(i < nu_ref[0])
    def _():
        x = _unpack_bf16_pairs(xs_ref[...])
        h = jnp.dot(x, wgu_sc[...], preferred_element_type=F32) + bgu_ref[0]
        glu = jnp.minimum(h[:, :d], SWIGLU_LIMIT)
        lin = jnp.clip(h[:, d:], -SWIGLU_LIMIT, SWIGLU_LIMIT)
        act = glu * jax.nn.sigmoid(SWIGLU_ALPHA * glu) * (lin + 1.0)
        y_ref[...] = jnp.dot(act.astype(BF16), wd_sc[...], preferred_element_type=F32) + bd_ref[0]

    @pl.when(i >= nu_ref[0])
    def _():
        y_ref[...] = jnp.zeros_like(y_ref)


def _experts(block_e, n_used, xs, w_gu, b_gu, w_down, b_down):
    cap = xs.shape[0]
    d = D_MODEL
    tm = EXPERT_TILE
    n_blocks = cap // tm
    grid_spec = pltpu.PrefetchScalarGridSpec(
        num_scalar_prefetch=2,
        grid=(n_blocks,),
        in_specs=[
            pl.BlockSpec((tm, d // 2), lambda i, be, nu: (i, 0)),
            pl.BlockSpec((1, d, 2 * d), lambda i, be, nu: (be[i], 0, 0)),
            pl.BlockSpec((1, 1, 2 * d), lambda i, be, nu: (be[i], 0, 0)),
            pl.BlockSpec((1, d, d), lambda i, be, nu: (be[i], 0, 0)),
            pl.BlockSpec((1, 1, d), lambda i, be, nu: (be[i], 0, 0)),
        ],
        out_specs=pl.BlockSpec((tm, d), lambda i, be, nu: (i, 0)),
        scratch_shapes=[pltpu.VMEM((d, 2 * d), BF16), pltpu.VMEM((d, d), BF16)],
    )
    return pl.pallas_call(
        _expert_kernel,
        out_shape=jax.ShapeDtypeStruct((cap, d), F32),
        grid_spec=grid_spec,
        compiler_params=_params("arbitrary"),
        name="moe_experts",
    )(block_e, n_used, xs, w_gu, b_gu[:, None, :], w_down, b_down[:, None, :])


def _combine_kernel(dest_ref, gate_ref, res_ref, g_ref, b_ref, ys_ref, o_ref, buf, sem):
    tt = res_ref.shape[0]

    def row_copy(t, k):
        d = dest_ref[0, 0, t * TOP_K + k]
        return pltpu.make_async_copy(ys_ref.at[pl.ds(d, 1), :], buf.at[k, pl.ds(t, 1), :], sem)

    def issue(t, c):
        for k in range(TOP_K):
            row_copy(t, k).start()
        return c

    def drain(t, c):
        for k in range(TOP_K):
            row_copy(t, k).wait()
        return c

    lax.fori_loop(0, tt, issue, 0)
    lax.fori_loop(0, tt, drain, 0)

    gates = gate_ref[...]
    ffn = gates[:, 0:1] * buf[0]
    for k in range(1, TOP_K):
        ffn = ffn + gates[:, k:k + 1] * buf[k]
    o_ref[...] = _layer_norm(DN_ALPHA * res_ref[...] + ffn, g_ref[...], b_ref[...])


def _combine(dest_tiles, gates, res, g, b, ys):
    n, d = res.shape
    tt = MOVE_TILE
    vec = pl.BlockSpec((1, d), lambda i: (0, 0))
    return pl.pallas_call(
        _combine_kernel,
        out_shape=jax.ShapeDtypeStruct((n, d), F32),
        grid=(n // tt,),
        in_specs=[pl.BlockSpec((1, 1, tt * TOP_K), lambda i: (i, 0, 0), memory_space=pltpu.SMEM),
                  pl.BlockSpec((tt, TOP_K), lambda i: (i, 0)),
                  pl.BlockSpec((tt, d), lambda i: (i, 0)),
                  vec, vec,
                  pl.BlockSpec(memory_space=pl.ANY)],
        out_specs=pl.BlockSpec((tt, d), lambda i: (i, 0)),
        scratch_shapes=[pltpu.VMEM((TOP_K, tt, d), F32), pltpu.SemaphoreType.DMA],
        compiler_params=_params("arbitrary"),
        name="moe_combine",
    )(dest_tiles, gates, res, g, b, ys)


def _moe_layer(h, router_w, router_b, w_gu, b_gu, w_down, b_down, ln_g, ln_b):
    n, d = h.shape
    tm = EXPERT_TILE
    idx, gates, rank, counts = _router(h, router_w, router_b[None, :])
    counts = counts[0]
    padded = (counts + tm - 1) // tm * tm
    pad_end = jnp.cumsum(padded)
    pad_start = pad_end - padded
    dest = pad_start[idx] + rank
    n_blocks = -(-(n * TOP_K + N_EXPERTS * (tm - 1)) // tm)
    block_e = jnp.minimum(
        jnp.searchsorted(pad_end, jnp.arange(n_blocks, dtype=I32) * tm, side='right'),
        N_EXPERTS - 1).astype(I32)
    n_used = (pad_end[-1:] // tm).astype(I32)
    dest_tiles = dest.reshape(n // MOVE_TILE, 1, MOVE_TILE * TOP_K)

    xs = _dispatch(dest_tiles, h, jnp.zeros((n_blocks * tm, d // 2), U32))
    ys = _experts(block_e, n_used, xs, w_gu, b_gu, w_down, b_down)
    return _combine(dest_tiles, gates, h, ln_g[None, :], ln_b[None, :], ys)


def _qkv_kernel(x_ref, w_ref, o_ref):
    xb = x_ref[...].astype(BF16)
    d = x_ref.shape[1]
    for part in range(3):
        acc = jnp.dot(xb, w_ref[:, part * d:(part + 1) * d], preferred_element_type=F32)
        if part == 0:
            acc = acc * (NA_HEAD_DIM ** -0.5)
        for p in range(HEAD_PAIRS):
            o_ref[part * HEAD_PAIRS + p] = acc[:, p * LANES:(p + 1) * LANES].astype(BF16)


def _qkv(x, w_bf16):
    n, d = x.shape
    return pl.pallas_call(
        _qkv_kernel,
        out_shape=jax.ShapeDtypeStruct((3 * HEAD_PAIRS, n, LANES), BF16),
        grid=(n // ROW_TILE,),
        in_specs=[pl.BlockSpec((ROW_TILE, d), lambda i: (i, 0)),
                  pl.BlockSpec((d, 3 * d), lambda i: (0, 0))],
        out_specs=pl.BlockSpec((3 * HEAD_PAIRS, ROW_TILE, LANES), lambda i: (0, i, 0)),
        compiler_params=_params("parallel"),
        name="na_qkv",
    )(x, w_bf16)


def _na_kernel(q_ref, k_ref, v_ref, bias_ref, mb_ref, o_ref, *, rows):
    s = pl.program_id(1)
    w = GRID_W
    lane = lax.broadcasted_iota(I32, (w, LANES), 1)
    low = lane < NA_HEAD_DIM
    contract_last = (((1,), (1,)), ((), ()))
    meta0 = FRONT_PAD

    @pl.when(s == 0)
    def _():
        o_ref[...] = jnp.zeros_like(o_ref)

    def attend(p, with_window, k0):
        qp = q_ref[p]
        km = k_ref[p, meta0:meta0 + N_META, :]
        vm = v_ref[p, meta0:meta0 + N_META, :]
        if with_window:
            kw = k_ref[p, pl.ds(k0, NA_KH * w), :]
            vw = v_ref[p, pl.ds(k0, NA_KH * w), :]
        outs = []
        for hh in range(2):
            head = 2 * p + hh
            qh = jnp.where(low if hh == 0 else jnp.logical_not(low), qp, jnp.zeros_like(qp))
            s_meta = lax.dot_general(qh, km, contract_last, preferred_element_type=F32) + mb_ref[head]
            m = jnp.max(s_meta, axis=-1, keepdims=True)
            if with_window:
                s_win = lax.dot_general(qh, kw, contract_last, preferred_element_type=F32) + bias_ref[0, head]
                m = jnp.maximum(m, jnp.max(s_win, axis=-1, keepdims=True))
                p_win = jnp.exp(s_win - m)
            p_meta = jnp.exp(s_meta - m)
            denom = jnp.sum(p_meta, axis=-1, keepdims=True)
            o = jnp.dot(p_meta.astype(BF16), vm, preferred_element_type=F32)
            if with_window:
                denom = denom + jnp.sum(p_win, axis=-1, keepdims=True)
                o = o + jnp.dot(p_win.astype(BF16), vw, preferred_element_type=F32)
            outs.append(o / denom)
        return jnp.where(low, outs[0], outs[1])

    @pl.when(s == 1)
    def _():
        qrow = lax.broadcasted_iota(I32, (w, LANES), 0)

        def body(p, c):
            o = attend(p, False, None)
            o_ref[p] = jnp.where(qrow >= w - N_META, o, 0.0).astype(o_ref.dtype)
            return c

        lax.fori_loop(0, HEAD_PAIRS, body, 0)

    @pl.when(s >= 2)
    def _():
        r = s - 2
        rs = jnp.clip(r - NA_KH // 2, 0, rows - NA_KH)
        k0 = pl.multiple_of(LANES + rs * w, w)

        def body(p, c):
            o_ref[p] = attend(p, True, k0).astype(o_ref.dtype)
            return c

        lax.fori_loop(0, HEAD_PAIRS, body, 0)


def _na_bias_table(rpb, rows):
    del rows
    w = GRID_W
    q = jnp.arange(w)
    col_start = jnp.clip(q - NA_KW // 2, 0, w - NA_KW)
    c = jnp.arange(w)
    in_win = (c[None, :] >= col_start[:, None]) & (c[None, :] < col_start[:, None] + NA_KW)
    col_off = jnp.clip(c[None, :] - q[:, None] + (NA_KW - 1), 0, 2 * NA_KW - 2)
    v = jnp.arange(NA_KH)
    j = jnp.arange(NA_KH)
    row_off = j[None, :] - v[:, None] + (NA_KH - 1)
    tab = rpb.astype(F32)[:, row_off[:, :, None, None], col_off[None, None, :, :]]
    tab = jnp.where(in_win[None, None, None], tab, NEG_BIG)
    tab = jnp.transpose(tab, (1, 0, 3, 2, 4))
    return tab.reshape(NA_KH, NA_HEADS, w, NA_KH * w)


def _na_attention(qkv, bias_tab, meta_bias, *, batch, lp):
    n = qkv.shape[1]
    w = GRID_W
    rows = (lp - LANES) // w
    tiles = lp // w

    def variant(b, s):
        r = jnp.clip(s - 2, 0, rows - 1)
        return r - jnp.clip(r - NA_KH // 2, 0, rows - NA_KH)

    kern = functools.partial(_na_kernel, rows=rows)
    return pl.pallas_call(
        kern,
        out_shape=jax.ShapeDtypeStruct((HEAD_PAIRS, n, LANES), BF16),
        grid=(batch, tiles),
        in_specs=[
            pl.BlockSpec((HEAD_PAIRS, w, LANES), lambda b, s: (0, b * tiles + s, 0)),
            pl.BlockSpec((HEAD_PAIRS, lp, LANES), lambda b, s: (1, b, 0)),
            pl.BlockSpec((HEAD_PAIRS, lp, LANES), lambda b, s: (2, b, 0)),
            pl.BlockSpec((1, NA_HEADS, w, NA_KH * w), lambda b, s: (variant(b, s), 0, 0, 0)),
            pl.BlockSpec((NA_HEADS, 1, N_META), lambda b, s: (0, 0, 0)),
        ],
        out_specs=pl.BlockSpec((HEAD_PAIRS, w, LANES), lambda b, s: (0, b * tiles + s, 0)),
        compiler_params=_params("parallel", "arbitrary"),
        name="na_attention",
    )(qkv, qkv, qkv, bias_tab, meta_bias[:, None, :])


def _na_out_kernel(o_ref, res_ref, w_ref, g_ref, b_ref, out_ref):
    att = jnp.concatenate([o_ref[p] for p in range(HEAD_PAIRS)], axis=-1)
    mix = jnp.dot(att, w_ref[...], preferred_element_type=F32)
    out_ref[...] = _layer_norm(DN_ALPHA * res_ref[...] + mix, g_ref[...], b_ref[...])


def _na_out(o, res, w_bf16, g, b):
    n, d = res.shape
    vec = pl.BlockSpec((1, d), lambda i: (0, 0))
    return pl.pallas_call(
        _na_out_kernel,
        out_shape=jax.ShapeDtypeStruct((n, d), F32),
        grid=(n // ROW_TILE,),
        in_specs=[pl.BlockSpec((HEAD_PAIRS, ROW_TILE, LANES), lambda i: (0, i, 0)),
                  pl.BlockSpec((ROW_TILE, d), lambda i: (i, 0)),
                  pl.BlockSpec((d, d), lambda i: (0, 0)), vec, vec],
        out_specs=pl.BlockSpec((ROW_TILE, d), lambda i: (i, 0)),
        compiler_params=_params("parallel"),
        name="na_out",
    )(o, res, w_bf16, g, b)


def kernel(x, meta_tokens, lru_w_in, lru_conv_w, lru_conv_b, lru_wa, lru_ba, lru_wx, lru_bx, lru_lambda, lru_w_out, na_w_qkv, na_rpb, na_meta_bias, na_w_out, ln_mix_g, ln_mix_b, router_w, router_b, moe_w_gu, moe_b_gu, moe_w_down, moe_b_down, ln_ffn_g, ln_ffn_b):
    batch, seq, d = x.shape
    lp = LANES + seq
    assert d == D_MODEL and seq % GRID_W == 0 and seq // GRID_W >= NA_KH
    assert lp % SCAN_CHUNK == 0 and (batch * lp) % ROW_TILE == 0
    n = batch * lp

    front = jnp.zeros((batch, FRONT_PAD, d), x.dtype)
    meta = jnp.broadcast_to(meta_tokens[None].astype(x.dtype), (batch, N_META, d))
    h = jnp.concatenate([front, meta, x], axis=1).reshape(n, d)

    u = _matmul(h, lru_w_in[0].astype(BF16))
    row = lambda v: v[None, :]
    scans = []
    for direction, reverse in ((0, False), (1, True)):
        scans.append(_lru_scan(
            u, lru_conv_w[0], row(lru_conv_b[0]),
            lru_wa[0, direction].astype(BF16), row(lru_ba[0, direction]),
            lru_wx[0, direction].astype(BF16), row(lru_bx[0, direction]),
            row(lru_lambda[0, direction]), reverse=reverse, lp=lp))
    h = _lru_out(scans[0], scans[1], u, h, lru_w_out[0].astype(BF16), row(ln_mix_g[0]), row(ln_mix_b[0]))
    h = _moe_layer(h, router_w[0], router_b[0], moe_w_gu[0], moe_b_gu[0], moe_w_down[0], moe_b_down[0],
                   ln_ffn_g[0], ln_ffn_b[0])

    qkv = _qkv(h, na_w_qkv[0].astype(BF16))
    att = _na_attention(qkv, _na_bias_table(na_rpb[0], seq // GRID_W), na_meta_bias[0].astype(F32),
                        batch=batch, lp=lp)
    h = _na_out(att, h, na_w_out[0].astype(BF16), row(ln_mix_g[1]), row(ln_mix_b[1]))
    h = _moe_layer(h, router_w[1], router_b[1], moe_w_gu[1], moe_b_gu[1], moe_w_down[1], moe_b_down[1],
                   ln_ffn_g[1], ln_ffn_b[1])

    return h.reshape(batch, lp, d)[:, LANES:]
```

```python
import functools

import jax
import jax.numpy as jnp
from jax import lax
from jax.experimental import pallas as pl
from jax.experimental.pallas import tpu as pltpu

F32 = jnp.float32
BF16 = jnp.bfloat16
I32 = jnp.int32
U32 = jnp.uint32

D_MODEL = 1024
N_META = 16
GRID_W = 64
LRU_BLOCKS = 4
LRU_BLOCK = D_MODEL // LRU_BLOCKS
CONV_W = 4
LRU_C = 8.0
NA_HEADS = 16
NA_HEAD_DIM = D_MODEL // NA_HEADS
NA_KH = 8
NA_KW = 16
N_EXPERTS = 32
TOP_K = 4
SWIGLU_LIMIT = 7.0
SWIGLU_ALPHA = 1.702
DEPTH = 2
DN_ALPHA = (2.0 * DEPTH) ** 0.25
LN_EPS = 1e-5

LANES = 128
SUBLANES = 8
FRONT_PAD = LANES - N_META
HEAD_PAIRS = D_MODEL // LANES
NEG_BIG = -1e30

ROW_TILE = 512
SCAN_CHUNK = 384
ROUTE_TILE = 512
MOVE_TILE = 256
EXPERT_TILE = 256
VMEM_LIMIT = 56 << 20


def _params(*sem):
    return pltpu.CompilerParams(dimension_semantics=sem, vmem_limit_bytes=VMEM_LIMIT)


def _layer_norm(x, g, b):
    mu = jnp.mean(x, axis=-1, keepdims=True)
    xc = x - mu
    var = jnp.mean(xc * xc, axis=-1, keepdims=True)
    return xc * lax.rsqrt(var + LN_EPS) * g + b


def _matmul_kernel(x_ref, w_ref, o_ref):
    o_ref[...] = jnp.dot(x_ref[...].astype(BF16), w_ref[...], preferred_element_type=F32)


def _matmul(x, w_bf16):
    n, k = x.shape
    m = w_bf16.shape[1]
    return pl.pallas_call(
        _matmul_kernel,
        out_shape=jax.ShapeDtypeStruct((n, m), F32),
        grid=(n // ROW_TILE,),
        in_specs=[pl.BlockSpec((ROW_TILE, k), lambda i: (i, 0)),
                  pl.BlockSpec((k, m), lambda i: (0, 0))],
        out_specs=pl.BlockSpec((ROW_TILE, m), lambda i: (i, 0)),
        compiler_params=_params("parallel"),
        name="in_proj",
    )(x, w_bf16)


def _lru_scan_kernel(xr_ref, xp_ref, xn_ref, cw_ref, cb_ref, wa_ref, ba_ref, wx_ref, bx_ref,
                     lam_ref, h_ref, a_sc, b_sc, carry_sc, *, reverse, chunks_per_batch, lp):
    t_rows = xr_ref.shape[0]
    step = pl.program_id(0)
    n_steps = pl.num_programs(0)
    chunk = (n_steps - 1 - step) if reverse else step
    cib = chunk % chunks_per_batch
    pos = cib * t_rows + lax.broadcasted_iota(I32, (t_rows, 1), 0)
    row = lax.broadcasted_iota(I32, (t_rows, 1), 0)
    real = pos >= FRONT_PAD

    first_chunk = cib == 0
    last_chunk = cib == chunks_per_batch - 1

    xr = jnp.where(real, xr_ref[...], 0.0)
    prev_pos = cib * t_rows - SUBLANES + lax.broadcasted_iota(I32, (SUBLANES, 1), 0)
    prev = jnp.where(prev_pos >= FRONT_PAD, xp_ref[...], 0.0)
    nxt = jnp.where(last_chunk, 0.0, xn_ref[...])

    xm1 = jnp.where(row == 0, prev[7:8, :], pltpu.roll(xr, 1, 0))
    xm2 = pltpu.roll(xr, 2, 0)
    xm2 = jnp.where(row == 0, prev[6:7, :], xm2)
    xm2 = jnp.where(row == 1, prev[7:8, :], xm2)
    xp1 = jnp.where(row == t_rows - 1, nxt[0:1, :], pltpu.roll(xr, t_rows - 1, 0))
    cw = cw_ref[...]
    xc = cw[0:1, :] * xm2 + cw[1:2, :] * xm1 + cw[2:3, :] * xr + cw[3:4, :] * xp1 + cb_ref[...]

    xcb = xc.astype(BF16)
    ga = []
    gx = []
    for blk in range(LRU_BLOCKS):
        xblk = xcb[:, blk * LRU_BLOCK:(blk + 1) * LRU_BLOCK]
        ga.append(jnp.dot(xblk, wa_ref[blk], preferred_element_type=F32))
        gx.append(jnp.dot(xblk, wx_ref[blk], preferred_element_type=F32))
    gate_a = jax.nn.sigmoid(jnp.concatenate(ga, axis=-1) + ba_ref[...])
    gate_x = jax.nn.sigmoid(jnp.concatenate(gx, axis=-1) + bx_ref[...])

    z = -lam_ref[...]
    softplus = jnp.maximum(z, 0.0) + jnp.log1p(jnp.exp(-jnp.abs(z)))
    log_a = (-LRU_C * gate_a) * softplus
    a = jnp.exp(log_a)
    mult = jnp.sqrt(-jnp.tanh(log_a) * (a * a + 1.0))
    start = lp - 1 if reverse else FRONT_PAD
    mult = jnp.where(pos == start, 1.0, mult)
    b = jnp.where(real, mult * (gate_x * xc), 0.0)
    a_sc[...] = a
    b_sc[...] = b

    @pl.when(last_chunk if reverse else first_chunk)
    def _():
        carry_sc[...] = jnp.zeros_like(carry_sc)

    row8 = lax.broadcasted_iota(I32, (SUBLANES, 1), 0)
    n_groups = t_rows // SUBLANES

    def group(i, carry):
        g = (n_groups - 1 - i) if reverse else i
        r0 = pl.multiple_of(g * SUBLANES, SUBLANES)
        av = a_sc[pl.ds(r0, SUBLANES), :]
        bv = b_sc[pl.ds(r0, SUBLANES), :]
        for s in (1, 2, 4):
            shift = SUBLANES - s if reverse else s
            keep = (row8 < SUBLANES - s) if reverse else (row8 >= s)
            a_sh = pltpu.roll(av, shift, 0)
            b_sh = pltpu.roll(bv, shift, 0)
            bv = jnp.where(keep, av * b_sh + bv, bv)
            av = jnp.where(keep, av * a_sh, av)
        hv = av * carry + bv
        h_ref[pl.ds(r0, SUBLANES), :] = hv
        return hv[0:1, :] if reverse else hv[SUBLANES - 1:SUBLANES, :]

    carry_sc[...] = lax.fori_loop(0, n_groups, group, carry_sc[...])


def _lru_scan(u, cw, cb, wa_bf16, ba, wx_bf16, bx, lam, *, reverse, lp):
    n = u.shape[0]
    d = D_MODEL
    t = SCAN_CHUNK
    n_chunks = n // t
    cpb = lp // t
    t8 = t // SUBLANES
    n8 = n // SUBLANES

    def chunk_of(i):
        return (n_chunks - 1 - i) if reverse else i

    kern = functools.partial(_lru_scan_kernel, reverse=reverse, chunks_per_batch=cpb, lp=lp)
    full2 = lambda shape: pl.BlockSpec(shape, lambda i: (0, 0))
    full3 = lambda shape: pl.BlockSpec(shape, lambda i: (0, 0, 0))
    return pl.pallas_call(
        kern,
        out_shape=jax.ShapeDtypeStruct((n, d), F32),
        grid=(n_chunks,),
        in_specs=[
            pl.BlockSpec((t, d), lambda i: (chunk_of(i), 0)),
            pl.BlockSpec((SUBLANES, d), lambda i: (jnp.maximum(chunk_of(i) * t8 - 1, 0), 0)),
            pl.BlockSpec((SUBLANES, d), lambda i: (jnp.minimum((chunk_of(i) + 1) * t8, n8 - 1), 0)),
            full2((CONV_W, d)), full2((1, d)),
            full3((LRU_BLOCKS, LRU_BLOCK, LRU_BLOCK)), full2((1, d)),
            full3((LRU_BLOCKS, LRU_BLOCK, LRU_BLOCK)), full2((1, d)),
            full2((1, d)),
        ],
        out_specs=pl.BlockSpec((t, d), lambda i: (chunk_of(i), 0)),
        scratch_shapes=[pltpu.VMEM((t, d), F32), pltpu.VMEM((t, d), F32), pltpu.VMEM((1, d), F32)],
        compiler_params=_params("arbitrary"),
        name="lru_scan_bwd" if reverse else "lru_scan_fwd",
    )(u, u, u, cw, cb, wa_bf16, ba, wx_bf16, bx, lam)


def _gelu_tanh(y):
    c = 0.7978845608028654
    return y * (0.5 * (1.0 + jnp.tanh(c * (y + 0.044715 * (y * y * y)))))


def _lru_out_kernel(hf_ref, hb_ref, y_ref, res_ref, w_ref, g_ref, b_ref, o_ref):
    gated = (hf_ref[...] + hb_ref[...]) * _gelu_tanh(y_ref[...])
    mix = jnp.dot(gated.astype(BF16), w_ref[...], preferred_element_type=F32)
    o_ref[...] = _layer_norm(DN_ALPHA * res_ref[...] + mix, g_ref[...], b_ref[...])


def _lru_out(hf, hb, u, res, w_bf16, g, b):
    n, d = res.shape
    row = lambda j: pl.BlockSpec((ROW_TILE, d), lambda i, j=j: (i, j))
    vec = pl.BlockSpec((1, d), lambda i: (0, 0))
    return pl.pallas_call(
        _lru_out_kernel,
        out_shape=jax.ShapeDtypeStruct((n, d), F32),
        grid=(n // ROW_TILE,),
        in_specs=[row(0), row(0), row(1), row(0), pl.BlockSpec((d, d), lambda i: (0, 0)), vec, vec],
        out_specs=row(0),
        compiler_params=_params("parallel"),
        name="lru_out",
    )(hf, hb, u, res, w_bf16, g, b)


def _router_kernel(h_ref, w_ref, b_ref, idx_ref, gate_ref, rank_ref, cnt_ref, base_sc):
    tm = h_ref.shape[0]

    @pl.when(pl.program_id(0) == 0)
    def _():
        base_sc[...] = jnp.zeros_like(base_sc)

    logits = jnp.dot(h_ref[...], w_ref[...], preferred_element_type=F32,
                     precision=lax.Precision.HIGHEST) + b_ref[...]
    lane = lax.broadcasted_iota(I32, (tm, N_EXPERTS), 1).astype(F32)
    lane4 = lax.broadcasted_iota(I32, (tm, TOP_K), 1)
    r_i = lax.broadcasted_iota(I32, (tm, tm), 0)
    c_i = lax.broadcasted_iota(I32, (tm, tm), 1)
    lower = jnp.where(r_i > c_i, 1.0, 0.0).astype(BF16)

    vals = logits
    base = base_sc[...]
    idx_out = jnp.zeros((tm, TOP_K), I32)
    val_out = jnp.zeros((tm, TOP_K), F32)
    rank_out = jnp.zeros((tm, TOP_K), I32)
    for k in range(TOP_K):
        m = jnp.max(vals, axis=-1, keepdims=True)
        idx = jnp.min(jnp.where(vals == m, lane, float(N_EXPERTS)), axis=-1, keepdims=True)
        hit = lane == idx
        onehot = jnp.where(hit, 1.0, 0.0)
        before = jnp.dot(lower, onehot.astype(BF16), preferred_element_type=F32)
        rank = jnp.sum(onehot * (base + before), axis=-1, keepdims=True)
        base = base + jnp.sum(onehot, axis=0, keepdims=True)
        idx_out = jnp.where(lane4 == k, idx.astype(I32), idx_out)
        val_out = jnp.where(lane4 == k, m, val_out)
        rank_out = jnp.where(lane4 == k, rank.astype(I32), rank_out)
        vals = jnp.where(hit, -jnp.inf, vals)

    e = jnp.exp(val_out - val_out[:, 0:1])
    idx_ref[...] = idx_out
    gate_ref[...] = e / jnp.sum(e, axis=-1, keepdims=True)
    rank_ref[...] = rank_out
    base_sc[...] = base
    cnt_ref[...] = base.astype(I32)


def _router(h, w, b):
    n, d = h.shape
    tm = ROUTE_TILE
    out4 = lambda dt: jax.ShapeDtypeStruct((n, TOP_K), dt)
    spec4 = pl.BlockSpec((tm, TOP_K), lambda i: (i, 0))
    return pl.pallas_call(
        _router_kernel,
        out_shape=(out4(I32), out4(F32), out4(I32), jax.ShapeDtypeStruct((1, N_EXPERTS), I32)),
        grid=(n // tm,),
        in_specs=[pl.BlockSpec((tm, d), lambda i: (i, 0)),
                  pl.BlockSpec((d, N_EXPERTS), lambda i: (0, 0)),
                  pl.BlockSpec((1, N_EXPERTS), lambda i: (0, 0))],
        out_specs=(spec4, spec4, spec4, pl.BlockSpec((1, N_EXPERTS), lambda i: (0, 0))),
        scratch_shapes=[pltpu.VMEM((1, N_EXPERTS), F32)],
        compiler_params=_params("arbitrary"),
        name="router",
    )(h, w, b)


def _pack_bf16_pairs(x):
    half = x.shape[1] // 2
    lo = pltpu.bitcast(x[:, :half].astype(BF16).astype(F32), U32)
    hi = pltpu.bitcast(x[:, half:].astype(BF16).astype(F32), U32)
    return (lo >> 16) | (hi & jnp.uint32(0xFFFF0000))


def _unpack_bf16_pairs(u):
    lo = pltpu.bitcast(u << 16, F32).astype(BF16)
    hi = pltpu.bitcast(u & jnp.uint32(0xFFFF0000), F32).astype(BF16)
    return jnp.concatenate([lo, hi], axis=-1)


def _dispatch_kernel(dest_ref, h_ref, xs_in_ref, xs_ref, buf, sem):
    del xs_in_ref
    tt = h_ref.shape[0]
    i = pl.program_id(0)
    last = pl.num_programs(0) - 1
    slot = i % 2

    def row_copy(s, t, d):
        return pltpu.make_async_copy(buf.at[s, pl.ds(t, 1), :], xs_ref.at[pl.ds(d, 1), :], sem.at[s])

    def issue(t, c):
        for k in range(TOP_K):
            row_copy(slot, t, dest_ref[0, 0, t * TOP_K + k]).start(priority=k % 2)
        return c

    def drain(s):
        def body(t, c):
            for _ in range(TOP_K):
                row_copy(s, 0, 0).wait()
            return c
        lax.fori_loop(0, tt, body, 0)

    buf[slot] = _pack_bf16_pairs(h_ref[...])
    lax.fori_loop(0, tt, issue, 0)

    @pl.when(i > 0)
    def _():
        drain(1 - slot)

    @pl.when(i == last)
    def _():
        drain(slot)


def _dispatch(dest_tiles, h, xs_zero):
    n, d = h.shape
    tt = MOVE_TILE
    return pl.pallas_call(
        _dispatch_kernel,
        out_shape=jax.ShapeDtypeStruct(xs_zero.shape, U32),
        grid=(n // tt,),
        in_specs=[pl.BlockSpec((1, 1, tt * TOP_K), lambda i: (i, 0, 0), memory_space=pltpu.SMEM),
                  pl.BlockSpec((tt, d), lambda i: (i, 0)),
                  pl.BlockSpec(memory_space=pl.ANY)],
        out_specs=pl.BlockSpec(memory_space=pl.ANY),
        scratch_shapes=[pltpu.VMEM((2, tt, d // 2), U32), pltpu.SemaphoreType.DMA((2,))],
        input_output_aliases={2: 0},
        compiler_params=_params("arbitrary"),
        name="moe_dispatch",
    )(dest_tiles, h, xs_zero)


def _expert_kernel(be_ref, nu_ref, xs_ref, wgu_ref, bgu_ref, wd_ref, bd_ref, y_ref, wgu_sc, wd_sc):
    i = pl.program_id(0)
    e = be_ref[i]
    e_prev = be_ref[jnp.maximum(i - 1, 0)]
    d = wd_sc.shape[0]

    @pl.when((i == 0) | (e != e_prev))
    def _():
        for r in range(0, d, LANES):
            wgu_sc[r:r + LANES, :] = wgu_ref[0, r:r + LANES, :].astype(BF16)
            wd_sc[r:r + LANES, :] = wd_ref[0, r:r + LANES, :].astype(BF16)

    @pl.when(i < nu_ref[0])
    def _():
        x = _unpack_bf16_pairs(xs_ref[...])
        h = jnp.dot(x, wgu_sc[...], preferred_element_type=F32) + bgu_ref[0]
        glu = jnp.minimum(h[:, :d], SWIGLU_LIMIT)
        lin = jnp.clip(h[:, d:], -SWIGLU_LIMIT, SWIGLU_LIMIT)
        act = glu * jax.nn.sigmoid(SWIGLU_ALPHA * glu) * (lin + 1.0)
        y_ref[...] = jnp.dot(act.astype(BF16), wd_sc[...], preferred_element_type=F32) + bd_ref[0]

    @pl.when(i >= nu_ref[0])
    def _():
        y_ref[...] = jnp.zeros_like(y_ref)


def _experts(block_e, n_used, xs, w_gu, b_gu, w_down, b_down):
    cap = xs.shape[0]
    d = D_MODEL
    tm = EXPERT_TILE
    n_blocks = cap // tm
    grid_spec = pltpu.PrefetchScalarGridSpec(
        num_scalar_prefetch=2,
        grid=(n_blocks,),
        in_specs=[
            pl.BlockSpec((tm, d // 2), lambda i, be, nu: (i, 0)),
            pl.BlockSpec((1, d, 2 * d), lambda i, be, nu: (be[i], 0, 0)),
            pl.BlockSpec((1, 1, 2 * d), lambda i, be, nu: (be[i], 0, 0)),
            pl.BlockSpec((1, d, d), lambda i, be, nu: (be[i], 0, 0)),
            pl.BlockSpec((1, 1, d), lambda i, be, nu: (be[i], 0, 0)),
        ],
        out_specs=pl.BlockSpec((tm, d), lambda i, be, nu: (i, 0)),
        scratch_shapes=[pltpu.VMEM((d, 2 * d), BF16), pltpu.VMEM((d, d), BF16)],
    )
    return pl.pallas_call(
        _expert_kernel,
        out_shape=jax.ShapeDtypeStruct((cap, d), F32),
        grid_spec=grid_spec,
        compiler_params=_params("arbitrary"),
        name="moe_experts",
    )(block_e, n_used, xs, w_gu, b_gu[:, None, :], w_down, b_down[:, None, :])


def _combine_kernel(dest_ref, dest_next_ref, gate_ref, res_ref, g_ref, b_ref, ys_ref, o_ref, buf, sem):
    tt = res_ref.shape[0]
    i = pl.program_id(0)
    last = pl.num_programs(0) - 1
    slot = i % 2

    def row_copy(s, k, t, d):
        return pltpu.make_async_copy(ys_ref.at[pl.ds(d, 1), :], buf.at[s, k, pl.ds(t, 1), :], sem.at[s])

    def issue(d_ref, s):
        def body(t, c):
            for k in range(TOP_K):
                row_copy(s, k, t, d_ref[0, 0, t * TOP_K + k]).start(priority=k % 2)
            return c
        lax.fori_loop(0, tt, body, 0)

    @pl.when(i == 0)
    def _():
        issue(dest_ref, slot)

    @pl.when(i < last)
    def _():
        issue(dest_next_ref, 1 - slot)

    def drain(t, c):
        for k in range(TOP_K):
            row_copy(slot, k, 0, 0).wait()
        return c

    lax.fori_loop(0, tt, drain, 0)

    gates = gate_ref[...]
    ffn = gates[:, 0:1] * buf[slot, 0]
    for k in range(1, TOP_K):
        ffn = ffn + gates[:, k:k + 1] * buf[slot, k]
    o_ref[...] = _layer_norm(DN_ALPHA * res_ref[...] + ffn, g_ref[...], b_ref[...])


def _combine(dest_tiles, gates, res, g, b, ys):
    n, d = res.shape
    tt = MOVE_TILE
    n_tiles = n // tt
    vec = pl.BlockSpec((1, d), lambda i: (0, 0))
    dest_spec = lambda f: pl.BlockSpec((1, 1, tt * TOP_K), lambda i: (f(i), 0, 0), memory_space=pltpu.SMEM)
    return pl.pallas_call(
        _combine_kernel,
        out_shape=jax.ShapeDtypeStruct((n, d), F32),
        grid=(n_tiles,),
        in_specs=[dest_spec(lambda i: i),
                  dest_spec(lambda i: jnp.minimum(i + 1, n_tiles - 1)),
                  pl.BlockSpec((tt, TOP_K), lambda i: (i, 0)),
                  pl.BlockSpec((tt, d), lambda i: (i, 0)),
                  vec, vec,
                  pl.BlockSpec(memory_space=pl.ANY)],
        out_specs=pl.BlockSpec((tt, d), lambda i: (i, 0)),
        scratch_shapes=[pltpu.VMEM((2, TOP_K, tt, d), F32), pltpu.SemaphoreType.DMA((2,))],
        compiler_params=_params("arbitrary"),
        name="moe_combine",
    )(dest_tiles, dest_tiles, gates, res, g, b, ys)


def _moe_layer(h, router_w, router_b, w_gu, b_gu, w_down, b_down, ln_g, ln_b):
    n, d = h.shape
    tm = EXPERT_TILE
    idx, gates, rank, counts = _router(h, router_w, router_b[None, :])
    counts = counts[0]
    padded = (counts + tm - 1) // tm * tm
    pad_end = jnp.cumsum(padded)
    pad_start = pad_end - padded
    dest = pad_start[idx] + rank
    n_blocks = -(-(n * TOP_K + N_EXPERTS * (tm - 1)) // tm)
    block_start = jnp.arange(n_blocks, dtype=I32) * tm
    block_e = jnp.minimum(jnp.sum((pad_end[None, :] <= block_start[:, None]).astype(I32), axis=1),
                          N_EXPERTS - 1)
    n_used = (pad_end[-1:] // tm).astype(I32)
    dest_tiles = dest.reshape(n // MOVE_TILE, 1, MOVE_TILE * TOP_K)

    xs = _dispatch(dest_tiles, h, jnp.zeros((n_blocks * tm, d // 2), U32))
    ys = _experts(block_e, n_used, xs, w_gu, b_gu, w_down, b_down)
    return _combine(dest_tiles, gates, h, ln_g[None, :], ln_b[None, :], ys)


def _qkv_kernel(x_ref, w_ref, o_ref):
    xb = x_ref[...].astype(BF16)
    d = x_ref.shape[1]
    for part in range(3):
        acc = jnp.dot(xb, w_ref[:, part * d:(part + 1) * d], preferred_element_type=F32)
        if part == 0:
            acc = acc * (NA_HEAD_DIM ** -0.5)
        for p in range(HEAD_PAIRS):
            o_ref[part * HEAD_PAIRS + p] = acc[:, p * LANES:(p + 1) * LANES].astype(BF16)


def _qkv(x, w_bf16):
    n, d = x.shape
    return pl.pallas_call(
        _qkv_kernel,
        out_shape=jax.ShapeDtypeStruct((3 * HEAD_PAIRS, n, LANES), BF16),
        grid=(n // ROW_TILE,),
        in_specs=[pl.BlockSpec((ROW_TILE, d), lambda i: (i, 0)),
                  pl.BlockSpec((d, 3 * d), lambda i: (0, 0))],
        out_specs=pl.BlockSpec((3 * HEAD_PAIRS, ROW_TILE, LANES), lambda i: (0, i, 0)),
        compiler_params=_params("parallel"),
        name="na_qkv",
    )(x, w_bf16)


def _na_kernel(q_ref, k_ref, v_ref, bias_ref, mb_ref, o_ref, *, rows):
    s = pl.program_id(1)
    w = GRID_W
    low = lax.broadcasted_iota(I32, (w, LANES), 1) < NA_HEAD_DIM
    contract_last = (((1,), (1,)), ((), ()))
    meta0 = FRONT_PAD

    @pl.when(s == 0)
    def _():
        o_ref[...] = jnp.zeros_like(o_ref)

    def stacked_q(p):
        qp = q_ref[p]
        zero = jnp.zeros_like(qp)
        return jnp.concatenate([jnp.where(low, qp, zero), jnp.where(low, zero, qp)], axis=0)

    def attend_all(k0):
        scores = []
        for p in range(HEAD_PAIRS):
            q2 = stacked_q(p)
            s_meta = lax.dot_general(q2, k_ref[p, meta0:meta0 + N_META, :], contract_last,
                                     preferred_element_type=F32) + mb_ref[p]
            s_win = None
            if k0 is not None:
                s_win = lax.dot_general(q2, k_ref[p, pl.ds(k0, NA_KH * w), :], contract_last,
                                        preferred_element_type=F32) + bias_ref[0, p]
            scores.append((s_meta, s_win))
        outs = []
        for p in range(HEAD_PAIRS):
            s_meta, s_win = scores[p]
            m = jnp.max(s_meta, axis=-1, keepdims=True)
            if s_win is not None:
                m = jnp.maximum(m, jnp.max(s_win, axis=-1, keepdims=True))
            p_meta = jnp.exp(s_meta - m)
            denom = jnp.sum(p_meta, axis=-1, keepdims=True)
            o = jnp.dot(p_meta.astype(BF16), v_ref[p, meta0:meta0 + N_META, :], preferred_element_type=F32)
            if s_win is not None:
                p_win = jnp.exp(s_win - m)
                denom = denom + jnp.sum(p_win, axis=-1, keepdims=True)
                o = o + jnp.dot(p_win.astype(BF16), v_ref[p, pl.ds(k0, NA_KH * w), :],
                                preferred_element_type=F32)
            o = o / denom
            outs.append(jnp.where(low, o[:w], o[w:]))
        return outs

    @pl.when(s == 1)
    def _():
        qrow = lax.broadcasted_iota(I32, (w, LANES), 0)
        for p, o in enumerate(attend_all(None)):
            o_ref[p] = jnp.where(qrow >= w - N_META, o, 0.0).astype(o_ref.dtype)

    @pl.when(s >= 2)
    def _():
        r = s - 2
        rs = jnp.clip(r - NA_KH // 2, 0, rows - NA_KH)
        k0 = pl.multiple_of(LANES + rs * w, w)
        for p, o in enumerate(attend_all(k0)):
            o_ref[p] = o.astype(o_ref.dtype)


def _na_bias_table(rpb, rows):
    del rows
    w = GRID_W
    q = jnp.arange(w)
    col_start = jnp.clip(q - NA_KW // 2, 0, w - NA_KW)
    c = jnp.arange(w)
    in_win = (c[None, :] >= col_start[:, None]) & (c[None, :] < col_start[:, None] + NA_KW)
    pad = w - NA_KW
    rp = jnp.pad(rpb.astype(F32), ((0, 0), (0, 0), (pad, pad)))
    toeplitz = jnp.stack([rp[:, :, w - 1 - qq:2 * w - 1 - qq] for qq in range(w)], axis=2)
    toeplitz = jnp.where(in_win[None, None], toeplitz, NEG_BIG)
    tabs = []
    for v in range(NA_KH):
        tv = toeplitz[:, NA_KH - 1 - v:2 * NA_KH - 1 - v]
        tabs.append(jnp.transpose(tv, (0, 2, 1, 3)).reshape(NA_HEADS, w, NA_KH * w))
    return jnp.stack(tabs, axis=0)


def _na_attention(qkv, bias_tab, meta_bias, *, batch, lp):
    n = qkv.shape[1]
    w = GRID_W
    rows = (lp - LANES) // w
    tiles = lp // w

    def variant(b, s):
        r = jnp.clip(s - 2, 0, rows - 1)
        return r - jnp.clip(r - NA_KH // 2, 0, rows - NA_KH)

    kern = functools.partial(_na_kernel, rows=rows)
    return pl.pallas_call(
        kern,
        out_shape=jax.ShapeDtypeStruct((HEAD_PAIRS, n, LANES), BF16),
        grid=(batch, tiles),
        in_specs=[
            pl.BlockSpec((HEAD_PAIRS, w, LANES), lambda b, s: (0, b * tiles + s, 0)),
            pl.BlockSpec((HEAD_PAIRS, lp, LANES), lambda b, s: (1, b, 0)),
            pl.BlockSpec((HEAD_PAIRS, lp, LANES), lambda b, s: (2, b, 0)),
            pl.BlockSpec((1, HEAD_PAIRS, 2 * w, NA_KH * w), lambda b, s: (variant(b, s), 0, 0, 0)),
            pl.BlockSpec((HEAD_PAIRS, 2 * w, N_META), lambda b, s: (0, 0, 0)),
        ],
        out_specs=pl.BlockSpec((HEAD_PAIRS, w, LANES), lambda b, s: (0, b * tiles + s, 0)),
        compiler_params=_params("parallel", "arbitrary"),
        name="na_attention",
    )(qkv, qkv, qkv,
      bias_tab.reshape(NA_KH, HEAD_PAIRS, 2 * w, NA_KH * w),
      jnp.repeat(meta_bias, w, axis=0).reshape(HEAD_PAIRS, 2 * w, N_META))


def _na_out_kernel(o_ref, res_ref, w_ref, g_ref, b_ref, out_ref):
    att = jnp.concatenate([o_ref[p] for p in range(HEAD_PAIRS)], axis=-1)
    mix = jnp.dot(att, w_ref[...], preferred_element_type=F32)
    out_ref[...] = _layer_norm(DN_ALPHA * res_ref[...] + mix, g_ref[...], b_ref[...])


def _na_out(o, res, w_bf16, g, b):
    n, d = res.shape
    vec = pl.BlockSpec((1, d), lambda i: (0, 0))
    return pl.pallas_call(
        _na_out_kernel,
        out_shape=jax.ShapeDtypeStruct((n, d), F32),
        grid=(n // ROW_TILE,),
        in_specs=[pl.BlockSpec((HEAD_PAIRS, ROW_TILE, LANES), lambda i: (0, i, 0)),
                  pl.BlockSpec((ROW_TILE, d), lambda i: (i, 0)),
                  pl.BlockSpec((d, d), lambda i: (0, 0)), vec, vec],
        out_specs=pl.BlockSpec((ROW_TILE, d), lambda i: (i, 0)),
        compiler_params=_params("parallel"),
        name="na_out",
    )(o, res, w_bf16, g, b)


def kernel(x, meta_tokens, lru_w_in, lru_conv_w, lru_conv_b, lru_wa, lru_ba, lru_wx, lru_bx, lru_lambda, lru_w_out, na_w_qkv, na_rpb, na_meta_bias, na_w_out, ln_mix_g, ln_mix_b, router_w, router_b, moe_w_gu, moe_b_gu, moe_w_down, moe_b_down, ln_ffn_g, ln_ffn_b):
    batch, seq, d = x.shape
    lp = LANES + seq
    assert d == D_MODEL and seq % GRID_W == 0 and seq // GRID_W >= NA_KH
    assert lp % SCAN_CHUNK == 0 and (batch * lp) % ROW_TILE == 0
    n = batch * lp

    front = jnp.zeros((batch, FRONT_PAD, d), x.dtype)
    meta = jnp.broadcast_to(meta_tokens[None].astype(x.dtype), (batch, N_META, d))
    h = jnp.concatenate([front, meta, x], axis=1).reshape(n, d)

    u = _matmul(h, lru_w_in[0].astype(BF16))
    row = lambda v: v[None, :]
    scans = []
    for direction, reverse in ((0, False), (1, True)):
        scans.append(_lru_scan(
            u, lru_conv_w[0], row(lru_conv_b[0]),
            lru_wa[0, direction].astype(BF16), row(lru_ba[0, direction]),
            lru_wx[0, direction].astype(BF16), row(lru_bx[0, direction]),
            row(lru_lambda[0, direction]), reverse=reverse, lp=lp))
    h = _lru_out(scans[0], scans[1], u, h, lru_w_out[0].astype(BF16), row(ln_mix_g[0]), row(ln_mix_b[0]))
    h = _moe_layer(h, router_w[0], router_b[0], moe_w_gu[0], moe_b_gu[0], moe_w_down[0], moe_b_down[0],
                   ln_ffn_g[0], ln_ffn_b[0])

    qkv = _qkv(h, na_w_qkv[0].astype(BF16))
    att = _na_attention(qkv, _na_bias_table(na_rpb[0], seq // GRID_W), na_meta_bias[0].astype(F32),
                        batch=batch, lp=lp)
    h = _na_out(att, h, na_w_out[0].astype(BF16), row(ln_mix_g[1]), row(ln_mix_b[1]))
    h = _moe_layer(h, router_w[1], router_b[1], moe_w_gu[1], moe_b_gu[1], moe_w_down[1], moe_b_down[1],
                   ln_ffn_g[1], ln_ffn_b[1])

    return h.reshape(batch, lp, d)[:, LANES:]
```

```python
import functools

import jax
import jax.numpy as jnp
from jax import lax
from jax.experimental import pallas as pl
from jax.experimental.pallas import tpu as pltpu
from jax.experimental.pallas import tpu_sc as plsc

F32 = jnp.float32
BF16 = jnp.bfloat16
I32 = jnp.int32
U32 = jnp.uint32

D_MODEL = 1024
N_META = 16
GRID_W = 64
LRU_BLOCKS = 4
LRU_BLOCK = D_MODEL // LRU_BLOCKS
CONV_W = 4
LRU_C = 8.0
NA_HEADS = 16
NA_HEAD_DIM = D_MODEL // NA_HEADS
NA_KH = 8
NA_KW = 16
N_EXPERTS = 32
TOP_K = 4
SWIGLU_LIMIT = 7.0
SWIGLU_ALPHA = 1.702
DEPTH = 2
DN_ALPHA = (2.0 * DEPTH) ** 0.25
LN_EPS = 1e-5

LANES = 128
SUBLANES = 8
FRONT_PAD = LANES - N_META
HEAD_PAIRS = D_MODEL // LANES
NEG_BIG = -1e30

ROW_TILE = 512
SCAN_CHUNK = 384
ROUTE_TILE = 512
MOVE_TILE = 256
EXPERT_TILE = 256
VMEM_LIMIT = 56 << 20

SC_CORES = 2
SC_SUBCORES = 16
SC_WORKERS = SC_CORES * SC_SUBCORES
SC_MAX_INDICES = 64
SC_ROW_BUFFER_BYTES = 128 << 10


def _params(*sem):
    return pltpu.CompilerParams(dimension_semantics=sem, vmem_limit_bytes=VMEM_LIMIT)


def _layer_norm(x, g, b):
    mu = jnp.mean(x, axis=-1, keepdims=True)
    xc = x - mu
    var = jnp.mean(xc * xc, axis=-1, keepdims=True)
    return xc * lax.rsqrt(var + LN_EPS) * g + b


def _matmul_kernel(x_ref, w_ref, o_ref):
    o_ref[...] = jnp.dot(x_ref[...].astype(BF16), w_ref[...], preferred_element_type=F32)


def _matmul(x, w_bf16):
    n, k = x.shape
    m = w_bf16.shape[1]
    return pl.pallas_call(
        _matmul_kernel,
        out_shape=jax.ShapeDtypeStruct((n, m), F32),
        grid=(n // ROW_TILE,),
        in_specs=[pl.BlockSpec((ROW_TILE, k), lambda i: (i, 0)),
                  pl.BlockSpec((k, m), lambda i: (0, 0))],
        out_specs=pl.BlockSpec((ROW_TILE, m), lambda i: (i, 0)),
        compiler_params=_params("parallel"),
        name="in_proj",
    )(x, w_bf16)


def _lru_scan_kernel(xr_ref, xp_ref, xn_ref, cw_ref, cb_ref, wa_ref, ba_ref, wx_ref, bx_ref,
                     lam_ref, h_ref, a_sc, b_sc, carry_sc, *, reverse, chunks_per_batch, lp):
    t_rows = xr_ref.shape[0]
    step = pl.program_id(0)
    n_steps = pl.num_programs(0)
    chunk = (n_steps - 1 - step) if reverse else step
    cib = chunk % chunks_per_batch
    pos = cib * t_rows + lax.broadcasted_iota(I32, (t_rows, 1), 0)
    row = lax.broadcasted_iota(I32, (t_rows, 1), 0)
    real = pos >= FRONT_PAD

    first_chunk = cib == 0
    last_chunk = cib == chunks_per_batch - 1

    xr = jnp.where(real, xr_ref[...], 0.0)
    prev_pos = cib * t_rows - SUBLANES + lax.broadcasted_iota(I32, (SUBLANES, 1), 0)
    prev = jnp.where(prev_pos >= FRONT_PAD, xp_ref[...], 0.0)
    nxt = jnp.where(last_chunk, 0.0, xn_ref[...])

    xm1 = jnp.where(row == 0, prev[7:8, :], pltpu.roll(xr, 1, 0))
    xm2 = pltpu.roll(xr, 2, 0)
    xm2 = jnp.where(row == 0, prev[6:7, :], xm2)
    xm2 = jnp.where(row == 1, prev[7:8, :], xm2)
    xp1 = jnp.where(row == t_rows - 1, nxt[0:1, :], pltpu.roll(xr, t_rows - 1, 0))
    cw = cw_ref[...]
    xc = cw[0:1, :] * xm2 + cw[1:2, :] * xm1 + cw[2:3, :] * xr + cw[3:4, :] * xp1 + cb_ref[...]

    xcb = xc.astype(BF16)
    ga = []
    gx = []
    for blk in range(LRU_BLOCKS):
        xblk = xcb[:, blk * LRU_BLOCK:(blk + 1) * LRU_BLOCK]
        ga.append(jnp.dot(xblk, wa_ref[blk], preferred_element_type=F32))
        gx.append(jnp.dot(xblk, wx_ref[blk], preferred_element_type=F32))
    gate_a = jax.nn.sigmoid(jnp.concatenate(ga, axis=-1) + ba_ref[...])
    gate_x = jax.nn.sigmoid(jnp.concatenate(gx, axis=-1) + bx_ref[...])

    z = -lam_ref[...]
    softplus = jnp.maximum(z, 0.0) + jnp.log1p(jnp.exp(-jnp.abs(z)))
    log_a = (-LRU_C * gate_a) * softplus
    a = jnp.exp(log_a)
    mult = jnp.sqrt(-jnp.tanh(log_a) * (a * a + 1.0))
    start = lp - 1 if reverse else FRONT_PAD
    mult = jnp.where(pos == start, 1.0, mult)
    b = jnp.where(real, mult * (gate_x * xc), 0.0)
    a_sc[...] = a
    b_sc[...] = b

    @pl.when(last_chunk if reverse else first_chunk)
    def _():
        carry_sc[...] = jnp.zeros_like(carry_sc)

    row8 = lax.broadcasted_iota(I32, (SUBLANES, 1), 0)
    n_groups = t_rows // SUBLANES

    def group(i, carry):
        g = (n_groups - 1 - i) if reverse else i
        r0 = pl.multiple_of(g * SUBLANES, SUBLANES)
        av = a_sc[pl.ds(r0, SUBLANES), :]
        bv = b_sc[pl.ds(r0, SUBLANES), :]
        for s in (1, 2, 4):
            shift = SUBLANES - s if reverse else s
            keep = (row8 < SUBLANES - s) if reverse else (row8 >= s)
            a_sh = pltpu.roll(av, shift, 0)
            b_sh = pltpu.roll(bv, shift, 0)
            bv = jnp.where(keep, av * b_sh + bv, bv)
            av = jnp.where(keep, av * a_sh, av)
        hv = av * carry + bv
        h_ref[pl.ds(r0, SUBLANES), :] = hv
        return hv[0:1, :] if reverse else hv[SUBLANES - 1:SUBLANES, :]

    carry_sc[...] = lax.fori_loop(0, n_groups, group, carry_sc[...])


def _lru_scan(u, cw, cb, wa_bf16, ba, wx_bf16, bx, lam, *, reverse, lp):
    n = u.shape[0]
    d = D_MODEL
    t = SCAN_CHUNK
    n_chunks = n // t
    cpb = lp // t
    t8 = t // SUBLANES
    n8 = n // SUBLANES

    def chunk_of(i):
        return (n_chunks - 1 - i) if reverse else i

    kern = functools.partial(_lru_scan_kernel, reverse=reverse, chunks_per_batch=cpb, lp=lp)
    full2 = lambda shape: pl.BlockSpec(shape, lambda i: (0, 0))
    full3 = lambda shape: pl.BlockSpec(shape, lambda i: (0, 0, 0))
    return pl.pallas_call(
        kern,
        out_shape=jax.ShapeDtypeStruct((n, d), F32),
        grid=(n_chunks,),
        in_specs=[
            pl.BlockSpec((t, d), lambda i: (chunk_of(i), 0)),
            pl.BlockSpec((SUBLANES, d), lambda i: (jnp.maximum(chunk_of(i) * t8 - 1, 0), 0)),
            pl.BlockSpec((SUBLANES, d), lambda i: (jnp.minimum((chunk_of(i) + 1) * t8, n8 - 1), 0)),
            full2((CONV_W, d)), full2((1, d)),
            full3((LRU_BLOCKS, LRU_BLOCK, LRU_BLOCK)), full2((1, d)),
            full3((LRU_BLOCKS, LRU_BLOCK, LRU_BLOCK)), full2((1, d)),
            full2((1, d)),
        ],
        out_specs=pl.BlockSpec((t, d), lambda i: (chunk_of(i), 0)),
        scratch_shapes=[pltpu.VMEM((t, d), F32), pltpu.VMEM((t, d), F32), pltpu.VMEM((1, d), F32)],
        compiler_params=_params("arbitrary"),
        name="lru_scan_bwd" if reverse else "lru_scan_fwd",
    )(u, u, u, cw, cb, wa_bf16, ba, wx_bf16, bx, lam)


def _gelu_tanh(y):
    c = 0.7978845608028654
    return y * (0.5 * (1.0 + jnp.tanh(c * (y + 0.044715 * (y * y * y)))))


def _lru_out_kernel(hf_ref, hb_ref, y_ref, res_ref, w_ref, g_ref, b_ref, o_ref):
    gated = (hf_ref[...] + hb_ref[...]) * _gelu_tanh(y_ref[...])
    mix = jnp.dot(gated.astype(BF16), w_ref[...], preferred_element_type=F32)
    o_ref[...] = _layer_norm(DN_ALPHA * res_ref[...] + mix, g_ref[...], b_ref[...])


def _lru_out(hf, hb, u, res, w_bf16, g, b):
    n, d = res.shape
    row = lambda j: pl.BlockSpec((ROW_TILE, d), lambda i, j=j: (i, j))
    vec = pl.BlockSpec((1, d), lambda i: (0, 0))
    return pl.pallas_call(
        _lru_out_kernel,
        out_shape=jax.ShapeDtypeStruct((n, d), F32),
        grid=(n // ROW_TILE,),
        in_specs=[row(0), row(0), row(1), row(0), pl.BlockSpec((d, d), lambda i: (0, 0)), vec, vec],
        out_specs=row(0),
        compiler_params=_params("parallel"),
        name="lru_out",
    )(hf, hb, u, res, w_bf16, g, b)


def _pack_bf16_pairs(x):
    half = x.shape[1] // 2
    lo = pltpu.bitcast(x[:, :half].astype(BF16).astype(F32), U32)
    hi = pltpu.bitcast(x[:, half:].astype(BF16).astype(F32), U32)
    return pltpu.bitcast((lo >> 16) | (hi & jnp.uint32(0xFFFF0000)), I32)


def _unpack_bf16_pairs(packed):
    u = pltpu.bitcast(packed, U32)
    lo = pltpu.bitcast(u << 16, F32).astype(BF16)
    hi = pltpu.bitcast(u & jnp.uint32(0xFFFF0000), F32).astype(BF16)
    return jnp.concatenate([lo, hi], axis=-1)


def _router_kernel(h_ref, w_ref, b_ref, idx_ref, gate_ref, rank_ref, cnt_ref, hp_ref, base_sc):
    tm = h_ref.shape[0]
    hp_ref[...] = _pack_bf16_pairs(h_ref[...])

    @pl.when(pl.program_id(0) == 0)
    def _():
        base_sc[...] = jnp.zeros_like(base_sc)

    logits = jnp.dot(h_ref[...], w_ref[...], preferred_element_type=F32,
                     precision=lax.Precision.HIGHEST) + b_ref[...]
    lane = lax.broadcasted_iota(I32, (tm, N_EXPERTS), 1).astype(F32)
    lane4 = lax.broadcasted_iota(I32, (tm, TOP_K), 1)
    r_i = lax.broadcasted_iota(I32, (tm, tm), 0)
    c_i = lax.broadcasted_iota(I32, (tm, tm), 1)
    lower = jnp.where(r_i > c_i, 1.0, 0.0).astype(BF16)

    vals = logits
    base = base_sc[...]
    idx_out = jnp.zeros((tm, TOP_K), I32)
    val_out = jnp.zeros((tm, TOP_K), F32)
    rank_out = jnp.zeros((tm, TOP_K), I32)
    for k in range(TOP_K):
        m = jnp.max(vals, axis=-1, keepdims=True)
        idx = jnp.min(jnp.where(vals == m, lane, float(N_EXPERTS)), axis=-1, keepdims=True)
        hit = lane == idx
        onehot = jnp.where(hit, 1.0, 0.0)
        before = jnp.dot(lower, onehot.astype(BF16), preferred_element_type=F32)
        rank = jnp.sum(onehot * (base + before), axis=-1, keepdims=True)
        base = base + jnp.sum(onehot, axis=0, keepdims=True)
        idx_out = jnp.where(lane4 == k, idx.astype(I32), idx_out)
        val_out = jnp.where(lane4 == k, m, val_out)
        rank_out = jnp.where(lane4 == k, rank.astype(I32), rank_out)
        vals = jnp.where(hit, -jnp.inf, vals)

    e = jnp.exp(val_out - val_out[:, 0:1])
    idx_ref[...] = idx_out
    gate_ref[...] = e / jnp.sum(e, axis=-1, keepdims=True)
    rank_ref[...] = rank_out
    base_sc[...] = base
    cnt_ref[...] = base.astype(I32)


def _router(h, w, b):
    n, d = h.shape
    tm = ROUTE_TILE
    out4 = lambda dt: jax.ShapeDtypeStruct((n, TOP_K), dt)
    spec4 = pl.BlockSpec((tm, TOP_K), lambda i: (i, 0))
    return pl.pallas_call(
        _router_kernel,
        out_shape=(out4(I32), out4(F32), out4(I32), jax.ShapeDtypeStruct((1, N_EXPERTS), I32),
                   jax.ShapeDtypeStruct((n, d // 2), I32)),
        grid=(n // tm,),
        in_specs=[pl.BlockSpec((tm, d), lambda i: (i, 0)),
                  pl.BlockSpec((d, N_EXPERTS), lambda i: (0, 0)),
                  pl.BlockSpec((1, N_EXPERTS), lambda i: (0, 0))],
        out_specs=(spec4, spec4, spec4, pl.BlockSpec((1, N_EXPERTS), lambda i: (0, 0)),
                   pl.BlockSpec((tm, d // 2), lambda i: (i, 0))),
        scratch_shapes=[pltpu.VMEM((1, N_EXPERTS), F32)],
        compiler_params=_params("arbitrary"),
        name="router",
    )(h, w, b)


def _sc_gather_rows(table, idx, name):
    rows, width = table.shape
    total = idx.shape[0]
    chunk = min(SC_MAX_INDICES, SC_ROW_BUFFER_BYTES // (width * table.dtype.itemsize))
    per_worker = total // SC_WORKERS
    n_chunks = per_worker // chunk
    assert total == SC_WORKERS * n_chunks * chunk and n_chunks % 2 == 0
    mesh = plsc.VectorSubcoreMesh(core_axis_name="c", subcore_axis_name="s",
                                  num_cores=SC_CORES, num_subcores=SC_SUBCORES)

    def body(table_hbm, idx_hbm, out_hbm, idx_v, rows0, rows1, gsem0, gsem1, psem0, psem1):
        worker = lax.axis_index("s") * SC_CORES + lax.axis_index("c")
        base = worker * per_worker
        pltpu.sync_copy(idx_hbm.at[worker], idx_v)
        bufs = (rows0, rows1)
        gsems = (gsem0, gsem1)
        psems = (psem0, psem1)

        def gather(c, slot):
            return pltpu.make_async_copy(table_hbm.at[idx_v.at[c]], bufs[slot], gsems[slot])

        def put(c, slot):
            return pltpu.make_async_copy(bufs[slot], out_hbm.at[pl.ds(base + c * chunk, chunk)], psems[slot])

        gather(0, 0).start()

        @pl.loop(0, n_chunks, step=2)
        def _(c0):
            for slot in range(2):
                c = c0 + slot

                @pl.when(c + 1 < n_chunks)
                def _():
                    @pl.when(c >= 1)
                    def _():
                        put(c - 1, 1 - slot).wait()
                    gather(c + 1, 1 - slot).start()

                gather(c, slot).wait()
                put(c, slot).start()

        put(n_chunks - 2, 0).wait()
        put(n_chunks - 1, 1).wait()

    return pl.kernel(
        body,
        out_type=jax.ShapeDtypeStruct((total, width), table.dtype),
        mesh=mesh,
        scratch_types=[pltpu.VMEM((n_chunks, chunk), I32),
                       pltpu.VMEM((chunk, width), table.dtype), pltpu.VMEM((chunk, width), table.dtype),
                       pltpu.SemaphoreType.DMA, pltpu.SemaphoreType.DMA,
                       pltpu.SemaphoreType.DMA, pltpu.SemaphoreType.DMA],
        name=name,
    )(table, idx.reshape(SC_WORKERS, n_chunks, chunk))


def _expert_kernel(be_ref, nu_ref, xs_ref, wgu_ref, bgu_ref, wd_ref, bd_ref, y_ref, wgu_sc, wd_sc):
    i = pl.program_id(0)
    e = be_ref[i]
    e_prev = be_ref[jnp.maximum(i - 1, 0)]
    d = wd_sc.shape[0]

    @pl.when((i == 0) | (e != e_prev))
    def _():
        for r in range(0, d, LANES):
            wgu_sc[r:r + LANES, :] = wgu_ref[0, r:r + LANES, :].astype(BF16)
            wd_sc[r:r + LANES, :] = wd_ref[0, r:r + LANES, :].astype(BF16)

    @pl.when(i < nu_ref[0])
    def _():
        x = _unpack_bf16_pairs(xs_ref[...])
        h = jnp.dot(x, wgu_sc[...], preferred_element_type=F32) + bgu_ref[0]
        glu = jnp.minimum(h[:, :d], SWIGLU_LIMIT)
        lin = jnp.clip(h[:, d:], -SWIGLU_LIMIT, SWIGLU_LIMIT)
        act = glu * jax.nn.sigmoid(SWIGLU_ALPHA * glu) * (lin + 1.0)
        y_ref[...] = jnp.dot(act.astype(BF16), wd_sc[...], preferred_element_type=F32) + bd_ref[0]

    @pl.when(i >= nu_ref[0])
    def _():
        y_ref[...] = jnp.zeros_like(y_ref)


def _experts(block_e, n_used, xs, w_gu, b_gu, w_down, b_down):
    cap = xs.shape[0]
    d = D_MODEL
    tm = EXPERT_TILE
    n_blocks = cap // tm
    grid_spec = pltpu.PrefetchScalarGridSpec(
        num_scalar_prefetch=2,
        grid=(n_blocks,),
        in_specs=[
            pl.BlockSpec((tm, d // 2), lambda i, be, nu: (i, 0)),
            pl.BlockSpec((1, d, 2 * d), lambda i, be, nu: (be[i], 0, 0)),
            pl.BlockSpec((1, 1, 2 * d), lambda i, be, nu: (be[i], 0, 0)),
            pl.BlockSpec((1, d, d), lambda i, be, nu: (be[i], 0, 0)),
            pl.BlockSpec((1, 1, d), lambda i, be, nu: (be[i], 0, 0)),
        ],
        out_specs=pl.BlockSpec((tm, d), lambda i, be, nu: (i, 0)),
        scratch_shapes=[pltpu.VMEM((d, 2 * d), BF16), pltpu.VMEM((d, d), BF16)],
    )
    return pl.pallas_call(
        _expert_kernel,
        out_shape=jax.ShapeDtypeStruct((cap, d), F32),
        grid_spec=grid_spec,
        compiler_params=_params("arbitrary"),
        name="moe_experts",
    )(block_e, n_used, xs, w_gu, b_gu[:, None, :], w_down, b_down[:, None, :])


def _combine_kernel(gate_ref, res_ref, yk_ref, g_ref, b_ref, o_ref):
    d = res_ref.shape[1]
    gates = gate_ref[...]
    ffn = gates[:, 0:1] * yk_ref[:, 0:d]
    for k in range(1, TOP_K):
        ffn = ffn + gates[:, k:k + 1] * yk_ref[:, k * d:(k + 1) * d]
    o_ref[...] = _layer_norm(DN_ALPHA * res_ref[...] + ffn, g_ref[...], b_ref[...])


def _combine(gates, res, yk, g, b):
    n, d = res.shape
    tt = MOVE_TILE
    vec = pl.BlockSpec((1, d), lambda i: (0, 0))
    return pl.pallas_call(
        _combine_kernel,
        out_shape=jax.ShapeDtypeStruct((n, d), F32),
        grid=(n // tt,),
        in_specs=[pl.BlockSpec((tt, TOP_K), lambda i: (i, 0)),
                  pl.BlockSpec((tt, d), lambda i: (i, 0)),
                  pl.BlockSpec((tt, TOP_K * d), lambda i: (i, 0)),
                  vec, vec],
        out_specs=pl.BlockSpec((tt, d), lambda i: (i, 0)),
        compiler_params=_params("parallel"),
        name="moe_combine",
    )(gates, res, yk, g, b)


def _moe_layer(h, router_w, router_b, w_gu, b_gu, w_down, b_down, ln_g, ln_b):
    n, d = h.shape
    tm = EXPERT_TILE
    idx, gates, rank, counts, h_packed = _router(h, router_w, router_b[None, :])
    counts = counts[0]
    padded = (counts + tm - 1) // tm * tm
    pad_end = jnp.cumsum(padded)
    pad_start = pad_end - padded
    dest = (pad_start[idx] + rank).reshape(n * TOP_K)
    align = SC_WORKERS * 2 * SC_MAX_INDICES // tm
    n_blocks = -(-(n * TOP_K + N_EXPERTS * (tm - 1)) // (tm * align)) * align
    block_start = jnp.arange(n_blocks, dtype=I32) * tm
    block_e = jnp.minimum(jnp.sum((pad_end[None, :] <= block_start[:, None]).astype(I32), axis=1),
                          N_EXPERTS - 1)
    n_used = (pad_end[-1:] // tm).astype(I32)
    src_tok = jnp.zeros((n_blocks * tm,), I32).at[dest].set(
        jnp.arange(n * TOP_K, dtype=I32) // TOP_K, unique_indices=True)

    xs = _sc_gather_rows(h_packed, src_tok, "moe_dispatch")
    ys = _experts(block_e, n_used, xs, w_gu, b_gu, w_down, b_down)
    yk = _sc_gather_rows(ys, dest, "moe_collect").reshape(n, TOP_K * d)
    return _combine(gates, h, yk, ln_g[None, :], ln_b[None, :])


def _qkv_kernel(x_ref, w_ref, o_ref):
    xb = x_ref[...].astype(BF16)
    d = x_ref.shape[1]
    for part in range(3):
        acc = jnp.dot(xb, w_ref[:, part * d:(part + 1) * d], preferred_element_type=F32)
        if part == 0:
            acc = acc * (NA_HEAD_DIM ** -0.5)
        for p in range(HEAD_PAIRS):
            o_ref[part * HEAD_PAIRS + p] = acc[:, p * LANES:(p + 1) * LANES].astype(BF16)


def _qkv(x, w_bf16):
    n, d = x.shape
    return pl.pallas_call(
        _qkv_kernel,
        out_shape=jax.ShapeDtypeStruct((3 * HEAD_PAIRS, n, LANES), BF16),
        grid=(n // ROW_TILE,),
        in_specs=[pl.BlockSpec((ROW_TILE, d), lambda i: (i, 0)),
                  pl.BlockSpec((d, 3 * d), lambda i: (0, 0))],
        out_specs=pl.BlockSpec((3 * HEAD_PAIRS, ROW_TILE, LANES), lambda i: (0, i, 0)),
        compiler_params=_params("parallel"),
        name="na_qkv",
    )(x, w_bf16)


def _na_kernel(q_ref, k_ref, v_ref, bias_ref, mb_ref, o_ref, *, rows):
    s = pl.program_id(1)
    w = GRID_W
    low = lax.broadcasted_iota(I32, (w, LANES), 1) < NA_HEAD_DIM
    contract_last = (((1,), (1,)), ((), ()))
    meta0 = FRONT_PAD

    @pl.when(s == 0)
    def _():
        o_ref[...] = jnp.zeros_like(o_ref)

    def stacked_q(p):
        qp = q_ref[p]
        zero = jnp.zeros_like(qp)
        return jnp.concatenate([jnp.where(low, qp, zero), jnp.where(low, zero, qp)], axis=0)

    def attend_all(k0):
        scores = []
        for p in range(HEAD_PAIRS):
            q2 = stacked_q(p)
            s_meta = lax.dot_general(q2, k_ref[p, meta0:meta0 + N_META, :], contract_last,
                                     preferred_element_type=F32) + mb_ref[p]
            s_win = None
            if k0 is not None:
                s_win = lax.dot_general(q2, k_ref[p, pl.ds(k0, NA_KH * w), :], contract_last,
                                        preferred_element_type=F32) + bias_ref[0, p]
            scores.append((s_meta, s_win))
        outs = []
        for p in range(HEAD_PAIRS):
            s_meta, s_win = scores[p]
            m = jnp.max(s_meta, axis=-1, keepdims=True)
            if s_win is not None:
                m = jnp.maximum(m, jnp.max(s_win, axis=-1, keepdims=True))
            p_meta = jnp.exp(s_meta - m)
            denom = jnp.sum(p_meta, axis=-1, keepdims=True)
            o = jnp.dot(p_meta.astype(BF16), v_ref[p, meta0:meta0 + N_META, :], preferred_element_type=F32)
            if s_win is not None:
                p_win = jnp.exp(s_win - m)
                denom = denom + jnp.sum(p_win, axis=-1, keepdims=True)
                o = o + jnp.dot(p_win.astype(BF16), v_ref[p, pl.ds(k0, NA_KH * w), :],
                                preferred_element_type=F32)
            o = o / denom
            outs.append(jnp.where(low, o[:w], o[w:]))
        return outs

    @pl.when(s == 1)
    def _():
        qrow = lax.broadcasted_iota(I32, (w, LANES), 0)
        for p, o in enumerate(attend_all(None)):
            o_ref[p] = jnp.where(qrow >= w - N_META, o, 0.0).astype(o_ref.dtype)

    @pl.when(s >= 2)
    def _():
        r = s - 2
        rs = jnp.clip(r - NA_KH // 2, 0, rows - NA_KH)
        k0 = pl.multiple_of(LANES + rs * w, w)
        for p, o in enumerate(attend_all(k0)):
            o_ref[p] = o.astype(o_ref.dtype)


def _na_bias_table(rpb, rows):
    del rows
    w = GRID_W
    q = jnp.arange(w)
    col_start = jnp.clip(q - NA_KW // 2, 0, w - NA_KW)
    c = jnp.arange(w)
    in_win = (c[None, :] >= col_start[:, None]) & (c[None, :] < col_start[:, None] + NA_KW)
    pad = w - NA_KW
    rp = jnp.pad(rpb.astype(F32), ((0, 0), (0, 0), (pad, pad)))
    toeplitz = jnp.stack([rp[:, :, w - 1 - qq:2 * w - 1 - qq] for qq in range(w)], axis=2)
    toeplitz = jnp.where(in_win[None, None], toeplitz, NEG_BIG)
    tabs = []
    for v in range(NA_KH):
        tv = toeplitz[:, NA_KH - 1 - v:2 * NA_KH - 1 - v]
        tabs.append(jnp.transpose(tv, (0, 2, 1, 3)).reshape(NA_HEADS, w, NA_KH * w))
    return jnp.stack(tabs, axis=0)


def _na_attention(qkv, bias_tab, meta_bias, *, batch, lp):
    n = qkv.shape[1]
    w = GRID_W
    rows = (lp - LANES) // w
    tiles = lp // w

    def variant(b, s):
        r = jnp.clip(s - 2, 0, rows - 1)
        return r - jnp.clip(r - NA_KH // 2, 0, rows - NA_KH)

    kern = functools.partial(_na_kernel, rows=rows)
    return pl.pallas_call(
        kern,
        out_shape=jax.ShapeDtypeStruct((HEAD_PAIRS, n, LANES), BF16),
        grid=(batch, tiles),
        in_specs=[
            pl.BlockSpec((HEAD_PAIRS, w, LANES), lambda b, s: (0, b * tiles + s, 0)),
            pl.BlockSpec((HEAD_PAIRS, lp, LANES), lambda b, s: (1, b, 0)),
            pl.BlockSpec((HEAD_PAIRS, lp, LANES), lambda b, s: (2, b, 0)),
            pl.BlockSpec((1, HEAD_PAIRS, 2 * w, NA_KH * w), lambda b, s: (variant(b, s), 0, 0, 0)),
            pl.BlockSpec((HEAD_PAIRS, 2 * w, N_META), lambda b, s: (0, 0, 0)),
        ],
        out_specs=pl.BlockSpec((HEAD_PAIRS, w, LANES), lambda b, s: (0, b * tiles + s, 0)),
        compiler_params=_params("parallel", "arbitrary"),
        name="na_attention",
    )(qkv, qkv, qkv,
      bias_tab.reshape(NA_KH, HEAD_PAIRS, 2 * w, NA_KH * w),
      jnp.repeat(meta_bias, w, axis=0).reshape(HEAD_PAIRS, 2 * w, N_META))


def _na_out_kernel(o_ref, res_ref, w_ref, g_ref, b_ref, out_ref):
    att = jnp.concatenate([o_ref[p] for p in range(HEAD_PAIRS)], axis=-1)
    mix = jnp.dot(att, w_ref[...], preferred_element_type=F32)
    out_ref[...] = _layer_norm(DN_ALPHA * res_ref[...] + mix, g_ref[...], b_ref[...])


def _na_out(o, res, w_bf16, g, b):
    n, d = res.shape
    vec = pl.BlockSpec((1, d), lambda i: (0, 0))
    return pl.pallas_call(
        _na_out_kernel,
        out_shape=jax.ShapeDtypeStruct((n, d), F32),
        grid=(n // ROW_TILE,),
        in_specs=[pl.BlockSpec((HEAD_PAIRS, ROW_TILE, LANES), lambda i: (0, i, 0)),
                  pl.BlockSpec((ROW_TILE, d), lambda i: (i, 0)),
                  pl.BlockSpec((d, d), lambda i: (0, 0)), vec, vec],
        out_specs=pl.BlockSpec((ROW_TILE, d), lambda i: (i, 0)),
        compiler_params=_params("parallel"),
        name="na_out",
    )(o, res, w_bf16, g, b)


def kernel(x, meta_tokens, lru_w_in, lru_conv_w, lru_conv_b, lru_wa, lru_ba, lru_wx, lru_bx, lru_lambda, lru_w_out, na_w_qkv, na_rpb, na_meta_bias, na_w_out, ln_mix_g, ln_mix_b, router_w, router_b, moe_w_gu, moe_b_gu, moe_w_down, moe_b_down, ln_ffn_g, ln_ffn_b):
    batch, seq, d = x.shape
    lp = LANES + seq
    assert d == D_MODEL and seq % GRID_W == 0 and seq // GRID_W >= NA_KH
    assert lp % SCAN_CHUNK == 0 and (batch * lp) % ROW_TILE == 0
    n = batch * lp

    front = jnp.zeros((batch, FRONT_PAD, d), x.dtype)
    meta = jnp.broadcast_to(meta_tokens[None].astype(x.dtype), (batch, N_META, d))
    h = jnp.concatenate([front, meta, x], axis=1).reshape(n, d)

    u = _matmul(h, lru_w_in[0].astype(BF16))
    row = lambda v: v[None, :]
    scans = []
    for direction, reverse in ((0, False), (1, True)):
        scans.append(_lru_scan(
            u, lru_conv_w[0], row(lru_conv_b[0]),
            lru_wa[0, direction].astype(BF16), row(lru_ba[0, direction]),
            lru_wx[0, direction].astype(BF16), row(lru_bx[0, direction]),
            row(lru_lambda[0, direction]), reverse=reverse, lp=lp))
    h = _lru_out(scans[0], scans[1], u, h, lru_w_out[0].astype(BF16), row(ln_mix_g[0]), row(ln_mix_b[0]))
    h = _moe_layer(h, router_w[0], router_b[0], moe_w_gu[0], moe_b_gu[0], moe_w_down[0], moe_b_down[0],
                   ln_ffn_g[0], ln_ffn_b[0])

    qkv = _qkv(h, na_w_qkv[0].astype(BF16))
    att = _na_attention(qkv, _na_bias_table(na_rpb[0], seq // GRID_W), na_meta_bias[0].astype(F32),
                        batch=batch, lp=lp)
    h = _na_out(att, h, na_w_out[0].astype(BF16), row(ln_mix_g[1]), row(ln_mix_b[1]))
    h = _moe_layer(h, router_w[1], router_b[1], moe_w_gu[1], moe_b_gu[1], moe_w_down[1], moe_b_down[1],
                   ln_ffn_g[1], ln_ffn_b[1])

    return h.reshape(batch, lp, d)[:, LANES:]
```

```python
import functools

import jax
import jax.numpy as jnp
from jax import lax
from jax.experimental import pallas as pl
from jax.experimental.pallas import tpu as pltpu
from jax.experimental.pallas import tpu_sc as plsc

F32 = jnp.float32
BF16 = jnp.bfloat16
I32 = jnp.int32
U32 = jnp.uint32

D_MODEL = 1024
N_META = 16
GRID_W = 64
LRU_BLOCKS = 4
LRU_BLOCK = D_MODEL // LRU_BLOCKS
CONV_W = 4
LRU_C = 8.0
NA_HEADS = 16
NA_HEAD_DIM = D_MODEL // NA_HEADS
NA_KH = 8
NA_KW = 16
N_EXPERTS = 32
TOP_K = 4
SWIGLU_LIMIT = 7.0
SWIGLU_ALPHA = 1.702
DEPTH = 2
DN_ALPHA = (2.0 * DEPTH) ** 0.25
LN_EPS = 1e-5

LANES = 128
SUBLANES = 8
FRONT_PAD = LANES - N_META
HEAD_PAIRS = D_MODEL // LANES
NEG_BIG = -1e30

ROW_TILE = 512
SCAN_CHUNK = 384
ROUTE_TILE = 512
MOVE_TILE = 256
EXPERT_TILE = 256
VMEM_LIMIT = 56 << 20

SC_CORES = 2
SC_SUBCORES = 16
SC_WORKERS = SC_CORES * SC_SUBCORES
SC_MAX_INDICES = 64
SC_ROW_BUFFER_BYTES = 128 << 10


def _params(*sem):
    return pltpu.CompilerParams(dimension_semantics=sem, vmem_limit_bytes=VMEM_LIMIT)


def _layer_norm(x, g, b):
    mu = jnp.mean(x, axis=-1, keepdims=True)
    xc = x - mu
    var = jnp.mean(xc * xc, axis=-1, keepdims=True)
    return xc * lax.rsqrt(var + LN_EPS) * g + b


def _matmul_kernel(x_ref, w_ref, o_ref):
    o_ref[...] = jnp.dot(x_ref[...].astype(BF16), w_ref[...], preferred_element_type=F32)


def _matmul(x, w_bf16):
    n, k = x.shape
    m = w_bf16.shape[1]
    return pl.pallas_call(
        _matmul_kernel,
        out_shape=jax.ShapeDtypeStruct((n, m), F32),
        grid=(n // ROW_TILE,),
        in_specs=[pl.BlockSpec((ROW_TILE, k), lambda i: (i, 0)),
                  pl.BlockSpec((k, m), lambda i: (0, 0))],
        out_specs=pl.BlockSpec((ROW_TILE, m), lambda i: (i, 0)),
        compiler_params=_params("parallel"),
        name="in_proj",
    )(x, w_bf16)


def _lru_scan_kernel(xr_ref, xp_ref, xn_ref, cw_ref, cb_ref, wa_ref, ba_ref, wx_ref, bx_ref,
                     lam_ref, h_ref, a_sc, b_sc, carry_sc, *, reverse, chunks_per_batch, lp):
    t_rows = xr_ref.shape[0]
    step = pl.program_id(0)
    n_steps = pl.num_programs(0)
    chunk = (n_steps - 1 - step) if reverse else step
    cib = chunk % chunks_per_batch
    pos = cib * t_rows + lax.broadcasted_iota(I32, (t_rows, 1), 0)
    row = lax.broadcasted_iota(I32, (t_rows, 1), 0)
    real = pos >= FRONT_PAD

    first_chunk = cib == 0
    last_chunk = cib == chunks_per_batch - 1

    xr = jnp.where(real, xr_ref[...], 0.0)
    prev_pos = cib * t_rows - SUBLANES + lax.broadcasted_iota(I32, (SUBLANES, 1), 0)
    prev = jnp.where(prev_pos >= FRONT_PAD, xp_ref[...], 0.0)
    nxt = jnp.where(last_chunk, 0.0, xn_ref[...])

    xm1 = jnp.where(row == 0, prev[7:8, :], pltpu.roll(xr, 1, 0))
    xm2 = pltpu.roll(xr, 2, 0)
    xm2 = jnp.where(row == 0, prev[6:7, :], xm2)
    xm2 = jnp.where(row == 1, prev[7:8, :], xm2)
    xp1 = jnp.where(row == t_rows - 1, nxt[0:1, :], pltpu.roll(xr, t_rows - 1, 0))
    cw = cw_ref[...]
    xc = cw[0:1, :] * xm2 + cw[1:2, :] * xm1 + cw[2:3, :] * xr + cw[3:4, :] * xp1 + cb_ref[...]

    xcb = xc.astype(BF16)
    ga = []
    gx = []
    for blk in range(LRU_BLOCKS):
        xblk = xcb[:, blk * LRU_BLOCK:(blk + 1) * LRU_BLOCK]
        ga.append(jnp.dot(xblk, wa_ref[blk], preferred_element_type=F32))
        gx.append(jnp.dot(xblk, wx_ref[blk], preferred_element_type=F32))
    gate_a = jax.nn.sigmoid(jnp.concatenate(ga, axis=-1) + ba_ref[...])
    gate_x = jax.nn.sigmoid(jnp.concatenate(gx, axis=-1) + bx_ref[...])

    z = -lam_ref[...]
    softplus = jnp.maximum(z, 0.0) + jnp.log1p(jnp.exp(-jnp.abs(z)))
    log_a = (-LRU_C * gate_a) * softplus
    a = jnp.exp(log_a)
    mult = jnp.sqrt(-jnp.tanh(log_a) * (a * a + 1.0))
    start = lp - 1 if reverse else FRONT_PAD
    mult = jnp.where(pos == start, 1.0, mult)
    b = jnp.where(real, mult * (gate_x * xc), 0.0)
    a_sc[...] = a
    b_sc[...] = b

    @pl.when(last_chunk if reverse else first_chunk)
    def _():
        carry_sc[...] = jnp.zeros_like(carry_sc)

    row8 = lax.broadcasted_iota(I32, (SUBLANES, 1), 0)
    n_groups = t_rows // SUBLANES

    def group(i, carry):
        g = (n_groups - 1 - i) if reverse else i
        r0 = pl.multiple_of(g * SUBLANES, SUBLANES)
        av = a_sc[pl.ds(r0, SUBLANES), :]
        bv = b_sc[pl.ds(r0, SUBLANES), :]
        for s in (1, 2, 4):
            shift = SUBLANES - s if reverse else s
            keep = (row8 < SUBLANES - s) if reverse else (row8 >= s)
            a_sh = pltpu.roll(av, shift, 0)
            b_sh = pltpu.roll(bv, shift, 0)
            bv = jnp.where(keep, av * b_sh + bv, bv)
            av = jnp.where(keep, av * a_sh, av)
        hv = av * carry + bv
        h_ref[pl.ds(r0, SUBLANES), :] = hv
        return hv[0:1, :] if reverse else hv[SUBLANES - 1:SUBLANES, :]

    carry_sc[...] = lax.fori_loop(0, n_groups, group, carry_sc[...])


def _lru_scan(u, cw, cb, wa_bf16, ba, wx_bf16, bx, lam, *, reverse, lp):
    n = u.shape[0]
    d = D_MODEL
    t = SCAN_CHUNK
    n_chunks = n // t
    cpb = lp // t
    t8 = t // SUBLANES
    n8 = n // SUBLANES

    def chunk_of(i):
        return (n_chunks - 1 - i) if reverse else i

    kern = functools.partial(_lru_scan_kernel, reverse=reverse, chunks_per_batch=cpb, lp=lp)
    full2 = lambda shape: pl.BlockSpec(shape, lambda i: (0, 0))
    full3 = lambda shape: pl.BlockSpec(shape, lambda i: (0, 0, 0))
    return pl.pallas_call(
        kern,
        out_shape=jax.ShapeDtypeStruct((n, d), F32),
        grid=(n_chunks,),
        in_specs=[
            pl.BlockSpec((t, d), lambda i: (chunk_of(i), 0)),
            pl.BlockSpec((SUBLANES, d), lambda i: (jnp.maximum(chunk_of(i) * t8 - 1, 0), 0)),
            pl.BlockSpec((SUBLANES, d), lambda i: (jnp.minimum((chunk_of(i) + 1) * t8, n8 - 1), 0)),
            full2((CONV_W, d)), full2((1, d)),
            full3((LRU_BLOCKS, LRU_BLOCK, LRU_BLOCK)), full2((1, d)),
            full3((LRU_BLOCKS, LRU_BLOCK, LRU_BLOCK)), full2((1, d)),
            full2((1, d)),
        ],
        out_specs=pl.BlockSpec((t, d), lambda i: (chunk_of(i), 0)),
        scratch_shapes=[pltpu.VMEM((t, d), F32), pltpu.VMEM((t, d), F32), pltpu.VMEM((1, d), F32)],
        compiler_params=_params("arbitrary"),
        name="lru_scan_bwd" if reverse else "lru_scan_fwd",
    )(u, u, u, cw, cb, wa_bf16, ba, wx_bf16, bx, lam)


def _gelu_tanh(y):
    c = 0.7978845608028654
    return y * (0.5 * (1.0 + jnp.tanh(c * (y + 0.044715 * (y * y * y)))))


def _lru_out_kernel(hf_ref, hb_ref, y_ref, res_ref, w_ref, g_ref, b_ref, o_ref):
    gated = (hf_ref[...] + hb_ref[...]) * _gelu_tanh(y_ref[...])
    mix = jnp.dot(gated.astype(BF16), w_ref[...], preferred_element_type=F32)
    o_ref[...] = _layer_norm(DN_ALPHA * res_ref[...] + mix, g_ref[...], b_ref[...])


def _lru_out(hf, hb, u, res, w_bf16, g, b):
    n, d = res.shape
    row = lambda j: pl.BlockSpec((ROW_TILE, d), lambda i, j=j: (i, j))
    vec = pl.BlockSpec((1, d), lambda i: (0, 0))
    return pl.pallas_call(
        _lru_out_kernel,
        out_shape=jax.ShapeDtypeStruct((n, d), F32),
        grid=(n // ROW_TILE,),
        in_specs=[row(0), row(0), row(1), row(0), pl.BlockSpec((d, d), lambda i: (0, 0)), vec, vec],
        out_specs=row(0),
        compiler_params=_params("parallel"),
        name="lru_out",
    )(hf, hb, u, res, w_bf16, g, b)


def _pack_bf16_pairs(x):
    half = x.shape[1] // 2
    lo = pltpu.bitcast(x[:, :half].astype(BF16).astype(F32), U32)
    hi = pltpu.bitcast(x[:, half:].astype(BF16).astype(F32), U32)
    return pltpu.bitcast((lo >> 16) | (hi & jnp.uint32(0xFFFF0000)), I32)


def _unpack_bf16_pairs(packed):
    u = pltpu.bitcast(packed, U32)
    lo = pltpu.bitcast(u << 16, F32).astype(BF16)
    hi = pltpu.bitcast(u & jnp.uint32(0xFFFF0000), F32).astype(BF16)
    return jnp.concatenate([lo, hi], axis=-1)


def _router_kernel(h_ref, w_ref, b_ref, idx_ref, gate_ref, rank_ref, cnt_ref, hp_ref, base_sc):
    tm = h_ref.shape[0]
    hp_ref[...] = _pack_bf16_pairs(h_ref[...])

    @pl.when(pl.program_id(0) == 0)
    def _():
        base_sc[...] = jnp.zeros_like(base_sc)

    logits = jnp.dot(h_ref[...], w_ref[...], preferred_element_type=F32,
                     precision=lax.Precision.HIGHEST) + b_ref[...]
    lane = lax.broadcasted_iota(I32, (tm, N_EXPERTS), 1).astype(F32)
    lane4 = lax.broadcasted_iota(I32, (tm, TOP_K), 1)
    r_i = lax.broadcasted_iota(I32, (tm, tm), 0)
    c_i = lax.broadcasted_iota(I32, (tm, tm), 1)
    lower = jnp.where(r_i > c_i, 1.0, 0.0).astype(BF16)

    vals = logits
    base = base_sc[...]
    idx_out = jnp.zeros((tm, TOP_K), I32)
    val_out = jnp.zeros((tm, TOP_K), F32)
    rank_out = jnp.zeros((tm, TOP_K), I32)
    for k in range(TOP_K):
        m = jnp.max(vals, axis=-1, keepdims=True)
        idx = jnp.min(jnp.where(vals == m, lane, float(N_EXPERTS)), axis=-1, keepdims=True)
        hit = lane == idx
        onehot = jnp.where(hit, 1.0, 0.0)
        before = jnp.dot(lower, onehot.astype(BF16), preferred_element_type=F32)
        rank = jnp.sum(onehot * (base + before), axis=-1, keepdims=True)
        base = base + jnp.sum(onehot, axis=0, keepdims=True)
        idx_out = jnp.where(lane4 == k, idx.astype(I32), idx_out)
        val_out = jnp.where(lane4 == k, m, val_out)
        rank_out = jnp.where(lane4 == k, rank.astype(I32), rank_out)
        vals = jnp.where(hit, -jnp.inf, vals)

    e = jnp.exp(val_out - val_out[:, 0:1])
    idx_ref[...] = idx_out
    gate_ref[...] = e / jnp.sum(e, axis=-1, keepdims=True)
    rank_ref[...] = rank_out
    base_sc[...] = base
    cnt_ref[...] = base.astype(I32)


def _router(h, w, b):
    n, d = h.shape
    tm = ROUTE_TILE
    out4 = lambda dt: jax.ShapeDtypeStruct((n, TOP_K), dt)
    spec4 = pl.BlockSpec((tm, TOP_K), lambda i: (i, 0))
    return pl.pallas_call(
        _router_kernel,
        out_shape=(out4(I32), out4(F32), out4(I32), jax.ShapeDtypeStruct((1, N_EXPERTS), I32),
                   jax.ShapeDtypeStruct((n, d // 2), I32)),
        grid=(n // tm,),
        in_specs=[pl.BlockSpec((tm, d), lambda i: (i, 0)),
                  pl.BlockSpec((d, N_EXPERTS), lambda i: (0, 0)),
                  pl.BlockSpec((1, N_EXPERTS), lambda i: (0, 0))],
        out_specs=(spec4, spec4, spec4, pl.BlockSpec((1, N_EXPERTS), lambda i: (0, 0)),
                   pl.BlockSpec((tm, d // 2), lambda i: (i, 0))),
        scratch_shapes=[pltpu.VMEM((1, N_EXPERTS), F32)],
        compiler_params=_params("arbitrary"),
        name="router",
    )(h, w, b)


def _sc_gather_rows(table, idx, name):
    rows, width = table.shape
    total = idx.shape[0]
    chunk = min(SC_MAX_INDICES, SC_ROW_BUFFER_BYTES // (width * table.dtype.itemsize))
    per_worker = total // SC_WORKERS
    n_chunks = per_worker // chunk
    assert total == SC_WORKERS * n_chunks * chunk and n_chunks % 2 == 0
    mesh = plsc.VectorSubcoreMesh(core_axis_name="c", subcore_axis_name="s",
                                  num_cores=SC_CORES, num_subcores=SC_SUBCORES)

    def body(table_hbm, idx_hbm, out_hbm, idx_v, rows0, rows1, gsem0, gsem1, psem0, psem1):
        worker = lax.axis_index("s") * SC_CORES + lax.axis_index("c")
        base = worker * per_worker
        pltpu.sync_copy(idx_hbm.at[worker], idx_v)
        bufs = (rows0, rows1)
        gsems = (gsem0, gsem1)
        psems = (psem0, psem1)

        def gather(c, slot):
            return pltpu.make_async_copy(table_hbm.at[idx_v.at[c]], bufs[slot], gsems[slot])

        def put(c, slot):
            return pltpu.make_async_copy(bufs[slot], out_hbm.at[pl.ds(base + c * chunk, chunk)], psems[slot])

        gather(0, 0).start()

        @pl.loop(0, n_chunks, step=2)
        def _(c0):
            for slot in range(2):
                c = c0 + slot

                @pl.when(c + 1 < n_chunks)
                def _():
                    @pl.when(c >= 1)
                    def _():
                        put(c - 1, 1 - slot).wait()
                    gather(c + 1, 1 - slot).start()

                gather(c, slot).wait()
                put(c, slot).start()

        put(n_chunks - 2, 0).wait()
        put(n_chunks - 1, 1).wait()

    return pl.kernel(
        body,
        out_type=jax.ShapeDtypeStruct((total, width), table.dtype),
        mesh=mesh,
        scratch_types=[pltpu.VMEM((n_chunks, chunk), I32),
                       pltpu.VMEM((chunk, width), table.dtype), pltpu.VMEM((chunk, width), table.dtype),
                       pltpu.SemaphoreType.DMA, pltpu.SemaphoreType.DMA,
                       pltpu.SemaphoreType.DMA, pltpu.SemaphoreType.DMA],
        name=name,
    )(table, idx.reshape(SC_WORKERS, n_chunks, chunk))


def _sc_scatter_rows(src, dest, out_rows, name):
    n, width = src.shape
    fan = dest.shape[1]
    per_worker = n // SC_WORKERS
    chunk = max(c for c in range(SUBLANES, SC_MAX_INDICES + 1, SUBLANES) if per_worker % (2 * c) == 0)
    n_chunks = per_worker // chunk
    assert n == SC_WORKERS * n_chunks * chunk and n_chunks % 2 == 0
    mesh = plsc.VectorSubcoreMesh(core_axis_name="c", subcore_axis_name="s",
                                  num_cores=SC_CORES, num_subcores=SC_SUBCORES)

    def body(src_hbm, idx_hbm, out_hbm, idx_v, rows0, rows1, lsem0, lsem1, ssem0, ssem1):
        worker = lax.axis_index("s") * SC_CORES + lax.axis_index("c")
        base = worker * per_worker
        pltpu.sync_copy(idx_hbm.at[worker], idx_v)
        bufs = (rows0, rows1)
        lsems = (lsem0, lsem1)
        ssems = (ssem0, ssem1)

        def load(c, slot):
            return pltpu.make_async_copy(src_hbm.at[pl.ds(base + c * chunk, chunk)], bufs[slot], lsems[slot])

        def scatter(c, k, slot):
            return pltpu.make_async_copy(bufs[slot], out_hbm.at[idx_v.at[k * n_chunks + c]], ssems[slot])

        load(0, 0).start()

        @pl.loop(0, n_chunks, step=2)
        def _(c0):
            for slot in range(2):
                c = c0 + slot

                @pl.when(c + 1 < n_chunks)
                def _():
                    @pl.when(c >= 1)
                    def _():
                        for k in range(fan):
                            scatter(c - 1, k, 1 - slot).wait()
                    load(c + 1, 1 - slot).start()

                load(c, slot).wait()
                for k in range(fan):
                    scatter(c, k, slot).start()

        for k in range(fan):
            scatter(n_chunks - 2, k, 0).wait()
        for k in range(fan):
            scatter(n_chunks - 1, k, 1).wait()

    idx = dest.reshape(SC_WORKERS, n_chunks, chunk, fan).transpose(0, 3, 1, 2)
    return pl.kernel(
        body,
        out_type=jax.ShapeDtypeStruct((out_rows, width), src.dtype),
        mesh=mesh,
        scratch_types=[pltpu.VMEM((fan * n_chunks, chunk), I32),
                       pltpu.VMEM((chunk, width), src.dtype), pltpu.VMEM((chunk, width), src.dtype),
                       pltpu.SemaphoreType.DMA, pltpu.SemaphoreType.DMA,
                       pltpu.SemaphoreType.DMA, pltpu.SemaphoreType.DMA],
        name=name,
    )(src, idx.reshape(SC_WORKERS, fan * n_chunks, chunk))


def _expert_kernel(be_ref, nu_ref, xs_ref, wgu_ref, bgu_ref, wd_ref, bd_ref, y_ref, wgu_sc, wd_sc):
    i = pl.program_id(0)
    e = be_ref[i]
    e_prev = be_ref[jnp.maximum(i - 1, 0)]
    d = wd_sc.shape[0]

    @pl.when((i == 0) | (e != e_prev))
    def _():
        for r in range(0, d, LANES):
            wgu_sc[r:r + LANES, :] = wgu_ref[0, r:r + LANES, :].astype(BF16)
            wd_sc[r:r + LANES, :] = wd_ref[0, r:r + LANES, :].astype(BF16)

    @pl.when(i < nu_ref[0])
    def _():
        x = _unpack_bf16_pairs(xs_ref[...])
        h = jnp.dot(x, wgu_sc[...], preferred_element_type=F32) + bgu_ref[0]
        glu = jnp.minimum(h[:, :d], SWIGLU_LIMIT)
        lin = jnp.clip(h[:, d:], -SWIGLU_LIMIT, SWIGLU_LIMIT)
        act = glu * jax.nn.sigmoid(SWIGLU_ALPHA * glu) * (lin + 1.0)
        y_ref[...] = jnp.dot(act.astype(BF16), wd_sc[...], preferred_element_type=F32) + bd_ref[0]

    @pl.when(i >= nu_ref[0])
    def _():
        y_ref[...] = jnp.zeros_like(y_ref)


def _experts(block_e, n_used, xs, w_gu, b_gu, w_down, b_down, layer):
    cap = xs.shape[0]
    d = D_MODEL
    tm = EXPERT_TILE
    n_blocks = cap // tm
    grid_spec = pltpu.PrefetchScalarGridSpec(
        num_scalar_prefetch=2,
        grid=(n_blocks,),
        in_specs=[
            pl.BlockSpec((tm, d // 2), lambda i, be, nu: (i, 0)),
            pl.BlockSpec((None, 1, d, 2 * d), lambda i, be, nu: (layer, be[i], 0, 0)),
            pl.BlockSpec((1, 1, 2 * d), lambda i, be, nu: (be[i], 0, 0)),
            pl.BlockSpec((None, 1, d, d), lambda i, be, nu: (layer, be[i], 0, 0)),
            pl.BlockSpec((1, 1, d), lambda i, be, nu: (be[i], 0, 0)),
        ],
        out_specs=pl.BlockSpec((tm, d), lambda i, be, nu: (i, 0)),
        scratch_shapes=[pltpu.VMEM((d, 2 * d), BF16), pltpu.VMEM((d, d), BF16)],
    )
    return pl.pallas_call(
        _expert_kernel,
        out_shape=jax.ShapeDtypeStruct((cap, d), F32),
        grid_spec=grid_spec,
        compiler_params=_params("arbitrary"),
        name="moe_experts",
    )(block_e, n_used, xs, w_gu, b_gu[layer][:, None, :], w_down, b_down[layer][:, None, :])


def _combine_kernel(gate_ref, res_ref, yk_ref, g_ref, b_ref, o_ref):
    d = res_ref.shape[1]
    gates = gate_ref[...]
    ffn = gates[:, 0:1] * yk_ref[0]
    for k in range(1, TOP_K):
        ffn = ffn + gates[:, k:k + 1] * yk_ref[k]
    o_ref[...] = _layer_norm(DN_ALPHA * res_ref[...] + ffn, g_ref[...], b_ref[...])


def _combine(gates, res, yk, g, b):
    n, d = res.shape
    tt = MOVE_TILE
    vec = pl.BlockSpec((1, d), lambda i: (0, 0))
    return pl.pallas_call(
        _combine_kernel,
        out_shape=jax.ShapeDtypeStruct((n, d), F32),
        grid=(n // tt,),
        in_specs=[pl.BlockSpec((tt, TOP_K), lambda i: (i, 0)),
                  pl.BlockSpec((tt, d), lambda i: (i, 0)),
                  pl.BlockSpec((TOP_K, tt, d), lambda i: (0, i, 0)),
                  vec, vec],
        out_specs=pl.BlockSpec((tt, d), lambda i: (i, 0)),
        compiler_params=_params("parallel"),
        name="moe_combine",
    )(gates, res, yk, g, b)


def _moe_layer(h, router_w, router_b, w_gu, b_gu, w_down, b_down, ln_g, ln_b, layer):
    n, d = h.shape
    tm = EXPERT_TILE
    idx, gates, rank, counts, h_packed = _router(h, router_w, router_b[None, :])
    counts = counts[0]
    padded = (counts + tm - 1) // tm * tm
    pad_end = jnp.cumsum(padded)
    pad_start = pad_end - padded
    dest = pad_start[idx] + rank
    n_blocks = -(-(n * TOP_K + N_EXPERTS * (tm - 1)) // tm)
    block_start = jnp.arange(n_blocks, dtype=I32) * tm
    block_e = jnp.minimum(jnp.sum((pad_end[None, :] <= block_start[:, None]).astype(I32), axis=1),
                          N_EXPERTS - 1)
    n_used = (pad_end[-1:] // tm).astype(I32)

    xs = _sc_scatter_rows(h_packed, dest, n_blocks * tm, "moe_dispatch")
    ys = _experts(block_e, n_used, xs, w_gu, b_gu, w_down, b_down, layer)
    yk = _sc_gather_rows(ys, dest.T.reshape(n * TOP_K), "moe_collect").reshape(TOP_K, n, d)
    return _combine(gates, h, yk, ln_g[None, :], ln_b[None, :])


def _qkv_kernel(x_ref, w_ref, o_ref):
    xb = x_ref[...].astype(BF16)
    d = x_ref.shape[1]
    for part in range(3):
        acc = jnp.dot(xb, w_ref[:, part * d:(part + 1) * d], preferred_element_type=F32)
        if part == 0:
            acc = acc * (NA_HEAD_DIM ** -0.5)
        for p in range(HEAD_PAIRS):
            o_ref[part * HEAD_PAIRS + p] = acc[:, p * LANES:(p + 1) * LANES].astype(BF16)


def _qkv(x, w_bf16):
    n, d = x.shape
    return pl.pallas_call(
        _qkv_kernel,
        out_shape=jax.ShapeDtypeStruct((3 * HEAD_PAIRS, n, LANES), BF16),
        grid=(n // ROW_TILE,),
        in_specs=[pl.BlockSpec((ROW_TILE, d), lambda i: (i, 0)),
                  pl.BlockSpec((d, 3 * d), lambda i: (0, 0))],
        out_specs=pl.BlockSpec((3 * HEAD_PAIRS, ROW_TILE, LANES), lambda i: (0, i, 0)),
        compiler_params=_params("parallel"),
        name="na_qkv",
    )(x, w_bf16)


def _na_kernel(q_ref, k_ref, v_ref, bias_ref, mb_ref, o_ref, *, rows):
    s = pl.program_id(1)
    w = GRID_W
    low = lax.broadcasted_iota(I32, (w, LANES), 1) < NA_HEAD_DIM
    contract_last = (((1,), (1,)), ((), ()))
    meta0 = FRONT_PAD

    @pl.when(s == 0)
    def _():
        o_ref[...] = jnp.zeros_like(o_ref)

    def stacked_q(p):
        qp = q_ref[p]
        zero = jnp.zeros_like(qp)
        return jnp.concatenate([jnp.where(low, qp, zero), jnp.where(low, zero, qp)], axis=0)

    def attend_all(k0):
        scores = []
        for p in range(HEAD_PAIRS):
            q2 = stacked_q(p)
            s_meta = lax.dot_general(q2, k_ref[p, meta0:meta0 + N_META, :], contract_last,
                                     preferred_element_type=F32) + mb_ref[p]
            s_win = None
            if k0 is not None:
                s_win = lax.dot_general(q2, k_ref[p, pl.ds(k0, NA_KH * w), :], contract_last,
                                        preferred_element_type=F32) + bias_ref[0, p]
            scores.append((s_meta, s_win))
        outs = []
        for p in range(HEAD_PAIRS):
            s_meta, s_win = scores[p]
            m = jnp.max(s_meta, axis=-1, keepdims=True)
            if s_win is not None:
                m = jnp.maximum(m, jnp.max(s_win, axis=-1, keepdims=True))
            p_meta = jnp.exp(s_meta - m)
            denom = jnp.sum(p_meta, axis=-1, keepdims=True)
            o = jnp.dot(p_meta.astype(BF16), v_ref[p, meta0:meta0 + N_META, :], preferred_element_type=F32)
            if s_win is not None:
                p_win = jnp.exp(s_win - m)
                denom = denom + jnp.sum(p_win, axis=-1, keepdims=True)
                o = o + jnp.dot(p_win.astype(BF16), v_ref[p, pl.ds(k0, NA_KH * w), :],
                                preferred_element_type=F32)
            o = o / denom
            outs.append(jnp.where(low, o[:w], o[w:]))
        return outs

    @pl.when(s == 1)
    def _():
        qrow = lax.broadcasted_iota(I32, (w, LANES), 0)
        for p, o in enumerate(attend_all(None)):
            o_ref[p] = jnp.where(qrow >= w - N_META, o, 0.0).astype(o_ref.dtype)

    @pl.when(s >= 2)
    def _():
        r = s - 2
        rs = jnp.clip(r - NA_KH // 2, 0, rows - NA_KH)
        k0 = pl.multiple_of(LANES + rs * w, w)
        for p, o in enumerate(attend_all(k0)):
            o_ref[p] = o.astype(o_ref.dtype)


def _na_bias_table(rpb, rows):
    del rows
    w = GRID_W
    q = jnp.arange(w)
    col_start = jnp.clip(q - NA_KW // 2, 0, w - NA_KW)
    c = jnp.arange(w)
    in_win = (c[None, :] >= col_start[:, None]) & (c[None, :] < col_start[:, None] + NA_KW)
    pad = w - NA_KW
    rp = jnp.pad(rpb.astype(F32), ((0, 0), (0, 0), (pad, pad)))
    toeplitz = jnp.stack([rp[:, :, w - 1 - qq:2 * w - 1 - qq] for qq in range(w)], axis=2)
    toeplitz = jnp.where(in_win[None, None], toeplitz, NEG_BIG)
    tabs = []
    for v in range(NA_KH):
        tv = toeplitz[:, NA_KH - 1 - v:2 * NA_KH - 1 - v]
        tabs.append(jnp.transpose(tv, (0, 2, 1, 3)).reshape(NA_HEADS, w, NA_KH * w))
    return jnp.stack(tabs, axis=0)


def _na_attention(qkv, bias_tab, meta_bias, *, batch, lp):
    n = qkv.shape[1]
    w = GRID_W
    rows = (lp - LANES) // w
    tiles = lp // w

    def variant(b, s):
        r = jnp.clip(s - 2, 0, rows - 1)
        return r - jnp.clip(r - NA_KH // 2, 0, rows - NA_KH)

    kern = functools.partial(_na_kernel, rows=rows)
    return pl.pallas_call(
        kern,
        out_shape=jax.ShapeDtypeStruct((HEAD_PAIRS, n, LANES), BF16),
        grid=(batch, tiles),
        in_specs=[
            pl.BlockSpec((HEAD_PAIRS, w, LANES), lambda b, s: (0, b * tiles + s, 0)),
            pl.BlockSpec((HEAD_PAIRS, lp, LANES), lambda b, s: (1, b, 0)),
            pl.BlockSpec((HEAD_PAIRS, lp, LANES), lambda b, s: (2, b, 0)),
            pl.BlockSpec((1, HEAD_PAIRS, 2 * w, NA_KH * w), lambda b, s: (variant(b, s), 0, 0, 0)),
            pl.BlockSpec((HEAD_PAIRS, 2 * w, N_META), lambda b, s: (0, 0, 0)),
        ],
        out_specs=pl.BlockSpec((HEAD_PAIRS, w, LANES), lambda b, s: (0, b * tiles + s, 0)),
        compiler_params=_params("parallel", "arbitrary"),
        name="na_attention",
    )(qkv, qkv, qkv,
      bias_tab.reshape(NA_KH, HEAD_PAIRS, 2 * w, NA_KH * w),
      jnp.repeat(meta_bias, w, axis=0).reshape(HEAD_PAIRS, 2 * w, N_META))


def _na_out_kernel(o_ref, res_ref, w_ref, g_ref, b_ref, out_ref):
    att = jnp.concatenate([o_ref[p] for p in range(HEAD_PAIRS)], axis=-1)
    mix = jnp.dot(att, w_ref[...], preferred_element_type=F32)
    out_ref[...] = _layer_norm(DN_ALPHA * res_ref[...] + mix, g_ref[...], b_ref[...])


def _na_out(o, res, w_bf16, g, b):
    n, d = res.shape
    vec = pl.BlockSpec((1, d), lambda i: (0, 0))
    return pl.pallas_call(
        _na_out_kernel,
        out_shape=jax.ShapeDtypeStruct((n, d), F32),
        grid=(n // ROW_TILE,),
        in_specs=[pl.BlockSpec((HEAD_PAIRS, ROW_TILE, LANES), lambda i: (0, i, 0)),
                  pl.BlockSpec((ROW_TILE, d), lambda i: (i, 0)),
                  pl.BlockSpec((d, d), lambda i: (0, 0)), vec, vec],
        out_specs=pl.BlockSpec((ROW_TILE, d), lambda i: (i, 0)),
        compiler_params=_params("parallel"),
        name="na_out",
    )(o, res, w_bf16, g, b)


def kernel(x, meta_tokens, lru_w_in, lru_conv_w, lru_conv_b, lru_wa, lru_ba, lru_wx, lru_bx, lru_lambda, lru_w_out, na_w_qkv, na_rpb, na_meta_bias, na_w_out, ln_mix_g, ln_mix_b, router_w, router_b, moe_w_gu, moe_b_gu, moe_w_down, moe_b_down, ln_ffn_g, ln_ffn_b):
    batch, seq, d = x.shape
    lp = LANES + seq
    assert d == D_MODEL and seq % GRID_W == 0 and seq // GRID_W >= NA_KH
    assert lp % SCAN_CHUNK == 0 and (batch * lp) % ROW_TILE == 0
    n = batch * lp

    front = jnp.zeros((batch, FRONT_PAD, d), x.dtype)
    meta = jnp.broadcast_to(meta_tokens[None].astype(x.dtype), (batch, N_META, d))
    h = jnp.concatenate([front, meta, x], axis=1).reshape(n, d)

    u = _matmul(h, lru_w_in[0].astype(BF16))
    row = lambda v: v[None, :]
    scans = []
    for direction, reverse in ((0, False), (1, True)):
        scans.append(_lru_scan(
            u, lru_conv_w[0], row(lru_conv_b[0]),
            lru_wa[0, direction].astype(BF16), row(lru_ba[0, direction]),
            lru_wx[0, direction].astype(BF16), row(lru_bx[0, direction]),
            row(lru_lambda[0, direction]), reverse=reverse, lp=lp))
    h = _lru_out(scans[0], scans[1], u, h, lru_w_out[0].astype(BF16), row(ln_mix_g[0]), row(ln_mix_b[0]))
    h = _moe_layer(h, router_w[0], router_b[0], moe_w_gu, moe_b_gu, moe_w_down, moe_b_down,
                   ln_ffn_g[0], ln_ffn_b[0], 0)

    qkv = _qkv(h, na_w_qkv[0].astype(BF16))
    att = _na_attention(qkv, _na_bias_table(na_rpb[0], seq // GRID_W), na_meta_bias[0].astype(F32),
                        batch=batch, lp=lp)
    h = _na_out(att, h, na_w_out[0].astype(BF16), row(ln_mix_g[1]), row(ln_mix_b[1]))
    h = _moe_layer(h, router_w[1], router_b[1], moe_w_gu, moe_b_gu, moe_w_down, moe_b_down,
                   ln_ffn_g[1], ln_ffn_b[1], 1)

    return h.reshape(batch, lp, d)[:, LANES:]
```

```python
import functools

import jax
import jax.numpy as jnp
from jax import lax
from jax.experimental import pallas as pl
from jax.experimental.pallas import tpu as pltpu
from jax.experimental.pallas import tpu_sc as plsc

F32 = jnp.float32
BF16 = jnp.bfloat16
I32 = jnp.int32
U32 = jnp.uint32

D_MODEL = 1024
N_META = 16
GRID_W = 64
LRU_BLOCKS = 4
LRU_BLOCK = D_MODEL // LRU_BLOCKS
CONV_W = 4
LRU_C = 8.0
NA_HEADS = 16
NA_HEAD_DIM = D_MODEL // NA_HEADS
NA_KH = 8
NA_KW = 16
N_EXPERTS = 32
TOP_K = 4
SWIGLU_LIMIT = 7.0
SWIGLU_ALPHA = 1.702
DEPTH = 2
DN_ALPHA = (2.0 * DEPTH) ** 0.25
LN_EPS = 1e-5

LANES = 128
SUBLANES = 8
FRONT_PAD = LANES - N_META
HEAD_PAIRS = D_MODEL // LANES
NEG_BIG = -1e30

ROW_TILE = 512
SCAN_CHUNK = 384
ROUTE_TILE = 512
MOVE_TILE = 256
EXPERT_TILE = 256
VMEM_LIMIT = 56 << 20

SC_CORES = 2
SC_SUBCORES = 16
SC_WORKERS = SC_CORES * SC_SUBCORES
SC_MAX_INDICES = 64
SC_ROW_BUFFER_BYTES = 128 << 10


def _params(*sem):
    return pltpu.CompilerParams(dimension_semantics=sem, vmem_limit_bytes=VMEM_LIMIT)


def _sigmoid(x):
    return 0.5 * jnp.tanh(0.5 * x) + 0.5


def _layer_norm(x, g, b):
    mu = jnp.mean(x, axis=-1, keepdims=True)
    xc = x - mu
    var = jnp.mean(xc * xc, axis=-1, keepdims=True)
    return xc * lax.rsqrt(var + LN_EPS) * g + b


def _matmul_kernel(x_ref, w_ref, o_ref):
    o_ref[...] = jnp.dot(x_ref[...].astype(BF16), w_ref[...], preferred_element_type=F32)


def _matmul(x, w_bf16):
    n, k = x.shape
    m = w_bf16.shape[1]
    return pl.pallas_call(
        _matmul_kernel,
        out_shape=jax.ShapeDtypeStruct((n, m), F32),
        grid=(n // ROW_TILE,),
        in_specs=[pl.BlockSpec((ROW_TILE, k), lambda i: (i, 0)),
                  pl.BlockSpec((k, m), lambda i: (0, 0))],
        out_specs=pl.BlockSpec((ROW_TILE, m), lambda i: (i, 0)),
        compiler_params=_params("parallel"),
        name="in_proj",
    )(x, w_bf16)


def _lru_scan_kernel(xr_ref, xp_ref, xn_ref, cw_ref, cb_ref, wa_ref, ba_ref, wx_ref, bx_ref,
                     lam_ref, h_ref, a_sc, b_sc, carry_sc, *, reverse, chunks_per_batch, lp):
    t_rows = xr_ref.shape[0]
    step = pl.program_id(0)
    n_steps = pl.num_programs(0)
    chunk = (n_steps - 1 - step) if reverse else step
    cib = chunk % chunks_per_batch
    pos = cib * t_rows + lax.broadcasted_iota(I32, (t_rows, 1), 0)
    row = lax.broadcasted_iota(I32, (t_rows, 1), 0)
    real = pos >= FRONT_PAD

    first_chunk = cib == 0
    last_chunk = cib == chunks_per_batch - 1

    xr = jnp.where(real, xr_ref[...], 0.0)
    prev_pos = cib * t_rows - SUBLANES + lax.broadcasted_iota(I32, (SUBLANES, 1), 0)
    prev = jnp.where(prev_pos >= FRONT_PAD, xp_ref[...], 0.0)
    nxt = jnp.where(last_chunk, 0.0, xn_ref[...])

    xm1 = jnp.where(row == 0, prev[7:8, :], pltpu.roll(xr, 1, 0))
    xm2 = pltpu.roll(xr, 2, 0)
    xm2 = jnp.where(row == 0, prev[6:7, :], xm2)
    xm2 = jnp.where(row == 1, prev[7:8, :], xm2)
    xp1 = jnp.where(row == t_rows - 1, nxt[0:1, :], pltpu.roll(xr, t_rows - 1, 0))
    cw = cw_ref[...]
    xc = cw[0:1, :] * xm2 + cw[1:2, :] * xm1 + cw[2:3, :] * xr + cw[3:4, :] * xp1 + cb_ref[...]

    xcb = xc.astype(BF16)
    ga = []
    gx = []
    for blk in range(LRU_BLOCKS):
        xblk = xcb[:, blk * LRU_BLOCK:(blk + 1) * LRU_BLOCK]
        ga.append(jnp.dot(xblk, wa_ref[blk], preferred_element_type=F32))
        gx.append(jnp.dot(xblk, wx_ref[blk], preferred_element_type=F32))
    gate_a = _sigmoid(jnp.concatenate(ga, axis=-1) + ba_ref[...])
    gate_x = _sigmoid(jnp.concatenate(gx, axis=-1) + bx_ref[...])

    z = -lam_ref[...]
    softplus = jnp.maximum(z, 0.0) + jnp.log1p(jnp.exp(-jnp.abs(z)))
    log_a = (-LRU_C * gate_a) * softplus
    a = jnp.exp(log_a)
    mult = jnp.sqrt(-jnp.tanh(log_a) * (a * a + 1.0))
    start = lp - 1 if reverse else FRONT_PAD
    mult = jnp.where(pos == start, 1.0, mult)
    b = jnp.where(real, mult * (gate_x * xc), 0.0)
    a_sc[...] = a
    b_sc[...] = b

    @pl.when(last_chunk if reverse else first_chunk)
    def _():
        carry_sc[...] = jnp.zeros_like(carry_sc)

    row8 = lax.broadcasted_iota(I32, (SUBLANES, 1), 0)
    n_groups = t_rows // SUBLANES

    def group(i, carry):
        g = (n_groups - 1 - i) if reverse else i
        r0 = pl.multiple_of(g * SUBLANES, SUBLANES)
        av = a_sc[pl.ds(r0, SUBLANES), :]
        bv = b_sc[pl.ds(r0, SUBLANES), :]
        for s in (1, 2, 4):
            shift = SUBLANES - s if reverse else s
            keep = (row8 < SUBLANES - s) if reverse else (row8 >= s)
            a_sh = pltpu.roll(av, shift, 0)
            b_sh = pltpu.roll(bv, shift, 0)
            bv = jnp.where(keep, av * b_sh + bv, bv)
            av = jnp.where(keep, av * a_sh, av)
        hv = av * carry + bv
        h_ref[pl.ds(r0, SUBLANES), :] = hv
        return hv[0:1, :] if reverse else hv[SUBLANES - 1:SUBLANES, :]

    carry_sc[...] = lax.fori_loop(0, n_groups, group, carry_sc[...])


def _lru_scan(u, cw, cb, wa_bf16, ba, wx_bf16, bx, lam, *, reverse, lp):
    n = u.shape[0]
    d = D_MODEL
    t = SCAN_CHUNK
    n_chunks = n // t
    cpb = lp // t
    t8 = t // SUBLANES
    n8 = n // SUBLANES

    def chunk_of(i):
        return (n_chunks - 1 - i) if reverse else i

    kern = functools.partial(_lru_scan_kernel, reverse=reverse, chunks_per_batch=cpb, lp=lp)
    full2 = lambda shape: pl.BlockSpec(shape, lambda i: (0, 0))
    full3 = lambda shape: pl.BlockSpec(shape, lambda i: (0, 0, 0))
    return pl.pallas_call(
        kern,
        out_shape=jax.ShapeDtypeStruct((n, d), F32),
        grid=(n_chunks,),
        in_specs=[
            pl.BlockSpec((t, d), lambda i: (chunk_of(i), 0)),
            pl.BlockSpec((SUBLANES, d), lambda i: (jnp.maximum(chunk_of(i) * t8 - 1, 0), 0)),
            pl.BlockSpec((SUBLANES, d), lambda i: (jnp.minimum((chunk_of(i) + 1) * t8, n8 - 1), 0)),
            full2((CONV_W, d)), full2((1, d)),
            full3((LRU_BLOCKS, LRU_BLOCK, LRU_BLOCK)), full2((1, d)),
            full3((LRU_BLOCKS, LRU_BLOCK, LRU_BLOCK)), full2((1, d)),
            full2((1, d)),
        ],
        out_specs=pl.BlockSpec((t, d), lambda i: (chunk_of(i), 0)),
        scratch_shapes=[pltpu.VMEM((t, d), F32), pltpu.VMEM((t, d), F32), pltpu.VMEM((1, d), F32)],
        compiler_params=_params("arbitrary"),
        name="lru_scan_bwd" if reverse else "lru_scan_fwd",
    )(u, u, u, cw, cb, wa_bf16, ba, wx_bf16, bx, lam)


def _gelu_tanh(y):
    c = 0.7978845608028654
    return y * (0.5 * (1.0 + jnp.tanh(c * (y + 0.044715 * (y * y * y)))))


def _lru_out_kernel(hf_ref, hb_ref, y_ref, res_ref, w_ref, g_ref, b_ref, o_ref):
    gated = (hf_ref[...] + hb_ref[...]) * _gelu_tanh(y_ref[...])
    mix = jnp.dot(gated.astype(BF16), w_ref[...], preferred_element_type=F32)
    o_ref[...] = _layer_norm(DN_ALPHA * res_ref[...] + mix, g_ref[...], b_ref[...])


def _lru_out(hf, hb, u, res, w_bf16, g, b):
    n, d = res.shape
    row = lambda j: pl.BlockSpec((ROW_TILE, d), lambda i, j=j: (i, j))
    vec = pl.BlockSpec((1, d), lambda i: (0, 0))
    return pl.pallas_call(
        _lru_out_kernel,
        out_shape=jax.ShapeDtypeStruct((n, d), F32),
        grid=(n // ROW_TILE,),
        in_specs=[row(0), row(0), row(1), row(0), pl.BlockSpec((d, d), lambda i: (0, 0)), vec, vec],
        out_specs=row(0),
        compiler_params=_params("parallel"),
        name="lru_out",
    )(hf, hb, u, res, w_bf16, g, b)


def _pack_bf16_pairs(x):
    half = x.shape[1] // 2
    lo = pltpu.bitcast(x[:, :half].astype(BF16).astype(F32), U32)
    hi = pltpu.bitcast(x[:, half:].astype(BF16).astype(F32), U32)
    return pltpu.bitcast((lo >> 16) | (hi & jnp.uint32(0xFFFF0000)), I32)


def _unpack_bf16_pairs(packed, dtype=BF16):
    u = pltpu.bitcast(packed, U32)
    lo = pltpu.bitcast(u << 16, F32).astype(dtype)
    hi = pltpu.bitcast(u & jnp.uint32(0xFFFF0000), F32).astype(dtype)
    return jnp.concatenate([lo, hi], axis=-1)


def _router_kernel(h_ref, w_ref, b_ref, idx_ref, gate_ref, rank_ref, cnt_ref, hp_ref, base_sc):
    tm = h_ref.shape[0]
    hp_ref[...] = _pack_bf16_pairs(h_ref[...])

    @pl.when(pl.program_id(0) == 0)
    def _():
        base_sc[...] = jnp.zeros_like(base_sc)

    logits = jnp.dot(h_ref[...], w_ref[...], preferred_element_type=F32,
                     precision=lax.Precision.HIGHEST) + b_ref[...]
    lane = lax.broadcasted_iota(I32, (tm, N_EXPERTS), 1).astype(F32)
    lane4 = lax.broadcasted_iota(I32, (tm, TOP_K), 1)
    r_i = lax.broadcasted_iota(I32, (tm, tm), 0)
    c_i = lax.broadcasted_iota(I32, (tm, tm), 1)
    lower = jnp.where(r_i > c_i, 1.0, 0.0).astype(BF16)

    vals = logits
    base = base_sc[...]
    idx_out = jnp.zeros((tm, TOP_K), I32)
    val_out = jnp.zeros((tm, TOP_K), F32)
    rank_out = jnp.zeros((tm, TOP_K), I32)
    for k in range(TOP_K):
        m = jnp.max(vals, axis=-1, keepdims=True)
        idx = jnp.min(jnp.where(vals == m, lane, float(N_EXPERTS)), axis=-1, keepdims=True)
        hit = lane == idx
        onehot = jnp.where(hit, 1.0, 0.0)
        before = jnp.dot(lower, onehot.astype(BF16), preferred_element_type=F32)
        rank = jnp.sum(onehot * (base + before), axis=-1, keepdims=True)
        base = base + jnp.sum(onehot, axis=0, keepdims=True)
        idx_out = jnp.where(lane4 == k, idx.astype(I32), idx_out)
        val_out = jnp.where(lane4 == k, m, val_out)
        rank_out = jnp.where(lane4 == k, rank.astype(I32), rank_out)
        vals = jnp.where(hit, -jnp.inf, vals)

    e = jnp.exp(val_out - val_out[:, 0:1])
    idx_ref[...] = idx_out
    gate_ref[...] = e / jnp.sum(e, axis=-1, keepdims=True)
    rank_ref[...] = rank_out
    base_sc[...] = base
    cnt_ref[...] = base.astype(I32)


def _router(h, w, b):
    n, d = h.shape
    tm = ROUTE_TILE
    out4 = lambda dt: jax.ShapeDtypeStruct((n, TOP_K), dt)
    spec4 = pl.BlockSpec((tm, TOP_K), lambda i: (i, 0))
    return pl.pallas_call(
        _router_kernel,
        out_shape=(out4(I32), out4(F32), out4(I32), jax.ShapeDtypeStruct((1, N_EXPERTS), I32),
                   jax.ShapeDtypeStruct((n, d // 2), I32)),
        grid=(n // tm,),
        in_specs=[pl.BlockSpec((tm, d), lambda i: (i, 0)),
                  pl.BlockSpec((d, N_EXPERTS), lambda i: (0, 0)),
                  pl.BlockSpec((1, N_EXPERTS), lambda i: (0, 0))],
        out_specs=(spec4, spec4, spec4, pl.BlockSpec((1, N_EXPERTS), lambda i: (0, 0)),
                   pl.BlockSpec((tm, d // 2), lambda i: (i, 0))),
        scratch_shapes=[pltpu.VMEM((1, N_EXPERTS), F32)],
        compiler_params=_params("arbitrary"),
        name="router",
    )(h, w, b)


def _sc_gather_rows(table, idx, name):
    rows, width = table.shape
    total = idx.shape[0]
    chunk = min(SC_MAX_INDICES, SC_ROW_BUFFER_BYTES // (width * table.dtype.itemsize))
    per_worker = total // SC_WORKERS
    n_chunks = per_worker // chunk
    assert total == SC_WORKERS * n_chunks * chunk and n_chunks % 2 == 0
    mesh = plsc.VectorSubcoreMesh(core_axis_name="c", subcore_axis_name="s",
                                  num_cores=SC_CORES, num_subcores=SC_SUBCORES)

    def body(table_hbm, idx_hbm, out_hbm, idx_v, rows0, rows1, gsem0, gsem1, psem0, psem1):
        worker = lax.axis_index("s") * SC_CORES + lax.axis_index("c")
        base = worker * per_worker
        pltpu.sync_copy(idx_hbm.at[worker], idx_v)
        bufs = (rows0, rows1)
        gsems = (gsem0, gsem1)
        psems = (psem0, psem1)

        def gather(c, slot):
            return pltpu.make_async_copy(table_hbm.at[idx_v.at[c]], bufs[slot], gsems[slot])

        def put(c, slot):
            return pltpu.make_async_copy(bufs[slot], out_hbm.at[pl.ds(base + c * chunk, chunk)], psems[slot])

        gather(0, 0).start()

        @pl.loop(0, n_chunks, step=2)
        def _(c0):
            for slot in range(2):
                c = c0 + slot

                @pl.when(c + 1 < n_chunks)
                def _():
                    @pl.when(c >= 1)
                    def _():
                        put(c - 1, 1 - slot).wait()
                    gather(c + 1, 1 - slot).start()

                gather(c, slot).wait()
                put(c, slot).start()

        put(n_chunks - 2, 0).wait()
        put(n_chunks - 1, 1).wait()

    return pl.kernel(
        body,
        out_type=jax.ShapeDtypeStruct((total, width), table.dtype),
        mesh=mesh,
        scratch_types=[pltpu.VMEM((n_chunks, chunk), I32),
                       pltpu.VMEM((chunk, width), table.dtype), pltpu.VMEM((chunk, width), table.dtype),
                       pltpu.SemaphoreType.DMA, pltpu.SemaphoreType.DMA,
                       pltpu.SemaphoreType.DMA, pltpu.SemaphoreType.DMA],
        name=name,
    )(table, idx.reshape(SC_WORKERS, n_chunks, chunk))


def _sc_scatter_rows(src, dest, out_rows, name):
    n, width = src.shape
    fan = dest.shape[1]
    per_worker = n // SC_WORKERS
    chunk = max(c for c in range(SUBLANES, SC_MAX_INDICES + 1, SUBLANES) if per_worker % (2 * c) == 0)
    n_chunks = per_worker // chunk
    assert n == SC_WORKERS * n_chunks * chunk and n_chunks % 2 == 0
    mesh = plsc.VectorSubcoreMesh(core_axis_name="c", subcore_axis_name="s",
                                  num_cores=SC_CORES, num_subcores=SC_SUBCORES)

    def body(src_hbm, idx_hbm, out_hbm, idx_v, rows0, rows1, lsem0, lsem1, ssem0, ssem1):
        worker = lax.axis_index("s") * SC_CORES + lax.axis_index("c")
        base = worker * per_worker
        pltpu.sync_copy(idx_hbm.at[worker], idx_v)
        bufs = (rows0, rows1)
        lsems = (lsem0, lsem1)
        ssems = (ssem0, ssem1)

        def load(c, slot):
            return pltpu.make_async_copy(src_hbm.at[pl.ds(base + c * chunk, chunk)], bufs[slot], lsems[slot])

        def scatter(c, k, slot):
            return pltpu.make_async_copy(bufs[slot], out_hbm.at[idx_v.at[k * n_chunks + c]], ssems[slot])

        load(0, 0).start()

        @pl.loop(0, n_chunks, step=2)
        def _(c0):
            for slot in range(2):
                c = c0 + slot

                @pl.when(c + 1 < n_chunks)
                def _():
                    @pl.when(c >= 1)
                    def _():
                        for k in range(fan):
                            scatter(c - 1, k, 1 - slot).wait()
                    load(c + 1, 1 - slot).start()

                load(c, slot).wait()
                for k in range(fan):
                    scatter(c, k, slot).start()

        for k in range(fan):
            scatter(n_chunks - 2, k, 0).wait()
        for k in range(fan):
            scatter(n_chunks - 1, k, 1).wait()

    idx = dest.reshape(SC_WORKERS, n_chunks, chunk, fan).transpose(0, 3, 1, 2)
    return pl.kernel(
        body,
        out_type=jax.ShapeDtypeStruct((out_rows, width), src.dtype),
        mesh=mesh,
        scratch_types=[pltpu.VMEM((fan * n_chunks, chunk), I32),
                       pltpu.VMEM((chunk, width), src.dtype), pltpu.VMEM((chunk, width), src.dtype),
                       pltpu.SemaphoreType.DMA, pltpu.SemaphoreType.DMA,
                       pltpu.SemaphoreType.DMA, pltpu.SemaphoreType.DMA],
        name=name,
    )(src, idx.reshape(SC_WORKERS, fan * n_chunks, chunk))


def _expert_kernel(be_ref, nu_ref, xs_ref, wgu_ref, bgu_ref, wd_ref, bd_ref, y_ref, wgu_sc, wd_sc):
    i = pl.program_id(0)
    e = be_ref[i]
    e_prev = be_ref[jnp.maximum(i - 1, 0)]
    d = wd_sc.shape[0]

    @pl.when((i == 0) | (e != e_prev))
    def _():
        for r in range(0, d, LANES):
            wgu_sc[r:r + LANES, :] = wgu_ref[0, r:r + LANES, :].astype(BF16)
            wd_sc[r:r + LANES, :] = wd_ref[0, r:r + LANES, :].astype(BF16)

    @pl.when(i < nu_ref[0])
    def _():
        x = _unpack_bf16_pairs(xs_ref[...])
        h = jnp.dot(x, wgu_sc[...], preferred_element_type=F32) + bgu_ref[0]
        glu = jnp.minimum(h[:, :d], SWIGLU_LIMIT)
        lin = jnp.clip(h[:, d:], -SWIGLU_LIMIT, SWIGLU_LIMIT)
        act = glu * _sigmoid(SWIGLU_ALPHA * glu) * (lin + 1.0)
        y = jnp.dot(act.astype(BF16), wd_sc[...], preferred_element_type=F32) + bd_ref[0]
        y_ref[...] = _pack_bf16_pairs(y)

    @pl.when(i >= nu_ref[0])
    def _():
        y_ref[...] = jnp.zeros_like(y_ref)


def _experts(block_e, n_used, xs, w_gu, b_gu, w_down, b_down, layer):
    cap = xs.shape[0]
    d = D_MODEL
    tm = EXPERT_TILE
    n_blocks = cap // tm
    grid_spec = pltpu.PrefetchScalarGridSpec(
        num_scalar_prefetch=2,
        grid=(n_blocks,),
        in_specs=[
            pl.BlockSpec((tm, d // 2), lambda i, be, nu: (i, 0)),
            pl.BlockSpec((None, 1, d, 2 * d), lambda i, be, nu: (layer, be[i], 0, 0)),
            pl.BlockSpec((1, 1, 2 * d), lambda i, be, nu: (be[i], 0, 0)),
            pl.BlockSpec((None, 1, d, d), lambda i, be, nu: (layer, be[i], 0, 0)),
            pl.BlockSpec((1, 1, d), lambda i, be, nu: (be[i], 0, 0)),
        ],
        out_specs=pl.BlockSpec((tm, d // 2), lambda i, be, nu: (i, 0)),
        scratch_shapes=[pltpu.VMEM((d, 2 * d), BF16), pltpu.VMEM((d, d), BF16)],
    )
    return pl.pallas_call(
        _expert_kernel,
        out_shape=jax.ShapeDtypeStruct((cap, d // 2), I32),
        grid_spec=grid_spec,
        compiler_params=_params("arbitrary"),
        name="moe_experts",
    )(block_e, n_used, xs, w_gu, b_gu[layer][:, None, :], w_down, b_down[layer][:, None, :])


def _combine_kernel(gate_ref, res_ref, yk_ref, g_ref, b_ref, o_ref):
    d = res_ref.shape[1]
    gates = gate_ref[...]
    ffn = gates[:, 0:1] * _unpack_bf16_pairs(yk_ref[0], F32)
    for k in range(1, TOP_K):
        ffn = ffn + gates[:, k:k + 1] * _unpack_bf16_pairs(yk_ref[k], F32)
    o_ref[...] = _layer_norm(DN_ALPHA * res_ref[...] + ffn, g_ref[...], b_ref[...])


def _combine(gates, res, yk, g, b):
    n, d = res.shape
    tt = MOVE_TILE
    vec = pl.BlockSpec((1, d), lambda i: (0, 0))
    return pl.pallas_call(
        _combine_kernel,
        out_shape=jax.ShapeDtypeStruct((n, d), F32),
        grid=(n // tt,),
        in_specs=[pl.BlockSpec((tt, TOP_K), lambda i: (i, 0)),
                  pl.BlockSpec((tt, d), lambda i: (i, 0)),
                  pl.BlockSpec((TOP_K, tt, d // 2), lambda i: (0, i, 0)),
                  vec, vec],
        out_specs=pl.BlockSpec((tt, d), lambda i: (i, 0)),
        compiler_params=_params("parallel"),
        name="moe_combine",
    )(gates, res, yk, g, b)


def _moe_layer(h, router_w, router_b, w_gu, b_gu, w_down, b_down, ln_g, ln_b, layer):
    n, d = h.shape
    tm = EXPERT_TILE
    idx, gates, rank, counts, h_packed = _router(h, router_w, router_b[None, :])
    counts = counts[0]
    padded = (counts + tm - 1) // tm * tm
    pad_end = jnp.cumsum(padded)
    pad_start = pad_end - padded
    group_start = jnp.sum(jnp.where(idx[:, :, None] == jnp.arange(N_EXPERTS, dtype=I32), pad_start, 0), axis=-1)
    dest = group_start + rank
    n_blocks = -(-(n * TOP_K + N_EXPERTS * (tm - 1)) // tm)
    block_start = jnp.arange(n_blocks, dtype=I32) * tm
    block_e = jnp.minimum(jnp.sum((pad_end[None, :] <= block_start[:, None]).astype(I32), axis=1),
                          N_EXPERTS - 1)
    n_used = (pad_end[-1:] // tm).astype(I32)

    xs = _sc_scatter_rows(h_packed, dest, n_blocks * tm, "moe_dispatch")
    ys = _experts(block_e, n_used, xs, w_gu, b_gu, w_down, b_down, layer)
    yk = _sc_gather_rows(ys, dest.T.reshape(n * TOP_K), "moe_collect").reshape(TOP_K, n, d // 2)
    return _combine(gates, h, yk, ln_g[None, :], ln_b[None, :])


def _qkv_kernel(x_ref, w_ref, o_ref):
    xb = x_ref[...].astype(BF16)
    d = x_ref.shape[1]
    for part in range(3):
        acc = jnp.dot(xb, w_ref[:, part * d:(part + 1) * d], preferred_element_type=F32)
        if part == 0:
            acc = acc * (NA_HEAD_DIM ** -0.5)
        for p in range(HEAD_PAIRS):
            o_ref[part * HEAD_PAIRS + p] = acc[:, p * LANES:(p + 1) * LANES].astype(BF16)


def _qkv(x, w_bf16):
    n, d = x.shape
    return pl.pallas_call(
        _qkv_kernel,
        out_shape=jax.ShapeDtypeStruct((3 * HEAD_PAIRS, n, LANES), BF16),
        grid=(n // ROW_TILE,),
        in_specs=[pl.BlockSpec((ROW_TILE, d), lambda i: (i, 0)),
                  pl.BlockSpec((d, 3 * d), lambda i: (0, 0))],
        out_specs=pl.BlockSpec((3 * HEAD_PAIRS, ROW_TILE, LANES), lambda i: (0, i, 0)),
        compiler_params=_params("parallel"),
        name="na_qkv",
    )(x, w_bf16)


def _na_kernel(q_ref, k_ref, v_ref, bias_ref, mb_ref, o_ref, *, rows):
    s = pl.program_id(1)
    w = GRID_W
    low = lax.broadcasted_iota(I32, (w, LANES), 1) < NA_HEAD_DIM
    contract_last = (((1,), (1,)), ((), ()))
    meta0 = FRONT_PAD

    @pl.when(s == 0)
    def _():
        o_ref[...] = jnp.zeros_like(o_ref)

    def stacked_q(p):
        qp = q_ref[p]
        zero = jnp.zeros_like(qp)
        return jnp.concatenate([jnp.where(low, qp, zero), jnp.where(low, zero, qp)], axis=0)

    def attend_all(k0):
        scores = []
        for p in range(HEAD_PAIRS):
            q2 = stacked_q(p)
            s_meta = lax.dot_general(q2, k_ref[p, meta0:meta0 + N_META, :], contract_last,
                                     preferred_element_type=F32) + mb_ref[p]
            s_win = None
            if k0 is not None:
                s_win = lax.dot_general(q2, k_ref[p, pl.ds(k0, NA_KH * w), :], contract_last,
                                        preferred_element_type=F32) + bias_ref[0, p]
            scores.append((s_meta, s_win))
        outs = []
        for p in range(HEAD_PAIRS):
            s_meta, s_win = scores[p]
            m = jnp.max(s_meta, axis=-1, keepdims=True)
            if s_win is not None:
                m = jnp.maximum(m, jnp.max(s_win, axis=-1, keepdims=True))
            p_meta = jnp.exp(s_meta - m)
            denom = jnp.sum(p_meta, axis=-1, keepdims=True)
            o = jnp.dot(p_meta.astype(BF16), v_ref[p, meta0:meta0 + N_META, :], preferred_element_type=F32)
            if s_win is not None:
                p_win = jnp.exp(s_win - m)
                denom = denom + jnp.sum(p_win, axis=-1, keepdims=True)
                o = o + jnp.dot(p_win.astype(BF16), v_ref[p, pl.ds(k0, NA_KH * w), :],
                                preferred_element_type=F32)
            o = o / denom
            outs.append(jnp.where(low, o[:w], o[w:]))
        return outs

    @pl.when(s == 1)
    def _():
        qrow = lax.broadcasted_iota(I32, (w, LANES), 0)
        for p, o in enumerate(attend_all(None)):
            o_ref[p] = jnp.where(qrow >= w - N_META, o, 0.0).astype(o_ref.dtype)

    @pl.when(s >= 2)
    def _():
        r = s - 2
        rs = jnp.clip(r - NA_KH // 2, 0, rows - NA_KH)
        k0 = pl.multiple_of(LANES + rs * w, w)
        for p, o in enumerate(attend_all(k0)):
            o_ref[p] = o.astype(o_ref.dtype)


def _na_bias_table(rpb, rows):
    del rows
    w = GRID_W
    q = jnp.arange(w)
    col_start = jnp.clip(q - NA_KW // 2, 0, w - NA_KW)
    c = jnp.arange(w)
    in_win = (c[None, :] >= col_start[:, None]) & (c[None, :] < col_start[:, None] + NA_KW)
    pad = w - NA_KW
    rp = jnp.pad(rpb.astype(F32), ((0, 0), (0, 0), (pad, pad)))
    toeplitz = jnp.stack([rp[:, :, w - 1 - qq:2 * w - 1 - qq] for qq in range(w)], axis=2)
    toeplitz = jnp.where(in_win[None, None], toeplitz, NEG_BIG)
    tabs = []
    for v in range(NA_KH):
        tv = toeplitz[:, NA_KH - 1 - v:2 * NA_KH - 1 - v]
        tabs.append(jnp.transpose(tv, (0, 2, 1, 3)).reshape(NA_HEADS, w, NA_KH * w))
    return jnp.stack(tabs, axis=0)


def _na_attention(qkv, bias_tab, meta_bias, *, batch, lp):
    n = qkv.shape[1]
    w = GRID_W
    rows = (lp - LANES) // w
    tiles = lp // w

    def variant(b, s):
        r = jnp.clip(s - 2, 0, rows - 1)
        return r - jnp.clip(r - NA_KH // 2, 0, rows - NA_KH)

    kern = functools.partial(_na_kernel, rows=rows)
    return pl.pallas_call(
        kern,
        out_shape=jax.ShapeDtypeStruct((HEAD_PAIRS, n, LANES), BF16),
        grid=(batch, tiles),
        in_specs=[
            pl.BlockSpec((HEAD_PAIRS, w, LANES), lambda b, s: (0, b * tiles + s, 0)),
            pl.BlockSpec((HEAD_PAIRS, lp, LANES), lambda b, s: (1, b, 0)),
            pl.BlockSpec((HEAD_PAIRS, lp, LANES), lambda b, s: (2, b, 0)),
            pl.BlockSpec((1, HEAD_PAIRS, 2 * w, NA_KH * w), lambda b, s: (variant(b, s), 0, 0, 0)),
            pl.BlockSpec((HEAD_PAIRS, 2 * w, N_META), lambda b, s: (0, 0, 0)),
        ],
        out_specs=pl.BlockSpec((HEAD_PAIRS, w, LANES), lambda b, s: (0, b * tiles + s, 0)),
        compiler_params=_params("parallel", "arbitrary"),
        name="na_attention",
    )(qkv, qkv, qkv,
      bias_tab.reshape(NA_KH, HEAD_PAIRS, 2 * w, NA_KH * w),
      jnp.repeat(meta_bias, w, axis=0).reshape(HEAD_PAIRS, 2 * w, N_META))


def _na_out_kernel(o_ref, res_ref, w_ref, g_ref, b_ref, out_ref):
    att = jnp.concatenate([o_ref[p] for p in range(HEAD_PAIRS)], axis=-1)
    mix = jnp.dot(att, w_ref[...], preferred_element_type=F32)
    out_ref[...] = _layer_norm(DN_ALPHA * res_ref[...] + mix, g_ref[...], b_ref[...])


def _na_out(o, res, w_bf16, g, b):
    n, d = res.shape
    vec = pl.BlockSpec((1, d), lambda i: (0, 0))
    return pl.pallas_call(
        _na_out_kernel,
        out_shape=jax.ShapeDtypeStruct((n, d), F32),
        grid=(n // ROW_TILE,),
        in_specs=[pl.BlockSpec((HEAD_PAIRS, ROW_TILE, LANES), lambda i: (0, i, 0)),
                  pl.BlockSpec((ROW_TILE, d), lambda i: (i, 0)),
                  pl.BlockSpec((d, d), lambda i: (0, 0)), vec, vec],
        out_specs=pl.BlockSpec((ROW_TILE, d), lambda i: (i, 0)),
        compiler_params=_params("parallel"),
        name="na_out",
    )(o, res, w_bf16, g, b)


def kernel(x, meta_tokens, lru_w_in, lru_conv_w, lru_conv_b, lru_wa, lru_ba, lru_wx, lru_bx, lru_lambda, lru_w_out, na_w_qkv, na_rpb, na_meta_bias, na_w_out, ln_mix_g, ln_mix_b, router_w, router_b, moe_w_gu, moe_b_gu, moe_w_down, moe_b_down, ln_ffn_g, ln_ffn_b):
    batch, seq, d = x.shape
    lp = LANES + seq
    assert d == D_MODEL and seq % GRID_W == 0 and seq // GRID_W >= NA_KH
    assert lp % SCAN_CHUNK == 0 and (batch * lp) % ROW_TILE == 0
    n = batch * lp

    front = jnp.zeros((batch, FRONT_PAD, d), x.dtype)
    meta = jnp.broadcast_to(meta_tokens[None].astype(x.dtype), (batch, N_META, d))
    h = jnp.concatenate([front, meta, x], axis=1).reshape(n, d)

    u = _matmul(h, lru_w_in[0].astype(BF16))
    row = lambda v: v[None, :]
    scans = []
    for direction, reverse in ((0, False), (1, True)):
        scans.append(_lru_scan(
            u, lru_conv_w[0], row(lru_conv_b[0]),
            lru_wa[0, direction].astype(BF16), row(lru_ba[0, direction]),
            lru_wx[0, direction].astype(BF16), row(lru_bx[0, direction]),
            row(lru_lambda[0, direction]), reverse=reverse, lp=lp))
    h = _lru_out(scans[0], scans[1], u, h, lru_w_out[0].astype(BF16), row(ln_mix_g[0]), row(ln_mix_b[0]))
    h = _moe_layer(h, router_w[0], router_b[0], moe_w_gu, moe_b_gu, moe_w_down, moe_b_down,
                   ln_ffn_g[0], ln_ffn_b[0], 0)

    qkv = _qkv(h, na_w_qkv[0].astype(BF16))
    att = _na_attention(qkv, _na_bias_table(na_rpb[0], seq // GRID_W), na_meta_bias[0].astype(F32),
                        batch=batch, lp=lp)
    h = _na_out(att, h, na_w_out[0].astype(BF16), row(ln_mix_g[1]), row(ln_mix_b[1]))
    h = _moe_layer(h, router_w[1], router_b[1], moe_w_gu, moe_b_gu, moe_w_down, moe_b_down,
                   ln_ffn_g[1], ln_ffn_b[1], 1)

    return h.reshape(batch, lp, d)[:, LANES:]
```

```python
import functools

import jax
import jax.numpy as jnp
from jax import lax
from jax.experimental import pallas as pl
from jax.experimental.pallas import tpu as pltpu
from jax.experimental.pallas import tpu_sc as plsc

F32 = jnp.float32
BF16 = jnp.bfloat16
I32 = jnp.int32
U32 = jnp.uint32

D_MODEL = 1024
N_META = 16
GRID_W = 64
LRU_BLOCKS = 4
LRU_BLOCK = D_MODEL // LRU_BLOCKS
CONV_W = 4
LRU_C = 8.0
NA_HEADS = 16
NA_HEAD_DIM = D_MODEL // NA_HEADS
NA_KH = 8
NA_KW = 16
N_EXPERTS = 32
TOP_K = 4
SWIGLU_LIMIT = 7.0
SWIGLU_ALPHA = 1.702
DEPTH = 2
DN_ALPHA = (2.0 * DEPTH) ** 0.25
LN_EPS = 1e-5

LANES = 128
SUBLANES = 8
FRONT_PAD = LANES - N_META
HEAD_PAIRS = D_MODEL // LANES
NEG_BIG = -1e30

ROW_TILE = 512
SCAN_CHUNK = 384
ROUTE_TILE = 512
MOVE_TILE = 256
EXPERT_TILE = 512
VMEM_LIMIT = 56 << 20

SC_CORES = 2
SC_SUBCORES = 16
SC_WORKERS = SC_CORES * SC_SUBCORES
SC_MAX_INDICES = 64
SC_ROW_BUFFER_BYTES = 128 << 10


def _params(*sem):
    return pltpu.CompilerParams(dimension_semantics=sem, vmem_limit_bytes=VMEM_LIMIT)


def _sigmoid(x):
    return 0.5 * jnp.tanh(0.5 * x) + 0.5


def _layer_norm(x, g, b):
    mu = jnp.mean(x, axis=-1, keepdims=True)
    xc = x - mu
    var = jnp.mean(xc * xc, axis=-1, keepdims=True)
    return xc * lax.rsqrt(var + LN_EPS) * g + b


def _matmul_kernel(x_ref, w_ref, o_ref):
    o_ref[...] = jnp.dot(x_ref[...].astype(BF16), w_ref[...], preferred_element_type=F32)


def _matmul(x, w_bf16):
    n, k = x.shape
    m = w_bf16.shape[1]
    return pl.pallas_call(
        _matmul_kernel,
        out_shape=jax.ShapeDtypeStruct((n, m), F32),
        grid=(n // ROW_TILE,),
        in_specs=[pl.BlockSpec((ROW_TILE, k), lambda i: (i, 0)),
                  pl.BlockSpec((k, m), lambda i: (0, 0))],
        out_specs=pl.BlockSpec((ROW_TILE, m), lambda i: (i, 0)),
        compiler_params=_params("parallel"),
        name="in_proj",
    )(x, w_bf16)


def _lru_scan_kernel(xr_ref, xp_ref, xn_ref, cw_ref, cb_ref, wa_ref, ba_ref, wx_ref, bx_ref,
                     lam_ref, h_ref, a_sc, b_sc, carry_sc, *, reverse, chunks_per_batch, lp):
    t_rows = xr_ref.shape[0]
    step = pl.program_id(0)
    n_steps = pl.num_programs(0)
    chunk = (n_steps - 1 - step) if reverse else step
    cib = chunk % chunks_per_batch
    pos = cib * t_rows + lax.broadcasted_iota(I32, (t_rows, 1), 0)
    row = lax.broadcasted_iota(I32, (t_rows, 1), 0)
    real = pos >= FRONT_PAD

    first_chunk = cib == 0
    last_chunk = cib == chunks_per_batch - 1

    xr = jnp.where(real, xr_ref[...], 0.0)
    prev_pos = cib * t_rows - SUBLANES + lax.broadcasted_iota(I32, (SUBLANES, 1), 0)
    prev = jnp.where(prev_pos >= FRONT_PAD, xp_ref[...], 0.0)
    nxt = jnp.where(last_chunk, 0.0, xn_ref[...])

    xm1 = jnp.where(row == 0, prev[7:8, :], pltpu.roll(xr, 1, 0))
    xm2 = pltpu.roll(xr, 2, 0)
    xm2 = jnp.where(row == 0, prev[6:7, :], xm2)
    xm2 = jnp.where(row == 1, prev[7:8, :], xm2)
    xp1 = jnp.where(row == t_rows - 1, nxt[0:1, :], pltpu.roll(xr, t_rows - 1, 0))
    cw = cw_ref[...]
    xc = cw[0:1, :] * xm2 + cw[1:2, :] * xm1 + cw[2:3, :] * xr + cw[3:4, :] * xp1 + cb_ref[...]

    xcb = xc.astype(BF16)
    ga = []
    gx = []
    for blk in range(LRU_BLOCKS):
        xblk = xcb[:, blk * LRU_BLOCK:(blk + 1) * LRU_BLOCK]
        ga.append(jnp.dot(xblk, wa_ref[blk], preferred_element_type=F32))
        gx.append(jnp.dot(xblk, wx_ref[blk], preferred_element_type=F32))
    gate_a = _sigmoid(jnp.concatenate(ga, axis=-1) + ba_ref[...])
    gate_x = _sigmoid(jnp.concatenate(gx, axis=-1) + bx_ref[...])

    z = -lam_ref[...]
    softplus = jnp.maximum(z, 0.0) + jnp.log1p(jnp.exp(-jnp.abs(z)))
    log_a = (-LRU_C * gate_a) * softplus
    a = jnp.exp(log_a)
    mult = jnp.sqrt(-jnp.tanh(log_a) * (a * a + 1.0))
    start = lp - 1 if reverse else FRONT_PAD
    mult = jnp.where(pos == start, 1.0, mult)
    b = jnp.where(real, mult * (gate_x * xc), 0.0)
    a_sc[...] = a
    b_sc[...] = b

    @pl.when(last_chunk if reverse else first_chunk)
    def _():
        carry_sc[...] = jnp.zeros_like(carry_sc)

    row8 = lax.broadcasted_iota(I32, (SUBLANES, 1), 0)
    n_groups = t_rows // SUBLANES

    def group(i, carry):
        g = (n_groups - 1 - i) if reverse else i
        r0 = pl.multiple_of(g * SUBLANES, SUBLANES)
        av = a_sc[pl.ds(r0, SUBLANES), :]
        bv = b_sc[pl.ds(r0, SUBLANES), :]
        for s in (1, 2, 4):
            shift = SUBLANES - s if reverse else s
            keep = (row8 < SUBLANES - s) if reverse else (row8 >= s)
            a_sh = pltpu.roll(av, shift, 0)
            b_sh = pltpu.roll(bv, shift, 0)
            bv = jnp.where(keep, av * b_sh + bv, bv)
            av = jnp.where(keep, av * a_sh, av)
        hv = av * carry + bv
        h_ref[pl.ds(r0, SUBLANES), :] = hv
        return hv[0:1, :] if reverse else hv[SUBLANES - 1:SUBLANES, :]

    carry_sc[...] = lax.fori_loop(0, n_groups, group, carry_sc[...])


def _lru_scan(u, cw, cb, wa_bf16, ba, wx_bf16, bx, lam, *, reverse, lp):
    n = u.shape[0]
    d = D_MODEL
    t = SCAN_CHUNK
    n_chunks = n // t
    cpb = lp // t
    t8 = t // SUBLANES
    n8 = n // SUBLANES

    def chunk_of(i):
        return (n_chunks - 1 - i) if reverse else i

    kern = functools.partial(_lru_scan_kernel, reverse=reverse, chunks_per_batch=cpb, lp=lp)
    full2 = lambda shape: pl.BlockSpec(shape, lambda i: (0, 0))
    full3 = lambda shape: pl.BlockSpec(shape, lambda i: (0, 0, 0))
    return pl.pallas_call(
        kern,
        out_shape=jax.ShapeDtypeStruct((n, d), F32),
        grid=(n_chunks,),
        in_specs=[
            pl.BlockSpec((t, d), lambda i: (chunk_of(i), 0)),
            pl.BlockSpec((SUBLANES, d), lambda i: (jnp.maximum(chunk_of(i) * t8 - 1, 0), 0)),
            pl.BlockSpec((SUBLANES, d), lambda i: (jnp.minimum((chunk_of(i) + 1) * t8, n8 - 1), 0)),
            full2((CONV_W, d)), full2((1, d)),
            full3((LRU_BLOCKS, LRU_BLOCK, LRU_BLOCK)), full2((1, d)),
            full3((LRU_BLOCKS, LRU_BLOCK, LRU_BLOCK)), full2((1, d)),
            full2((1, d)),
        ],
        out_specs=pl.BlockSpec((t, d), lambda i: (chunk_of(i), 0)),
        scratch_shapes=[pltpu.VMEM((t, d), F32), pltpu.VMEM((t, d), F32), pltpu.VMEM((1, d), F32)],
        compiler_params=_params("arbitrary"),
        name="lru_scan_bwd" if reverse else "lru_scan_fwd",
    )(u, u, u, cw, cb, wa_bf16, ba, wx_bf16, bx, lam)


def _gelu_tanh(y):
    c = 0.7978845608028654
    return y * (0.5 * (1.0 + jnp.tanh(c * (y + 0.044715 * (y * y * y)))))


def _lru_out_kernel(hf_ref, hb_ref, y_ref, res_ref, w_ref, g_ref, b_ref, o_ref):
    gated = (hf_ref[...] + hb_ref[...]) * _gelu_tanh(y_ref[...])
    mix = jnp.dot(gated.astype(BF16), w_ref[...], preferred_element_type=F32)
    o_ref[...] = _layer_norm(DN_ALPHA * res_ref[...] + mix, g_ref[...], b_ref[...])


def _lru_out(hf, hb, u, res, w_bf16, g, b):
    n, d = res.shape
    row = lambda j: pl.BlockSpec((ROW_TILE, d), lambda i, j=j: (i, j))
    vec = pl.BlockSpec((1, d), lambda i: (0, 0))
    return pl.pallas_call(
        _lru_out_kernel,
        out_shape=jax.ShapeDtypeStruct((n, d), F32),
        grid=(n // ROW_TILE,),
        in_specs=[row(0), row(0), row(1), row(0), pl.BlockSpec((d, d), lambda i: (0, 0)), vec, vec],
        out_specs=row(0),
        compiler_params=_params("parallel"),
        name="lru_out",
    )(hf, hb, u, res, w_bf16, g, b)


def _pack_bf16_pairs(x):
    half = x.shape[1] // 2
    lo = pltpu.bitcast(x[:, :half].astype(BF16).astype(F32), U32)
    hi = pltpu.bitcast(x[:, half:].astype(BF16).astype(F32), U32)
    return pltpu.bitcast((lo >> 16) | (hi & jnp.uint32(0xFFFF0000)), I32)


def _unpack_bf16_pairs(packed, dtype=BF16):
    u = pltpu.bitcast(packed, U32)
    lo = pltpu.bitcast(u << 16, F32).astype(dtype)
    hi = pltpu.bitcast(u & jnp.uint32(0xFFFF0000), F32).astype(dtype)
    return jnp.concatenate([lo, hi], axis=-1)


def _router_kernel(h_ref, w_ref, b_ref, idx_ref, gate_ref, rank_ref, cnt_ref, hp_ref, base_sc):
    tm = h_ref.shape[0]
    hp_ref[...] = _pack_bf16_pairs(h_ref[...])

    @pl.when(pl.program_id(0) == 0)
    def _():
        base_sc[...] = jnp.zeros_like(base_sc)

    h = h_ref[...]
    w = w_ref[...]
    h_hi = h.astype(BF16)
    h_lo = (h - h_hi.astype(F32)).astype(BF16)
    w_hi = w.astype(BF16)
    w_lo = (w - w_hi.astype(F32)).astype(BF16)
    logits = (jnp.dot(h_hi, w_hi, preferred_element_type=F32)
              + (jnp.dot(h_lo, w_hi, preferred_element_type=F32) + jnp.dot(h_hi, w_lo, preferred_element_type=F32))
              + b_ref[...])
    lane = lax.broadcasted_iota(I32, (tm, N_EXPERTS), 1).astype(F32)
    lane4 = lax.broadcasted_iota(I32, (tm, TOP_K), 1)
    r_i = lax.broadcasted_iota(I32, (tm, tm), 0)
    c_i = lax.broadcasted_iota(I32, (tm, tm), 1)
    lower = jnp.where(r_i > c_i, 1.0, 0.0).astype(BF16)

    vals = logits
    idx_out = jnp.zeros((tm, TOP_K), I32)
    val_out = jnp.zeros((tm, TOP_K), F32)
    onehots = []
    for k in range(TOP_K):
        m = jnp.max(vals, axis=-1, keepdims=True)
        idx = jnp.min(jnp.where(vals == m, lane, float(N_EXPERTS)), axis=-1, keepdims=True)
        hit = lane == idx
        onehots.append(jnp.where(hit, 1.0, 0.0))
        idx_out = jnp.where(lane4 == k, idx.astype(I32), idx_out)
        val_out = jnp.where(lane4 == k, m, val_out)
        vals = jnp.where(hit, -jnp.inf, vals)

    onehot_all = jnp.concatenate(onehots, axis=-1)
    before = jnp.dot(lower, onehot_all.astype(BF16), preferred_element_type=F32)
    base = base_sc[...]
    starts = []
    for k in range(TOP_K):
        starts.append(base)
        base = base + jnp.sum(onehots[k], axis=0, keepdims=True)
    contrib = onehot_all * (jnp.concatenate(starts, axis=-1) + before)
    segment = lax.broadcasted_iota(I32, (tm, TOP_K * N_EXPERTS), 1) // N_EXPERTS
    rank_out = jnp.zeros((tm, TOP_K), I32)
    for k in range(TOP_K):
        rank = jnp.sum(jnp.where(segment == k, contrib, 0.0), axis=-1, keepdims=True)
        rank_out = jnp.where(lane4 == k, rank.astype(I32), rank_out)

    e = jnp.exp(val_out - val_out[:, 0:1])
    idx_ref[...] = idx_out
    gate_ref[...] = e / jnp.sum(e, axis=-1, keepdims=True)
    rank_ref[...] = rank_out
    base_sc[...] = base
    cnt_ref[...] = base.astype(I32)


def _router(h, w, b):
    n, d = h.shape
    tm = ROUTE_TILE
    out4 = lambda dt: jax.ShapeDtypeStruct((n, TOP_K), dt)
    spec4 = pl.BlockSpec((tm, TOP_K), lambda i: (i, 0))
    return pl.pallas_call(
        _router_kernel,
        out_shape=(out4(I32), out4(F32), out4(I32), jax.ShapeDtypeStruct((1, N_EXPERTS), I32),
                   jax.ShapeDtypeStruct((n, d // 2), I32)),
        grid=(n // tm,),
        in_specs=[pl.BlockSpec((tm, d), lambda i: (i, 0)),
                  pl.BlockSpec((d, N_EXPERTS), lambda i: (0, 0)),
                  pl.BlockSpec((1, N_EXPERTS), lambda i: (0, 0))],
        out_specs=(spec4, spec4, spec4, pl.BlockSpec((1, N_EXPERTS), lambda i: (0, 0)),
                   pl.BlockSpec((tm, d // 2), lambda i: (i, 0))),
        scratch_shapes=[pltpu.VMEM((1, N_EXPERTS), F32)],
        compiler_params=_params("arbitrary"),
        name="router",
    )(h, w, b)


def _sc_gather_rows(table, idx, name):
    rows, width = table.shape
    total = idx.shape[0]
    chunk = min(SC_MAX_INDICES, SC_ROW_BUFFER_BYTES // (width * table.dtype.itemsize))
    per_worker = total // SC_WORKERS
    n_chunks = per_worker // chunk
    assert total == SC_WORKERS * n_chunks * chunk and n_chunks % 2 == 0
    mesh = plsc.VectorSubcoreMesh(core_axis_name="c", subcore_axis_name="s",
                                  num_cores=SC_CORES, num_subcores=SC_SUBCORES)

    def body(table_hbm, idx_hbm, out_hbm, idx_v, rows0, rows1, gsem0, gsem1, psem0, psem1):
        worker = lax.axis_index("s") * SC_CORES + lax.axis_index("c")
        base = worker * per_worker
        pltpu.sync_copy(idx_hbm.at[worker], idx_v)
        bufs = (rows0, rows1)
        gsems = (gsem0, gsem1)
        psems = (psem0, psem1)

        def gather(c, slot):
            return pltpu.make_async_copy(table_hbm.at[idx_v.at[c]], bufs[slot], gsems[slot])

        def put(c, slot):
            return pltpu.make_async_copy(bufs[slot], out_hbm.at[pl.ds(base + c * chunk, chunk)], psems[slot])

        gather(0, 0).start()

        @pl.loop(0, n_chunks, step=2)
        def _(c0):
            for slot in range(2):
                c = c0 + slot

                @pl.when(c + 1 < n_chunks)
                def _():
                    @pl.when(c >= 1)
                    def _():
                        put(c - 1, 1 - slot).wait()
                    gather(c + 1, 1 - slot).start()

                gather(c, slot).wait()
                put(c, slot).start()

        put(n_chunks - 2, 0).wait()
        put(n_chunks - 1, 1).wait()

    return pl.kernel(
        body,
        out_type=jax.ShapeDtypeStruct((total, width), table.dtype),
        mesh=mesh,
        scratch_types=[pltpu.VMEM((n_chunks, chunk), I32),
                       pltpu.VMEM((chunk, width), table.dtype), pltpu.VMEM((chunk, width), table.dtype),
                       pltpu.SemaphoreType.DMA, pltpu.SemaphoreType.DMA,
                       pltpu.SemaphoreType.DMA, pltpu.SemaphoreType.DMA],
        name=name,
    )(table, idx.reshape(SC_WORKERS, n_chunks, chunk))


def _sc_scatter_rows(src, dest, out_rows, name):
    n, width = src.shape
    fan = dest.shape[1]
    per_worker = n // SC_WORKERS
    chunk = max(c for c in range(SUBLANES, SC_MAX_INDICES + 1, SUBLANES) if per_worker % (2 * c) == 0)
    n_chunks = per_worker // chunk
    assert n == SC_WORKERS * n_chunks * chunk and n_chunks % 2 == 0
    mesh = plsc.VectorSubcoreMesh(core_axis_name="c", subcore_axis_name="s",
                                  num_cores=SC_CORES, num_subcores=SC_SUBCORES)

    def body(src_hbm, idx_hbm, out_hbm, idx_v, rows0, rows1, lsem0, lsem1, ssem0, ssem1):
        worker = lax.axis_index("s") * SC_CORES + lax.axis_index("c")
        base = worker * per_worker
        pltpu.sync_copy(idx_hbm.at[worker], idx_v)
        bufs = (rows0, rows1)
        lsems = (lsem0, lsem1)
        ssems = (ssem0, ssem1)

        def load(c, slot):
            return pltpu.make_async_copy(src_hbm.at[pl.ds(base + c * chunk, chunk)], bufs[slot], lsems[slot])

        def scatter(c, k, slot):
            return pltpu.make_async_copy(bufs[slot], out_hbm.at[idx_v.at[k * n_chunks + c]], ssems[slot])

        load(0, 0).start()

        @pl.loop(0, n_chunks, step=2)
        def _(c0):
            for slot in range(2):
                c = c0 + slot

                @pl.when(c + 1 < n_chunks)
                def _():
                    @pl.when(c >= 1)
                    def _():
                        for k in range(fan):
                            scatter(c - 1, k, 1 - slot).wait()
                    load(c + 1, 1 - slot).start()

                load(c, slot).wait()
                for k in range(fan):
                    scatter(c, k, slot).start()

        for k in range(fan):
            scatter(n_chunks - 2, k, 0).wait()
        for k in range(fan):
            scatter(n_chunks - 1, k, 1).wait()

    idx = dest.reshape(SC_WORKERS, n_chunks, chunk, fan).transpose(0, 3, 1, 2)
    return pl.kernel(
        body,
        out_type=jax.ShapeDtypeStruct((out_rows, width), src.dtype),
        mesh=mesh,
        scratch_types=[pltpu.VMEM((fan * n_chunks, chunk), I32),
                       pltpu.VMEM((chunk, width), src.dtype), pltpu.VMEM((chunk, width), src.dtype),
                       pltpu.SemaphoreType.DMA, pltpu.SemaphoreType.DMA,
                       pltpu.SemaphoreType.DMA, pltpu.SemaphoreType.DMA],
        name=name,
    )(src, idx.reshape(SC_WORKERS, fan * n_chunks, chunk))


def _expert_kernel(be_ref, nu_ref, xs_ref, wgu_ref, bgu_ref, wd_ref, bd_ref, y_ref, wgu_sc, wd_sc):
    i = pl.program_id(0)
    e = be_ref[i]
    e_prev = be_ref[jnp.maximum(i - 1, 0)]
    d = wd_sc.shape[0]

    @pl.when((i == 0) | (e != e_prev))
    def _():
        for r in range(0, d, LANES):
            wgu_sc[r:r + LANES, :] = wgu_ref[0, r:r + LANES, :].astype(BF16)
            wd_sc[r:r + LANES, :] = wd_ref[0, r:r + LANES, :].astype(BF16)

    @pl.when(i < nu_ref[0])
    def _():
        x = _unpack_bf16_pairs(xs_ref[...])
        h = jnp.dot(x, wgu_sc[...], preferred_element_type=F32) + bgu_ref[0]
        glu = jnp.minimum(h[:, :d], SWIGLU_LIMIT)
        lin = jnp.clip(h[:, d:], -SWIGLU_LIMIT, SWIGLU_LIMIT)
        act = glu * _sigmoid(SWIGLU_ALPHA * glu) * (lin + 1.0)
        y = jnp.dot(act.astype(BF16), wd_sc[...], preferred_element_type=F32) + bd_ref[0]
        y_ref[...] = _pack_bf16_pairs(y)

    @pl.when(i >= nu_ref[0])
    def _():
        y_ref[...] = jnp.zeros_like(y_ref)


def _experts(block_e, n_used, xs, w_gu, b_gu, w_down, b_down, layer):
    cap = xs.shape[0]
    d = D_MODEL
    tm = EXPERT_TILE
    n_blocks = cap // tm
    grid_spec = pltpu.PrefetchScalarGridSpec(
        num_scalar_prefetch=2,
        grid=(n_blocks,),
        in_specs=[
            pl.BlockSpec((tm, d // 2), lambda i, be, nu: (i, 0)),
            pl.BlockSpec((None, 1, d, 2 * d), lambda i, be, nu: (layer, be[i], 0, 0)),
            pl.BlockSpec((1, 1, 2 * d), lambda i, be, nu: (be[i], 0, 0)),
            pl.BlockSpec((None, 1, d, d), lambda i, be, nu: (layer, be[i], 0, 0)),
            pl.BlockSpec((1, 1, d), lambda i, be, nu: (be[i], 0, 0)),
        ],
        out_specs=pl.BlockSpec((tm, d // 2), lambda i, be, nu: (i, 0)),
        scratch_shapes=[pltpu.VMEM((d, 2 * d), BF16), pltpu.VMEM((d, d), BF16)],
    )
    return pl.pallas_call(
        _expert_kernel,
        out_shape=jax.ShapeDtypeStruct((cap, d // 2), I32),
        grid_spec=grid_spec,
        compiler_params=_params("arbitrary"),
        name="moe_experts",
    )(block_e, n_used, xs, w_gu, b_gu[layer][:, None, :], w_down, b_down[layer][:, None, :])


def _combine_kernel(gate_ref, res_ref, yk_ref, g_ref, b_ref, o_ref):
    d = res_ref.shape[1]
    gates = gate_ref[...]
    ffn = gates[:, 0:1] * _unpack_bf16_pairs(yk_ref[0], F32)
    for k in range(1, TOP_K):
        ffn = ffn + gates[:, k:k + 1] * _unpack_bf16_pairs(yk_ref[k], F32)
    o_ref[...] = _layer_norm(DN_ALPHA * res_ref[...] + ffn, g_ref[...], b_ref[...])


def _combine(gates, res, yk, g, b):
    n, d = res.shape
    tt = MOVE_TILE
    vec = pl.BlockSpec((1, d), lambda i: (0, 0))
    return pl.pallas_call(
        _combine_kernel,
        out_shape=jax.ShapeDtypeStruct((n, d), F32),
        grid=(n // tt,),
        in_specs=[pl.BlockSpec((tt, TOP_K), lambda i: (i, 0)),
                  pl.BlockSpec((tt, d), lambda i: (i, 0)),
                  pl.BlockSpec((TOP_K, tt, d // 2), lambda i: (0, i, 0)),
                  vec, vec],
        out_specs=pl.BlockSpec((tt, d), lambda i: (i, 0)),
        compiler_params=_params("parallel"),
        name="moe_combine",
    )(gates, res, yk, g, b)


def _moe_layer(h, router_w, router_b, w_gu, b_gu, w_down, b_down, ln_g, ln_b, layer):
    n, d = h.shape
    tm = EXPERT_TILE
    idx, gates, rank, counts, h_packed = _router(h, router_w, router_b[None, :])
    counts = counts[0]
    padded = (counts + tm - 1) // tm * tm
    pad_end = jnp.cumsum(padded)
    pad_start = pad_end - padded
    group_start = jnp.sum(jnp.where(idx[:, :, None] == jnp.arange(N_EXPERTS, dtype=I32), pad_start, 0), axis=-1)
    dest = group_start + rank
    n_blocks = -(-(n * TOP_K + N_EXPERTS * (tm - 1)) // tm)
    block_start = jnp.arange(n_blocks, dtype=I32) * tm
    block_e = jnp.minimum(jnp.sum((pad_end[None, :] <= block_start[:, None]).astype(I32), axis=1),
                          N_EXPERTS - 1)
    n_used = (pad_end[-1:] // tm).astype(I32)

    xs = _sc_scatter_rows(h_packed, dest, n_blocks * tm, "moe_dispatch")
    ys = _experts(block_e, n_used, xs, w_gu, b_gu, w_down, b_down, layer)
    yk = _sc_gather_rows(ys, dest.T.reshape(n * TOP_K), "moe_collect").reshape(TOP_K, n, d // 2)
    return _combine(gates, h, yk, ln_g[None, :], ln_b[None, :])


def _qkv_kernel(x_ref, w_ref, o_ref):
    xb = x_ref[...].astype(BF16)
    d = x_ref.shape[1]
    for part in range(3):
        acc = jnp.dot(xb, w_ref[:, part * d:(part + 1) * d], preferred_element_type=F32)
        if part == 0:
            acc = acc * (NA_HEAD_DIM ** -0.5)
        for p in range(HEAD_PAIRS):
            o_ref[part * HEAD_PAIRS + p] = acc[:, p * LANES:(p + 1) * LANES].astype(BF16)


def _qkv(x, w_bf16):
    n, d = x.shape
    return pl.pallas_call(
        _qkv_kernel,
        out_shape=jax.ShapeDtypeStruct((3 * HEAD_PAIRS, n, LANES), BF16),
        grid=(n // ROW_TILE,),
        in_specs=[pl.BlockSpec((ROW_TILE, d), lambda i: (i, 0)),
                  pl.BlockSpec((d, 3 * d), lambda i: (0, 0))],
        out_specs=pl.BlockSpec((3 * HEAD_PAIRS, ROW_TILE, LANES), lambda i: (0, i, 0)),
        compiler_params=_params("parallel"),
        name="na_qkv",
    )(x, w_bf16)


def _na_kernel(q_ref, k_ref, v_ref, bias_ref, mb_ref, o_ref, *, rows):
    s = pl.program_id(1)
    w = GRID_W
    low = lax.broadcasted_iota(I32, (w, LANES), 1) < NA_HEAD_DIM
    contract_last = (((1,), (1,)), ((), ()))
    meta0 = FRONT_PAD

    @pl.when(s == 0)
    def _():
        o_ref[...] = jnp.zeros_like(o_ref)

    def stacked_q(p):
        qp = q_ref[p]
        zero = jnp.zeros_like(qp)
        return jnp.concatenate([jnp.where(low, qp, zero), jnp.where(low, zero, qp)], axis=0)

    def attend_all(k0):
        scores = []
        for p in range(HEAD_PAIRS):
            q2 = stacked_q(p)
            s_meta = lax.dot_general(q2, k_ref[p, meta0:meta0 + N_META, :], contract_last,
                                     preferred_element_type=F32) + mb_ref[p]
            s_win = None
            if k0 is not None:
                s_win = lax.dot_general(q2, k_ref[p, pl.ds(k0, NA_KH * w), :], contract_last,
                                        preferred_element_type=F32) + bias_ref[0, p]
            scores.append((s_meta, s_win))
        outs = []
        for p in range(HEAD_PAIRS):
            s_meta, s_win = scores[p]
            m = jnp.max(s_meta, axis=-1, keepdims=True)
            if s_win is not None:
                m = jnp.maximum(m, jnp.max(s_win, axis=-1, keepdims=True))
            p_meta = jnp.exp(s_meta - m)
            denom = jnp.sum(p_meta, axis=-1, keepdims=True)
            o = jnp.dot(p_meta.astype(BF16), v_ref[p, meta0:meta0 + N_META, :], preferred_element_type=F32)
            if s_win is not None:
                p_win = jnp.exp(s_win - m)
                denom = denom + jnp.sum(p_win, axis=-1, keepdims=True)
                o = o + jnp.dot(p_win.astype(BF16), v_ref[p, pl.ds(k0, NA_KH * w), :],
                                preferred_element_type=F32)
            o = o / denom
            outs.append(jnp.where(low, o[:w], o[w:]))
        return outs

    @pl.when(s == 1)
    def _():
        qrow = lax.broadcasted_iota(I32, (w, LANES), 0)
        for p, o in enumerate(attend_all(None)):
            o_ref[p] = jnp.where(qrow >= w - N_META, o, 0.0).astype(o_ref.dtype)

    @pl.when(s >= 2)
    def _():
        r = s - 2
        rs = jnp.clip(r - NA_KH // 2, 0, rows - NA_KH)
        k0 = pl.multiple_of(LANES + rs * w, w)
        for p, o in enumerate(attend_all(k0)):
            o_ref[p] = o.astype(o_ref.dtype)


def _na_bias_table(rpb, rows):
    del rows
    w = GRID_W
    q = jnp.arange(w)
    col_start = jnp.clip(q - NA_KW // 2, 0, w - NA_KW)
    c = jnp.arange(w)
    in_win = (c[None, :] >= col_start[:, None]) & (c[None, :] < col_start[:, None] + NA_KW)
    pad = w - NA_KW
    rp = jnp.pad(rpb.astype(F32), ((0, 0), (0, 0), (pad, pad)))
    toeplitz = jnp.stack([rp[:, :, w - 1 - qq:2 * w - 1 - qq] for qq in range(w)], axis=2)
    toeplitz = jnp.where(in_win[None, None], toeplitz, NEG_BIG)
    tabs = []
    for v in range(NA_KH):
        tv = toeplitz[:, NA_KH - 1 - v:2 * NA_KH - 1 - v]
        tabs.append(jnp.transpose(tv, (0, 2, 1, 3)).reshape(NA_HEADS, w, NA_KH * w))
    return jnp.stack(tabs, axis=0)


def _na_attention(qkv, bias_tab, meta_bias, *, batch, lp):
    n = qkv.shape[1]
    w = GRID_W
    rows = (lp - LANES) // w
    tiles = lp // w

    def variant(b, s):
        r = jnp.clip(s - 2, 0, rows - 1)
        return r - jnp.clip(r - NA_KH // 2, 0, rows - NA_KH)

    kern = functools.partial(_na_kernel, rows=rows)
    return pl.pallas_call(
        kern,
        out_shape=jax.ShapeDtypeStruct((HEAD_PAIRS, n, LANES), BF16),
        grid=(batch, tiles),
        in_specs=[
            pl.BlockSpec((HEAD_PAIRS, w, LANES), lambda b, s: (0, b * tiles + s, 0)),
            pl.BlockSpec((HEAD_PAIRS, lp, LANES), lambda b, s: (1, b, 0)),
            pl.BlockSpec((HEAD_PAIRS, lp, LANES), lambda b, s: (2, b, 0)),
            pl.BlockSpec((1, HEAD_PAIRS, 2 * w, NA_KH * w), lambda b, s: (variant(b, s), 0, 0, 0)),
            pl.BlockSpec((HEAD_PAIRS, 2 * w, N_META), lambda b, s: (0, 0, 0)),
        ],
        out_specs=pl.BlockSpec((HEAD_PAIRS, w, LANES), lambda b, s: (0, b * tiles + s, 0)),
        compiler_params=_params("parallel", "arbitrary"),
        name="na_attention",
    )(qkv, qkv, qkv,
      bias_tab.reshape(NA_KH, HEAD_PAIRS, 2 * w, NA_KH * w),
      jnp.repeat(meta_bias, w, axis=0).reshape(HEAD_PAIRS, 2 * w, N_META))


def _na_out_kernel(o_ref, res_ref, w_ref, g_ref, b_ref, out_ref):
    att = jnp.concatenate([o_ref[p] for p in range(HEAD_PAIRS)], axis=-1)
    mix = jnp.dot(att, w_ref[...], preferred_element_type=F32)
    out_ref[...] = _layer_norm(DN_ALPHA * res_ref[...] + mix, g_ref[...], b_ref[...])


def _na_out(o, res, w_bf16, g, b):
    n, d = res.shape
    vec = pl.BlockSpec((1, d), lambda i: (0, 0))
    return pl.pallas_call(
        _na_out_kernel,
        out_shape=jax.ShapeDtypeStruct((n, d), F32),
        grid=(n // ROW_TILE,),
        in_specs=[pl.BlockSpec((HEAD_PAIRS, ROW_TILE, LANES), lambda i: (0, i, 0)),
                  pl.BlockSpec((ROW_TILE, d), lambda i: (i, 0)),
                  pl.BlockSpec((d, d), lambda i: (0, 0)), vec, vec],
        out_specs=pl.BlockSpec((ROW_TILE, d), lambda i: (i, 0)),
        compiler_params=_params("parallel"),
        name="na_out",
    )(o, res, w_bf16, g, b)


def kernel(x, meta_tokens, lru_w_in, lru_conv_w, lru_conv_b, lru_wa, lru_ba, lru_wx, lru_bx, lru_lambda, lru_w_out, na_w_qkv, na_rpb, na_meta_bias, na_w_out, ln_mix_g, ln_mix_b, router_w, router_b, moe_w_gu, moe_b_gu, moe_w_down, moe_b_down, ln_ffn_g, ln_ffn_b):
    batch, seq, d = x.shape
    lp = LANES + seq
    assert d == D_MODEL and seq % GRID_W == 0 and seq // GRID_W >= NA_KH
    assert lp % SCAN_CHUNK == 0 and (batch * lp) % ROW_TILE == 0
    n = batch * lp

    front = jnp.zeros((batch, FRONT_PAD, d), x.dtype)
    meta = jnp.broadcast_to(meta_tokens[None].astype(x.dtype), (batch, N_META, d))
    h = jnp.concatenate([front, meta, x], axis=1).reshape(n, d)

    u = _matmul(h, lru_w_in[0].astype(BF16))
    row = lambda v: v[None, :]
    scans = []
    for direction, reverse in ((0, False), (1, True)):
        scans.append(_lru_scan(
            u, lru_conv_w[0], row(lru_conv_b[0]),
            lru_wa[0, direction].astype(BF16), row(lru_ba[0, direction]),
            lru_wx[0, direction].astype(BF16), row(lru_bx[0, direction]),
            row(lru_lambda[0, direction]), reverse=reverse, lp=lp))
    h = _lru_out(scans[0], scans[1], u, h, lru_w_out[0].astype(BF16), row(ln_mix_g[0]), row(ln_mix_b[0]))
    h = _moe_layer(h, router_w[0], router_b[0], moe_w_gu, moe_b_gu, moe_w_down, moe_b_down,
                   ln_ffn_g[0], ln_ffn_b[0], 0)

    qkv = _qkv(h, na_w_qkv[0].astype(BF16))
    att = _na_attention(qkv, _na_bias_table(na_rpb[0], seq // GRID_W), na_meta_bias[0].astype(F32),
                        batch=batch, lp=lp)
    h = _na_out(att, h, na_w_out[0].astype(BF16), row(ln_mix_g[1]), row(ln_mix_b[1]))
    h = _moe_layer(h, router_w[1], router_b[1], moe_w_gu, moe_b_gu, moe_w_down, moe_b_down,
                   ln_ffn_g[1], ln_ffn_b[1], 1)

    return h.reshape(batch, lp, d)[:, LANES:]
```

```python
import functools

import jax
import jax.numpy as jnp
from jax import lax
from jax.experimental import pallas as pl
from jax.experimental.pallas import tpu as pltpu
from jax.experimental.pallas import tpu_sc as plsc

F32 = jnp.float32
BF16 = jnp.bfloat16
I32 = jnp.int32
U32 = jnp.uint32

D_MODEL = 1024
N_META = 16
GRID_W = 64
LRU_BLOCKS = 4
LRU_BLOCK = D_MODEL // LRU_BLOCKS
CONV_W = 4
LRU_C = 8.0
NA_HEADS = 16
NA_HEAD_DIM = D_MODEL // NA_HEADS
NA_KH = 8
NA_KW = 16
N_EXPERTS = 32
TOP_K = 4
SWIGLU_LIMIT = 7.0
SWIGLU_ALPHA = 1.702
DEPTH = 2
DN_ALPHA = (2.0 * DEPTH) ** 0.25
LN_EPS = 1e-5

LANES = 128
SUBLANES = 8
FRONT_PAD = LANES - N_META
HEAD_PAIRS = D_MODEL // LANES
NEG_BIG = -1e30

ROW_TILE = 512
SCAN_CHUNK = 384
ROUTE_TILE = 512
MOVE_TILE = 256
EXPERT_TILE = 512
VMEM_LIMIT = 56 << 20

SC_CORES = 2
SC_SUBCORES = 16
SC_WORKERS = SC_CORES * SC_SUBCORES
SC_MAX_INDICES = 64
SC_ROW_BUFFER_BYTES = 128 << 10


def _params(*sem):
    return pltpu.CompilerParams(dimension_semantics=sem, vmem_limit_bytes=VMEM_LIMIT)


def _sigmoid(x):
    return 0.5 * jnp.tanh(0.5 * x) + 0.5


def _layer_norm(x, g, b):
    mu = jnp.mean(x, axis=-1, keepdims=True)
    xc = x - mu
    var = jnp.mean(xc * xc, axis=-1, keepdims=True)
    return xc * lax.rsqrt(var + LN_EPS) * g + b


def _matmul_kernel(x_ref, w_ref, o_ref):
    o_ref[...] = jnp.dot(x_ref[...].astype(BF16), w_ref[...], preferred_element_type=F32)


def _matmul(x, w_bf16):
    n, k = x.shape
    m = w_bf16.shape[1]
    return pl.pallas_call(
        _matmul_kernel,
        out_shape=jax.ShapeDtypeStruct((n, m), F32),
        grid=(n // ROW_TILE,),
        in_specs=[pl.BlockSpec((ROW_TILE, k), lambda i: (i, 0)),
                  pl.BlockSpec((k, m), lambda i: (0, 0))],
        out_specs=pl.BlockSpec((ROW_TILE, m), lambda i: (i, 0)),
        compiler_params=_params("parallel"),
        name="in_proj",
    )(x, w_bf16)


def _lru_scan_kernel(xr_ref, xp_ref, xn_ref, cw_ref, cb_ref, wa_ref, ba_ref, wx_ref, bx_ref,
                     lam_ref, h_ref, a_sc, b_sc, carry_sc, *, reverse, chunks_per_batch, lp):
    t_rows = xr_ref.shape[0]
    step = pl.program_id(0)
    n_steps = pl.num_programs(0)
    chunk = (n_steps - 1 - step) if reverse else step
    cib = chunk % chunks_per_batch
    pos = cib * t_rows + lax.broadcasted_iota(I32, (t_rows, 1), 0)
    row = lax.broadcasted_iota(I32, (t_rows, 1), 0)
    real = pos >= FRONT_PAD

    first_chunk = cib == 0
    last_chunk = cib == chunks_per_batch - 1

    xr = jnp.where(real, xr_ref[...], 0.0)
    prev_pos = cib * t_rows - SUBLANES + lax.broadcasted_iota(I32, (SUBLANES, 1), 0)
    prev = jnp.where(prev_pos >= FRONT_PAD, xp_ref[...], 0.0)
    nxt = jnp.where(last_chunk, 0.0, xn_ref[...])

    xm1 = jnp.where(row == 0, prev[7:8, :], pltpu.roll(xr, 1, 0))
    xm2 = pltpu.roll(xr, 2, 0)
    xm2 = jnp.where(row == 0, prev[6:7, :], xm2)
    xm2 = jnp.where(row == 1, prev[7:8, :], xm2)
    xp1 = jnp.where(row == t_rows - 1, nxt[0:1, :], pltpu.roll(xr, t_rows - 1, 0))
    cw = cw_ref[...]
    xc = cw[0:1, :] * xm2 + cw[1:2, :] * xm1 + cw[2:3, :] * xr + cw[3:4, :] * xp1 + cb_ref[...]

    xcb = xc.astype(BF16)
    ga = []
    gx = []
    for blk in range(LRU_BLOCKS):
        xblk = xcb[:, blk * LRU_BLOCK:(blk + 1) * LRU_BLOCK]
        ga.append(jnp.dot(xblk, wa_ref[blk], preferred_element_type=F32))
        gx.append(jnp.dot(xblk, wx_ref[blk], preferred_element_type=F32))
    gate_a = _sigmoid(jnp.concatenate(ga, axis=-1) + ba_ref[...])
    gate_x = _sigmoid(jnp.concatenate(gx, axis=-1) + bx_ref[...])

    z = -lam_ref[...]
    softplus = jnp.maximum(z, 0.0) + jnp.log1p(jnp.exp(-jnp.abs(z)))
    log_a = (-LRU_C * gate_a) * softplus
    a = jnp.exp(log_a)
    mult = jnp.sqrt(-jnp.tanh(log_a) * (a * a + 1.0))
    start = lp - 1 if reverse else FRONT_PAD
    mult = jnp.where(pos == start, 1.0, mult)
    b = jnp.where(real, mult * (gate_x * xc), 0.0)
    a_sc[...] = a
    b_sc[...] = b

    @pl.when(last_chunk if reverse else first_chunk)
    def _():
        carry_sc[...] = jnp.zeros_like(carry_sc)

    row8 = lax.broadcasted_iota(I32, (SUBLANES, 1), 0)
    n_groups = t_rows // SUBLANES

    def group(i, carry):
        g = (n_groups - 1 - i) if reverse else i
        r0 = pl.multiple_of(g * SUBLANES, SUBLANES)
        av = a_sc[pl.ds(r0, SUBLANES), :]
        bv = b_sc[pl.ds(r0, SUBLANES), :]
        for s in (1, 2, 4):
            shift = SUBLANES - s if reverse else s
            keep = (row8 < SUBLANES - s) if reverse else (row8 >= s)
            a_sh = pltpu.roll(av, shift, 0)
            b_sh = pltpu.roll(bv, shift, 0)
            bv = jnp.where(keep, av * b_sh + bv, bv)
            av = jnp.where(keep, av * a_sh, av)
        hv = av * carry + bv
        h_ref[pl.ds(r0, SUBLANES), :] = hv
        return hv[0:1, :] if reverse else hv[SUBLANES - 1:SUBLANES, :]

    carry_sc[...] = lax.fori_loop(0, n_groups, group, carry_sc[...])


def _lru_scan(u, cw, cb, wa_bf16, ba, wx_bf16, bx, lam, *, reverse, lp):
    n = u.shape[0]
    d = D_MODEL
    t = SCAN_CHUNK
    n_chunks = n // t
    cpb = lp // t
    t8 = t // SUBLANES
    n8 = n // SUBLANES

    def chunk_of(i):
        return (n_chunks - 1 - i) if reverse else i

    kern = functools.partial(_lru_scan_kernel, reverse=reverse, chunks_per_batch=cpb, lp=lp)
    full2 = lambda shape: pl.BlockSpec(shape, lambda i: (0, 0))
    full3 = lambda shape: pl.BlockSpec(shape, lambda i: (0, 0, 0))
    return pl.pallas_call(
        kern,
        out_shape=jax.ShapeDtypeStruct((n, d), F32),
        grid=(n_chunks,),
        in_specs=[
            pl.BlockSpec((t, d), lambda i: (chunk_of(i), 0)),
            pl.BlockSpec((SUBLANES, d), lambda i: (jnp.maximum(chunk_of(i) * t8 - 1, 0), 0)),
            pl.BlockSpec((SUBLANES, d), lambda i: (jnp.minimum((chunk_of(i) + 1) * t8, n8 - 1), 0)),
            full2((CONV_W, d)), full2((1, d)),
            full3((LRU_BLOCKS, LRU_BLOCK, LRU_BLOCK)), full2((1, d)),
            full3((LRU_BLOCKS, LRU_BLOCK, LRU_BLOCK)), full2((1, d)),
            full2((1, d)),
        ],
        out_specs=pl.BlockSpec((t, d), lambda i: (chunk_of(i), 0)),
        scratch_shapes=[pltpu.VMEM((t, d), F32), pltpu.VMEM((t, d), F32), pltpu.VMEM((1, d), F32)],
        compiler_params=_params("arbitrary"),
        name="lru_scan_bwd" if reverse else "lru_scan_fwd",
    )(u, u, u, cw, cb, wa_bf16, ba, wx_bf16, bx, lam)


def _gelu_tanh(y):
    c = 0.7978845608028654
    return y * (0.5 * (1.0 + jnp.tanh(c * (y + 0.044715 * (y * y * y)))))


def _lru_out_kernel(hf_ref, hb_ref, y_ref, res_ref, w_ref, g_ref, b_ref, o_ref):
    gated = (hf_ref[...] + hb_ref[...]) * _gelu_tanh(y_ref[...])
    mix = jnp.dot(gated.astype(BF16), w_ref[...], preferred_element_type=F32)
    o_ref[...] = _layer_norm(DN_ALPHA * res_ref[...] + mix, g_ref[...], b_ref[...])


def _lru_out(hf, hb, u, res, w_bf16, g, b):
    n, d = res.shape
    row = lambda j: pl.BlockSpec((ROW_TILE, d), lambda i, j=j: (i, j))
    vec = pl.BlockSpec((1, d), lambda i: (0, 0))
    return pl.pallas_call(
        _lru_out_kernel,
        out_shape=jax.ShapeDtypeStruct((n, d), F32),
        grid=(n // ROW_TILE,),
        in_specs=[row(0), row(0), row(1), row(0), pl.BlockSpec((d, d), lambda i: (0, 0)), vec, vec],
        out_specs=row(0),
        compiler_params=_params("parallel"),
        name="lru_out",
    )(hf, hb, u, res, w_bf16, g, b)


def _pack_bf16_pairs(x):
    half = x.shape[1] // 2
    lo = pltpu.bitcast(x[:, :half].astype(BF16).astype(F32), U32)
    hi = pltpu.bitcast(x[:, half:].astype(BF16).astype(F32), U32)
    return pltpu.bitcast((lo >> 16) | (hi & jnp.uint32(0xFFFF0000)), I32)


def _unpack_bf16_pairs(packed, dtype=BF16):
    u = pltpu.bitcast(packed, U32)
    lo = pltpu.bitcast(u << 16, F32).astype(dtype)
    hi = pltpu.bitcast(u & jnp.uint32(0xFFFF0000), F32).astype(dtype)
    return jnp.concatenate([lo, hi], axis=-1)


def _router_kernel(h_ref, wt_ref, b_ref, idx_ref, gate_ref, rank_ref, cnt_ref, hp_ref, base_sc):
    tm = h_ref.shape[0]
    h = h_ref[...]
    hp_ref[...] = _pack_bf16_pairs(h)

    @pl.when(pl.program_id(0) == 0)
    def _():
        base_sc[...] = jnp.zeros_like(base_sc)

    wt = wt_ref[...]
    h_hi = h.astype(BF16)
    h_lo = (h - h_hi.astype(F32)).astype(BF16)
    w_hi = wt.astype(BF16)
    w_lo = (wt - w_hi.astype(F32)).astype(BF16)
    contract_last = (((1,), (1,)), ((), ()))
    mm = lambda a, b: lax.dot_general(a, b, contract_last, preferred_element_type=F32)
    logits = mm(w_hi, h_hi) + (mm(w_hi, h_lo) + mm(w_lo, h_hi)) + b_ref[...]

    expert = lax.broadcasted_iota(I32, (N_EXPERTS, tm), 0).astype(F32)
    vals = logits
    idx_rows, val_rows, onehots = [], [], []
    for _ in range(TOP_K):
        m = jnp.max(vals, axis=0, keepdims=True)
        idx = jnp.min(jnp.where(vals == m, expert, float(N_EXPERTS)), axis=0, keepdims=True)
        hit = expert == idx
        onehots.append(jnp.where(hit, 1.0, 0.0))
        idx_rows.append(idx)
        val_rows.append(m)
        vals = jnp.where(hit, -jnp.inf, vals)

    onehot_all = jnp.concatenate(onehots, axis=0)
    t_from = lax.broadcasted_iota(I32, (tm, tm), 0)
    t_to = lax.broadcasted_iota(I32, (tm, tm), 1)
    earlier = jnp.where(t_from < t_to, 1.0, 0.0).astype(BF16)
    before = jnp.dot(onehot_all.astype(BF16), earlier, preferred_element_type=F32)
    base = base_sc[...]
    starts = []
    for k in range(TOP_K):
        starts.append(base)
        base = base + jnp.sum(onehots[k], axis=1, keepdims=True)
    contrib = onehot_all * (jnp.concatenate(starts, axis=0) + before)
    rank_rows = [jnp.sum(contrib[k * N_EXPERTS:(k + 1) * N_EXPERTS], axis=0, keepdims=True)
                 for k in range(TOP_K)]

    top_vals = jnp.concatenate(val_rows, axis=0)
    e = jnp.exp(top_vals - top_vals[0:1])
    idx_ref[...] = jnp.concatenate(idx_rows, axis=0).astype(I32)
    gate_ref[...] = e / jnp.sum(e, axis=0, keepdims=True)
    rank_ref[...] = jnp.concatenate(rank_rows, axis=0).astype(I32)
    base_sc[...] = base
    cnt_ref[...] = base.astype(I32)


def _router(h, w, b):
    n, d = h.shape
    tm = ROUTE_TILE
    out4 = lambda dt: jax.ShapeDtypeStruct((TOP_K, n), dt)
    spec4 = pl.BlockSpec((TOP_K, tm), lambda i: (0, i))
    return pl.pallas_call(
        _router_kernel,
        out_shape=(out4(I32), out4(F32), out4(I32), jax.ShapeDtypeStruct((N_EXPERTS, 1), I32),
                   jax.ShapeDtypeStruct((n, d // 2), I32)),
        grid=(n // tm,),
        in_specs=[pl.BlockSpec((tm, d), lambda i: (i, 0)),
                  pl.BlockSpec((N_EXPERTS, d), lambda i: (0, 0)),
                  pl.BlockSpec((N_EXPERTS, 1), lambda i: (0, 0))],
        out_specs=(spec4, spec4, spec4, pl.BlockSpec((N_EXPERTS, 1), lambda i: (0, 0)),
                   pl.BlockSpec((tm, d // 2), lambda i: (i, 0))),
        scratch_shapes=[pltpu.VMEM((N_EXPERTS, 1), F32)],
        compiler_params=_params("arbitrary"),
        name="router",
    )(h, w.T, b[:, None])


def _sc_gather_rows(table, idx, name):
    rows, width = table.shape
    total = idx.shape[0]
    chunk = min(SC_MAX_INDICES, SC_ROW_BUFFER_BYTES // (width * table.dtype.itemsize))
    per_worker = total // SC_WORKERS
    n_chunks = per_worker // chunk
    assert total == SC_WORKERS * n_chunks * chunk and n_chunks % 2 == 0
    mesh = plsc.VectorSubcoreMesh(core_axis_name="c", subcore_axis_name="s",
                                  num_cores=SC_CORES, num_subcores=SC_SUBCORES)

    def body(table_hbm, idx_hbm, out_hbm, idx_v, rows0, rows1, gsem0, gsem1, psem0, psem1):
        worker = lax.axis_index("s") * SC_CORES + lax.axis_index("c")
        base = worker * per_worker
        pltpu.sync_copy(idx_hbm.at[worker], idx_v)
        bufs = (rows0, rows1)
        gsems = (gsem0, gsem1)
        psems = (psem0, psem1)

        def gather(c, slot):
            return pltpu.make_async_copy(table_hbm.at[idx_v.at[c]], bufs[slot], gsems[slot])

        def put(c, slot):
            return pltpu.make_async_copy(bufs[slot], out_hbm.at[pl.ds(base + c * chunk, chunk)], psems[slot])

        gather(0, 0).start()

        @pl.loop(0, n_chunks, step=2)
        def _(c0):
            for slot in range(2):
                c = c0 + slot

                @pl.when(c + 1 < n_chunks)
                def _():
                    @pl.when(c >= 1)
                    def _():
                        put(c - 1, 1 - slot).wait()
                    gather(c + 1, 1 - slot).start()

                gather(c, slot).wait()
                put(c, slot).start()

        put(n_chunks - 2, 0).wait()
        put(n_chunks - 1, 1).wait()

    return pl.kernel(
        body,
        out_type=jax.ShapeDtypeStruct((total, width), table.dtype),
        mesh=mesh,
        scratch_types=[pltpu.VMEM((n_chunks, chunk), I32),
                       pltpu.VMEM((chunk, width), table.dtype), pltpu.VMEM((chunk, width), table.dtype),
                       pltpu.SemaphoreType.DMA, pltpu.SemaphoreType.DMA,
                       pltpu.SemaphoreType.DMA, pltpu.SemaphoreType.DMA],
        name=name,
    )(table, idx.reshape(SC_WORKERS, n_chunks, chunk))


def _sc_scatter_rows(src, dest, out_rows, name):
    n, width = src.shape
    fan = dest.shape[0]
    per_worker = n // SC_WORKERS
    chunk = max(c for c in range(SUBLANES, SC_MAX_INDICES + 1, SUBLANES) if per_worker % (2 * c) == 0)
    n_chunks = per_worker // chunk
    assert n == SC_WORKERS * n_chunks * chunk and n_chunks % 2 == 0
    mesh = plsc.VectorSubcoreMesh(core_axis_name="c", subcore_axis_name="s",
                                  num_cores=SC_CORES, num_subcores=SC_SUBCORES)

    def body(src_hbm, idx_hbm, out_hbm, idx_v, rows0, rows1, lsem0, lsem1, ssem0, ssem1):
        worker = lax.axis_index("s") * SC_CORES + lax.axis_index("c")
        base = worker * per_worker
        pltpu.sync_copy(idx_hbm.at[worker], idx_v)
        bufs = (rows0, rows1)
        lsems = (lsem0, lsem1)
        ssems = (ssem0, ssem1)

        def load(c, slot):
            return pltpu.make_async_copy(src_hbm.at[pl.ds(base + c * chunk, chunk)], bufs[slot], lsems[slot])

        def scatter(c, k, slot):
            return pltpu.make_async_copy(bufs[slot], out_hbm.at[idx_v.at[k * n_chunks + c]], ssems[slot])

        load(0, 0).start()

        @pl.loop(0, n_chunks, step=2)
        def _(c0):
            for slot in range(2):
                c = c0 + slot

                @pl.when(c + 1 < n_chunks)
                def _():
                    @pl.when(c >= 1)
                    def _():
                        for k in range(fan):
                            scatter(c - 1, k, 1 - slot).wait()
                    load(c + 1, 1 - slot).start()

                load(c, slot).wait()
                for k in range(fan):
                    scatter(c, k, slot).start()

        for k in range(fan):
            scatter(n_chunks - 2, k, 0).wait()
        for k in range(fan):
            scatter(n_chunks - 1, k, 1).wait()

    idx = dest.reshape(fan, SC_WORKERS, n_chunks, chunk).transpose(1, 0, 2, 3)
    return pl.kernel(
        body,
        out_type=jax.ShapeDtypeStruct((out_rows, width), src.dtype),
        mesh=mesh,
        scratch_types=[pltpu.VMEM((fan * n_chunks, chunk), I32),
                       pltpu.VMEM((chunk, width), src.dtype), pltpu.VMEM((chunk, width), src.dtype),
                       pltpu.SemaphoreType.DMA, pltpu.SemaphoreType.DMA,
                       pltpu.SemaphoreType.DMA, pltpu.SemaphoreType.DMA],
        name=name,
    )(src, idx.reshape(SC_WORKERS, fan * n_chunks, chunk))


def _expert_kernel(be_ref, nu_ref, xs_ref, wgu_ref, bgu_ref, wd_ref, bd_ref, y_ref, wgu_sc, wd_sc):
    i = pl.program_id(0)
    e = be_ref[i]
    e_prev = be_ref[jnp.maximum(i - 1, 0)]
    d = wd_sc.shape[0]

    @pl.when((i == 0) | (e != e_prev))
    def _():
        for r in range(0, d, LANES):
            wgu_sc[r:r + LANES, :] = wgu_ref[0, r:r + LANES, :].astype(BF16)
            wd_sc[r:r + LANES, :] = wd_ref[0, r:r + LANES, :].astype(BF16)

    @pl.when(i < nu_ref[0])
    def _():
        x = _unpack_bf16_pairs(xs_ref[...])
        h = jnp.dot(x, wgu_sc[...], preferred_element_type=F32) + bgu_ref[0]
        glu = jnp.minimum(h[:, :d], SWIGLU_LIMIT)
        lin = jnp.clip(h[:, d:], -SWIGLU_LIMIT, SWIGLU_LIMIT)
        act = glu * _sigmoid(SWIGLU_ALPHA * glu) * (lin + 1.0)
        y = jnp.dot(act.astype(BF16), wd_sc[...], preferred_element_type=F32) + bd_ref[0]
        y_ref[...] = _pack_bf16_pairs(y)

    @pl.when(i >= nu_ref[0])
    def _():
        y_ref[...] = jnp.zeros_like(y_ref)


def _experts(block_e, n_used, xs, w_gu, b_gu, w_down, b_down, layer):
    cap = xs.shape[0]
    d = D_MODEL
    tm = EXPERT_TILE
    n_blocks = cap // tm
    grid_spec = pltpu.PrefetchScalarGridSpec(
        num_scalar_prefetch=2,
        grid=(n_blocks,),
        in_specs=[
            pl.BlockSpec((tm, d // 2), lambda i, be, nu: (i, 0)),
            pl.BlockSpec((None, 1, d, 2 * d), lambda i, be, nu: (layer, be[i], 0, 0)),
            pl.BlockSpec((1, 1, 2 * d), lambda i, be, nu: (be[i], 0, 0)),
            pl.BlockSpec((None, 1, d, d), lambda i, be, nu: (layer, be[i], 0, 0)),
            pl.BlockSpec((1, 1, d), lambda i, be, nu: (be[i], 0, 0)),
        ],
        out_specs=pl.BlockSpec((tm, d // 2), lambda i, be, nu: (i, 0)),
        scratch_shapes=[pltpu.VMEM((d, 2 * d), BF16), pltpu.VMEM((d, d), BF16)],
    )
    return pl.pallas_call(
        _expert_kernel,
        out_shape=jax.ShapeDtypeStruct((cap, d // 2), I32),
        grid_spec=grid_spec,
        compiler_params=_params("arbitrary"),
        name="moe_experts",
    )(block_e, n_used, xs, w_gu, b_gu[layer][:, None, :], w_down, b_down[layer][:, None, :])


def _combine_kernel(gate_ref, res_ref, yk_ref, g_ref, b_ref, o_ref):
    d = res_ref.shape[1]
    gates = gate_ref[...]
    ffn = gates[:, 0:1] * _unpack_bf16_pairs(yk_ref[0], F32)
    for k in range(1, TOP_K):
        ffn = ffn + gates[:, k:k + 1] * _unpack_bf16_pairs(yk_ref[k], F32)
    o_ref[...] = _layer_norm(DN_ALPHA * res_ref[...] + ffn, g_ref[...], b_ref[...])


def _combine(gates, res, yk, g, b):
    n, d = res.shape
    tt = MOVE_TILE
    vec = pl.BlockSpec((1, d), lambda i: (0, 0))
    return pl.pallas_call(
        _combine_kernel,
        out_shape=jax.ShapeDtypeStruct((n, d), F32),
        grid=(n // tt,),
        in_specs=[pl.BlockSpec((tt, TOP_K), lambda i: (i, 0)),
                  pl.BlockSpec((tt, d), lambda i: (i, 0)),
                  pl.BlockSpec((TOP_K, tt, d // 2), lambda i: (0, i, 0)),
                  vec, vec],
        out_specs=pl.BlockSpec((tt, d), lambda i: (i, 0)),
        compiler_params=_params("parallel"),
        name="moe_combine",
    )(gates, res, yk, g, b)


def _moe_layer(h, router_w, router_b, w_gu, b_gu, w_down, b_down, ln_g, ln_b, layer):
    n, d = h.shape
    tm = EXPERT_TILE
    idx, gates, rank, counts, h_packed = _router(h, router_w, router_b)
    counts = counts[:, 0]
    padded = (counts + tm - 1) // tm * tm
    pad_end = jnp.cumsum(padded)
    pad_start = pad_end - padded
    group_start = jnp.sum(jnp.where(idx[:, :, None] == jnp.arange(N_EXPERTS, dtype=I32), pad_start, 0), axis=-1)
    dest = group_start + rank
    n_blocks = -(-(n * TOP_K + N_EXPERTS * (tm - 1)) // tm)
    block_start = jnp.arange(n_blocks, dtype=I32) * tm
    block_e = jnp.minimum(jnp.sum((pad_end[None, :] <= block_start[:, None]).astype(I32), axis=1),
                          N_EXPERTS - 1)
    n_used = (pad_end[-1:] // tm).astype(I32)

    xs = _sc_scatter_rows(h_packed, dest, n_blocks * tm, "moe_dispatch")
    ys = _experts(block_e, n_used, xs, w_gu, b_gu, w_down, b_down, layer)
    yk = _sc_gather_rows(ys, dest.reshape(TOP_K * n), "moe_collect").reshape(TOP_K, n, d // 2)
    return _combine(gates.T, h, yk, ln_g[None, :], ln_b[None, :])


def _qkv_kernel(x_ref, w_ref, o_ref):
    xb = x_ref[...].astype(BF16)
    d = x_ref.shape[1]
    for part in range(3):
        acc = jnp.dot(xb, w_ref[:, part * d:(part + 1) * d], preferred_element_type=F32)
        if part == 0:
            acc = acc * (NA_HEAD_DIM ** -0.5)
        for p in range(HEAD_PAIRS):
            o_ref[part * HEAD_PAIRS + p] = acc[:, p * LANES:(p + 1) * LANES].astype(BF16)


def _qkv(x, w_bf16):
    n, d = x.shape
    return pl.pallas_call(
        _qkv_kernel,
        out_shape=jax.ShapeDtypeStruct((3 * HEAD_PAIRS, n, LANES), BF16),
        grid=(n // ROW_TILE,),
        in_specs=[pl.BlockSpec((ROW_TILE, d), lambda i: (i, 0)),
                  pl.BlockSpec((d, 3 * d), lambda i: (0, 0))],
        out_specs=pl.BlockSpec((3 * HEAD_PAIRS, ROW_TILE, LANES), lambda i: (0, i, 0)),
        compiler_params=_params("parallel"),
        name="na_qkv",
    )(x, w_bf16)


def _na_kernel(q_ref, k_ref, v_ref, bias_ref, mb_ref, o_ref, *, rows):
    s = pl.program_id(1)
    w = GRID_W
    low = lax.broadcasted_iota(I32, (w, LANES), 1) < NA_HEAD_DIM
    contract_last = (((1,), (1,)), ((), ()))
    meta0 = FRONT_PAD

    @pl.when(s == 0)
    def _():
        o_ref[...] = jnp.zeros_like(o_ref)

    def stacked_q(p):
        qp = q_ref[p]
        zero = jnp.zeros_like(qp)
        return jnp.concatenate([jnp.where(low, qp, zero), jnp.where(low, zero, qp)], axis=0)

    def attend_all(k0):
        scores = []
        for p in range(HEAD_PAIRS):
            q2 = stacked_q(p)
            s_meta = lax.dot_general(q2, k_ref[p, meta0:meta0 + N_META, :], contract_last,
                                     preferred_element_type=F32) + mb_ref[p]
            s_win = None
            if k0 is not None:
                s_win = lax.dot_general(q2, k_ref[p, pl.ds(k0, NA_KH * w), :], contract_last,
                                        preferred_element_type=F32) + bias_ref[0, p]
            scores.append((s_meta, s_win))
        outs = []
        for p in range(HEAD_PAIRS):
            s_meta, s_win = scores[p]
            m = jnp.max(s_meta, axis=-1, keepdims=True)
            if s_win is not None:
                m = jnp.maximum(m, jnp.max(s_win, axis=-1, keepdims=True))
            p_meta = jnp.exp(s_meta - m)
            denom = jnp.sum(p_meta, axis=-1, keepdims=True)
            o = jnp.dot(p_meta.astype(BF16), v_ref[p, meta0:meta0 + N_META, :], preferred_element_type=F32)
            if s_win is not None:
                p_win = jnp.exp(s_win - m)
                denom = denom + jnp.sum(p_win, axis=-1, keepdims=True)
                o = o + jnp.dot(p_win.astype(BF16), v_ref[p, pl.ds(k0, NA_KH * w), :],
                                preferred_element_type=F32)
            o = o / denom
            outs.append(jnp.where(low, o[:w], o[w:]))
        return outs

    @pl.when(s == 1)
    def _():
        qrow = lax.broadcasted_iota(I32, (w, LANES), 0)
        for p, o in enumerate(attend_all(None)):
            o_ref[p] = jnp.where(qrow >= w - N_META, o, 0.0).astype(o_ref.dtype)

    @pl.when(s >= 2)
    def _():
        r = s - 2
        rs = jnp.clip(r - NA_KH // 2, 0, rows - NA_KH)
        k0 = pl.multiple_of(LANES + rs * w, w)
        for p, o in enumerate(attend_all(k0)):
            o_ref[p] = o.astype(o_ref.dtype)


def _na_bias_table(rpb, rows):
    del rows
    w = GRID_W
    q = jnp.arange(w)
    col_start = jnp.clip(q - NA_KW // 2, 0, w - NA_KW)
    c = jnp.arange(w)
    in_win = (c[None, :] >= col_start[:, None]) & (c[None, :] < col_start[:, None] + NA_KW)
    pad = w - NA_KW
    rp = jnp.pad(rpb.astype(F32), ((0, 0), (0, 0), (pad, pad)))
    toeplitz = jnp.stack([rp[:, :, w - 1 - qq:2 * w - 1 - qq] for qq in range(w)], axis=2)
    toeplitz = jnp.where(in_win[None, None], toeplitz, NEG_BIG)
    tabs = []
    for v in range(NA_KH):
        tv = toeplitz[:, NA_KH - 1 - v:2 * NA_KH - 1 - v]
        tabs.append(jnp.transpose(tv, (0, 2, 1, 3)).reshape(NA_HEADS, w, NA_KH * w))
    return jnp.stack(tabs, axis=0)


def _na_attention(qkv, bias_tab, meta_bias, *, batch, lp):
    n = qkv.shape[1]
    w = GRID_W
    rows = (lp - LANES) // w
    tiles = lp // w

    def variant(b, s):
        r = jnp.clip(s - 2, 0, rows - 1)
        return r - jnp.clip(r - NA_KH // 2, 0, rows - NA_KH)

    kern = functools.partial(_na_kernel, rows=rows)
    return pl.pallas_call(
        kern,
        out_shape=jax.ShapeDtypeStruct((HEAD_PAIRS, n, LANES), BF16),
        grid=(batch, tiles),
        in_specs=[
            pl.BlockSpec((HEAD_PAIRS, w, LANES), lambda b, s: (0, b * tiles + s, 0)),
            pl.BlockSpec((HEAD_PAIRS, lp, LANES), lambda b, s: (1, b, 0)),
            pl.BlockSpec((HEAD_PAIRS, lp, LANES), lambda b, s: (2, b, 0)),
            pl.BlockSpec((1, HEAD_PAIRS, 2 * w, NA_KH * w), lambda b, s: (variant(b, s), 0, 0, 0)),
            pl.BlockSpec((HEAD_PAIRS, 2 * w, N_META), lambda b, s: (0, 0, 0)),
        ],
        out_specs=pl.BlockSpec((HEAD_PAIRS, w, LANES), lambda b, s: (0, b * tiles + s, 0)),
        compiler_params=_params("parallel", "arbitrary"),
        name="na_attention",
    )(qkv, qkv, qkv,
      bias_tab.reshape(NA_KH, HEAD_PAIRS, 2 * w, NA_KH * w),
      jnp.repeat(meta_bias, w, axis=0).reshape(HEAD_PAIRS, 2 * w, N_META))


def _na_out_kernel(o_ref, res_ref, w_ref, g_ref, b_ref, out_ref):
    att = jnp.concatenate([o_ref[p] for p in range(HEAD_PAIRS)], axis=-1)
    mix = jnp.dot(att, w_ref[...], preferred_element_type=F32)
    out_ref[...] = _layer_norm(DN_ALPHA * res_ref[...] + mix, g_ref[...], b_ref[...])


def _na_out(o, res, w_bf16, g, b):
    n, d = res.shape
    vec = pl.BlockSpec((1, d), lambda i: (0, 0))
    return pl.pallas_call(
        _na_out_kernel,
        out_shape=jax.ShapeDtypeStruct((n, d), F32),
        grid=(n // ROW_TILE,),
        in_specs=[pl.BlockSpec((HEAD_PAIRS, ROW_TILE, LANES), lambda i: (0, i, 0)),
                  pl.BlockSpec((ROW_TILE, d), lambda i: (i, 0)),
                  pl.BlockSpec((d, d), lambda i: (0, 0)), vec, vec],
        out_specs=pl.BlockSpec((ROW_TILE, d), lambda i: (i, 0)),
        compiler_params=_params("parallel"),
        name="na_out",
    )(o, res, w_bf16, g, b)


def kernel(x, meta_tokens, lru_w_in, lru_conv_w, lru_conv_b, lru_wa, lru_ba, lru_wx, lru_bx, lru_lambda, lru_w_out, na_w_qkv, na_rpb, na_meta_bias, na_w_out, ln_mix_g, ln_mix_b, router_w, router_b, moe_w_gu, moe_b_gu, moe_w_down, moe_b_down, ln_ffn_g, ln_ffn_b):
    batch, seq, d = x.shape
    lp = LANES + seq
    assert d == D_MODEL and seq % GRID_W == 0 and seq // GRID_W >= NA_KH
    assert lp % SCAN_CHUNK == 0 and (batch * lp) % ROW_TILE == 0
    n = batch * lp

    front = jnp.zeros((batch, FRONT_PAD, d), x.dtype)
    meta = jnp.broadcast_to(meta_tokens[None].astype(x.dtype), (batch, N_META, d))
    h = jnp.concatenate([front, meta, x], axis=1).reshape(n, d)

    u = _matmul(h, lru_w_in[0].astype(BF16))
    row = lambda v: v[None, :]
    scans = []
    for direction, reverse in ((0, False), (1, True)):
        scans.append(_lru_scan(
            u, lru_conv_w[0], row(lru_conv_b[0]),
            lru_wa[0, direction].astype(BF16), row(lru_ba[0, direction]),
            lru_wx[0, direction].astype(BF16), row(lru_bx[0, direction]),
            row(lru_lambda[0, direction]), reverse=reverse, lp=lp))
    h = _lru_out(scans[0], scans[1], u, h, lru_w_out[0].astype(BF16), row(ln_mix_g[0]), row(ln_mix_b[0]))
    h = _moe_layer(h, router_w[0], router_b[0], moe_w_gu, moe_b_gu, moe_w_down, moe_b_down,
                   ln_ffn_g[0], ln_ffn_b[0], 0)

    qkv = _qkv(h, na_w_qkv[0].astype(BF16))
    att = _na_attention(qkv, _na_bias_table(na_rpb[0], seq // GRID_W), na_meta_bias[0].astype(F32),
                        batch=batch, lp=lp)
    h = _na_out(att, h, na_w_out[0].astype(BF16), row(ln_mix_g[1]), row(ln_mix_b[1]))
    h = _moe_layer(h, router_w[1], router_b[1], moe_w_gu, moe_b_gu, moe_w_down, moe_b_down,
                   ln_ffn_g[1], ln_ffn_b[1], 1)

    return h.reshape(batch, lp, d)[:, LANES:]
```

```python
import functools

import jax
import jax.numpy as jnp
from jax import lax
from jax.experimental import pallas as pl
from jax.experimental.pallas import tpu as pltpu
from jax.experimental.pallas import tpu_sc as plsc

F32 = jnp.float32
BF16 = jnp.bfloat16
I32 = jnp.int32
U32 = jnp.uint32

D_MODEL = 1024
N_META = 16
GRID_W = 64
LRU_BLOCKS = 4
LRU_BLOCK = D_MODEL // LRU_BLOCKS
CONV_W = 4
LRU_C = 8.0
NA_HEADS = 16
NA_HEAD_DIM = D_MODEL // NA_HEADS
NA_KH = 8
NA_KW = 16
N_EXPERTS = 32
TOP_K = 4
SWIGLU_LIMIT = 7.0
SWIGLU_ALPHA = 1.702
DEPTH = 2
DN_ALPHA = (2.0 * DEPTH) ** 0.25
LN_EPS = 1e-5

LANES = 128
SUBLANES = 8
FRONT_PAD = LANES - N_META
HEAD_PAIRS = D_MODEL // LANES
NEG_BIG = -1e30

ROW_TILE = 512
SCAN_CHUNK = 352
ROUTE_TILE = 512
MOVE_TILE = 256
EXPERT_TILE = 512
VMEM_LIMIT = 56 << 20

SC_CORES = 2
SC_SUBCORES = 16
SC_WORKERS = SC_CORES * SC_SUBCORES
SC_MAX_INDICES = 64
SC_ROW_BUFFER_BYTES = 128 << 10


def _params(*sem):
    return pltpu.CompilerParams(dimension_semantics=sem, vmem_limit_bytes=VMEM_LIMIT)


def _sigmoid(x):
    return 0.5 * jnp.tanh(0.5 * x) + 0.5


def _layer_norm(x, g, b):
    mu = jnp.mean(x, axis=-1, keepdims=True)
    xc = x - mu
    var = jnp.mean(xc * xc, axis=-1, keepdims=True)
    return xc * lax.rsqrt(var + LN_EPS) * g + b


def _matmul_kernel(x_ref, w_ref, o_ref):
    o_ref[...] = jnp.dot(x_ref[...].astype(BF16), w_ref[...], preferred_element_type=F32).astype(o_ref.dtype)


def _matmul(x, w_bf16):
    n, k = x.shape
    m = w_bf16.shape[1]
    return pl.pallas_call(
        _matmul_kernel,
        out_shape=jax.ShapeDtypeStruct((n, m), BF16),
        grid=(n // ROW_TILE,),
        in_specs=[pl.BlockSpec((ROW_TILE, k), lambda i: (i, 0)),
                  pl.BlockSpec((k, m), lambda i: (0, 0))],
        out_specs=pl.BlockSpec((ROW_TILE, m), lambda i: (i, 0)),
        compiler_params=_params("parallel"),
        name="in_proj",
    )(x, w_bf16)


def _lru_scan_kernel(xr_ref, xp_ref, xn_ref, cw_ref, cb_ref, wa_ref, ba_ref, wx_ref, bx_ref,
                     lam_ref, h_ref, a_sc, b_sc, h_sc, carry_sc, *, reverse, chunks_per_batch, lp):
    del lp
    t_rows = xr_ref.shape[0]
    seg = t_rows // SUBLANES
    halo = xp_ref.shape[0]
    lane_tiles = xr_ref.shape[1] // LANES
    step = pl.program_id(0)
    chunk = (pl.num_programs(0) - 1 - step) if reverse else step
    cib = chunk % chunks_per_batch
    first_chunk = cib == 0
    last_chunk = cib == chunks_per_batch - 1
    row8 = lax.broadcasted_iota(I32, (SUBLANES, 1), 0)
    cat = lambda parts: jnp.concatenate(parts, axis=0)

    head_pos = cib * t_rows + lax.broadcasted_iota(I32, (LANES, 1), 0)
    head_real = head_pos >= FRONT_PAD
    xr = xr_ref[...].astype(F32)
    xr = cat([jnp.where(head_real, xr[:LANES], 0.0), xr[LANES:]])
    prev_pos = cib * t_rows - halo + lax.broadcasted_iota(I32, (halo, 1), 0)
    prev = jnp.where(prev_pos >= FRONT_PAD, xp_ref[...].astype(F32), 0.0)
    nxt = jnp.where(last_chunk, 0.0, xn_ref[...].astype(F32))

    xm1 = pltpu.roll(xr, 1, 0)
    xm2 = pltpu.roll(xr, 2, 0)
    xp1 = pltpu.roll(xr, t_rows - 1, 0)
    xm1 = cat([jnp.where(row8 == 0, prev[halo - 1:halo], xm1[:SUBLANES]), xm1[SUBLANES:]])
    xm2_head = jnp.where(row8 == 0, prev[halo - 2:halo - 1],
                         jnp.where(row8 == 1, prev[halo - 1:halo], xm2[:SUBLANES]))
    xm2 = cat([xm2_head, xm2[SUBLANES:]])
    xp1 = cat([xp1[:t_rows - SUBLANES], jnp.where(row8 == SUBLANES - 1, nxt[0:1], xp1[t_rows - SUBLANES:])])
    cw = cw_ref[...]
    xc = cw[0:1, :] * xm2 + cw[1:2, :] * xm1 + cw[2:3, :] * xr + cw[3:4, :] * xp1 + cb_ref[...]

    xcb = xc.astype(BF16)
    za, zx = [], []
    for blk in range(LRU_BLOCKS):
        xblk = xcb[:, blk * LRU_BLOCK:(blk + 1) * LRU_BLOCK]
        za.append(jnp.dot(xblk, wa_ref[blk], preferred_element_type=F32))
        zx.append(jnp.dot(xblk, wx_ref[blk], preferred_element_type=F32))
    tanh_a = jnp.tanh(jnp.concatenate(za, axis=-1) + ba_ref[...])
    tanh_x = jnp.tanh(jnp.concatenate(zx, axis=-1) + bx_ref[...])

    z = -lam_ref[...]
    softplus = jnp.maximum(z, 0.0) + jnp.log1p(jnp.exp(-jnp.abs(z)))
    half_rate = (-0.5 * LRU_C) * softplus
    log_a = tanh_a * half_rate + half_rate
    a = jnp.exp(log_a)
    gap = jnp.tanh(log_a) * (-1.0 - a * a)
    mult = jnp.where(gap > 0.0, gap * lax.rsqrt(gap), 0.0)
    if reverse:
        tail = jnp.where(jnp.logical_and(last_chunk, row8 == SUBLANES - 1), 1.0, mult[t_rows - SUBLANES:])
        mult = cat([mult[:t_rows - SUBLANES], tail])
    else:
        mult = cat([jnp.where(head_pos == FRONT_PAD, 1.0, mult[:LANES]), mult[LANES:]])
    b = mult * ((0.5 * tanh_x + 0.5) * xc)
    b = cat([jnp.where(head_real, b[:LANES], 0.0), b[LANES:]])
    for c in range(lane_tiles):
        a_sc[c] = a[:, c * LANES:(c + 1) * LANES]
        b_sc[c] = b[:, c * LANES:(c + 1) * LANES]

    @pl.when(last_chunk if reverse else first_chunk)
    def _():
        carry_sc[...] = jnp.zeros_like(carry_sc)

    order = range(seg - 1, -1, -1) if reverse else range(seg)
    rows_of = lambda j: pl.ds(j, SUBLANES, stride=seg)

    local = [jnp.zeros((SUBLANES, LANES), F32)] * lane_tiles
    prod = [jnp.ones((SUBLANES, LANES), F32)] * lane_tiles
    for j in order:
        for c in range(lane_tiles):
            av = a_sc[c, rows_of(j), :]
            local[c] = av * local[c] + b_sc[c, rows_of(j), :]
            prod[c] = av * prod[c]
            h_sc[c, rows_of(j), :] = local[c]
            a_sc[c, rows_of(j), :] = prod[c]

    carry_in = carry_sc[...]
    carry_out = []
    seg_carry = []
    for c in range(lane_tiles):
        state = carry_in[:, c * LANES:(c + 1) * LANES]
        rows = [None] * SUBLANES
        for s in (range(SUBLANES - 1, -1, -1) if reverse else range(SUBLANES)):
            rows[s] = state
            state = local[c][s:s + 1] + prod[c][s:s + 1] * state
        seg_carry.append(cat(rows))
        carry_out.append(state)
    carry_sc[...] = jnp.concatenate(carry_out, axis=-1)

    for j in order:
        for c in range(lane_tiles):
            h_sc[c, rows_of(j), :] = h_sc[c, rows_of(j), :] + a_sc[c, rows_of(j), :] * seg_carry[c]
    for c in range(lane_tiles):
        h_ref[:, c * LANES:(c + 1) * LANES] = h_sc[c]


def _lru_scan(u, cw, cb, wa_half, ba_half, wx_half, bx_half, lam, *, reverse, lp):
    n = u.shape[0]
    d = D_MODEL
    t = SCAN_CHUNK
    halo = 2 * SUBLANES
    n_chunks = n // t
    cpb = lp // t
    t_h = t // halo
    n_h = n // halo

    def chunk_of(i):
        return (n_chunks - 1 - i) if reverse else i

    kern = functools.partial(_lru_scan_kernel, reverse=reverse, chunks_per_batch=cpb, lp=lp)
    full2 = lambda shape: pl.BlockSpec(shape, lambda i: (0, 0))
    full3 = lambda shape: pl.BlockSpec(shape, lambda i: (0, 0, 0))
    tile_major = pltpu.VMEM((d // LANES, t, LANES), F32)
    return pl.pallas_call(
        kern,
        out_shape=jax.ShapeDtypeStruct((n, d), F32),
        grid=(n_chunks,),
        in_specs=[
            pl.BlockSpec((t, d), lambda i: (chunk_of(i), 0)),
            pl.BlockSpec((halo, d), lambda i: (jnp.maximum(chunk_of(i) * t_h - 1, 0), 0)),
            pl.BlockSpec((halo, d), lambda i: (jnp.minimum((chunk_of(i) + 1) * t_h, n_h - 1), 0)),
            full2((CONV_W, d)), full2((1, d)),
            full3((LRU_BLOCKS, LRU_BLOCK, LRU_BLOCK)), full2((1, d)),
            full3((LRU_BLOCKS, LRU_BLOCK, LRU_BLOCK)), full2((1, d)),
            full2((1, d)),
        ],
        out_specs=pl.BlockSpec((t, d), lambda i: (chunk_of(i), 0)),
        scratch_shapes=[tile_major, tile_major, tile_major, pltpu.VMEM((1, d), F32)],
        compiler_params=_params("arbitrary"),
        name="lru_scan_bwd" if reverse else "lru_scan_fwd",
    )(u, u, u, cw, cb, wa_half, ba_half, wx_half, bx_half, lam)


def _gelu_tanh(y):
    c = 0.7978845608028654
    return y * (0.5 * (1.0 + jnp.tanh(c * (y + 0.044715 * (y * y * y)))))


def _lru_out_kernel(hf_ref, hb_ref, y_ref, res_ref, w_ref, g_ref, b_ref, o_ref):
    gated = (hf_ref[...] + hb_ref[...]) * _gelu_tanh(y_ref[...].astype(F32))
    mix = jnp.dot(gated.astype(BF16), w_ref[...], preferred_element_type=F32)
    o_ref[...] = _layer_norm(DN_ALPHA * res_ref[...] + mix, g_ref[...], b_ref[...])


def _lru_out(hf, hb, u, res, w_bf16, g, b):
    n, d = res.shape
    row = lambda j: pl.BlockSpec((ROW_TILE, d), lambda i, j=j: (i, j))
    vec = pl.BlockSpec((1, d), lambda i: (0, 0))
    return pl.pallas_call(
        _lru_out_kernel,
        out_shape=jax.ShapeDtypeStruct((n, d), F32),
        grid=(n // ROW_TILE,),
        in_specs=[row(0), row(0), row(1), row(0), pl.BlockSpec((d, d), lambda i: (0, 0)), vec, vec],
        out_specs=row(0),
        compiler_params=_params("parallel"),
        name="lru_out",
    )(hf, hb, u, res, w_bf16, g, b)


def _pack_bf16_pairs(x):
    half = x.shape[1] // 2
    lo = pltpu.bitcast(x[:, :half].astype(BF16).astype(F32), U32)
    hi = pltpu.bitcast(x[:, half:].astype(BF16).astype(F32), U32)
    return pltpu.bitcast((lo >> 16) | (hi & jnp.uint32(0xFFFF0000)), I32)


def _unpack_bf16_pairs(packed, dtype=BF16):
    u = pltpu.bitcast(packed, U32)
    lo = pltpu.bitcast(u << 16, F32).astype(dtype)
    hi = pltpu.bitcast(u & jnp.uint32(0xFFFF0000), F32).astype(dtype)
    return jnp.concatenate([lo, hi], axis=-1)


def _router_kernel(h_ref, wt_ref, b_ref, idx_ref, gate_ref, rank_ref, cnt_ref, hp_ref, base_sc):
    tm = h_ref.shape[0]
    h = h_ref[...]
    hp_ref[...] = _pack_bf16_pairs(h)

    @pl.when(pl.program_id(0) == 0)
    def _():
        base_sc[...] = jnp.zeros_like(base_sc)

    wt = wt_ref[...]
    h_hi = h.astype(BF16)
    h_lo = (h - h_hi.astype(F32)).astype(BF16)
    w_hi = wt.astype(BF16)
    w_lo = (wt - w_hi.astype(F32)).astype(BF16)
    contract_last = (((1,), (1,)), ((), ()))
    mm = lambda a, b: lax.dot_general(a, b, contract_last, preferred_element_type=F32)
    logits = mm(w_hi, h_hi) + (mm(w_hi, h_lo) + mm(w_lo, h_hi)) + b_ref[...]

    expert = lax.broadcasted_iota(I32, (N_EXPERTS, tm), 0).astype(F32)
    vals = logits
    idx_rows, val_rows, onehots = [], [], []
    for _ in range(TOP_K):
        m = jnp.max(vals, axis=0, keepdims=True)
        idx = jnp.min(jnp.where(vals == m, expert, float(N_EXPERTS)), axis=0, keepdims=True)
        hit = expert == idx
        onehots.append(jnp.where(hit, 1.0, 0.0))
        idx_rows.append(idx)
        val_rows.append(m)
        vals = jnp.where(hit, -jnp.inf, vals)

    onehot_all = jnp.concatenate(onehots, axis=0)
    t_from = lax.broadcasted_iota(I32, (tm, tm), 0)
    t_to = lax.broadcasted_iota(I32, (tm, tm), 1)
    earlier = jnp.where(t_from < t_to, 1.0, 0.0).astype(BF16)
    before = jnp.dot(onehot_all.astype(BF16), earlier, preferred_element_type=F32)
    base = base_sc[...]
    starts = []
    for k in range(TOP_K):
        starts.append(base)
        base = base + jnp.sum(onehots[k], axis=1, keepdims=True)
    contrib = onehot_all * (jnp.concatenate(starts, axis=0) + before)
    rank_rows = [jnp.sum(contrib[k * N_EXPERTS:(k + 1) * N_EXPERTS], axis=0, keepdims=True)
                 for k in range(TOP_K)]

    top_vals = jnp.concatenate(val_rows, axis=0)
    e = jnp.exp(top_vals - top_vals[0:1])
    idx_ref[...] = jnp.concatenate(idx_rows, axis=0).astype(I32)
    gate_ref[...] = e / jnp.sum(e, axis=0, keepdims=True)
    rank_ref[...] = jnp.concatenate(rank_rows, axis=0).astype(I32)
    base_sc[...] = base
    cnt_ref[...] = base.astype(I32)


def _router(h, w, b):
    n, d = h.shape
    tm = ROUTE_TILE
    out4 = lambda dt: jax.ShapeDtypeStruct((TOP_K, n), dt)
    spec4 = pl.BlockSpec((TOP_K, tm), lambda i: (0, i))
    return pl.pallas_call(
        _router_kernel,
        out_shape=(out4(I32), out4(F32), out4(I32), jax.ShapeDtypeStruct((N_EXPERTS, 1), I32),
                   jax.ShapeDtypeStruct((n, d // 2), I32)),
        grid=(n // tm,),
        in_specs=[pl.BlockSpec((tm, d), lambda i: (i, 0)),
                  pl.BlockSpec((N_EXPERTS, d), lambda i: (0, 0)),
                  pl.BlockSpec((N_EXPERTS, 1), lambda i: (0, 0))],
        out_specs=(spec4, spec4, spec4, pl.BlockSpec((N_EXPERTS, 1), lambda i: (0, 0)),
                   pl.BlockSpec((tm, d // 2), lambda i: (i, 0))),
        scratch_shapes=[pltpu.VMEM((N_EXPERTS, 1), F32)],
        compiler_params=_params("arbitrary"),
        name="router",
    )(h, w.T, b[:, None])


def _sc_gather_rows(table, idx, name):
    rows, width = table.shape
    total = idx.shape[0]
    chunk = min(SC_MAX_INDICES, SC_ROW_BUFFER_BYTES // (width * table.dtype.itemsize))
    per_worker = total // SC_WORKERS
    n_chunks = per_worker // chunk
    assert total == SC_WORKERS * n_chunks * chunk and n_chunks % 2 == 0
    mesh = plsc.VectorSubcoreMesh(core_axis_name="c", subcore_axis_name="s",
                                  num_cores=SC_CORES, num_subcores=SC_SUBCORES)

    def body(table_hbm, idx_hbm, out_hbm, idx_v, rows0, rows1, gsem0, gsem1, psem0, psem1):
        worker = lax.axis_index("s") * SC_CORES + lax.axis_index("c")
        base = worker * per_worker
        pltpu.sync_copy(idx_hbm.at[worker], idx_v)
        bufs = (rows0, rows1)
        gsems = (gsem0, gsem1)
        psems = (psem0, psem1)

        def gather(c, slot):
            return pltpu.make_async_copy(table_hbm.at[idx_v.at[c]], bufs[slot], gsems[slot])

        def put(c, slot):
            return pltpu.make_async_copy(bufs[slot], out_hbm.at[pl.ds(base + c * chunk, chunk)], psems[slot])

        gather(0, 0).start()

        @pl.loop(0, n_chunks, step=2)
        def _(c0):
            for slot in range(2):
                c = c0 + slot

                @pl.when(c + 1 < n_chunks)
                def _():
                    @pl.when(c >= 1)
                    def _():
                        put(c - 1, 1 - slot).wait()
                    gather(c + 1, 1 - slot).start()

                gather(c, slot).wait()
                put(c, slot).start()

        put(n_chunks - 2, 0).wait()
        put(n_chunks - 1, 1).wait()

    return pl.kernel(
        body,
        out_type=jax.ShapeDtypeStruct((total, width), table.dtype),
        mesh=mesh,
        scratch_types=[pltpu.VMEM((n_chunks, chunk), I32),
                       pltpu.VMEM((chunk, width), table.dtype), pltpu.VMEM((chunk, width), table.dtype),
                       pltpu.SemaphoreType.DMA, pltpu.SemaphoreType.DMA,
                       pltpu.SemaphoreType.DMA, pltpu.SemaphoreType.DMA],
        name=name,
    )(table, idx.reshape(SC_WORKERS, n_chunks, chunk))


def _sc_scatter_rows(src, dest, out_rows, name):
    n, width = src.shape
    fan = dest.shape[0]
    per_worker = n // SC_WORKERS
    chunk = max(c for c in range(SUBLANES, SC_MAX_INDICES + 1, SUBLANES) if per_worker % (2 * c) == 0)
    n_chunks = per_worker // chunk
    assert n == SC_WORKERS * n_chunks * chunk and n_chunks % 2 == 0
    mesh = plsc.VectorSubcoreMesh(core_axis_name="c", subcore_axis_name="s",
                                  num_cores=SC_CORES, num_subcores=SC_SUBCORES)

    def body(src_hbm, idx_hbm, out_hbm, idx_v, rows0, rows1, lsem0, lsem1, ssem0, ssem1):
        worker = lax.axis_index("s") * SC_CORES + lax.axis_index("c")
        base = worker * per_worker
        pltpu.sync_copy(idx_hbm.at[worker], idx_v)
        bufs = (rows0, rows1)
        lsems = (lsem0, lsem1)
        ssems = (ssem0, ssem1)

        def load(c, slot):
            return pltpu.make_async_copy(src_hbm.at[pl.ds(base + c * chunk, chunk)], bufs[slot], lsems[slot])

        def scatter(c, k, slot):
            return pltpu.make_async_copy(bufs[slot], out_hbm.at[idx_v.at[k * n_chunks + c]], ssems[slot])

        load(0, 0).start()

        @pl.loop(0, n_chunks, step=2)
        def _(c0):
            for slot in range(2):
                c = c0 + slot

                @pl.when(c + 1 < n_chunks)
                def _():
                    @pl.when(c >= 1)
                    def _():
                        for k in range(fan):
                            scatter(c - 1, k, 1 - slot).wait()
                    load(c + 1, 1 - slot).start()

                load(c, slot).wait()
                for k in range(fan):
                    scatter(c, k, slot).start()

        for k in range(fan):
            scatter(n_chunks - 2, k, 0).wait()
        for k in range(fan):
            scatter(n_chunks - 1, k, 1).wait()

    idx = dest.reshape(fan, SC_WORKERS, n_chunks, chunk).transpose(1, 0, 2, 3)
    return pl.kernel(
        body,
        out_type=jax.ShapeDtypeStruct((out_rows, width), src.dtype),
        mesh=mesh,
        scratch_types=[pltpu.VMEM((fan * n_chunks, chunk), I32),
                       pltpu.VMEM((chunk, width), src.dtype), pltpu.VMEM((chunk, width), src.dtype),
                       pltpu.SemaphoreType.DMA, pltpu.SemaphoreType.DMA,
                       pltpu.SemaphoreType.DMA, pltpu.SemaphoreType.DMA],
        name=name,
    )(src, idx.reshape(SC_WORKERS, fan * n_chunks, chunk))


def _expert_kernel(be_ref, nu_ref, xs_ref, wgu_ref, bgu_ref, wd_ref, bd_ref, y_ref, wgu_sc, wd_sc):
    i = pl.program_id(0)
    e = be_ref[i]
    e_prev = be_ref[jnp.maximum(i - 1, 0)]
    d = wd_sc.shape[0]

    @pl.when((i == 0) | (e != e_prev))
    def _():
        for r in range(0, d, LANES):
            wgu_sc[r:r + LANES, :] = wgu_ref[0, r:r + LANES, :].astype(BF16)
            wd_sc[r:r + LANES, :] = wd_ref[0, r:r + LANES, :].astype(BF16)

    @pl.when(i < nu_ref[0])
    def _():
        x = _unpack_bf16_pairs(xs_ref[...])
        h = jnp.dot(x, wgu_sc[...], preferred_element_type=F32) + bgu_ref[0]
        glu = jnp.minimum(h[:, :d], SWIGLU_LIMIT)
        lin = jnp.clip(h[:, d:], -SWIGLU_LIMIT, SWIGLU_LIMIT)
        act = glu * _sigmoid(SWIGLU_ALPHA * glu) * (lin + 1.0)
        y = jnp.dot(act.astype(BF16), wd_sc[...], preferred_element_type=F32) + bd_ref[0]
        y_ref[...] = _pack_bf16_pairs(y)

    @pl.when(i >= nu_ref[0])
    def _():
        y_ref[...] = jnp.zeros_like(y_ref)


def _experts(block_e, n_used, xs, w_gu, b_gu, w_down, b_down, layer):
    cap = xs.shape[0]
    d = D_MODEL
    tm = EXPERT_TILE
    n_blocks = cap // tm
    grid_spec = pltpu.PrefetchScalarGridSpec(
        num_scalar_prefetch=2,
        grid=(n_blocks,),
        in_specs=[
            pl.BlockSpec((tm, d // 2), lambda i, be, nu: (i, 0)),
            pl.BlockSpec((None, 1, d, 2 * d), lambda i, be, nu: (layer, be[i], 0, 0)),
            pl.BlockSpec((1, 1, 2 * d), lambda i, be, nu: (be[i], 0, 0)),
            pl.BlockSpec((None, 1, d, d), lambda i, be, nu: (layer, be[i], 0, 0)),
            pl.BlockSpec((1, 1, d), lambda i, be, nu: (be[i], 0, 0)),
        ],
        out_specs=pl.BlockSpec((tm, d // 2), lambda i, be, nu: (i, 0)),
        scratch_shapes=[pltpu.VMEM((d, 2 * d), BF16), pltpu.VMEM((d, d), BF16)],
    )
    return pl.pallas_call(
        _expert_kernel,
        out_shape=jax.ShapeDtypeStruct((cap, d // 2), I32),
        grid_spec=grid_spec,
        compiler_params=_params("arbitrary"),
        name="moe_experts",
    )(block_e, n_used, xs, w_gu, b_gu[layer][:, None, :], w_down, b_down[layer][:, None, :])


def _combine_kernel(gate_ref, res_ref, yk_ref, g_ref, b_ref, o_ref):
    d = res_ref.shape[1]
    gates = gate_ref[...]
    ffn = gates[:, 0:1] * _unpack_bf16_pairs(yk_ref[0], F32)
    for k in range(1, TOP_K):
        ffn = ffn + gates[:, k:k + 1] * _unpack_bf16_pairs(yk_ref[k], F32)
    o_ref[...] = _layer_norm(DN_ALPHA * res_ref[...] + ffn, g_ref[...], b_ref[...])


def _combine(gates, res, yk, g, b):
    n, d = res.shape
    tt = MOVE_TILE
    vec = pl.BlockSpec((1, d), lambda i: (0, 0))
    return pl.pallas_call(
        _combine_kernel,
        out_shape=jax.ShapeDtypeStruct((n, d), F32),
        grid=(n // tt,),
        in_specs=[pl.BlockSpec((tt, TOP_K), lambda i: (i, 0)),
                  pl.BlockSpec((tt, d), lambda i: (i, 0)),
                  pl.BlockSpec((TOP_K, tt, d // 2), lambda i: (0, i, 0)),
                  vec, vec],
        out_specs=pl.BlockSpec((tt, d), lambda i: (i, 0)),
        compiler_params=_params("parallel"),
        name="moe_combine",
    )(gates, res, yk, g, b)


def _moe_layer(h, router_w, router_b, w_gu, b_gu, w_down, b_down, ln_g, ln_b, layer):
    n, d = h.shape
    tm = EXPERT_TILE
    idx, gates, rank, counts, h_packed = _router(h, router_w, router_b)
    counts = counts[:, 0]
    padded = (counts + tm - 1) // tm * tm
    pad_end = jnp.cumsum(padded)
    pad_start = pad_end - padded
    group_start = jnp.sum(jnp.where(idx[:, :, None] == jnp.arange(N_EXPERTS, dtype=I32), pad_start, 0), axis=-1)
    dest = group_start + rank
    n_blocks = -(-(n * TOP_K + N_EXPERTS * (tm - 1)) // tm)
    block_start = jnp.arange(n_blocks, dtype=I32) * tm
    block_e = jnp.minimum(jnp.sum((pad_end[None, :] <= block_start[:, None]).astype(I32), axis=1),
                          N_EXPERTS - 1)
    n_used = (pad_end[-1:] // tm).astype(I32)

    xs = _sc_scatter_rows(h_packed, dest, n_blocks * tm, "moe_dispatch")
    ys = _experts(block_e, n_used, xs, w_gu, b_gu, w_down, b_down, layer)
    yk = _sc_gather_rows(ys, dest.reshape(TOP_K * n), "moe_collect").reshape(TOP_K, n, d // 2)
    return _combine(gates.T, h, yk, ln_g[None, :], ln_b[None, :])


def _qkv_kernel(x_ref, w_ref, o_ref):
    xb = x_ref[...].astype(BF16)
    d = x_ref.shape[1]
    for part in range(3):
        acc = jnp.dot(xb, w_ref[:, part * d:(part + 1) * d], preferred_element_type=F32)
        if part == 0:
            acc = acc * (NA_HEAD_DIM ** -0.5)
        for p in range(HEAD_PAIRS):
            o_ref[part * HEAD_PAIRS + p] = acc[:, p * LANES:(p + 1) * LANES].astype(BF16)


def _qkv(x, w_bf16):
    n, d = x.shape
    return pl.pallas_call(
        _qkv_kernel,
        out_shape=jax.ShapeDtypeStruct((3 * HEAD_PAIRS, n, LANES), BF16),
        grid=(n // ROW_TILE,),
        in_specs=[pl.BlockSpec((ROW_TILE, d), lambda i: (i, 0)),
                  pl.BlockSpec((d, 3 * d), lambda i: (0, 0))],
        out_specs=pl.BlockSpec((3 * HEAD_PAIRS, ROW_TILE, LANES), lambda i: (0, i, 0)),
        compiler_params=_params("parallel"),
        name="na_qkv",
    )(x, w_bf16)


def _na_kernel(q_ref, k_ref, v_ref, bias_ref, mb_ref, o_ref, *, rows):
    s = pl.program_id(1)
    w = GRID_W
    low = lax.broadcasted_iota(I32, (w, LANES), 1) < NA_HEAD_DIM
    contract_last = (((1,), (1,)), ((), ()))
    meta0 = FRONT_PAD

    @pl.when(s == 0)
    def _():
        o_ref[...] = jnp.zeros_like(o_ref)

    def stacked_q(p):
        qp = q_ref[p]
        zero = jnp.zeros_like(qp)
        return jnp.concatenate([jnp.where(low, qp, zero), jnp.where(low, zero, qp)], axis=0)

    def attend_all(k0):
        scores = []
        for p in range(HEAD_PAIRS):
            q2 = stacked_q(p)
            s_meta = lax.dot_general(q2, k_ref[p, meta0:meta0 + N_META, :], contract_last,
                                     preferred_element_type=F32) + mb_ref[p]
            s_win = None
            if k0 is not None:
                s_win = lax.dot_general(q2, k_ref[p, pl.ds(k0, NA_KH * w), :], contract_last,
                                        preferred_element_type=F32) + bias_ref[0, p]
            scores.append((s_meta, s_win))
        outs = []
        for p in range(HEAD_PAIRS):
            s_meta, s_win = scores[p]
            m = jnp.max(s_meta, axis=-1, keepdims=True)
            if s_win is not None:
                m = jnp.maximum(m, jnp.max(s_win, axis=-1, keepdims=True))
            p_meta = jnp.exp(s_meta - m)
            denom = jnp.sum(p_meta, axis=-1, keepdims=True)
            o = jnp.dot(p_meta.astype(BF16), v_ref[p, meta0:meta0 + N_META, :], preferred_element_type=F32)
            if s_win is not None:
                p_win = jnp.exp(s_win - m)
                denom = denom + jnp.sum(p_win, axis=-1, keepdims=True)
                o = o + jnp.dot(p_win.astype(BF16), v_ref[p, pl.ds(k0, NA_KH * w), :],
                                preferred_element_type=F32)
            o = o / denom
            outs.append(jnp.where(low, o[:w], o[w:]))
        return outs

    @pl.when(s == 1)
    def _():
        qrow = lax.broadcasted_iota(I32, (w, LANES), 0)
        for p, o in enumerate(attend_all(None)):
            o_ref[p] = jnp.where(qrow >= w - N_META, o, 0.0).astype(o_ref.dtype)

    @pl.when(s >= 2)
    def _():
        r = s - 2
        rs = jnp.clip(r - NA_KH // 2, 0, rows - NA_KH)
        k0 = pl.multiple_of(LANES + rs * w, w)
        for p, o in enumerate(attend_all(k0)):
            o_ref[p] = o.astype(o_ref.dtype)


def _na_bias_table(rpb, rows):
    del rows
    w = GRID_W
    q = jnp.arange(w)
    col_start = jnp.clip(q - NA_KW // 2, 0, w - NA_KW)
    c = jnp.arange(w)
    in_win = (c[None, :] >= col_start[:, None]) & (c[None, :] < col_start[:, None] + NA_KW)
    pad = w - NA_KW
    rp = jnp.pad(rpb.astype(F32), ((0, 0), (0, 0), (pad, pad)))
    toeplitz = jnp.stack([rp[:, :, w - 1 - qq:2 * w - 1 - qq] for qq in range(w)], axis=2)
    toeplitz = jnp.where(in_win[None, None], toeplitz, NEG_BIG)
    tabs = []
    for v in range(NA_KH):
        tv = toeplitz[:, NA_KH - 1 - v:2 * NA_KH - 1 - v]
        tabs.append(jnp.transpose(tv, (0, 2, 1, 3)).reshape(NA_HEADS, w, NA_KH * w))
    return jnp.stack(tabs, axis=0)


def _na_attention(qkv, bias_tab, meta_bias, *, batch, lp):
    n = qkv.shape[1]
    w = GRID_W
    rows = (lp - LANES) // w
    tiles = lp // w

    def variant(b, s):
        r = jnp.clip(s - 2, 0, rows - 1)
        return r - jnp.clip(r - NA_KH // 2, 0, rows - NA_KH)

    kern = functools.partial(_na_kernel, rows=rows)
    return pl.pallas_call(
        kern,
        out_shape=jax.ShapeDtypeStruct((HEAD_PAIRS, n, LANES), BF16),
        grid=(batch, tiles),
        in_specs=[
            pl.BlockSpec((HEAD_PAIRS, w, LANES), lambda b, s: (0, b * tiles + s, 0)),
            pl.BlockSpec((HEAD_PAIRS, lp, LANES), lambda b, s: (1, b, 0)),
            pl.BlockSpec((HEAD_PAIRS, lp, LANES), lambda b, s: (2, b, 0)),
            pl.BlockSpec((1, HEAD_PAIRS, 2 * w, NA_KH * w), lambda b, s: (variant(b, s), 0, 0, 0)),
            pl.BlockSpec((HEAD_PAIRS, 2 * w, N_META), lambda b, s: (0, 0, 0)),
        ],
        out_specs=pl.BlockSpec((HEAD_PAIRS, w, LANES), lambda b, s: (0, b * tiles + s, 0)),
        compiler_params=_params("parallel", "arbitrary"),
        name="na_attention",
    )(qkv, qkv, qkv,
      bias_tab.reshape(NA_KH, HEAD_PAIRS, 2 * w, NA_KH * w),
      jnp.repeat(meta_bias, w, axis=0).reshape(HEAD_PAIRS, 2 * w, N_META))


def _na_out_kernel(o_ref, res_ref, w_ref, g_ref, b_ref, out_ref):
    att = jnp.concatenate([o_ref[p] for p in range(HEAD_PAIRS)], axis=-1)
    mix = jnp.dot(att, w_ref[...], preferred_element_type=F32)
    out_ref[...] = _layer_norm(DN_ALPHA * res_ref[...] + mix, g_ref[...], b_ref[...])


def _na_out(o, res, w_bf16, g, b):
    n, d = res.shape
    vec = pl.BlockSpec((1, d), lambda i: (0, 0))
    return pl.pallas_call(
        _na_out_kernel,
        out_shape=jax.ShapeDtypeStruct((n, d), F32),
        grid=(n // ROW_TILE,),
        in_specs=[pl.BlockSpec((HEAD_PAIRS, ROW_TILE, LANES), lambda i: (0, i, 0)),
                  pl.BlockSpec((ROW_TILE, d), lambda i: (i, 0)),
                  pl.BlockSpec((d, d), lambda i: (0, 0)), vec, vec],
        out_specs=pl.BlockSpec((ROW_TILE, d), lambda i: (i, 0)),
        compiler_params=_params("parallel"),
        name="na_out",
    )(o, res, w_bf16, g, b)


def kernel(x, meta_tokens, lru_w_in, lru_conv_w, lru_conv_b, lru_wa, lru_ba, lru_wx, lru_bx, lru_lambda, lru_w_out, na_w_qkv, na_rpb, na_meta_bias, na_w_out, ln_mix_g, ln_mix_b, router_w, router_b, moe_w_gu, moe_b_gu, moe_w_down, moe_b_down, ln_ffn_g, ln_ffn_b):
    batch, seq, d = x.shape
    lp = LANES + seq
    assert d == D_MODEL and seq % GRID_W == 0 and seq // GRID_W >= NA_KH
    assert lp % SCAN_CHUNK == 0 and (batch * lp) % ROW_TILE == 0
    n = batch * lp

    front = jnp.zeros((batch, FRONT_PAD, d), x.dtype)
    meta = jnp.broadcast_to(meta_tokens[None].astype(x.dtype), (batch, N_META, d))
    h = jnp.concatenate([front, meta, x], axis=1).reshape(n, d)

    u = _matmul(h, lru_w_in[0].astype(BF16))
    row = lambda v: v[None, :]
    scans = []
    for direction, reverse in ((0, False), (1, True)):
        scans.append(_lru_scan(
            u, lru_conv_w[0], row(lru_conv_b[0]),
            (0.5 * lru_wa[0, direction]).astype(BF16), row(0.5 * lru_ba[0, direction]),
            (0.5 * lru_wx[0, direction]).astype(BF16), row(0.5 * lru_bx[0, direction]),
            row(lru_lambda[0, direction]), reverse=reverse, lp=lp))
    h = _lru_out(scans[0], scans[1], u, h, lru_w_out[0].astype(BF16), row(ln_mix_g[0]), row(ln_mix_b[0]))
    h = _moe_layer(h, router_w[0], router_b[0], moe_w_gu, moe_b_gu, moe_w_down, moe_b_down,
                   ln_ffn_g[0], ln_ffn_b[0], 0)

    qkv = _qkv(h, na_w_qkv[0].astype(BF16))
    att = _na_attention(qkv, _na_bias_table(na_rpb[0], seq // GRID_W), na_meta_bias[0].astype(F32),
                        batch=batch, lp=lp)
    h = _na_out(att, h, na_w_out[0].astype(BF16), row(ln_mix_g[1]), row(ln_mix_b[1]))
    h = _moe_layer(h, router_w[1], router_b[1], moe_w_gu, moe_b_gu, moe_w_down, moe_b_down,
                   ln_ffn_g[1], ln_ffn_b[1], 1)

    return h.reshape(batch, lp, d)[:, LANES:]
```

```python
import functools

import jax
import jax.numpy as jnp
from jax import lax
from jax.experimental import pallas as pl
from jax.experimental.pallas import tpu as pltpu
from jax.experimental.pallas import tpu_sc as plsc

F32 = jnp.float32
BF16 = jnp.bfloat16
I32 = jnp.int32
U32 = jnp.uint32

D_MODEL = 1024
N_META = 16
GRID_W = 64
LRU_BLOCKS = 4
LRU_BLOCK = D_MODEL // LRU_BLOCKS
CONV_W = 4
LRU_C = 8.0
NA_HEADS = 16
NA_HEAD_DIM = D_MODEL // NA_HEADS
NA_KH = 8
NA_KW = 16
N_EXPERTS = 32
TOP_K = 4
SWIGLU_LIMIT = 7.0
SWIGLU_ALPHA = 1.702
DEPTH = 2
DN_ALPHA = (2.0 * DEPTH) ** 0.25
LN_EPS = 1e-5

LANES = 128
SUBLANES = 8
FRONT_PAD = LANES - N_META
HEAD_PAIRS = D_MODEL // LANES
NEG_BIG = -1e30

ROW_TILE = 512
SCAN_CHUNK = 352
ROUTE_TILE = 512
MOVE_TILE = 256
EXPERT_TILE = 512
VMEM_LIMIT = 56 << 20

SC_CORES = 2
SC_SUBCORES = 16
SC_WORKERS = SC_CORES * SC_SUBCORES
SC_MAX_INDICES = 64
SC_ROW_BUFFER_BYTES = 128 << 10


def _params(*sem):
    return pltpu.CompilerParams(dimension_semantics=sem, vmem_limit_bytes=VMEM_LIMIT)


def _sigmoid(x):
    return 0.5 * jnp.tanh(0.5 * x) + 0.5


def _layer_norm(x, g, b):
    mu = jnp.mean(x, axis=-1, keepdims=True)
    xc = x - mu
    var = jnp.mean(xc * xc, axis=-1, keepdims=True)
    return xc * lax.rsqrt(var + LN_EPS) * g + b


def _matmul_kernel(x_ref, w_ref, o_ref):
    o_ref[...] = jnp.dot(x_ref[...].astype(BF16), w_ref[...], preferred_element_type=F32).astype(o_ref.dtype)


def _matmul(x, w_bf16):
    n, k = x.shape
    m = w_bf16.shape[1]
    return pl.pallas_call(
        _matmul_kernel,
        out_shape=jax.ShapeDtypeStruct((n, m), BF16),
        grid=(n // ROW_TILE,),
        in_specs=[pl.BlockSpec((ROW_TILE, k), lambda i: (i, 0)),
                  pl.BlockSpec((k, m), lambda i: (0, 0))],
        out_specs=pl.BlockSpec((ROW_TILE, m), lambda i: (i, 0)),
        compiler_params=_params("parallel"),
        name="in_proj",
    )(x, w_bf16)


def _lru_scan_kernel(xr_ref, xp_ref, xn_ref, cw_ref, cb_ref, wa_ref, ba_ref, wx_ref, bx_ref,
                     lam_ref, h_ref, a_sc, b_sc, h_sc, carry_sc, *, reverse, chunks_per_batch, lp):
    del lp
    t_rows = xr_ref.shape[0]
    seg = t_rows // SUBLANES
    halo = xp_ref.shape[0]
    lane_tiles = xr_ref.shape[1] // LANES
    step = pl.program_id(0)
    chunk = (pl.num_programs(0) - 1 - step) if reverse else step
    cib = chunk % chunks_per_batch
    first_chunk = cib == 0
    last_chunk = cib == chunks_per_batch - 1
    row8 = lax.broadcasted_iota(I32, (SUBLANES, 1), 0)
    cat = lambda parts: jnp.concatenate(parts, axis=0)

    head_pos = cib * t_rows + lax.broadcasted_iota(I32, (LANES, 1), 0)
    head_real = head_pos >= FRONT_PAD
    xr = xr_ref[...].astype(F32)
    xr = cat([jnp.where(head_real, xr[:LANES], 0.0), xr[LANES:]])
    prev_pos = cib * t_rows - halo + lax.broadcasted_iota(I32, (halo, 1), 0)
    prev = jnp.where(prev_pos >= FRONT_PAD, xp_ref[...].astype(F32), 0.0)
    nxt = jnp.where(last_chunk, 0.0, xn_ref[...].astype(F32))

    xm1 = pltpu.roll(xr, 1, 0)
    xm2 = pltpu.roll(xr, 2, 0)
    xp1 = pltpu.roll(xr, t_rows - 1, 0)
    xm1 = cat([jnp.where(row8 == 0, prev[halo - 1:halo], xm1[:SUBLANES]), xm1[SUBLANES:]])
    xm2_head = jnp.where(row8 == 0, prev[halo - 2:halo - 1],
                         jnp.where(row8 == 1, prev[halo - 1:halo], xm2[:SUBLANES]))
    xm2 = cat([xm2_head, xm2[SUBLANES:]])
    xp1 = cat([xp1[:t_rows - SUBLANES], jnp.where(row8 == SUBLANES - 1, nxt[0:1], xp1[t_rows - SUBLANES:])])
    cw = cw_ref[...]
    xc = cw[0:1, :] * xm2 + cw[1:2, :] * xm1 + cw[2:3, :] * xr + cw[3:4, :] * xp1 + cb_ref[...]

    xcb = xc.astype(BF16)
    za, zx = [], []
    for blk in range(LRU_BLOCKS):
        xblk = xcb[:, blk * LRU_BLOCK:(blk + 1) * LRU_BLOCK]
        za.append(jnp.dot(xblk, wa_ref[blk], preferred_element_type=F32))
        zx.append(jnp.dot(xblk, wx_ref[blk], preferred_element_type=F32))
    tanh_a = jnp.tanh(jnp.concatenate(za, axis=-1) + ba_ref[...])
    tanh_x = jnp.tanh(jnp.concatenate(zx, axis=-1) + bx_ref[...])

    z = -lam_ref[...]
    softplus = jnp.maximum(z, 0.0) + jnp.log1p(jnp.exp(-jnp.abs(z)))
    half_rate = (-0.5 * LRU_C) * softplus
    log_a = tanh_a * half_rate + half_rate
    a = jnp.exp(log_a)
    gap = jnp.tanh(log_a) * (-1.0 - a * a)
    mult = jnp.where(gap > 0.0, gap * lax.rsqrt(gap), 0.0)
    if reverse:
        tail = jnp.where(jnp.logical_and(last_chunk, row8 == SUBLANES - 1), 1.0, mult[t_rows - SUBLANES:])
        mult = cat([mult[:t_rows - SUBLANES], tail])
    else:
        mult = cat([jnp.where(head_pos == FRONT_PAD, 1.0, mult[:LANES]), mult[LANES:]])
    b = mult * ((0.5 * tanh_x + 0.5) * xc)
    b = cat([jnp.where(head_real, b[:LANES], 0.0), b[LANES:]])
    for c in range(lane_tiles):
        a_sc[c] = a[:, c * LANES:(c + 1) * LANES]
        b_sc[c] = b[:, c * LANES:(c + 1) * LANES]

    @pl.when(last_chunk if reverse else first_chunk)
    def _():
        carry_sc[...] = jnp.zeros_like(carry_sc)

    order = range(seg - 1, -1, -1) if reverse else range(seg)
    rows_of = lambda j: pl.ds(j, SUBLANES, stride=seg)

    local = [jnp.zeros((SUBLANES, LANES), F32)] * lane_tiles
    prod = [jnp.ones((SUBLANES, LANES), F32)] * lane_tiles
    for j in order:
        for c in range(lane_tiles):
            av = a_sc[c, rows_of(j), :]
            local[c] = av * local[c] + b_sc[c, rows_of(j), :]
            prod[c] = av * prod[c]
            h_sc[c, rows_of(j), :] = local[c]
            a_sc[c, rows_of(j), :] = prod[c]

    carry_in = carry_sc[...]
    carry_out = []
    seg_carry = []
    for c in range(lane_tiles):
        state = carry_in[:, c * LANES:(c + 1) * LANES]
        rows = [None] * SUBLANES
        for s in (range(SUBLANES - 1, -1, -1) if reverse else range(SUBLANES)):
            rows[s] = state
            state = local[c][s:s + 1] + prod[c][s:s + 1] * state
        seg_carry.append(cat(rows))
        carry_out.append(state)
    carry_sc[...] = jnp.concatenate(carry_out, axis=-1)

    for j in order:
        for c in range(lane_tiles):
            h_sc[c, rows_of(j), :] = h_sc[c, rows_of(j), :] + a_sc[c, rows_of(j), :] * seg_carry[c]
    for c in range(lane_tiles):
        h_ref[:, c * LANES:(c + 1) * LANES] = h_sc[c]


def _lru_scan(u, cw, cb, wa_half, ba_half, wx_half, bx_half, lam, *, reverse, lp):
    n = u.shape[0]
    d = D_MODEL
    t = SCAN_CHUNK
    halo = 2 * SUBLANES
    n_chunks = n // t
    cpb = lp // t
    t_h = t // halo
    n_h = n // halo

    def chunk_of(i):
        return (n_chunks - 1 - i) if reverse else i

    kern = functools.partial(_lru_scan_kernel, reverse=reverse, chunks_per_batch=cpb, lp=lp)
    full2 = lambda shape: pl.BlockSpec(shape, lambda i: (0, 0))
    full3 = lambda shape: pl.BlockSpec(shape, lambda i: (0, 0, 0))
    tile_major = pltpu.VMEM((d // LANES, t, LANES), F32)
    return pl.pallas_call(
        kern,
        out_shape=jax.ShapeDtypeStruct((n, d), F32),
        grid=(n_chunks,),
        in_specs=[
            pl.BlockSpec((t, d), lambda i: (chunk_of(i), 0)),
            pl.BlockSpec((halo, d), lambda i: (jnp.maximum(chunk_of(i) * t_h - 1, 0), 0)),
            pl.BlockSpec((halo, d), lambda i: (jnp.minimum((chunk_of(i) + 1) * t_h, n_h - 1), 0)),
            full2((CONV_W, d)), full2((1, d)),
            full3((LRU_BLOCKS, LRU_BLOCK, LRU_BLOCK)), full2((1, d)),
            full3((LRU_BLOCKS, LRU_BLOCK, LRU_BLOCK)), full2((1, d)),
            full2((1, d)),
        ],
        out_specs=pl.BlockSpec((t, d), lambda i: (chunk_of(i), 0)),
        scratch_shapes=[tile_major, tile_major, tile_major, pltpu.VMEM((1, d), F32)],
        compiler_params=_params("arbitrary"),
        name="lru_scan_bwd" if reverse else "lru_scan_fwd",
    )(u, u, u, cw, cb, wa_half, ba_half, wx_half, bx_half, lam)


def _gelu_tanh(y):
    c = 0.7978845608028654
    return y * (0.5 * (1.0 + jnp.tanh(c * (y + 0.044715 * (y * y * y)))))


def _lru_out_kernel(hf_ref, hb_ref, y_ref, res_ref, w_ref, g_ref, b_ref, o_ref):
    gated = (hf_ref[...] + hb_ref[...]) * _gelu_tanh(y_ref[...].astype(F32))
    mix = jnp.dot(gated.astype(BF16), w_ref[...], preferred_element_type=F32)
    o_ref[...] = _layer_norm(DN_ALPHA * res_ref[...] + mix, g_ref[...], b_ref[...])


def _lru_out(hf, hb, u, res, w_bf16, g, b):
    n, d = res.shape
    row = lambda j: pl.BlockSpec((ROW_TILE, d), lambda i, j=j: (i, j))
    vec = pl.BlockSpec((1, d), lambda i: (0, 0))
    return pl.pallas_call(
        _lru_out_kernel,
        out_shape=jax.ShapeDtypeStruct((n, d), F32),
        grid=(n // ROW_TILE,),
        in_specs=[row(0), row(0), row(1), row(0), pl.BlockSpec((d, d), lambda i: (0, 0)), vec, vec],
        out_specs=row(0),
        compiler_params=_params("parallel"),
        name="lru_out",
    )(hf, hb, u, res, w_bf16, g, b)


def _pack_bf16_pairs(x):
    half = x.shape[1] // 2
    lo = pltpu.bitcast(x[:, :half].astype(BF16).astype(F32), U32)
    hi = pltpu.bitcast(x[:, half:].astype(BF16).astype(F32), U32)
    return pltpu.bitcast((lo >> 16) | (hi & jnp.uint32(0xFFFF0000)), I32)


def _unpack_bf16_pairs(packed, dtype=BF16):
    u = pltpu.bitcast(packed, U32)
    lo = pltpu.bitcast(u << 16, F32).astype(dtype)
    hi = pltpu.bitcast(u & jnp.uint32(0xFFFF0000), F32).astype(dtype)
    return jnp.concatenate([lo, hi], axis=-1)


def _router_kernel(h_ref, wt_ref, b_ref, idx_ref, gate_ref, rank_ref, cnt_ref, hp_ref, base_sc):
    tm = h_ref.shape[0]
    h = h_ref[...]
    hp_ref[...] = _pack_bf16_pairs(h)

    @pl.when(pl.program_id(0) == 0)
    def _():
        base_sc[...] = jnp.zeros_like(base_sc)

    wt = wt_ref[...]
    h_hi = h.astype(BF16)
    h_lo = (h - h_hi.astype(F32)).astype(BF16)
    w_hi = wt.astype(BF16)
    w_lo = (wt - w_hi.astype(F32)).astype(BF16)
    contract_last = (((1,), (1,)), ((), ()))
    mm = lambda a, b: lax.dot_general(a, b, contract_last, preferred_element_type=F32)
    logits = mm(w_hi, h_hi) + (mm(w_hi, h_lo) + mm(w_lo, h_hi)) + b_ref[...]

    expert = lax.broadcasted_iota(I32, (N_EXPERTS, tm), 0).astype(F32)
    vals = logits
    idx_rows, val_rows, onehots = [], [], []
    for _ in range(TOP_K):
        m = jnp.max(vals, axis=0, keepdims=True)
        idx = jnp.min(jnp.where(vals == m, expert, float(N_EXPERTS)), axis=0, keepdims=True)
        hit = expert == idx
        onehots.append(jnp.where(hit, 1.0, 0.0))
        idx_rows.append(idx)
        val_rows.append(m)
        vals = jnp.where(hit, -jnp.inf, vals)

    onehot_all = jnp.concatenate(onehots, axis=0)
    t_from = lax.broadcasted_iota(I32, (tm, tm), 0)
    t_to = lax.broadcasted_iota(I32, (tm, tm), 1)
    earlier = jnp.where(t_from < t_to, 1.0, 0.0).astype(BF16)
    before = jnp.dot(onehot_all.astype(BF16), earlier, preferred_element_type=F32)
    base = base_sc[...]
    starts = []
    for k in range(TOP_K):
        starts.append(base)
        base = base + jnp.sum(onehots[k], axis=1, keepdims=True)
    contrib = onehot_all * (jnp.concatenate(starts, axis=0) + before)
    rank_rows = [jnp.sum(contrib[k * N_EXPERTS:(k + 1) * N_EXPERTS], axis=0, keepdims=True)
                 for k in range(TOP_K)]

    top_vals = jnp.concatenate(val_rows, axis=0)
    e = jnp.exp(top_vals - top_vals[0:1])
    idx_ref[...] = jnp.concatenate(idx_rows, axis=0).astype(I32)
    gate_ref[...] = e / jnp.sum(e, axis=0, keepdims=True)
    rank_ref[...] = jnp.concatenate(rank_rows, axis=0).astype(I32)
    base_sc[...] = base
    cnt_ref[...] = base.astype(I32)


def _router(h, w, b):
    n, d = h.shape
    tm = ROUTE_TILE
    out4 = lambda dt: jax.ShapeDtypeStruct((TOP_K, n), dt)
    spec4 = pl.BlockSpec((TOP_K, tm), lambda i: (0, i))
    return pl.pallas_call(
        _router_kernel,
        out_shape=(out4(I32), out4(F32), out4(I32), jax.ShapeDtypeStruct((N_EXPERTS, 1), I32),
                   jax.ShapeDtypeStruct((n, d // 2), I32)),
        grid=(n // tm,),
        in_specs=[pl.BlockSpec((tm, d), lambda i: (i, 0)),
                  pl.BlockSpec((N_EXPERTS, d), lambda i: (0, 0)),
                  pl.BlockSpec((N_EXPERTS, 1), lambda i: (0, 0))],
        out_specs=(spec4, spec4, spec4, pl.BlockSpec((N_EXPERTS, 1), lambda i: (0, 0)),
                   pl.BlockSpec((tm, d // 2), lambda i: (i, 0))),
        scratch_shapes=[pltpu.VMEM((N_EXPERTS, 1), F32)],
        compiler_params=_params("arbitrary"),
        name="router",
    )(h, w.T, b[:, None])


def _sc_gather_rows(table, idx, name):
    rows, width = table.shape
    total = idx.shape[0]
    chunk = min(SC_MAX_INDICES, SC_ROW_BUFFER_BYTES // (width * table.dtype.itemsize))
    per_worker = total // SC_WORKERS
    n_chunks = per_worker // chunk
    assert total == SC_WORKERS * n_chunks * chunk and n_chunks % 2 == 0
    mesh = plsc.VectorSubcoreMesh(core_axis_name="c", subcore_axis_name="s",
                                  num_cores=SC_CORES, num_subcores=SC_SUBCORES)

    def body(table_hbm, idx_hbm, out_hbm, idx_v, rows0, rows1, gsem0, gsem1, psem0, psem1):
        worker = lax.axis_index("s") * SC_CORES + lax.axis_index("c")
        base = worker * per_worker
        pltpu.sync_copy(idx_hbm.at[worker], idx_v)
        bufs = (rows0, rows1)
        gsems = (gsem0, gsem1)
        psems = (psem0, psem1)

        def gather(c, slot):
            return pltpu.make_async_copy(table_hbm.at[idx_v.at[c]], bufs[slot], gsems[slot])

        def put(c, slot):
            return pltpu.make_async_copy(bufs[slot], out_hbm.at[pl.ds(base + c * chunk, chunk)], psems[slot])

        gather(0, 0).start()

        @pl.loop(0, n_chunks, step=2)
        def _(c0):
            for slot in range(2):
                c = c0 + slot

                @pl.when(c + 1 < n_chunks)
                def _():
                    @pl.when(c >= 1)
                    def _():
                        put(c - 1, 1 - slot).wait()
                    gather(c + 1, 1 - slot).start()

                gather(c, slot).wait()
                put(c, slot).start()

        put(n_chunks - 2, 0).wait()
        put(n_chunks - 1, 1).wait()

    return pl.kernel(
        body,
        out_type=jax.ShapeDtypeStruct((total, width), table.dtype),
        mesh=mesh,
        scratch_types=[pltpu.VMEM((n_chunks, chunk), I32),
                       pltpu.VMEM((chunk, width), table.dtype), pltpu.VMEM((chunk, width), table.dtype),
                       pltpu.SemaphoreType.DMA, pltpu.SemaphoreType.DMA,
                       pltpu.SemaphoreType.DMA, pltpu.SemaphoreType.DMA],
        name=name,
    )(table, idx.reshape(SC_WORKERS, n_chunks, chunk))


def _sc_scatter_rows(src, dest, out_rows, name):
    n, width = src.shape
    fan = dest.shape[0]
    per_worker = n // SC_WORKERS
    chunk = max(c for c in range(SUBLANES, SC_MAX_INDICES + 1, SUBLANES) if per_worker % (2 * c) == 0)
    n_chunks = per_worker // chunk
    assert n == SC_WORKERS * n_chunks * chunk and n_chunks % 2 == 0
    mesh = plsc.VectorSubcoreMesh(core_axis_name="c", subcore_axis_name="s",
                                  num_cores=SC_CORES, num_subcores=SC_SUBCORES)

    def body(src_hbm, idx_hbm, out_hbm, idx_v, rows0, rows1, lsem0, lsem1, ssem0, ssem1):
        worker = lax.axis_index("s") * SC_CORES + lax.axis_index("c")
        base = worker * per_worker
        pltpu.sync_copy(idx_hbm.at[worker], idx_v)
        bufs = (rows0, rows1)
        lsems = (lsem0, lsem1)
        ssems = (ssem0, ssem1)

        def load(c, slot):
            return pltpu.make_async_copy(src_hbm.at[pl.ds(base + c * chunk, chunk)], bufs[slot], lsems[slot])

        def scatter(c, k, slot):
            return pltpu.make_async_copy(bufs[slot], out_hbm.at[idx_v.at[k * n_chunks + c]], ssems[slot])

        load(0, 0).start()

        @pl.loop(0, n_chunks, step=2)
        def _(c0):
            for slot in range(2):
                c = c0 + slot

                @pl.when(c + 1 < n_chunks)
                def _():
                    @pl.when(c >= 1)
                    def _():
                        for k in range(fan):
                            scatter(c - 1, k, 1 - slot).wait()
                    load(c + 1, 1 - slot).start()

                load(c, slot).wait()
                for k in range(fan):
                    scatter(c, k, slot).start()

        for k in range(fan):
            scatter(n_chunks - 2, k, 0).wait()
        for k in range(fan):
            scatter(n_chunks - 1, k, 1).wait()

    idx = dest.reshape(fan, SC_WORKERS, n_chunks, chunk).transpose(1, 0, 2, 3)
    return pl.kernel(
        body,
        out_type=jax.ShapeDtypeStruct((out_rows, width), src.dtype),
        mesh=mesh,
        scratch_types=[pltpu.VMEM((fan * n_chunks, chunk), I32),
                       pltpu.VMEM((chunk, width), src.dtype), pltpu.VMEM((chunk, width), src.dtype),
                       pltpu.SemaphoreType.DMA, pltpu.SemaphoreType.DMA,
                       pltpu.SemaphoreType.DMA, pltpu.SemaphoreType.DMA],
        name=name,
    )(src, idx.reshape(SC_WORKERS, fan * n_chunks, chunk))


def _expert_kernel(be_ref, nu_ref, xs_ref, wgu_ref, bgu_ref, wd_ref, bd_ref, y_ref, wgu_sc, wd_sc):
    i = pl.program_id(0)
    e = be_ref[i]
    e_prev = be_ref[jnp.maximum(i - 1, 0)]
    d = wd_sc.shape[0]

    @pl.when((i == 0) | (e != e_prev))
    def _():
        for r in range(0, d, LANES):
            wgu_sc[r:r + LANES, :] = wgu_ref[0, r:r + LANES, :].astype(BF16)
            wd_sc[r:r + LANES, :] = wd_ref[0, r:r + LANES, :].astype(BF16)

    @pl.when(i < nu_ref[0])
    def _():
        x = _unpack_bf16_pairs(xs_ref[...])
        h = jnp.dot(x, wgu_sc[...], preferred_element_type=F32) + bgu_ref[0]
        glu = jnp.minimum(h[:, :d], SWIGLU_LIMIT)
        lin = jnp.clip(h[:, d:], -SWIGLU_LIMIT, SWIGLU_LIMIT)
        act = glu * _sigmoid(SWIGLU_ALPHA * glu) * (lin + 1.0)
        y = jnp.dot(act.astype(BF16), wd_sc[...], preferred_element_type=F32) + bd_ref[0]
        y_ref[...] = _pack_bf16_pairs(y)

    @pl.when(i >= nu_ref[0])
    def _():
        y_ref[...] = jnp.zeros_like(y_ref)


def _experts(block_e, n_used, xs, w_gu, b_gu, w_down, b_down, layer):
    cap = xs.shape[0]
    d = D_MODEL
    tm = EXPERT_TILE
    n_blocks = cap // tm
    grid_spec = pltpu.PrefetchScalarGridSpec(
        num_scalar_prefetch=2,
        grid=(n_blocks,),
        in_specs=[
            pl.BlockSpec((tm, d // 2), lambda i, be, nu: (i, 0)),
            pl.BlockSpec((None, 1, d, 2 * d), lambda i, be, nu: (layer, be[i], 0, 0)),
            pl.BlockSpec((1, 1, 2 * d), lambda i, be, nu: (be[i], 0, 0)),
            pl.BlockSpec((None, 1, d, d), lambda i, be, nu: (layer, be[i], 0, 0)),
            pl.BlockSpec((1, 1, d), lambda i, be, nu: (be[i], 0, 0)),
        ],
        out_specs=pl.BlockSpec((tm, d // 2), lambda i, be, nu: (i, 0)),
        scratch_shapes=[pltpu.VMEM((d, 2 * d), BF16), pltpu.VMEM((d, d), BF16)],
    )
    return pl.pallas_call(
        _expert_kernel,
        out_shape=jax.ShapeDtypeStruct((cap, d // 2), I32),
        grid_spec=grid_spec,
        compiler_params=_params("arbitrary"),
        name="moe_experts",
    )(block_e, n_used, xs, w_gu, b_gu[layer][:, None, :], w_down, b_down[layer][:, None, :])


def _combine_kernel(gate_ref, res_ref, yk_ref, g_ref, b_ref, o_ref):
    d = res_ref.shape[1]
    gates = gate_ref[...]
    ffn = gates[:, 0:1] * _unpack_bf16_pairs(yk_ref[0], F32)
    for k in range(1, TOP_K):
        ffn = ffn + gates[:, k:k + 1] * _unpack_bf16_pairs(yk_ref[k], F32)
    o_ref[...] = _layer_norm(DN_ALPHA * res_ref[...] + ffn, g_ref[...], b_ref[...])


def _combine(gates, res, yk, g, b, drop_front=None):
    n, d = res.shape
    vec_spec = lambda nd: pl.BlockSpec((1, d), lambda *_: (0, 0))
    if drop_front is None:
        tt = MOVE_TILE
        grid = (n // tt,)
        src = lambda i: i
        dst = src
        out_rows = n
    else:
        batch, lp = drop_front
        tt = LANES
        tiles = lp // tt
        grid = (batch, tiles - 1)
        src = lambda b, j: b * tiles + 1 + j
        dst = lambda b, j: b * (tiles - 1) + j
        out_rows = n - batch * tt
    return pl.pallas_call(
        _combine_kernel,
        out_shape=jax.ShapeDtypeStruct((out_rows, d), F32),
        grid=grid,
        in_specs=[pl.BlockSpec((tt, TOP_K), lambda *i: (src(*i), 0)),
                  pl.BlockSpec((tt, d), lambda *i: (src(*i), 0)),
                  pl.BlockSpec((TOP_K, tt, d // 2), lambda *i: (0, src(*i), 0)),
                  vec_spec(len(grid)), vec_spec(len(grid))],
        out_specs=pl.BlockSpec((tt, d), lambda *i: (dst(*i), 0)),
        compiler_params=_params(*(["parallel"] * len(grid))),
        name="moe_combine",
    )(gates, res, yk, g, b)


def _moe_layer(h, router_w, router_b, w_gu, b_gu, w_down, b_down, ln_g, ln_b, layer, drop_front=None):
    n, d = h.shape
    tm = EXPERT_TILE
    idx, gates, rank, counts, h_packed = _router(h, router_w, router_b)
    counts = counts[:, 0]
    padded = (counts + tm - 1) // tm * tm
    pad_end = jnp.cumsum(padded)
    pad_start = pad_end - padded
    group_start = jnp.sum(jnp.where(idx[:, :, None] == jnp.arange(N_EXPERTS, dtype=I32), pad_start, 0), axis=-1)
    dest = group_start + rank
    n_blocks = -(-(n * TOP_K + N_EXPERTS * (tm - 1)) // tm)
    block_start = jnp.arange(n_blocks, dtype=I32) * tm
    block_e = jnp.minimum(jnp.sum((pad_end[None, :] <= block_start[:, None]).astype(I32), axis=1),
                          N_EXPERTS - 1)
    n_used = (pad_end[-1:] // tm).astype(I32)

    xs = _sc_scatter_rows(h_packed, dest, n_blocks * tm, "moe_dispatch")
    ys = _experts(block_e, n_used, xs, w_gu, b_gu, w_down, b_down, layer)
    yk = _sc_gather_rows(ys, dest.reshape(TOP_K * n), "moe_collect").reshape(TOP_K, n, d // 2)
    return _combine(gates.T, h, yk, ln_g[None, :], ln_b[None, :], drop_front)


def _qkv_kernel(x_ref, w_ref, o_ref):
    xb = x_ref[...].astype(BF16)
    d = x_ref.shape[1]
    for part in range(3):
        acc = jnp.dot(xb, w_ref[:, part * d:(part + 1) * d], preferred_element_type=F32)
        if part == 0:
            acc = acc * (NA_HEAD_DIM ** -0.5)
        for p in range(HEAD_PAIRS):
            o_ref[part * HEAD_PAIRS + p] = acc[:, p * LANES:(p + 1) * LANES].astype(BF16)


def _qkv(x, w_bf16):
    n, d = x.shape
    return pl.pallas_call(
        _qkv_kernel,
        out_shape=jax.ShapeDtypeStruct((3 * HEAD_PAIRS, n, LANES), BF16),
        grid=(n // ROW_TILE,),
        in_specs=[pl.BlockSpec((ROW_TILE, d), lambda i: (i, 0)),
                  pl.BlockSpec((d, 3 * d), lambda i: (0, 0))],
        out_specs=pl.BlockSpec((3 * HEAD_PAIRS, ROW_TILE, LANES), lambda i: (0, i, 0)),
        compiler_params=_params("parallel"),
        name="na_qkv",
    )(x, w_bf16)


def _na_kernel(q_ref, k_ref, v_ref, bias_a_ref, bias_b_ref, mb_ref, o_ref, *, rows):
    s = pl.program_id(1)
    w = GRID_W
    low = lax.broadcasted_iota(I32, (w, LANES), 1) < NA_HEAD_DIM
    contract_last = (((1,), (1,)), ((), ()))
    meta0 = FRONT_PAD

    def stacked_q(p, q0):
        qp = q_ref[p, q0:q0 + w, :]
        zero = jnp.zeros_like(qp)
        return jnp.concatenate([jnp.where(low, qp, zero), jnp.where(low, zero, qp)], axis=0)

    def attend_all(q0, k0, bias_ref):
        scores = []
        for p in range(HEAD_PAIRS):
            q2 = stacked_q(p, q0)
            s_meta = lax.dot_general(q2, k_ref[p, meta0:meta0 + N_META, :], contract_last,
                                     preferred_element_type=F32) + mb_ref[p]
            s_win = None
            if k0 is not None:
                s_win = lax.dot_general(q2, k_ref[p, pl.ds(k0, NA_KH * w), :], contract_last,
                                        preferred_element_type=F32) + bias_ref[0, p]
            scores.append((s_meta, s_win))
        outs = []
        for p in range(HEAD_PAIRS):
            s_meta, s_win = scores[p]
            m = jnp.max(s_meta, axis=-1, keepdims=True)
            if s_win is not None:
                m = jnp.maximum(m, jnp.max(s_win, axis=-1, keepdims=True))
            p_meta = jnp.exp(s_meta - m)
            denom = jnp.sum(p_meta, axis=-1, keepdims=True)
            o = jnp.dot(p_meta.astype(BF16), v_ref[p, meta0:meta0 + N_META, :], preferred_element_type=F32)
            if s_win is not None:
                p_win = jnp.exp(s_win - m)
                denom = denom + jnp.sum(p_win, axis=-1, keepdims=True)
                o = o + jnp.dot(p_win.astype(BF16), v_ref[p, pl.ds(k0, NA_KH * w), :],
                                preferred_element_type=F32)
            o = o / denom
            outs.append(jnp.where(low, o[:w], o[w:]))
        return outs

    @pl.when(s == 0)
    def _():
        qrow = lax.broadcasted_iota(I32, (w, LANES), 0)
        for p, o in enumerate(attend_all(LANES - w, None, None)):
            o_ref[p, 0:LANES - w, :] = jnp.zeros((LANES - w, LANES), o_ref.dtype)
            o_ref[p, LANES - w:LANES, :] = jnp.where(qrow >= w - N_META, o, 0.0).astype(o_ref.dtype)

    @pl.when(s >= 1)
    def _():
        for half, bias_ref in enumerate((bias_a_ref, bias_b_ref)):
            r = 2 * (s - 1) + half
            rs = jnp.clip(r - NA_KH // 2, 0, rows - NA_KH)
            k0 = pl.multiple_of(LANES + rs * w, w)
            for p, o in enumerate(attend_all(half * w, k0, bias_ref)):
                o_ref[p, half * w:(half + 1) * w, :] = o.astype(o_ref.dtype)


def _na_bias_table(rpb, rows):
    del rows
    w = GRID_W
    q = jnp.arange(w)
    col_start = jnp.clip(q - NA_KW // 2, 0, w - NA_KW)
    c = jnp.arange(w)
    in_win = (c[None, :] >= col_start[:, None]) & (c[None, :] < col_start[:, None] + NA_KW)
    pad = w - NA_KW
    rp = jnp.pad(rpb.astype(F32), ((0, 0), (0, 0), (pad, pad)))
    toeplitz = jnp.stack([rp[:, :, w - 1 - qq:2 * w - 1 - qq] for qq in range(w)], axis=2)
    toeplitz = jnp.where(in_win[None, None], toeplitz, NEG_BIG)
    tabs = []
    for v in range(NA_KH):
        tv = toeplitz[:, NA_KH - 1 - v:2 * NA_KH - 1 - v]
        tabs.append(jnp.transpose(tv, (0, 2, 1, 3)).reshape(NA_HEADS, w, NA_KH * w))
    return jnp.stack(tabs, axis=0)


def _na_attention(qkv, bias_tab, meta_bias, *, batch, lp):
    n = qkv.shape[1]
    w = GRID_W
    rows = (lp - LANES) // w
    tiles = lp // LANES

    def variant(half):
        def index(b, s):
            r = jnp.clip(2 * (s - 1) + half, 0, rows - 1)
            return (r - jnp.clip(r - NA_KH // 2, 0, rows - NA_KH), 0, 0, 0)
        return index

    kern = functools.partial(_na_kernel, rows=rows)
    bias_block = (1, HEAD_PAIRS, 2 * w, NA_KH * w)
    bias_tab = bias_tab.reshape(NA_KH, HEAD_PAIRS, 2 * w, NA_KH * w)
    return pl.pallas_call(
        kern,
        out_shape=jax.ShapeDtypeStruct((HEAD_PAIRS, n, LANES), BF16),
        grid=(batch, tiles),
        in_specs=[
            pl.BlockSpec((HEAD_PAIRS, LANES, LANES), lambda b, s: (0, b * tiles + s, 0)),
            pl.BlockSpec((HEAD_PAIRS, lp, LANES), lambda b, s: (1, b, 0)),
            pl.BlockSpec((HEAD_PAIRS, lp, LANES), lambda b, s: (2, b, 0)),
            pl.BlockSpec(bias_block, variant(0)),
            pl.BlockSpec(bias_block, variant(1)),
            pl.BlockSpec((HEAD_PAIRS, 2 * w, N_META), lambda b, s: (0, 0, 0)),
        ],
        out_specs=pl.BlockSpec((HEAD_PAIRS, LANES, LANES), lambda b, s: (0, b * tiles + s, 0)),
        compiler_params=_params("parallel", "arbitrary"),
        name="na_attention",
    )(qkv, qkv, qkv, bias_tab, bias_tab,
      jnp.repeat(meta_bias, w, axis=0).reshape(HEAD_PAIRS, 2 * w, N_META))


def _na_out_kernel(o_ref, res_ref, w_ref, g_ref, b_ref, out_ref):
    att = jnp.concatenate([o_ref[p] for p in range(HEAD_PAIRS)], axis=-1)
    mix = jnp.dot(att, w_ref[...], preferred_element_type=F32)
    out_ref[...] = _layer_norm(DN_ALPHA * res_ref[...] + mix, g_ref[...], b_ref[...])


def _na_out(o, res, w_bf16, g, b):
    n, d = res.shape
    vec = pl.BlockSpec((1, d), lambda i: (0, 0))
    return pl.pallas_call(
        _na_out_kernel,
        out_shape=jax.ShapeDtypeStruct((n, d), F32),
        grid=(n // ROW_TILE,),
        in_specs=[pl.BlockSpec((HEAD_PAIRS, ROW_TILE, LANES), lambda i: (0, i, 0)),
                  pl.BlockSpec((ROW_TILE, d), lambda i: (i, 0)),
                  pl.BlockSpec((d, d), lambda i: (0, 0)), vec, vec],
        out_specs=pl.BlockSpec((ROW_TILE, d), lambda i: (i, 0)),
        compiler_params=_params("parallel"),
        name="na_out",
    )(o, res, w_bf16, g, b)


def kernel(x, meta_tokens, lru_w_in, lru_conv_w, lru_conv_b, lru_wa, lru_ba, lru_wx, lru_bx, lru_lambda, lru_w_out, na_w_qkv, na_rpb, na_meta_bias, na_w_out, ln_mix_g, ln_mix_b, router_w, router_b, moe_w_gu, moe_b_gu, moe_w_down, moe_b_down, ln_ffn_g, ln_ffn_b):
    batch, seq, d = x.shape
    lp = LANES + seq
    assert d == D_MODEL and seq % GRID_W == 0 and seq // GRID_W >= NA_KH
    assert lp % SCAN_CHUNK == 0 and (batch * lp) % ROW_TILE == 0
    n = batch * lp

    front = jnp.zeros((batch, FRONT_PAD, d), x.dtype)
    meta = jnp.broadcast_to(meta_tokens[None].astype(x.dtype), (batch, N_META, d))
    h = jnp.concatenate([front, meta, x], axis=1).reshape(n, d)

    u = _matmul(h, lru_w_in[0].astype(BF16))
    row = lambda v: v[None, :]
    scans = []
    for direction, reverse in ((0, False), (1, True)):
        scans.append(_lru_scan(
            u, lru_conv_w[0], row(lru_conv_b[0]),
            (0.5 * lru_wa[0, direction]).astype(BF16), row(0.5 * lru_ba[0, direction]),
            (0.5 * lru_wx[0, direction]).astype(BF16), row(0.5 * lru_bx[0, direction]),
            row(lru_lambda[0, direction]), reverse=reverse, lp=lp))
    h = _lru_out(scans[0], scans[1], u, h, lru_w_out[0].astype(BF16), row(ln_mix_g[0]), row(ln_mix_b[0]))
    h = _moe_layer(h, router_w[0], router_b[0], moe_w_gu, moe_b_gu, moe_w_down, moe_b_down,
                   ln_ffn_g[0], ln_ffn_b[0], 0)

    qkv = _qkv(h, na_w_qkv[0].astype(BF16))
    att = _na_attention(qkv, _na_bias_table(na_rpb[0], seq // GRID_W), na_meta_bias[0].astype(F32),
                        batch=batch, lp=lp)
    h = _na_out(att, h, na_w_out[0].astype(BF16), row(ln_mix_g[1]), row(ln_mix_b[1]))
    h = _moe_layer(h, router_w[1], router_b[1], moe_w_gu, moe_b_gu, moe_w_down, moe_b_down,
                   ln_ffn_g[1], ln_ffn_b[1], 1, drop_front=(batch, lp))

    return h.reshape(batch, seq, d)
```

```python
import functools

import jax
import jax.numpy as jnp
from jax import lax
from jax.experimental import pallas as pl
from jax.experimental.pallas import tpu as pltpu
from jax.experimental.pallas import tpu_sc as plsc

F32 = jnp.float32
BF16 = jnp.bfloat16
I32 = jnp.int32
U32 = jnp.uint32

D_MODEL = 1024
N_META = 16
GRID_W = 64
LRU_BLOCKS = 4
LRU_BLOCK = D_MODEL // LRU_BLOCKS
CONV_W = 4
LRU_C = 8.0
NA_HEADS = 16
NA_HEAD_DIM = D_MODEL // NA_HEADS
NA_KH = 8
NA_KW = 16
N_EXPERTS = 32
TOP_K = 4
SWIGLU_LIMIT = 7.0
SWIGLU_ALPHA = 1.702
DEPTH = 2
DN_ALPHA = (2.0 * DEPTH) ** 0.25
LN_EPS = 1e-5

LANES = 128
SUBLANES = 8
FRONT_PAD = LANES - N_META
HEAD_PAIRS = D_MODEL // LANES
NEG_BIG = -1e30

ROW_TILE = 512
SCAN_CHUNK = 352
ROUTE_TILE = 512
MOVE_TILE = 256
EXPERT_TILE = 512
MOE_PARTS = 2
VMEM_LIMIT = 56 << 20

SC_CORES = 2
SC_SUBCORES = 16
SC_WORKERS = SC_CORES * SC_SUBCORES
SC_MAX_INDICES = 64
SC_ROW_BUFFER_BYTES = 128 << 10


def _params(*sem):
    return pltpu.CompilerParams(dimension_semantics=sem, vmem_limit_bytes=VMEM_LIMIT)


def _sigmoid(x):
    return 0.5 * jnp.tanh(0.5 * x) + 0.5


def _layer_norm(x, g, b):
    mu = jnp.mean(x, axis=-1, keepdims=True)
    xc = x - mu
    var = jnp.mean(xc * xc, axis=-1, keepdims=True)
    return xc * lax.rsqrt(var + LN_EPS) * g + b


def _matmul_kernel(x_ref, w_ref, o_ref):
    o_ref[...] = jnp.dot(x_ref[...].astype(BF16), w_ref[...], preferred_element_type=F32).astype(o_ref.dtype)


def _matmul(x, w_bf16):
    n, k = x.shape
    m = w_bf16.shape[1]
    return pl.pallas_call(
        _matmul_kernel,
        out_shape=jax.ShapeDtypeStruct((n, m), BF16),
        grid=(n // ROW_TILE,),
        in_specs=[pl.BlockSpec((ROW_TILE, k), lambda i: (i, 0)),
                  pl.BlockSpec((k, m), lambda i: (0, 0))],
        out_specs=pl.BlockSpec((ROW_TILE, m), lambda i: (i, 0)),
        compiler_params=_params("parallel"),
        name="in_proj",
    )(x, w_bf16)


def _lru_scan_kernel(xr_ref, xp_ref, xn_ref, cw_ref, cb_ref, wa_ref, ba_ref, wx_ref, bx_ref,
                     lam_ref, *rest, reverse, chunks_per_batch, fused):
    if fused:
        hf_ref, y_ref, res_ref, wout_ref, g_ref, beta_ref, h_ref, a_sc, b_sc, h_sc, carry_sc = rest
    else:
        h_ref, a_sc, b_sc, h_sc, carry_sc = rest
    t_rows = xr_ref.shape[0]
    seg = t_rows // SUBLANES
    halo = xp_ref.shape[0]
    lane_tiles = xr_ref.shape[1] // LANES
    step = pl.program_id(0)
    chunk = (pl.num_programs(0) - 1 - step) if reverse else step
    cib = chunk % chunks_per_batch
    first_chunk = cib == 0
    last_chunk = cib == chunks_per_batch - 1
    row8 = lax.broadcasted_iota(I32, (SUBLANES, 1), 0)
    cat = lambda parts: jnp.concatenate(parts, axis=0)

    head_pos = cib * t_rows + lax.broadcasted_iota(I32, (LANES, 1), 0)
    head_real = head_pos >= FRONT_PAD
    xr = xr_ref[...].astype(F32)
    xr = cat([jnp.where(head_real, xr[:LANES], 0.0), xr[LANES:]])
    prev_pos = cib * t_rows - halo + lax.broadcasted_iota(I32, (halo, 1), 0)
    prev = jnp.where(prev_pos >= FRONT_PAD, xp_ref[...].astype(F32), 0.0)
    nxt = jnp.where(last_chunk, 0.0, xn_ref[...].astype(F32))

    xm1 = pltpu.roll(xr, 1, 0)
    xm2 = pltpu.roll(xr, 2, 0)
    xp1 = pltpu.roll(xr, t_rows - 1, 0)
    xm1 = cat([jnp.where(row8 == 0, prev[halo - 1:halo], xm1[:SUBLANES]), xm1[SUBLANES:]])
    xm2_head = jnp.where(row8 == 0, prev[halo - 2:halo - 1],
                         jnp.where(row8 == 1, prev[halo - 1:halo], xm2[:SUBLANES]))
    xm2 = cat([xm2_head, xm2[SUBLANES:]])
    xp1 = cat([xp1[:t_rows - SUBLANES], jnp.where(row8 == SUBLANES - 1, nxt[0:1], xp1[t_rows - SUBLANES:])])
    cw = cw_ref[...]
    xc = cw[0:1, :] * xm2 + cw[1:2, :] * xm1 + cw[2:3, :] * xr + cw[3:4, :] * xp1 + cb_ref[...]

    xcb = xc.astype(BF16)
    za, zx = [], []
    for blk in range(LRU_BLOCKS):
        xblk = xcb[:, blk * LRU_BLOCK:(blk + 1) * LRU_BLOCK]
        za.append(jnp.dot(xblk, wa_ref[blk], preferred_element_type=F32))
        zx.append(jnp.dot(xblk, wx_ref[blk], preferred_element_type=F32))
    tanh_a = jnp.tanh(jnp.concatenate(za, axis=-1) + ba_ref[...])
    tanh_x = jnp.tanh(jnp.concatenate(zx, axis=-1) + bx_ref[...])

    z = -lam_ref[...]
    softplus = jnp.maximum(z, 0.0) + jnp.log1p(jnp.exp(-jnp.abs(z)))
    half_rate = (-0.5 * LRU_C) * softplus
    log_a = tanh_a * half_rate + half_rate
    a = jnp.exp(log_a)
    gap = jnp.tanh(log_a) * (-1.0 - a * a)
    mult = jnp.where(gap > 0.0, gap * lax.rsqrt(gap), 0.0)
    if reverse:
        tail = jnp.where(jnp.logical_and(last_chunk, row8 == SUBLANES - 1), 1.0, mult[t_rows - SUBLANES:])
        mult = cat([mult[:t_rows - SUBLANES], tail])
    else:
        mult = cat([jnp.where(head_pos == FRONT_PAD, 1.0, mult[:LANES]), mult[LANES:]])
    b = mult * ((0.5 * tanh_x + 0.5) * xc)
    b = cat([jnp.where(head_real, b[:LANES], 0.0), b[LANES:]])
    for c in range(lane_tiles):
        a_sc[c] = a[:, c * LANES:(c + 1) * LANES]
        b_sc[c] = b[:, c * LANES:(c + 1) * LANES]

    @pl.when(last_chunk if reverse else first_chunk)
    def _():
        carry_sc[...] = jnp.zeros_like(carry_sc)

    order = range(seg - 1, -1, -1) if reverse else range(seg)
    rows_of = lambda j: pl.ds(j, SUBLANES, stride=seg)

    local = [jnp.zeros((SUBLANES, LANES), F32)] * lane_tiles
    prod = [jnp.ones((SUBLANES, LANES), F32)] * lane_tiles
    for j in order:
        for c in range(lane_tiles):
            av = a_sc[c, rows_of(j), :]
            local[c] = av * local[c] + b_sc[c, rows_of(j), :]
            prod[c] = av * prod[c]
            h_sc[c, rows_of(j), :] = local[c]
            a_sc[c, rows_of(j), :] = prod[c]

    carry_in = carry_sc[...]
    carry_out = []
    seg_carry = []
    for c in range(lane_tiles):
        state = carry_in[:, c * LANES:(c + 1) * LANES]
        rows = [None] * SUBLANES
        for s in (range(SUBLANES - 1, -1, -1) if reverse else range(SUBLANES)):
            rows[s] = state
            state = local[c][s:s + 1] + prod[c][s:s + 1] * state
        seg_carry.append(cat(rows))
        carry_out.append(state)
    carry_sc[...] = jnp.concatenate(carry_out, axis=-1)

    for j in order:
        for c in range(lane_tiles):
            h_sc[c, rows_of(j), :] = h_sc[c, rows_of(j), :] + a_sc[c, rows_of(j), :] * seg_carry[c]
    states = jnp.concatenate([h_sc[c] for c in range(lane_tiles)], axis=-1)
    if fused:
        gated = (hf_ref[...] + states) * _gelu_tanh(y_ref[...].astype(F32))
        mix = jnp.dot(gated.astype(BF16), wout_ref[...], preferred_element_type=F32)
        h_ref[...] = _layer_norm(DN_ALPHA * res_ref[...] + mix, g_ref[...], beta_ref[...])
    else:
        h_ref[...] = states


def _lru_scan(u, cw, cb, wa_half, ba_half, wx_half, bx_half, lam, *, reverse, lp, fuse_out=None):
    n = u.shape[0]
    d = D_MODEL
    t = SCAN_CHUNK
    halo = 2 * SUBLANES
    n_chunks = n // t
    cpb = lp // t
    t_h = t // halo
    n_h = n // halo

    def chunk_of(i):
        return (n_chunks - 1 - i) if reverse else i

    kern = functools.partial(_lru_scan_kernel, reverse=reverse, chunks_per_batch=cpb, fused=fuse_out is not None)
    chunk_rows = lambda col: pl.BlockSpec((t, d), lambda i: (chunk_of(i), col))
    extra_specs, extra_args = [], ()
    if fuse_out is not None:
        h_other, res, w_out, ln_g, ln_b = fuse_out
        extra_specs = [chunk_rows(0), chunk_rows(1), chunk_rows(0),
                       pl.BlockSpec((d, d), lambda i: (0, 0)),
                       pl.BlockSpec((1, d), lambda i: (0, 0)), pl.BlockSpec((1, d), lambda i: (0, 0))]
        extra_args = (h_other, u, res, w_out, ln_g, ln_b)
    full2 = lambda shape: pl.BlockSpec(shape, lambda i: (0, 0))
    full3 = lambda shape: pl.BlockSpec(shape, lambda i: (0, 0, 0))
    tile_major = pltpu.VMEM((d // LANES, t, LANES), F32)
    return pl.pallas_call(
        kern,
        out_shape=jax.ShapeDtypeStruct((n, d), F32),
        grid=(n_chunks,),
        in_specs=[
            pl.BlockSpec((t, d), lambda i: (chunk_of(i), 0)),
            pl.BlockSpec((halo, d), lambda i: (jnp.maximum(chunk_of(i) * t_h - 1, 0), 0)),
            pl.BlockSpec((halo, d), lambda i: (jnp.minimum((chunk_of(i) + 1) * t_h, n_h - 1), 0)),
            full2((CONV_W, d)), full2((1, d)),
            full3((LRU_BLOCKS, LRU_BLOCK, LRU_BLOCK)), full2((1, d)),
            full3((LRU_BLOCKS, LRU_BLOCK, LRU_BLOCK)), full2((1, d)),
            full2((1, d)),
        ] + extra_specs,
        out_specs=pl.BlockSpec((t, d), lambda i: (chunk_of(i), 0)),
        scratch_shapes=[tile_major, tile_major, tile_major, pltpu.VMEM((1, d), F32)],
        compiler_params=_params("arbitrary"),
        name=("lru_scan_bwd" if reverse else "lru_scan_fwd") + ("_out" if fuse_out is not None else ""),
    )(u, u, u, cw, cb, wa_half, ba_half, wx_half, bx_half, lam, *extra_args)


def _gelu_tanh(y):
    c = 0.7978845608028654
    return y * (0.5 * (1.0 + jnp.tanh(c * (y + 0.044715 * (y * y * y)))))


def _pack_bf16_pairs(x):
    half = x.shape[1] // 2
    lo = pltpu.bitcast(x[:, :half].astype(BF16).astype(F32), U32)
    hi = pltpu.bitcast(x[:, half:].astype(BF16).astype(F32), U32)
    return pltpu.bitcast((lo >> 16) | (hi & jnp.uint32(0xFFFF0000)), I32)


def _unpack_bf16_pairs(packed, dtype=BF16):
    u = pltpu.bitcast(packed, U32)
    lo = pltpu.bitcast(u << 16, F32).astype(dtype)
    hi = pltpu.bitcast(u & jnp.uint32(0xFFFF0000), F32).astype(dtype)
    return jnp.concatenate([lo, hi], axis=-1)


def _router_kernel(h_ref, wt_ref, b_ref, real_ref, idx_ref, gate_ref, rank_ref, cnt_ref, hp_ref, base_sc):
    tm = h_ref.shape[0]
    h = h_ref[...]
    hp_ref[...] = _pack_bf16_pairs(h)

    @pl.when(pl.program_id(0) == 0)
    def _():
        base_sc[...] = jnp.zeros_like(base_sc)

    wt = wt_ref[...]
    h_hi = h.astype(BF16)
    h_lo = (h - h_hi.astype(F32)).astype(BF16)
    w_hi = wt.astype(BF16)
    w_lo = (wt - w_hi.astype(F32)).astype(BF16)
    contract_last = (((1,), (1,)), ((), ()))
    mm = lambda a, b: lax.dot_general(a, b, contract_last, preferred_element_type=F32)
    logits = mm(w_hi, h_hi) + (mm(w_hi, h_lo) + mm(w_lo, h_hi)) + b_ref[...]

    expert = lax.broadcasted_iota(I32, (N_EXPERTS, tm), 0).astype(F32)
    vals = logits
    idx_rows, val_rows, onehots = [], [], []
    for _ in range(TOP_K):
        m = jnp.max(vals, axis=0, keepdims=True)
        idx = jnp.min(jnp.where(vals == m, expert, float(N_EXPERTS)), axis=0, keepdims=True)
        hit = expert == idx
        onehots.append(jnp.where(hit, real_ref[...], 0.0))
        idx_rows.append(idx)
        val_rows.append(m)
        vals = jnp.where(hit, -jnp.inf, vals)

    onehot_all = jnp.concatenate(onehots, axis=0)
    t_from = lax.broadcasted_iota(I32, (tm, tm), 0)
    t_to = lax.broadcasted_iota(I32, (tm, tm), 1)
    earlier = jnp.where(t_from < t_to, 1.0, 0.0).astype(BF16)
    before = jnp.dot(onehot_all.astype(BF16), earlier, preferred_element_type=F32)
    base = base_sc[...]
    starts = []
    for k in range(TOP_K):
        starts.append(base)
        base = base + jnp.sum(onehots[k], axis=1, keepdims=True)
    contrib = onehot_all * (jnp.concatenate(starts, axis=0) + before)
    rank_rows = [jnp.sum(contrib[k * N_EXPERTS:(k + 1) * N_EXPERTS], axis=0, keepdims=True)
                 for k in range(TOP_K)]

    top_vals = jnp.concatenate(val_rows, axis=0)
    e = jnp.exp(top_vals - top_vals[0:1])
    idx_ref[...] = jnp.concatenate(idx_rows, axis=0).astype(I32)
    gate_ref[...] = e / jnp.sum(e, axis=0, keepdims=True)
    rank_ref[...] = jnp.concatenate(rank_rows, axis=0).astype(I32)
    base_sc[...] = base
    cnt_ref[...] = base.astype(I32)


def _router(h, w, b, real, part):
    d = h.shape[1]
    n = h.shape[0] // MOE_PARTS
    tm = ROUTE_TILE
    first = part * (n // tm)
    out4 = lambda dt: jax.ShapeDtypeStruct((TOP_K, n), dt)
    spec4 = pl.BlockSpec((TOP_K, tm), lambda i: (0, i))
    return pl.pallas_call(
        _router_kernel,
        out_shape=(out4(I32), out4(F32), out4(I32), jax.ShapeDtypeStruct((N_EXPERTS, 1), I32),
                   jax.ShapeDtypeStruct((n, d // 2), I32)),
        grid=(n // tm,),
        in_specs=[pl.BlockSpec((tm, d), lambda i: (first + i, 0)),
                  pl.BlockSpec((N_EXPERTS, d), lambda i: (0, 0)),
                  pl.BlockSpec((N_EXPERTS, 1), lambda i: (0, 0)),
                  pl.BlockSpec((1, tm), lambda i: (0, first + i))],
        out_specs=(spec4, spec4, spec4, pl.BlockSpec((N_EXPERTS, 1), lambda i: (0, 0)),
                   pl.BlockSpec((tm, d // 2), lambda i: (i, 0))),
        scratch_shapes=[pltpu.VMEM((N_EXPERTS, 1), F32)],
        compiler_params=_params("arbitrary"),
        name="router",
    )(h, w.T, b[:, None], real)


def _sc_gather_rows(table, idx, name):
    rows, width = table.shape
    total = idx.shape[0]
    per_worker = total // SC_WORKERS
    max_chunk = min(SC_MAX_INDICES, SC_ROW_BUFFER_BYTES // (width * table.dtype.itemsize))
    chunk = max(c for c in range(SUBLANES, max_chunk + 1, SUBLANES) if per_worker % (2 * c) == 0)
    n_chunks = per_worker // chunk
    assert total == SC_WORKERS * n_chunks * chunk and n_chunks % 2 == 0
    mesh = plsc.VectorSubcoreMesh(core_axis_name="c", subcore_axis_name="s",
                                  num_cores=SC_CORES, num_subcores=SC_SUBCORES)

    def body(table_hbm, idx_hbm, out_hbm, idx_v, rows0, rows1, gsem0, gsem1, psem0, psem1):
        worker = lax.axis_index("s") * SC_CORES + lax.axis_index("c")
        base = worker * per_worker
        pltpu.sync_copy(idx_hbm.at[worker], idx_v)
        bufs = (rows0, rows1)
        gsems = (gsem0, gsem1)
        psems = (psem0, psem1)

        def gather(c, slot):
            return pltpu.make_async_copy(table_hbm.at[idx_v.at[c]], bufs[slot], gsems[slot])

        def put(c, slot):
            return pltpu.make_async_copy(bufs[slot], out_hbm.at[pl.ds(base + c * chunk, chunk)], psems[slot])

        gather(0, 0).start()

        @pl.loop(0, n_chunks, step=2)
        def _(c0):
            for slot in range(2):
                c = c0 + slot

                @pl.when(c + 1 < n_chunks)
                def _():
                    @pl.when(c >= 1)
                    def _():
                        put(c - 1, 1 - slot).wait()
                    gather(c + 1, 1 - slot).start()

                gather(c, slot).wait()
                put(c, slot).start()

        put(n_chunks - 2, 0).wait()
        put(n_chunks - 1, 1).wait()

    return pl.kernel(
        body,
        out_type=jax.ShapeDtypeStruct((total, width), table.dtype),
        mesh=mesh,
        scratch_types=[pltpu.VMEM((n_chunks, chunk), I32),
                       pltpu.VMEM((chunk, width), table.dtype), pltpu.VMEM((chunk, width), table.dtype),
                       pltpu.SemaphoreType.DMA, pltpu.SemaphoreType.DMA,
                       pltpu.SemaphoreType.DMA, pltpu.SemaphoreType.DMA],
        name=name,
    )(table, idx.reshape(SC_WORKERS, n_chunks, chunk))


def _sc_scatter_rows(src, dest, out_rows, name):
    n, width = src.shape
    fan = dest.shape[0]
    per_worker = n // SC_WORKERS
    chunk = max(c for c in range(SUBLANES, SC_MAX_INDICES + 1, SUBLANES) if per_worker % (2 * c) == 0)
    n_chunks = per_worker // chunk
    assert n == SC_WORKERS * n_chunks * chunk and n_chunks % 2 == 0
    mesh = plsc.VectorSubcoreMesh(core_axis_name="c", subcore_axis_name="s",
                                  num_cores=SC_CORES, num_subcores=SC_SUBCORES)

    def body(src_hbm, idx_hbm, out_hbm, idx_v, rows0, rows1, lsem0, lsem1, ssem0, ssem1):
        worker = lax.axis_index("s") * SC_CORES + lax.axis_index("c")
        base = worker * per_worker
        pltpu.sync_copy(idx_hbm.at[worker], idx_v)
        bufs = (rows0, rows1)
        lsems = (lsem0, lsem1)
        ssems = (ssem0, ssem1)

        def load(c, slot):
            return pltpu.make_async_copy(src_hbm.at[pl.ds(base + c * chunk, chunk)], bufs[slot], lsems[slot])

        def scatter(c, k, slot):
            return pltpu.make_async_copy(bufs[slot], out_hbm.at[idx_v.at[k * n_chunks + c]], ssems[slot])

        load(0, 0).start()

        @pl.loop(0, n_chunks, step=2)
        def _(c0):
            for slot in range(2):
                c = c0 + slot

                @pl.when(c + 1 < n_chunks)
                def _():
                    @pl.when(c >= 1)
                    def _():
                        for k in range(fan):
                            scatter(c - 1, k, 1 - slot).wait()
                    load(c + 1, 1 - slot).start()

                load(c, slot).wait()
                for k in range(fan):
                    scatter(c, k, slot).start()

        for k in range(fan):
            scatter(n_chunks - 2, k, 0).wait()
        for k in range(fan):
            scatter(n_chunks - 1, k, 1).wait()

    idx = dest.reshape(fan, SC_WORKERS, n_chunks, chunk).transpose(1, 0, 2, 3)
    return pl.kernel(
        body,
        out_type=jax.ShapeDtypeStruct((out_rows, width), src.dtype),
        mesh=mesh,
        scratch_types=[pltpu.VMEM((fan * n_chunks, chunk), I32),
                       pltpu.VMEM((chunk, width), src.dtype), pltpu.VMEM((chunk, width), src.dtype),
                       pltpu.SemaphoreType.DMA, pltpu.SemaphoreType.DMA,
                       pltpu.SemaphoreType.DMA, pltpu.SemaphoreType.DMA],
        name=name,
    )(src, idx.reshape(SC_WORKERS, fan * n_chunks, chunk))


def _expert_kernel(be_ref, valid_ref, xs_ref, wgu_ref, bgu_ref, wd_ref, bd_ref, y_ref, wgu_sc, wd_sc):
    i = pl.program_id(0)
    e = be_ref[i]
    e_prev = be_ref[jnp.maximum(i - 1, 0)]
    d = wd_sc.shape[0]

    @pl.when((i == 0) | (e != e_prev))
    def _():
        for r in range(0, d, LANES):
            wgu_sc[r:r + LANES, :] = wgu_ref[0, r:r + LANES, :].astype(BF16)
            wd_sc[r:r + LANES, :] = wd_ref[0, r:r + LANES, :].astype(BF16)

    tm = xs_ref.shape[0]
    valid = valid_ref[i]

    def run(rows):
        x = _unpack_bf16_pairs(xs_ref[0:rows, :])
        h = jnp.dot(x, wgu_sc[...], preferred_element_type=F32) + bgu_ref[0]
        glu = jnp.minimum(h[:, :d], SWIGLU_LIMIT)
        lin = jnp.clip(h[:, d:], -SWIGLU_LIMIT, SWIGLU_LIMIT)
        act = glu * _sigmoid(SWIGLU_ALPHA * glu) * (lin + 1.0)
        y = jnp.dot(act.astype(BF16), wd_sc[...], preferred_element_type=F32) + bd_ref[0]
        y_ref[0:rows, :] = _pack_bf16_pairs(y)
        if rows < tm:
            y_ref[rows:tm, :] = jnp.zeros((tm - rows, y_ref.shape[1]), y_ref.dtype)

    @pl.when(valid > tm // 2)
    def _():
        run(tm)

    @pl.when((valid > 0) & (valid <= tm // 2))
    def _():
        run(tm // 2)

    @pl.when(valid == 0)
    def _():
        y_ref[...] = jnp.zeros_like(y_ref)


def _experts(block_e, valid_rows, xs, w_gu, b_gu, w_down, b_down, layer):
    cap = xs.shape[0]
    d = D_MODEL
    tm = EXPERT_TILE
    n_blocks = cap // tm
    grid_spec = pltpu.PrefetchScalarGridSpec(
        num_scalar_prefetch=2,
        grid=(n_blocks,),
        in_specs=[
            pl.BlockSpec((tm, d // 2), lambda i, be, nu: (i, 0)),
            pl.BlockSpec((None, 1, d, 2 * d), lambda i, be, nu: (layer, be[i], 0, 0)),
            pl.BlockSpec((1, 1, 2 * d), lambda i, be, nu: (be[i], 0, 0)),
            pl.BlockSpec((None, 1, d, d), lambda i, be, nu: (layer, be[i], 0, 0)),
            pl.BlockSpec((1, 1, d), lambda i, be, nu: (be[i], 0, 0)),
        ],
        out_specs=pl.BlockSpec((tm, d // 2), lambda i, be, nu: (i, 0)),
        scratch_shapes=[pltpu.VMEM((d, 2 * d), BF16), pltpu.VMEM((d, d), BF16)],
    )
    return pl.pallas_call(
        _expert_kernel,
        out_shape=jax.ShapeDtypeStruct((cap, d // 2), I32),
        grid_spec=grid_spec,
        compiler_params=_params("arbitrary"),
        name="moe_experts",
    )(block_e, valid_rows, xs, w_gu, b_gu[layer][:, None, :], w_down, b_down[layer][:, None, :])


def _combine_kernel(gate_ref, res_ref, yk_ref, g_ref, b_ref, *rest):
    o_ref = rest[-1]
    gates = gate_ref[...]
    ffn = gates[:, 0:1] * _unpack_bf16_pairs(yk_ref[0], F32)
    for k in range(1, TOP_K):
        ffn = ffn + gates[:, k:k + 1] * _unpack_bf16_pairs(yk_ref[k], F32)
    o_ref[...] = _layer_norm(DN_ALPHA * res_ref[...] + ffn, g_ref[...], b_ref[...])


def _combine(gates, res, yk, g, b, part, acc, drop_front=None):
    n, d = res.shape
    if drop_front is None:
        tt = MOVE_TILE
        steps = n // MOE_PARTS // tt
        grid = (steps,)
        local = lambda i: i
        src = lambda i: part * steps + i
        dst = src
        out_rows = n
    else:
        batch, lp = drop_front
        tt = LANES
        tiles = lp // tt
        part_batch = batch // MOE_PARTS
        grid = (part_batch, tiles - 1)
        local = lambda b_, j: b_ * tiles + 1 + j
        src = lambda b_, j: (part * part_batch + b_) * tiles + 1 + j
        dst = lambda b_, j: (part * part_batch + b_) * (tiles - 1) + j
        out_rows = n - batch * tt
    vec = pl.BlockSpec((1, d), lambda *_: (0, 0))
    in_specs = [pl.BlockSpec((tt, TOP_K), lambda *i: (local(*i), 0)),
                pl.BlockSpec((tt, d), lambda *i: (src(*i), 0)),
                pl.BlockSpec((TOP_K, tt, d // 2), lambda *i: (0, local(*i), 0)),
                vec, vec]
    args = (gates, res, yk, g, b)
    aliases = {}
    if acc is not None:
        in_specs.append(pl.BlockSpec(memory_space=pl.ANY))
        args += (acc,)
        aliases = {len(args) - 1: 0}
    return pl.pallas_call(
        _combine_kernel,
        out_shape=jax.ShapeDtypeStruct((out_rows, d), F32),
        grid=grid,
        in_specs=in_specs,
        out_specs=pl.BlockSpec((tt, d), lambda *i: (dst(*i), 0)),
        input_output_aliases=aliases,
        compiler_params=_params(*(["parallel"] * len(grid))),
        name="moe_combine",
    )(*args)


def _moe_layer(h, router_w, router_b, w_gu, b_gu, w_down, b_down, ln_g, ln_b, layer, lp, drop_front=None):
    n, d = h.shape
    tm = EXPERT_TILE
    n_part = n // MOE_PARTS
    assert n_part % lp == 0
    real_all = ((jnp.arange(n, dtype=I32) % lp) >= FRONT_PAD).astype(F32)[None, :]
    token = jnp.arange(n_part, dtype=I32)
    real = (token % lp) >= FRONT_PAD
    n_real = n_part - (n_part // lp) * FRONT_PAD
    n_blocks = -(-(n_real * TOP_K + N_EXPERTS * (tm - 1)) // tm) + 1
    block_start = jnp.arange(n_blocks, dtype=I32) * tm
    out = None
    for part in range(MOE_PARTS):
        idx, gates, rank, counts, h_packed = _router(h, router_w, router_b, real_all, part)
        counts = counts[:, 0]
        padded = (counts + tm - 1) // tm * tm
        pad_end = jnp.cumsum(padded)
        pad_start = pad_end - padded
        group_start = jnp.sum(jnp.where(idx[:, :, None] == jnp.arange(N_EXPERTS, dtype=I32), pad_start, 0), axis=-1)
        dest = jnp.where(real, group_start + rank, (n_blocks - 1) * tm + token % tm)
        block_e = jnp.minimum(jnp.sum((pad_end[None, :] <= block_start[:, None]).astype(I32), axis=1),
                              N_EXPERTS - 1)
        group_end = (pad_start + counts)[block_e]
        valid_rows = jnp.where(block_start < pad_end[-1], jnp.clip(group_end - block_start, 0, tm), 0).astype(I32)

        xs = _sc_scatter_rows(h_packed, dest, n_blocks * tm, "moe_dispatch")
        ys = _experts(block_e, valid_rows, xs, w_gu, b_gu, w_down, b_down, layer)
        yk = _sc_gather_rows(ys, dest.reshape(TOP_K * n_part), "moe_collect").reshape(TOP_K, n_part, d // 2)
        out = _combine(gates.T, h, yk, ln_g[None, :], ln_b[None, :], part, out, drop_front)
    return out


def _qkv_kernel(x_ref, w_ref, o_ref):
    xb = x_ref[...].astype(BF16)
    d = x_ref.shape[1]
    for part in range(3):
        acc = jnp.dot(xb, w_ref[:, part * d:(part + 1) * d], preferred_element_type=F32)
        if part == 0:
            acc = acc * (NA_HEAD_DIM ** -0.5)
        for p in range(HEAD_PAIRS):
            o_ref[part * HEAD_PAIRS + p] = acc[:, p * LANES:(p + 1) * LANES].astype(BF16)


def _qkv(x, w_bf16):
    n, d = x.shape
    return pl.pallas_call(
        _qkv_kernel,
        out_shape=jax.ShapeDtypeStruct((3 * HEAD_PAIRS, n, LANES), BF16),
        grid=(n // ROW_TILE,),
        in_specs=[pl.BlockSpec((ROW_TILE, d), lambda i: (i, 0)),
                  pl.BlockSpec((d, 3 * d), lambda i: (0, 0))],
        out_specs=pl.BlockSpec((3 * HEAD_PAIRS, ROW_TILE, LANES), lambda i: (0, i, 0)),
        compiler_params=_params("parallel"),
        name="na_qkv",
    )(x, w_bf16)


def _na_kernel(q_ref, k_ref, v_ref, bias_a_ref, bias_b_ref, mb_ref, o_ref, *, rows):
    s = pl.program_id(1)
    w = GRID_W
    low = lax.broadcasted_iota(I32, (w, LANES), 1) < NA_HEAD_DIM
    contract_last = (((1,), (1,)), ((), ()))
    meta0 = FRONT_PAD

    def stacked_q(p, q0):
        qp = q_ref[p, q0:q0 + w, :]
        zero = jnp.zeros_like(qp)
        return jnp.concatenate([jnp.where(low, qp, zero), jnp.where(low, zero, qp)], axis=0)

    def attend_all(q0, k0, bias_ref):
        scores = []
        for p in range(HEAD_PAIRS):
            q2 = stacked_q(p, q0)
            s_meta = lax.dot_general(q2, k_ref[p, meta0:meta0 + N_META, :], contract_last,
                                     preferred_element_type=F32) + mb_ref[p]
            s_win = None
            if k0 is not None:
                s_win = lax.dot_general(q2, k_ref[p, pl.ds(k0, NA_KH * w), :], contract_last,
                                        preferred_element_type=F32) + bias_ref[0, p]
            scores.append((s_meta, s_win))
        outs = []
        for p in range(HEAD_PAIRS):
            s_meta, s_win = scores[p]
            m = jnp.max(s_meta, axis=-1, keepdims=True)
            if s_win is not None:
                m = jnp.maximum(m, jnp.max(s_win, axis=-1, keepdims=True))
            p_meta = jnp.exp(s_meta - m)
            denom = jnp.sum(p_meta, axis=-1, keepdims=True)
            o = jnp.dot(p_meta.astype(BF16), v_ref[p, meta0:meta0 + N_META, :], preferred_element_type=F32)
            if s_win is not None:
                p_win = jnp.exp(s_win - m)
                denom = denom + jnp.sum(p_win, axis=-1, keepdims=True)
                o = o + jnp.dot(p_win.astype(BF16), v_ref[p, pl.ds(k0, NA_KH * w), :],
                                preferred_element_type=F32)
            o = o / denom
            outs.append(jnp.where(low, o[:w], o[w:]))
        return outs

    @pl.when(s == 0)
    def _():
        qrow = lax.broadcasted_iota(I32, (w, LANES), 0)
        for p, o in enumerate(attend_all(LANES - w, None, None)):
            o_ref[p, 0:LANES - w, :] = jnp.zeros((LANES - w, LANES), o_ref.dtype)
            o_ref[p, LANES - w:LANES, :] = jnp.where(qrow >= w - N_META, o, 0.0).astype(o_ref.dtype)

    @pl.when(s >= 1)
    def _():
        for half, bias_ref in enumerate((bias_a_ref, bias_b_ref)):
            r = 2 * (s - 1) + half
            rs = jnp.clip(r - NA_KH // 2, 0, rows - NA_KH)
            k0 = pl.multiple_of(LANES + rs * w, w)
            for p, o in enumerate(attend_all(half * w, k0, bias_ref)):
                o_ref[p, half * w:(half + 1) * w, :] = o.astype(o_ref.dtype)


def _na_bias_table(rpb, rows):
    del rows
    w = GRID_W
    q = jnp.arange(w)
    col_start = jnp.clip(q - NA_KW // 2, 0, w - NA_KW)
    c = jnp.arange(w)
    in_win = (c[None, :] >= col_start[:, None]) & (c[None, :] < col_start[:, None] + NA_KW)
    pad = w - NA_KW
    rp = jnp.pad(rpb.astype(F32), ((0, 0), (0, 0), (pad, pad)))
    toeplitz = jnp.stack([rp[:, :, w - 1 - qq:2 * w - 1 - qq] for qq in range(w)], axis=2)
    toeplitz = jnp.where(in_win[None, None], toeplitz, NEG_BIG)
    tabs = []
    for v in range(NA_KH):
        tv = toeplitz[:, NA_KH - 1 - v:2 * NA_KH - 1 - v]
        tabs.append(jnp.transpose(tv, (0, 2, 1, 3)).reshape(NA_HEADS, w, NA_KH * w))
    return jnp.stack(tabs, axis=0)


def _na_attention(qkv, bias_tab, meta_bias, *, batch, lp):
    n = qkv.shape[1]
    w = GRID_W
    rows = (lp - LANES) // w
    tiles = lp // LANES

    def variant(half):
        def index(b, s):
            r = jnp.clip(2 * (s - 1) + half, 0, rows - 1)
            return (r - jnp.clip(r - NA_KH // 2, 0, rows - NA_KH), 0, 0, 0)
        return index

    kern = functools.partial(_na_kernel, rows=rows)
    bias_block = (1, HEAD_PAIRS, 2 * w, NA_KH * w)
    bias_tab = bias_tab.reshape(NA_KH, HEAD_PAIRS, 2 * w, NA_KH * w)
    return pl.pallas_call(
        kern,
        out_shape=jax.ShapeDtypeStruct((HEAD_PAIRS, n, LANES), BF16),
        grid=(batch, tiles),
        in_specs=[
            pl.BlockSpec((HEAD_PAIRS, LANES, LANES), lambda b, s: (0, b * tiles + s, 0)),
            pl.BlockSpec((HEAD_PAIRS, lp, LANES), lambda b, s: (1, b, 0)),
            pl.BlockSpec((HEAD_PAIRS, lp, LANES), lambda b, s: (2, b, 0)),
            pl.BlockSpec(bias_block, variant(0)),
            pl.BlockSpec(bias_block, variant(1)),
            pl.BlockSpec((HEAD_PAIRS, 2 * w, N_META), lambda b, s: (0, 0, 0)),
        ],
        out_specs=pl.BlockSpec((HEAD_PAIRS, LANES, LANES), lambda b, s: (0, b * tiles + s, 0)),
        compiler_params=_params("parallel", "arbitrary"),
        name="na_attention",
    )(qkv, qkv, qkv, bias_tab, bias_tab,
      jnp.repeat(meta_bias, w, axis=0).reshape(HEAD_PAIRS, 2 * w, N_META))


def _na_out_kernel(o_ref, res_ref, w_ref, g_ref, b_ref, out_ref):
    att = jnp.concatenate([o_ref[p] for p in range(HEAD_PAIRS)], axis=-1)
    mix = jnp.dot(att, w_ref[...], preferred_element_type=F32)
    out_ref[...] = _layer_norm(DN_ALPHA * res_ref[...] + mix, g_ref[...], b_ref[...])


def _na_out(o, res, w_bf16, g, b):
    n, d = res.shape
    vec = pl.BlockSpec((1, d), lambda i: (0, 0))
    return pl.pallas_call(
        _na_out_kernel,
        out_shape=jax.ShapeDtypeStruct((n, d), F32),
        grid=(n // ROW_TILE,),
        in_specs=[pl.BlockSpec((HEAD_PAIRS, ROW_TILE, LANES), lambda i: (0, i, 0)),
                  pl.BlockSpec((ROW_TILE, d), lambda i: (i, 0)),
                  pl.BlockSpec((d, d), lambda i: (0, 0)), vec, vec],
        out_specs=pl.BlockSpec((ROW_TILE, d), lambda i: (i, 0)),
        compiler_params=_params("parallel"),
        name="na_out",
    )(o, res, w_bf16, g, b)


def kernel(x, meta_tokens, lru_w_in, lru_conv_w, lru_conv_b, lru_wa, lru_ba, lru_wx, lru_bx, lru_lambda, lru_w_out, na_w_qkv, na_rpb, na_meta_bias, na_w_out, ln_mix_g, ln_mix_b, router_w, router_b, moe_w_gu, moe_b_gu, moe_w_down, moe_b_down, ln_ffn_g, ln_ffn_b):
    batch, seq, d = x.shape
    lp = LANES + seq
    assert d == D_MODEL and seq % GRID_W == 0 and seq // GRID_W >= NA_KH
    assert lp % SCAN_CHUNK == 0 and (batch * lp) % ROW_TILE == 0
    assert batch % MOE_PARTS == 0 and (batch // MOE_PARTS * lp) % ROUTE_TILE == 0
    n = batch * lp

    front = jnp.zeros((batch, FRONT_PAD, d), x.dtype)
    meta = jnp.broadcast_to(meta_tokens[None].astype(x.dtype), (batch, N_META, d))
    h = jnp.concatenate([front, meta, x], axis=1).reshape(n, d)

    u = _matmul(h, lru_w_in[0].astype(BF16))
    row = lambda v: v[None, :]
    def scan(direction, reverse, fuse_out=None):
        return _lru_scan(
            u, lru_conv_w[0], row(lru_conv_b[0]),
            (0.5 * lru_wa[0, direction]).astype(BF16), row(0.5 * lru_ba[0, direction]),
            (0.5 * lru_wx[0, direction]).astype(BF16), row(0.5 * lru_bx[0, direction]),
            row(lru_lambda[0, direction]), reverse=reverse, lp=lp, fuse_out=fuse_out)

    h_fwd = scan(0, False)
    h = scan(1, True, fuse_out=(h_fwd, h, lru_w_out[0].astype(BF16), row(ln_mix_g[0]), row(ln_mix_b[0])))
    h = _moe_layer(h, router_w[0], router_b[0], moe_w_gu, moe_b_gu, moe_w_down, moe_b_down,
                   ln_ffn_g[0], ln_ffn_b[0], 0, lp)

    qkv = _qkv(h, na_w_qkv[0].astype(BF16))
    att = _na_attention(qkv, _na_bias_table(na_rpb[0], seq // GRID_W), na_meta_bias[0].astype(F32),
                        batch=batch, lp=lp)
    h = _na_out(att, h, na_w_out[0].astype(BF16), row(ln_mix_g[1]), row(ln_mix_b[1]))
    h = _moe_layer(h, router_w[1], router_b[1], moe_w_gu, moe_b_gu, moe_w_down, moe_b_down,
                   ln_ffn_g[1], ln_ffn_b[1], 1, lp, drop_front=(batch, lp))

    return h.reshape(batch, seq, d)
```

```python
import functools

import jax
import jax.numpy as jnp
from jax import lax
from jax.experimental import pallas as pl
from jax.experimental.pallas import tpu as pltpu
from jax.experimental.pallas import tpu_sc as plsc

F32 = jnp.float32
BF16 = jnp.bfloat16
I32 = jnp.int32
U32 = jnp.uint32

D_MODEL = 1024
N_META = 16
GRID_W = 64
LRU_BLOCKS = 4
LRU_BLOCK = D_MODEL // LRU_BLOCKS
CONV_W = 4
LRU_C = 8.0
NA_HEADS = 16
NA_HEAD_DIM = D_MODEL // NA_HEADS
NA_KH = 8
NA_KW = 16
N_EXPERTS = 32
TOP_K = 4
SWIGLU_LIMIT = 7.0
SWIGLU_ALPHA = 1.702
DEPTH = 2
DN_ALPHA = (2.0 * DEPTH) ** 0.25
LN_EPS = 1e-5

LANES = 128
SUBLANES = 8
FRONT_PAD = LANES - N_META
HEAD_PAIRS = D_MODEL // LANES
NEG_BIG = -1e30

ROW_TILE = 512
SCAN_CHUNK = 352
ROUTE_TILE = 512
MOVE_TILE = 256
EXPERT_TILE = 512
MOE_PARTS = 1
VMEM_LIMIT = 56 << 20

SC_CORES = 2
SC_SUBCORES = 16
SC_WORKERS = SC_CORES * SC_SUBCORES
SC_MAX_INDICES = 64
SC_ROW_BUFFER_BYTES = 128 << 10


def _params(*sem):
    return pltpu.CompilerParams(dimension_semantics=sem, vmem_limit_bytes=VMEM_LIMIT)


def _sigmoid(x):
    return 0.5 * jnp.tanh(0.5 * x) + 0.5


def _layer_norm(x, g, b):
    mu = jnp.mean(x, axis=-1, keepdims=True)
    xc = x - mu
    var = jnp.mean(xc * xc, axis=-1, keepdims=True)
    return xc * lax.rsqrt(var + LN_EPS) * g + b


def _matmul_kernel(x_ref, w_ref, o_ref):
    o_ref[...] = jnp.dot(x_ref[...].astype(BF16), w_ref[...], preferred_element_type=F32).astype(o_ref.dtype)


def _matmul(x, w_bf16):
    n, k = x.shape
    m = w_bf16.shape[1]
    return pl.pallas_call(
        _matmul_kernel,
        out_shape=jax.ShapeDtypeStruct((n, m), BF16),
        grid=(n // ROW_TILE,),
        in_specs=[pl.BlockSpec((ROW_TILE, k), lambda i: (i, 0)),
                  pl.BlockSpec((k, m), lambda i: (0, 0))],
        out_specs=pl.BlockSpec((ROW_TILE, m), lambda i: (i, 0)),
        compiler_params=_params("parallel"),
        name="in_proj",
    )(x, w_bf16)


def _lru_scan_kernel(xr_ref, xp_ref, xn_ref, cw_ref, cb_ref, wa_ref, ba_ref, wx_ref, bx_ref,
                     lam_ref, *rest, reverse, chunks_per_batch, fused):
    if fused:
        hf_ref, y_ref, res_ref, wout_ref, g_ref, beta_ref, h_ref, a_sc, b_sc, h_sc, carry_sc = rest
    else:
        h_ref, a_sc, b_sc, h_sc, carry_sc = rest
    t_rows = xr_ref.shape[0]
    seg = t_rows // SUBLANES
    halo = xp_ref.shape[0]
    lane_tiles = xr_ref.shape[1] // LANES
    step = pl.program_id(0)
    chunk = (pl.num_programs(0) - 1 - step) if reverse else step
    cib = chunk % chunks_per_batch
    first_chunk = cib == 0
    last_chunk = cib == chunks_per_batch - 1
    row8 = lax.broadcasted_iota(I32, (SUBLANES, 1), 0)
    cat = lambda parts: jnp.concatenate(parts, axis=0)

    head_pos = cib * t_rows + lax.broadcasted_iota(I32, (LANES, 1), 0)
    head_real = head_pos >= FRONT_PAD
    xr = xr_ref[...].astype(F32)
    xr = cat([jnp.where(head_real, xr[:LANES], 0.0), xr[LANES:]])
    prev_pos = cib * t_rows - halo + lax.broadcasted_iota(I32, (halo, 1), 0)
    prev = jnp.where(prev_pos >= FRONT_PAD, xp_ref[...].astype(F32), 0.0)
    nxt = jnp.where(last_chunk, 0.0, xn_ref[...].astype(F32))

    xm1 = pltpu.roll(xr, 1, 0)
    xm2 = pltpu.roll(xr, 2, 0)
    xp1 = pltpu.roll(xr, t_rows - 1, 0)
    xm1 = cat([jnp.where(row8 == 0, prev[halo - 1:halo], xm1[:SUBLANES]), xm1[SUBLANES:]])
    xm2_head = jnp.where(row8 == 0, prev[halo - 2:halo - 1],
                         jnp.where(row8 == 1, prev[halo - 1:halo], xm2[:SUBLANES]))
    xm2 = cat([xm2_head, xm2[SUBLANES:]])
    xp1 = cat([xp1[:t_rows - SUBLANES], jnp.where(row8 == SUBLANES - 1, nxt[0:1], xp1[t_rows - SUBLANES:])])
    cw = cw_ref[...]
    xc = cw[0:1, :] * xm2 + cw[1:2, :] * xm1 + cw[2:3, :] * xr + cw[3:4, :] * xp1 + cb_ref[...]

    xcb = xc.astype(BF16)
    za, zx = [], []
    for blk in range(LRU_BLOCKS):
        xblk = xcb[:, blk * LRU_BLOCK:(blk + 1) * LRU_BLOCK]
        za.append(jnp.dot(xblk, wa_ref[blk], preferred_element_type=F32))
        zx.append(jnp.dot(xblk, wx_ref[blk], preferred_element_type=F32))
    tanh_a = jnp.tanh(jnp.concatenate(za, axis=-1) + ba_ref[...])
    tanh_x = jnp.tanh(jnp.concatenate(zx, axis=-1) + bx_ref[...])

    z = -lam_ref[...]
    softplus = jnp.maximum(z, 0.0) + jnp.log1p(jnp.exp(-jnp.abs(z)))
    half_rate = (-0.5 * LRU_C) * softplus
    log_a = tanh_a * half_rate + half_rate
    a = jnp.exp(log_a)
    gap = jnp.tanh(log_a) * (-1.0 - a * a)
    mult = jnp.where(gap > 0.0, gap * lax.rsqrt(gap), 0.0)
    if reverse:
        tail = jnp.where(jnp.logical_and(last_chunk, row8 == SUBLANES - 1), 1.0, mult[t_rows - SUBLANES:])
        mult = cat([mult[:t_rows - SUBLANES], tail])
    else:
        mult = cat([jnp.where(head_pos == FRONT_PAD, 1.0, mult[:LANES]), mult[LANES:]])
    b = mult * ((0.5 * tanh_x + 0.5) * xc)
    b = cat([jnp.where(head_real, b[:LANES], 0.0), b[LANES:]])
    for c in range(lane_tiles):
        a_sc[c] = a[:, c * LANES:(c + 1) * LANES]
        b_sc[c] = b[:, c * LANES:(c + 1) * LANES]

    @pl.when(last_chunk if reverse else first_chunk)
    def _():
        carry_sc[...] = jnp.zeros_like(carry_sc)

    order = range(seg - 1, -1, -1) if reverse else range(seg)
    rows_of = lambda j: pl.ds(j, SUBLANES, stride=seg)

    local = [jnp.zeros((SUBLANES, LANES), F32)] * lane_tiles
    prod = [jnp.ones((SUBLANES, LANES), F32)] * lane_tiles
    for j in order:
        for c in range(lane_tiles):
            av = a_sc[c, rows_of(j), :]
            local[c] = av * local[c] + b_sc[c, rows_of(j), :]
            prod[c] = av * prod[c]
            h_sc[c, rows_of(j), :] = local[c]
            a_sc[c, rows_of(j), :] = prod[c]

    carry_in = carry_sc[...]
    carry_out = []
    seg_carry = []
    for c in range(lane_tiles):
        state = carry_in[:, c * LANES:(c + 1) * LANES]
        rows = [None] * SUBLANES
        for s in (range(SUBLANES - 1, -1, -1) if reverse else range(SUBLANES)):
            rows[s] = state
            state = local[c][s:s + 1] + prod[c][s:s + 1] * state
        seg_carry.append(cat(rows))
        carry_out.append(state)
    carry_sc[...] = jnp.concatenate(carry_out, axis=-1)

    for j in order:
        for c in range(lane_tiles):
            h_sc[c, rows_of(j), :] = h_sc[c, rows_of(j), :] + a_sc[c, rows_of(j), :] * seg_carry[c]
    states = jnp.concatenate([h_sc[c] for c in range(lane_tiles)], axis=-1)
    if fused:
        gated = (hf_ref[...] + states) * _gelu_tanh(y_ref[...].astype(F32))
        mix = jnp.dot(gated.astype(BF16), wout_ref[...], preferred_element_type=F32)
        h_ref[...] = _layer_norm(DN_ALPHA * res_ref[...] + mix, g_ref[...], beta_ref[...])
    else:
        h_ref[...] = states


def _lru_scan(u, cw, cb, wa_half, ba_half, wx_half, bx_half, lam, *, reverse, lp, fuse_out=None):
    n = u.shape[0]
    d = D_MODEL
    t = SCAN_CHUNK
    halo = 2 * SUBLANES
    n_chunks = n // t
    cpb = lp // t
    t_h = t // halo
    n_h = n // halo

    def chunk_of(i):
        return (n_chunks - 1 - i) if reverse else i

    kern = functools.partial(_lru_scan_kernel, reverse=reverse, chunks_per_batch=cpb, fused=fuse_out is not None)
    chunk_rows = lambda col: pl.BlockSpec((t, d), lambda i: (chunk_of(i), col))
    extra_specs, extra_args = [], ()
    if fuse_out is not None:
        h_other, res, w_out, ln_g, ln_b = fuse_out
        extra_specs = [chunk_rows(0), chunk_rows(1), chunk_rows(0),
                       pl.BlockSpec((d, d), lambda i: (0, 0)),
                       pl.BlockSpec((1, d), lambda i: (0, 0)), pl.BlockSpec((1, d), lambda i: (0, 0))]
        extra_args = (h_other, u, res, w_out, ln_g, ln_b)
    full2 = lambda shape: pl.BlockSpec(shape, lambda i: (0, 0))
    full3 = lambda shape: pl.BlockSpec(shape, lambda i: (0, 0, 0))
    tile_major = pltpu.VMEM((d // LANES, t, LANES), F32)
    return pl.pallas_call(
        kern,
        out_shape=jax.ShapeDtypeStruct((n, d), F32),
        grid=(n_chunks,),
        in_specs=[
            pl.BlockSpec((t, d), lambda i: (chunk_of(i), 0)),
            pl.BlockSpec((halo, d), lambda i: (jnp.maximum(chunk_of(i) * t_h - 1, 0), 0)),
            pl.BlockSpec((halo, d), lambda i: (jnp.minimum((chunk_of(i) + 1) * t_h, n_h - 1), 0)),
            full2((CONV_W, d)), full2((1, d)),
            full3((LRU_BLOCKS, LRU_BLOCK, LRU_BLOCK)), full2((1, d)),
            full3((LRU_BLOCKS, LRU_BLOCK, LRU_BLOCK)), full2((1, d)),
            full2((1, d)),
        ] + extra_specs,
        out_specs=pl.BlockSpec((t, d), lambda i: (chunk_of(i), 0)),
        scratch_shapes=[tile_major, tile_major, tile_major, pltpu.VMEM((1, d), F32)],
        compiler_params=_params("arbitrary"),
        name=("lru_scan_bwd" if reverse else "lru_scan_fwd") + ("_out" if fuse_out is not None else ""),
    )(u, u, u, cw, cb, wa_half, ba_half, wx_half, bx_half, lam, *extra_args)


def _gelu_tanh(y):
    c = 0.7978845608028654
    return y * (0.5 * (1.0 + jnp.tanh(c * (y + 0.044715 * (y * y * y)))))


def _pack_bf16_pairs(x):
    half = x.shape[1] // 2
    lo = pltpu.bitcast(x[:, :half].astype(BF16).astype(F32), U32)
    hi = pltpu.bitcast(x[:, half:].astype(BF16).astype(F32), U32)
    return pltpu.bitcast((lo >> 16) | (hi & jnp.uint32(0xFFFF0000)), I32)


def _unpack_bf16_pairs(packed, dtype=BF16):
    u = pltpu.bitcast(packed, U32)
    lo = pltpu.bitcast(u << 16, F32).astype(dtype)
    hi = pltpu.bitcast(u & jnp.uint32(0xFFFF0000), F32).astype(dtype)
    return jnp.concatenate([lo, hi], axis=-1)


def _router_kernel(h_ref, wt_ref, b_ref, real_ref, idx_ref, gate_ref, rank_ref, cnt_ref, hp_ref, base_sc):
    tm = h_ref.shape[0]
    h = h_ref[...]
    hp_ref[...] = _pack_bf16_pairs(h)

    @pl.when(pl.program_id(0) == 0)
    def _():
        base_sc[...] = jnp.zeros_like(base_sc)

    wt = wt_ref[...]
    h_hi = h.astype(BF16)
    h_lo = (h - h_hi.astype(F32)).astype(BF16)
    w_hi = wt.astype(BF16)
    w_lo = (wt - w_hi.astype(F32)).astype(BF16)
    contract_last = (((1,), (1,)), ((), ()))
    mm = lambda a, b: lax.dot_general(a, b, contract_last, preferred_element_type=F32)
    logits = mm(w_hi, h_hi) + (mm(w_hi, h_lo) + mm(w_lo, h_hi)) + b_ref[...]

    expert = lax.broadcasted_iota(I32, (N_EXPERTS, tm), 0).astype(F32)
    vals = logits
    idx_rows, val_rows, onehots = [], [], []
    for _ in range(TOP_K):
        m = jnp.max(vals, axis=0, keepdims=True)
        idx = jnp.min(jnp.where(vals == m, expert, float(N_EXPERTS)), axis=0, keepdims=True)
        hit = expert == idx
        onehots.append(jnp.where(hit, real_ref[...], 0.0))
        idx_rows.append(idx)
        val_rows.append(m)
        vals = jnp.where(hit, -jnp.inf, vals)

    onehot_all = jnp.concatenate(onehots, axis=0)
    t_from = lax.broadcasted_iota(I32, (tm, tm), 0)
    t_to = lax.broadcasted_iota(I32, (tm, tm), 1)
    earlier = jnp.where(t_from < t_to, 1.0, 0.0).astype(BF16)
    before = jnp.dot(onehot_all.astype(BF16), earlier, preferred_element_type=F32)
    base = base_sc[...]
    starts = []
    for k in range(TOP_K):
        starts.append(base)
        base = base + jnp.sum(onehots[k], axis=1, keepdims=True)
    contrib = onehot_all * (jnp.concatenate(starts, axis=0) + before)
    rank_rows = [jnp.sum(contrib[k * N_EXPERTS:(k + 1) * N_EXPERTS], axis=0, keepdims=True)
                 for k in range(TOP_K)]

    top_vals = jnp.concatenate(val_rows, axis=0)
    e = jnp.exp(top_vals - top_vals[0:1])
    idx_ref[...] = jnp.concatenate(idx_rows, axis=0).astype(I32)
    gate_ref[...] = e / jnp.sum(e, axis=0, keepdims=True)
    rank_ref[...] = jnp.concatenate(rank_rows, axis=0).astype(I32)
    base_sc[...] = base
    cnt_ref[...] = base.astype(I32)


def _router(h, w, b, real, part):
    d = h.shape[1]
    n = h.shape[0] // MOE_PARTS
    tm = ROUTE_TILE
    first = part * (n // tm)
    out4 = lambda dt: jax.ShapeDtypeStruct((TOP_K, n), dt)
    spec4 = pl.BlockSpec((TOP_K, tm), lambda i: (0, i))
    return pl.pallas_call(
        _router_kernel,
        out_shape=(out4(I32), out4(F32), out4(I32), jax.ShapeDtypeStruct((N_EXPERTS, 1), I32),
                   jax.ShapeDtypeStruct((n, d // 2), I32)),
        grid=(n // tm,),
        in_specs=[pl.BlockSpec((tm, d), lambda i: (first + i, 0)),
                  pl.BlockSpec((N_EXPERTS, d), lambda i: (0, 0)),
                  pl.BlockSpec((N_EXPERTS, 1), lambda i: (0, 0)),
                  pl.BlockSpec((1, tm), lambda i: (0, first + i))],
        out_specs=(spec4, spec4, spec4, pl.BlockSpec((N_EXPERTS, 1), lambda i: (0, 0)),
                   pl.BlockSpec((tm, d // 2), lambda i: (i, 0))),
        scratch_shapes=[pltpu.VMEM((N_EXPERTS, 1), F32)],
        compiler_params=_params("arbitrary"),
        name="router",
    )(h, w.T, b[:, None], real)


def _sc_gather_rows(table, idx, name):
    rows, width = table.shape
    total = idx.shape[0]
    per_worker = total // SC_WORKERS
    max_chunk = min(SC_MAX_INDICES, SC_ROW_BUFFER_BYTES // (width * table.dtype.itemsize))
    chunk = max(c for c in range(SUBLANES, max_chunk + 1, SUBLANES) if per_worker % (2 * c) == 0)
    n_chunks = per_worker // chunk
    assert total == SC_WORKERS * n_chunks * chunk and n_chunks % 2 == 0
    mesh = plsc.VectorSubcoreMesh(core_axis_name="c", subcore_axis_name="s",
                                  num_cores=SC_CORES, num_subcores=SC_SUBCORES)

    def body(table_hbm, idx_hbm, out_hbm, idx_v, rows0, rows1, gsem0, gsem1, psem0, psem1):
        worker = lax.axis_index("s") * SC_CORES + lax.axis_index("c")
        base = worker * per_worker
        pltpu.sync_copy(idx_hbm.at[worker], idx_v)
        bufs = (rows0, rows1)
        gsems = (gsem0, gsem1)
        psems = (psem0, psem1)

        def gather(c, slot):
            return pltpu.make_async_copy(table_hbm.at[idx_v.at[c]], bufs[slot], gsems[slot])

        def put(c, slot):
            return pltpu.make_async_copy(bufs[slot], out_hbm.at[pl.ds(base + c * chunk, chunk)], psems[slot])

        gather(0, 0).start()

        @pl.loop(0, n_chunks, step=2)
        def _(c0):
            for slot in range(2):
                c = c0 + slot

                @pl.when(c + 1 < n_chunks)
                def _():
                    @pl.when(c >= 1)
                    def _():
                        put(c - 1, 1 - slot).wait()
                    gather(c + 1, 1 - slot).start()

                gather(c, slot).wait()
                put(c, slot).start()

        put(n_chunks - 2, 0).wait()
        put(n_chunks - 1, 1).wait()

    return pl.kernel(
        body,
        out_type=jax.ShapeDtypeStruct((total, width), table.dtype),
        mesh=mesh,
        scratch_types=[pltpu.VMEM((n_chunks, chunk), I32),
                       pltpu.VMEM((chunk, width), table.dtype), pltpu.VMEM((chunk, width), table.dtype),
                       pltpu.SemaphoreType.DMA, pltpu.SemaphoreType.DMA,
                       pltpu.SemaphoreType.DMA, pltpu.SemaphoreType.DMA],
        name=name,
    )(table, idx.reshape(SC_WORKERS, n_chunks, chunk))


def _sc_scatter_rows(src, dest, out_rows, name):
    n, width = src.shape
    fan = dest.shape[0]
    per_worker = n // SC_WORKERS
    chunk = max(c for c in range(SUBLANES, SC_MAX_INDICES + 1, SUBLANES) if per_worker % (2 * c) == 0)
    n_chunks = per_worker // chunk
    assert n == SC_WORKERS * n_chunks * chunk and n_chunks % 2 == 0
    mesh = plsc.VectorSubcoreMesh(core_axis_name="c", subcore_axis_name="s",
                                  num_cores=SC_CORES, num_subcores=SC_SUBCORES)

    def body(src_hbm, idx_hbm, out_hbm, idx_v, rows0, rows1, lsem0, lsem1, ssem0, ssem1):
        worker = lax.axis_index("s") * SC_CORES + lax.axis_index("c")
        base = worker * per_worker
        pltpu.sync_copy(idx_hbm.at[worker], idx_v)
        bufs = (rows0, rows1)
        lsems = (lsem0, lsem1)
        ssems = (ssem0, ssem1)

        def load(c, slot):
            return pltpu.make_async_copy(src_hbm.at[pl.ds(base + c * chunk, chunk)], bufs[slot], lsems[slot])

        def scatter(c, k, slot):
            return pltpu.make_async_copy(bufs[slot], out_hbm.at[idx_v.at[k * n_chunks + c]], ssems[slot])

        load(0, 0).start()

        @pl.loop(0, n_chunks, step=2)
        def _(c0):
            for slot in range(2):
                c = c0 + slot

                @pl.when(c + 1 < n_chunks)
                def _():
                    @pl.when(c >= 1)
                    def _():
                        for k in range(fan):
                            scatter(c - 1, k, 1 - slot).wait()
                    load(c + 1, 1 - slot).start()

                load(c, slot).wait()
                for k in range(fan):
                    scatter(c, k, slot).start()

        for k in range(fan):
            scatter(n_chunks - 2, k, 0).wait()
        for k in range(fan):
            scatter(n_chunks - 1, k, 1).wait()

    idx = dest.reshape(fan, SC_WORKERS, n_chunks, chunk).transpose(1, 0, 2, 3)
    return pl.kernel(
        body,
        out_type=jax.ShapeDtypeStruct((out_rows, width), src.dtype),
        mesh=mesh,
        scratch_types=[pltpu.VMEM((fan * n_chunks, chunk), I32),
                       pltpu.VMEM((chunk, width), src.dtype), pltpu.VMEM((chunk, width), src.dtype),
                       pltpu.SemaphoreType.DMA, pltpu.SemaphoreType.DMA,
                       pltpu.SemaphoreType.DMA, pltpu.SemaphoreType.DMA],
        name=name,
    )(src, idx.reshape(SC_WORKERS, fan * n_chunks, chunk))


def _expert_kernel(be_ref, valid_ref, xs_ref, wgu_ref, bgu_ref, wd_ref, bd_ref, y_ref, wgu_sc, wd_sc):
    i = pl.program_id(0)
    e = be_ref[i]
    e_prev = be_ref[jnp.maximum(i - 1, 0)]
    d = wd_sc.shape[0]

    @pl.when((i == 0) | (e != e_prev))
    def _():
        for r in range(0, d, LANES):
            wgu_sc[r:r + LANES, :] = wgu_ref[0, r:r + LANES, :].astype(BF16)
            wd_sc[r:r + LANES, :] = wd_ref[0, r:r + LANES, :].astype(BF16)

    @pl.when(valid_ref[i] > 0)
    def _():
        x = _unpack_bf16_pairs(xs_ref[...])
        h = jnp.dot(x, wgu_sc[...], preferred_element_type=F32) + bgu_ref[0]
        glu = jnp.minimum(h[:, :d], SWIGLU_LIMIT)
        lin = jnp.clip(h[:, d:], -SWIGLU_LIMIT, SWIGLU_LIMIT)
        act = glu * _sigmoid(SWIGLU_ALPHA * glu) * (lin + 1.0)
        y = jnp.dot(act.astype(BF16), wd_sc[...], preferred_element_type=F32) + bd_ref[0]
        y_ref[...] = _pack_bf16_pairs(y)

    @pl.when(valid_ref[i] == 0)
    def _():
        y_ref[...] = jnp.zeros_like(y_ref)


def _experts(block_e, valid_rows, xs, w_gu, b_gu, w_down, b_down, layer):
    cap = xs.shape[0]
    d = D_MODEL
    tm = EXPERT_TILE
    n_blocks = cap // tm
    grid_spec = pltpu.PrefetchScalarGridSpec(
        num_scalar_prefetch=2,
        grid=(n_blocks,),
        in_specs=[
            pl.BlockSpec((tm, d // 2), lambda i, be, nu: (i, 0)),
            pl.BlockSpec((None, 1, d, 2 * d), lambda i, be, nu: (layer, be[i], 0, 0)),
            pl.BlockSpec((1, 1, 2 * d), lambda i, be, nu: (be[i], 0, 0)),
            pl.BlockSpec((None, 1, d, d), lambda i, be, nu: (layer, be[i], 0, 0)),
            pl.BlockSpec((1, 1, d), lambda i, be, nu: (be[i], 0, 0)),
        ],
        out_specs=pl.BlockSpec((tm, d // 2), lambda i, be, nu: (i, 0)),
        scratch_shapes=[pltpu.VMEM((d, 2 * d), BF16), pltpu.VMEM((d, d), BF16)],
    )
    return pl.pallas_call(
        _expert_kernel,
        out_shape=jax.ShapeDtypeStruct((cap, d // 2), I32),
        grid_spec=grid_spec,
        compiler_params=_params("arbitrary"),
        name="moe_experts",
    )(block_e, valid_rows, xs, w_gu, b_gu[layer][:, None, :], w_down, b_down[layer][:, None, :])


def _combine_kernel(gate_ref, res_ref, yk_ref, g_ref, b_ref, *rest):
    o_ref = rest[-1]
    gates = gate_ref[...]
    ffn = gates[:, 0:1] * _unpack_bf16_pairs(yk_ref[0], F32)
    for k in range(1, TOP_K):
        ffn = ffn + gates[:, k:k + 1] * _unpack_bf16_pairs(yk_ref[k], F32)
    o_ref[...] = _layer_norm(DN_ALPHA * res_ref[...] + ffn, g_ref[...], b_ref[...])


def _combine(gates, res, yk, g, b, part, acc, drop_front=None):
    n, d = res.shape
    if drop_front is None:
        tt = MOVE_TILE
        steps = n // MOE_PARTS // tt
        grid = (steps,)
        local = lambda i: i
        src = lambda i: part * steps + i
        dst = src
        out_rows = n
    else:
        batch, lp = drop_front
        tt = LANES
        tiles = lp // tt
        part_batch = batch // MOE_PARTS
        grid = (part_batch, tiles - 1)
        local = lambda b_, j: b_ * tiles + 1 + j
        src = lambda b_, j: (part * part_batch + b_) * tiles + 1 + j
        dst = lambda b_, j: (part * part_batch + b_) * (tiles - 1) + j
        out_rows = n - batch * tt
    vec = pl.BlockSpec((1, d), lambda *_: (0, 0))
    in_specs = [pl.BlockSpec((tt, TOP_K), lambda *i: (local(*i), 0)),
                pl.BlockSpec((tt, d), lambda *i: (src(*i), 0)),
                pl.BlockSpec((TOP_K, tt, d // 2), lambda *i: (0, local(*i), 0)),
                vec, vec]
    args = (gates, res, yk, g, b)
    aliases = {}
    if acc is not None:
        in_specs.append(pl.BlockSpec(memory_space=pl.ANY))
        args += (acc,)
        aliases = {len(args) - 1: 0}
    return pl.pallas_call(
        _combine_kernel,
        out_shape=jax.ShapeDtypeStruct((out_rows, d), F32),
        grid=grid,
        in_specs=in_specs,
        out_specs=pl.BlockSpec((tt, d), lambda *i: (dst(*i), 0)),
        input_output_aliases=aliases,
        compiler_params=_params(*(["parallel"] * len(grid))),
        name="moe_combine",
    )(*args)


def _moe_layer(h, router_w, router_b, w_gu, b_gu, w_down, b_down, ln_g, ln_b, layer, lp, drop_front=None):
    n, d = h.shape
    tm = EXPERT_TILE
    n_part = n // MOE_PARTS
    assert n_part % lp == 0
    real_all = ((jnp.arange(n, dtype=I32) % lp) >= FRONT_PAD).astype(F32)[None, :]
    token = jnp.arange(n_part, dtype=I32)
    real = (token % lp) >= FRONT_PAD
    n_real = n_part - (n_part // lp) * FRONT_PAD
    n_blocks = -(-(n_real * TOP_K + N_EXPERTS * (tm - 1)) // tm) + 1
    block_start = jnp.arange(n_blocks, dtype=I32) * tm
    out = None
    for part in range(MOE_PARTS):
        idx, gates, rank, counts, h_packed = _router(h, router_w, router_b, real_all, part)
        counts = counts[:, 0]
        padded = (counts + tm - 1) // tm * tm
        pad_end = jnp.cumsum(padded)
        pad_start = pad_end - padded
        group_start = jnp.sum(jnp.where(idx[:, :, None] == jnp.arange(N_EXPERTS, dtype=I32), pad_start, 0), axis=-1)
        dest = jnp.where(real, group_start + rank, (n_blocks - 1) * tm + token % tm)
        block_e = jnp.minimum(jnp.sum((pad_end[None, :] <= block_start[:, None]).astype(I32), axis=1),
                              N_EXPERTS - 1)
        group_end = (pad_start + counts)[block_e]
        valid_rows = jnp.where(block_start < pad_end[-1], jnp.clip(group_end - block_start, 0, tm), 0).astype(I32)

        xs = _sc_scatter_rows(h_packed, dest, n_blocks * tm, "moe_dispatch")
        ys = _experts(block_e, valid_rows, xs, w_gu, b_gu, w_down, b_down, layer)
        yk = _sc_gather_rows(ys, dest.reshape(TOP_K * n_part), "moe_collect").reshape(TOP_K, n_part, d // 2)
        out = _combine(gates.T, h, yk, ln_g[None, :], ln_b[None, :], part, out, drop_front)
    return out


def _qkv_kernel(x_ref, w_ref, o_ref):
    xb = x_ref[...].astype(BF16)
    d = x_ref.shape[1]
    for part in range(3):
        acc = jnp.dot(xb, w_ref[:, part * d:(part + 1) * d], preferred_element_type=F32)
        if part == 0:
            acc = acc * (NA_HEAD_DIM ** -0.5)
        for p in range(HEAD_PAIRS):
            o_ref[part * HEAD_PAIRS + p] = acc[:, p * LANES:(p + 1) * LANES].astype(BF16)


def _qkv(x, w_bf16):
    n, d = x.shape
    return pl.pallas_call(
        _qkv_kernel,
        out_shape=jax.ShapeDtypeStruct((3 * HEAD_PAIRS, n, LANES), BF16),
        grid=(n // ROW_TILE,),
        in_specs=[pl.BlockSpec((ROW_TILE, d), lambda i: (i, 0)),
                  pl.BlockSpec((d, 3 * d), lambda i: (0, 0))],
        out_specs=pl.BlockSpec((3 * HEAD_PAIRS, ROW_TILE, LANES), lambda i: (0, i, 0)),
        compiler_params=_params("parallel"),
        name="na_qkv",
    )(x, w_bf16)


def _na_kernel(q_ref, k_ref, v_ref, bias_a_ref, bias_b_ref, mb_ref, o_ref, *, rows):
    s = pl.program_id(1)
    w = GRID_W
    low = lax.broadcasted_iota(I32, (w, LANES), 1) < NA_HEAD_DIM
    contract_last = (((1,), (1,)), ((), ()))
    meta0 = FRONT_PAD

    def stacked_q(p, q0):
        qp = q_ref[p, q0:q0 + w, :]
        zero = jnp.zeros_like(qp)
        return jnp.concatenate([jnp.where(low, qp, zero), jnp.where(low, zero, qp)], axis=0)

    def attend_all(q0, k0, bias_ref):
        scores = []
        for p in range(HEAD_PAIRS):
            q2 = stacked_q(p, q0)
            s_meta = lax.dot_general(q2, k_ref[p, meta0:meta0 + N_META, :], contract_last,
                                     preferred_element_type=F32) + mb_ref[p]
            s_win = None
            if k0 is not None:
                s_win = lax.dot_general(q2, k_ref[p, pl.ds(k0, NA_KH * w), :], contract_last,
                                        preferred_element_type=F32) + bias_ref[0, p]
            scores.append((s_meta, s_win))
        outs = []
        for p in range(HEAD_PAIRS):
            s_meta, s_win = scores[p]
            m = jnp.max(s_meta, axis=-1, keepdims=True)
            if s_win is not None:
                m = jnp.maximum(m, jnp.max(s_win, axis=-1, keepdims=True))
            p_meta = jnp.exp(s_meta - m)
            denom = jnp.sum(p_meta, axis=-1, keepdims=True)
            o = jnp.dot(p_meta.astype(BF16), v_ref[p, meta0:meta0 + N_META, :], preferred_element_type=F32)
            if s_win is not None:
                p_win = jnp.exp(s_win - m)
                denom = denom + jnp.sum(p_win, axis=-1, keepdims=True)
                o = o + jnp.dot(p_win.astype(BF16), v_ref[p, pl.ds(k0, NA_KH * w), :],
                                preferred_element_type=F32)
            o = o / denom
            outs.append(jnp.where(low, o[:w], o[w:]))
        return outs

    @pl.when(s == 0)
    def _():
        qrow = lax.broadcasted_iota(I32, (w, LANES), 0)
        for p, o in enumerate(attend_all(LANES - w, None, None)):
            o_ref[p, 0:LANES - w, :] = jnp.zeros((LANES - w, LANES), o_ref.dtype)
            o_ref[p, LANES - w:LANES, :] = jnp.where(qrow >= w - N_META, o, 0.0).astype(o_ref.dtype)

    @pl.when(s >= 1)
    def _():
        for half, bias_ref in enumerate((bias_a_ref, bias_b_ref)):
            r = 2 * (s - 1) + half
            rs = jnp.clip(r - NA_KH // 2, 0, rows - NA_KH)
            k0 = pl.multiple_of(LANES + rs * w, w)
            for p, o in enumerate(attend_all(half * w, k0, bias_ref)):
                o_ref[p, half * w:(half + 1) * w, :] = o.astype(o_ref.dtype)


def _na_bias_table(rpb, rows):
    del rows
    w = GRID_W
    q = jnp.arange(w)
    col_start = jnp.clip(q - NA_KW // 2, 0, w - NA_KW)
    c = jnp.arange(w)
    in_win = (c[None, :] >= col_start[:, None]) & (c[None, :] < col_start[:, None] + NA_KW)
    pad = w - NA_KW
    rp = jnp.pad(rpb.astype(F32), ((0, 0), (0, 0), (pad, pad)))
    toeplitz = jnp.stack([rp[:, :, w - 1 - qq:2 * w - 1 - qq] for qq in range(w)], axis=2)
    toeplitz = jnp.where(in_win[None, None], toeplitz, NEG_BIG)
    tabs = [jnp.concatenate([toeplitz[:, NA_KH - 1 - v + j] for j in range(NA_KH)], axis=-1) for v in range(NA_KH)]
    return jnp.stack(tabs, axis=0)


def _na_attention(qkv, bias_tab, meta_bias, *, batch, lp):
    n = qkv.shape[1]
    w = GRID_W
    rows = (lp - LANES) // w
    tiles = lp // LANES

    def variant(half):
        def index(b, s):
            r = jnp.clip(2 * (s - 1) + half, 0, rows - 1)
            return (r - jnp.clip(r - NA_KH // 2, 0, rows - NA_KH), 0, 0, 0)
        return index

    kern = functools.partial(_na_kernel, rows=rows)
    bias_block = (1, HEAD_PAIRS, 2 * w, NA_KH * w)
    bias_tab = bias_tab.reshape(NA_KH, HEAD_PAIRS, 2 * w, NA_KH * w)
    return pl.pallas_call(
        kern,
        out_shape=jax.ShapeDtypeStruct((HEAD_PAIRS, n, LANES), BF16),
        grid=(batch, tiles),
        in_specs=[
            pl.BlockSpec((HEAD_PAIRS, LANES, LANES), lambda b, s: (0, b * tiles + s, 0)),
            pl.BlockSpec((HEAD_PAIRS, lp, LANES), lambda b, s: (1, b, 0)),
            pl.BlockSpec((HEAD_PAIRS, lp, LANES), lambda b, s: (2, b, 0)),
            pl.BlockSpec(bias_block, variant(0)),
            pl.BlockSpec(bias_block, variant(1)),
            pl.BlockSpec((HEAD_PAIRS, 2 * w, N_META), lambda b, s: (0, 0, 0)),
        ],
        out_specs=pl.BlockSpec((HEAD_PAIRS, LANES, LANES), lambda b, s: (0, b * tiles + s, 0)),
        compiler_params=_params("parallel", "arbitrary"),
        name="na_attention",
    )(qkv, qkv, qkv, bias_tab, bias_tab,
      jnp.repeat(meta_bias, w, axis=0).reshape(HEAD_PAIRS, 2 * w, N_META))


def _na_out_kernel(o_ref, res_ref, w_ref, g_ref, b_ref, out_ref):
    att = jnp.concatenate([o_ref[p] for p in range(HEAD_PAIRS)], axis=-1)
    mix = jnp.dot(att, w_ref[...], preferred_element_type=F32)
    out_ref[...] = _layer_norm(DN_ALPHA * res_ref[...] + mix, g_ref[...], b_ref[...])


def _na_out(o, res, w_bf16, g, b):
    n, d = res.shape
    vec = pl.BlockSpec((1, d), lambda i: (0, 0))
    return pl.pallas_call(
        _na_out_kernel,
        out_shape=jax.ShapeDtypeStruct((n, d), F32),
        grid=(n // ROW_TILE,),
        in_specs=[pl.BlockSpec((HEAD_PAIRS, ROW_TILE, LANES), lambda i: (0, i, 0)),
                  pl.BlockSpec((ROW_TILE, d), lambda i: (i, 0)),
                  pl.BlockSpec((d, d), lambda i: (0, 0)), vec, vec],
        out_specs=pl.BlockSpec((ROW_TILE, d), lambda i: (i, 0)),
        compiler_params=_params("parallel"),
        name="na_out",
    )(o, res, w_bf16, g, b)


def kernel(x, meta_tokens, lru_w_in, lru_conv_w, lru_conv_b, lru_wa, lru_ba, lru_wx, lru_bx, lru_lambda, lru_w_out, na_w_qkv, na_rpb, na_meta_bias, na_w_out, ln_mix_g, ln_mix_b, router_w, router_b, moe_w_gu, moe_b_gu, moe_w_down, moe_b_down, ln_ffn_g, ln_ffn_b):
    batch, seq, d = x.shape
    lp = LANES + seq
    assert d == D_MODEL and seq % (2 * GRID_W) == 0 and seq // GRID_W >= NA_KH
    assert lp % SCAN_CHUNK == 0 and (batch * lp) % ROW_TILE == 0
    assert batch % MOE_PARTS == 0 and (batch // MOE_PARTS * lp) % ROUTE_TILE == 0
    n = batch * lp

    front = jnp.zeros((batch, FRONT_PAD, d), x.dtype)
    meta = jnp.broadcast_to(meta_tokens[None].astype(x.dtype), (batch, N_META, d))
    h = jnp.concatenate([front, meta, x], axis=1).reshape(n, d)

    u = _matmul(h, lru_w_in[0].astype(BF16))
    row = lambda v: v[None, :]
    def scan(direction, reverse, fuse_out=None):
        return _lru_scan(
            u, lru_conv_w[0], row(lru_conv_b[0]),
            (0.5 * lru_wa[0, direction]).astype(BF16), row(0.5 * lru_ba[0, direction]),
            (0.5 * lru_wx[0, direction]).astype(BF16), row(0.5 * lru_bx[0, direction]),
            row(lru_lambda[0, direction]), reverse=reverse, lp=lp, fuse_out=fuse_out)

    h_fwd = scan(0, False)
    h = scan(1, True, fuse_out=(h_fwd, h, lru_w_out[0].astype(BF16), row(ln_mix_g[0]), row(ln_mix_b[0])))
    h = _moe_layer(h, router_w[0], router_b[0], moe_w_gu, moe_b_gu, moe_w_down, moe_b_down,
                   ln_ffn_g[0], ln_ffn_b[0], 0, lp)

    qkv = _qkv(h, na_w_qkv[0].astype(BF16))
    att = _na_attention(qkv, _na_bias_table(na_rpb[0], seq // GRID_W), na_meta_bias[0].astype(F32),
                        batch=batch, lp=lp)
    h = _na_out(att, h, na_w_out[0].astype(BF16), row(ln_mix_g[1]), row(ln_mix_b[1]))
    h = _moe_layer(h, router_w[1], router_b[1], moe_w_gu, moe_b_gu, moe_w_down, moe_b_down,
                   ln_ffn_g[1], ln_ffn_b[1], 1, lp, drop_front=(batch, lp))

    return h.reshape(batch, seq, d)
```

```python
import functools

import jax
import jax.numpy as jnp
from jax import lax
from jax.experimental import pallas as pl
from jax.experimental.pallas import tpu as pltpu
from jax.experimental.pallas import tpu_sc as plsc

F32 = jnp.float32
BF16 = jnp.bfloat16
I32 = jnp.int32
U32 = jnp.uint32

D_MODEL = 1024
N_META = 16
GRID_W = 64
LRU_BLOCKS = 4
LRU_BLOCK = D_MODEL // LRU_BLOCKS
CONV_W = 4
LRU_C = 8.0
NA_HEADS = 16
NA_HEAD_DIM = D_MODEL // NA_HEADS
NA_KH = 8
NA_KW = 16
N_EXPERTS = 32
TOP_K = 4
SWIGLU_LIMIT = 7.0
SWIGLU_ALPHA = 1.702
DEPTH = 2
DN_ALPHA = (2.0 * DEPTH) ** 0.25
LN_EPS = 1e-5

LANES = 128
SUBLANES = 8
FRONT_PAD = LANES - N_META
HEAD_PAIRS = D_MODEL // LANES
NEG_BIG = -1e30

ROW_TILE = 512
SCAN_CHUNK = 352
ROUTE_TILE = 512
MOVE_TILE = 256
EXPERT_TILE = 512
MOE_PARTS = 1
VMEM_LIMIT = 56 << 20

SC_CORES = 2
SC_SUBCORES = 16
SC_WORKERS = SC_CORES * SC_SUBCORES
SC_MAX_INDICES = 64
SC_ROW_BUFFER_BYTES = 128 << 10


def _params(*sem):
    return pltpu.CompilerParams(dimension_semantics=sem, vmem_limit_bytes=VMEM_LIMIT)


def _sigmoid(x):
    return 0.5 * jnp.tanh(0.5 * x) + 0.5


def _layer_norm(x, g, b):
    mu = jnp.mean(x, axis=-1, keepdims=True)
    xc = x - mu
    var = jnp.mean(xc * xc, axis=-1, keepdims=True)
    return xc * lax.rsqrt(var + LN_EPS) * g + b


def _matmul_kernel(x_ref, w_ref, o_ref):
    o_ref[...] = jnp.dot(x_ref[...].astype(BF16), w_ref[...], preferred_element_type=F32).astype(o_ref.dtype)


def _matmul(x, w_bf16):
    n, k = x.shape
    m = w_bf16.shape[1]
    return pl.pallas_call(
        _matmul_kernel,
        out_shape=jax.ShapeDtypeStruct((n, m), BF16),
        grid=(n // ROW_TILE,),
        in_specs=[pl.BlockSpec((ROW_TILE, k), lambda i: (i, 0)),
                  pl.BlockSpec((k, m), lambda i: (0, 0))],
        out_specs=pl.BlockSpec((ROW_TILE, m), lambda i: (i, 0)),
        compiler_params=_params("parallel"),
        name="in_proj",
    )(x, w_bf16)


def _lru_scan_kernel(xr_ref, xp_ref, xn_ref, cw_ref, cb_ref, wa_ref, ba_ref, wx_ref, bx_ref,
                     lam_ref, *rest, reverse, chunks_per_batch, fused):
    if fused:
        hf_ref, y_ref, res_ref, wout_ref, g_ref, beta_ref, h_ref, a_sc, b_sc, h_sc, carry_sc = rest
    else:
        h_ref, a_sc, b_sc, h_sc, carry_sc = rest
    t_rows = xr_ref.shape[0]
    seg = t_rows // SUBLANES
    halo = xp_ref.shape[0]
    lane_tiles = xr_ref.shape[1] // LANES
    step = pl.program_id(0)
    chunk = (pl.num_programs(0) - 1 - step) if reverse else step
    cib = chunk % chunks_per_batch
    first_chunk = cib == 0
    last_chunk = cib == chunks_per_batch - 1
    row8 = lax.broadcasted_iota(I32, (SUBLANES, 1), 0)
    cat = lambda parts: jnp.concatenate(parts, axis=0)

    head_pos = cib * t_rows + lax.broadcasted_iota(I32, (LANES, 1), 0)
    head_real = head_pos >= FRONT_PAD
    xr = xr_ref[...].astype(F32)
    xr = cat([jnp.where(head_real, xr[:LANES], 0.0), xr[LANES:]])
    prev_pos = cib * t_rows - halo + lax.broadcasted_iota(I32, (halo, 1), 0)
    prev = jnp.where(prev_pos >= FRONT_PAD, xp_ref[...].astype(F32), 0.0)
    nxt = jnp.where(last_chunk, 0.0, xn_ref[...].astype(F32))

    xm1 = pltpu.roll(xr, 1, 0)
    xm2 = pltpu.roll(xr, 2, 0)
    xp1 = pltpu.roll(xr, t_rows - 1, 0)
    xm1 = cat([jnp.where(row8 == 0, prev[halo - 1:halo], xm1[:SUBLANES]), xm1[SUBLANES:]])
    xm2_head = jnp.where(row8 == 0, prev[halo - 2:halo - 1],
                         jnp.where(row8 == 1, prev[halo - 1:halo], xm2[:SUBLANES]))
    xm2 = cat([xm2_head, xm2[SUBLANES:]])
    xp1 = cat([xp1[:t_rows - SUBLANES], jnp.where(row8 == SUBLANES - 1, nxt[0:1], xp1[t_rows - SUBLANES:])])
    cw = cw_ref[...]
    xc = cw[0:1, :] * xm2 + cw[1:2, :] * xm1 + cw[2:3, :] * xr + cw[3:4, :] * xp1 + cb_ref[...]

    xcb = xc.astype(BF16)
    za, zx = [], []
    for blk in range(LRU_BLOCKS):
        xblk = xcb[:, blk * LRU_BLOCK:(blk + 1) * LRU_BLOCK]
        za.append(jnp.dot(xblk, wa_ref[blk], preferred_element_type=F32))
        zx.append(jnp.dot(xblk, wx_ref[blk], preferred_element_type=F32))
    tanh_a = jnp.tanh(jnp.concatenate(za, axis=-1) + ba_ref[...])
    tanh_x = jnp.tanh(jnp.concatenate(zx, axis=-1) + bx_ref[...])

    z = -lam_ref[...]
    softplus = jnp.maximum(z, 0.0) + jnp.log1p(jnp.exp(-jnp.abs(z)))
    half_rate = (-0.5 * LRU_C) * softplus
    log_a = tanh_a * half_rate + half_rate
    a = jnp.exp(log_a)
    gap = jnp.tanh(log_a) * (-1.0 - a * a)
    mult = jnp.where(gap > 0.0, gap * lax.rsqrt(gap), 0.0)
    if reverse:
        tail = jnp.where(jnp.logical_and(last_chunk, row8 == SUBLANES - 1), 1.0, mult[t_rows - SUBLANES:])
        mult = cat([mult[:t_rows - SUBLANES], tail])
    else:
        mult = cat([jnp.where(head_pos == FRONT_PAD, 1.0, mult[:LANES]), mult[LANES:]])
    b = mult * ((0.5 * tanh_x + 0.5) * xc)
    b = cat([jnp.where(head_real, b[:LANES], 0.0), b[LANES:]])
    for c in range(lane_tiles):
        a_sc[c] = a[:, c * LANES:(c + 1) * LANES]
        b_sc[c] = b[:, c * LANES:(c + 1) * LANES]

    @pl.when(last_chunk if reverse else first_chunk)
    def _():
        carry_sc[...] = jnp.zeros_like(carry_sc)

    order = range(seg - 1, -1, -1) if reverse else range(seg)
    rows_of = lambda j: pl.ds(j, SUBLANES, stride=seg)

    local = [jnp.zeros((SUBLANES, LANES), F32)] * lane_tiles
    prod = [jnp.ones((SUBLANES, LANES), F32)] * lane_tiles
    for j in order:
        for c in range(lane_tiles):
            av = a_sc[c, rows_of(j), :]
            local[c] = av * local[c] + b_sc[c, rows_of(j), :]
            prod[c] = av * prod[c]
            h_sc[c, rows_of(j), :] = local[c]
            a_sc[c, rows_of(j), :] = prod[c]

    carry_in = carry_sc[...]
    carry_out = []
    seg_carry = []
    for c in range(lane_tiles):
        state = carry_in[:, c * LANES:(c + 1) * LANES]
        rows = [None] * SUBLANES
        for s in (range(SUBLANES - 1, -1, -1) if reverse else range(SUBLANES)):
            rows[s] = state
            state = local[c][s:s + 1] + prod[c][s:s + 1] * state
        seg_carry.append(cat(rows))
        carry_out.append(state)
    carry_sc[...] = jnp.concatenate(carry_out, axis=-1)

    for j in order:
        for c in range(lane_tiles):
            h_sc[c, rows_of(j), :] = h_sc[c, rows_of(j), :] + a_sc[c, rows_of(j), :] * seg_carry[c]
    states = jnp.concatenate([h_sc[c] for c in range(lane_tiles)], axis=-1)
    if fused:
        gated = (hf_ref[...] + states) * _gelu_tanh(y_ref[...].astype(F32))
        mix = jnp.dot(gated.astype(BF16), wout_ref[...], preferred_element_type=F32)
        h_ref[...] = _layer_norm(DN_ALPHA * res_ref[...] + mix, g_ref[...], beta_ref[...])
    else:
        h_ref[...] = states


def _lru_scan(u, cw, cb, wa_half, ba_half, wx_half, bx_half, lam, *, reverse, lp, fuse_out=None):
    n = u.shape[0]
    d = D_MODEL
    t = SCAN_CHUNK
    halo = 2 * SUBLANES
    n_chunks = n // t
    cpb = lp // t
    t_h = t // halo
    n_h = n // halo

    def chunk_of(i):
        return (n_chunks - 1 - i) if reverse else i

    kern = functools.partial(_lru_scan_kernel, reverse=reverse, chunks_per_batch=cpb, fused=fuse_out is not None)
    chunk_rows = lambda col: pl.BlockSpec((t, d), lambda i: (chunk_of(i), col))
    extra_specs, extra_args = [], ()
    if fuse_out is not None:
        h_other, res, w_out, ln_g, ln_b = fuse_out
        extra_specs = [chunk_rows(0), chunk_rows(1), chunk_rows(0),
                       pl.BlockSpec((d, d), lambda i: (0, 0)),
                       pl.BlockSpec((1, d), lambda i: (0, 0)), pl.BlockSpec((1, d), lambda i: (0, 0))]
        extra_args = (h_other, u, res, w_out, ln_g, ln_b)
    full2 = lambda shape: pl.BlockSpec(shape, lambda i: (0, 0))
    full3 = lambda shape: pl.BlockSpec(shape, lambda i: (0, 0, 0))
    tile_major = pltpu.VMEM((d // LANES, t, LANES), F32)
    return pl.pallas_call(
        kern,
        out_shape=jax.ShapeDtypeStruct((n, d), F32),
        grid=(n_chunks,),
        in_specs=[
            pl.BlockSpec((t, d), lambda i: (chunk_of(i), 0)),
            pl.BlockSpec((halo, d), lambda i: (jnp.maximum(chunk_of(i) * t_h - 1, 0), 0)),
            pl.BlockSpec((halo, d), lambda i: (jnp.minimum((chunk_of(i) + 1) * t_h, n_h - 1), 0)),
            full2((CONV_W, d)), full2((1, d)),
            full3((LRU_BLOCKS, LRU_BLOCK, LRU_BLOCK)), full2((1, d)),
            full3((LRU_BLOCKS, LRU_BLOCK, LRU_BLOCK)), full2((1, d)),
            full2((1, d)),
        ] + extra_specs,
        out_specs=pl.BlockSpec((t, d), lambda i: (chunk_of(i), 0)),
        scratch_shapes=[tile_major, tile_major, tile_major, pltpu.VMEM((1, d), F32)],
        compiler_params=_params("arbitrary"),
        name=("lru_scan_bwd" if reverse else "lru_scan_fwd") + ("_out" if fuse_out is not None else ""),
    )(u, u, u, cw, cb, wa_half, ba_half, wx_half, bx_half, lam, *extra_args)


def _gelu_tanh(y):
    c = 0.7978845608028654
    return y * (0.5 * (1.0 + jnp.tanh(c * (y + 0.044715 * (y * y * y)))))


def _pack_bf16_pairs(x):
    half = x.shape[1] // 2
    lo = pltpu.bitcast(x[:, :half].astype(BF16).astype(F32), U32)
    hi = pltpu.bitcast(x[:, half:].astype(BF16).astype(F32), U32)
    return pltpu.bitcast((lo >> 16) | (hi & jnp.uint32(0xFFFF0000)), I32)


def _unpack_bf16_pairs(packed, dtype=BF16):
    u = pltpu.bitcast(packed, U32)
    lo = pltpu.bitcast(u << 16, F32).astype(dtype)
    hi = pltpu.bitcast(u & jnp.uint32(0xFFFF0000), F32).astype(dtype)
    return jnp.concatenate([lo, hi], axis=-1)


def _router_kernel(h_ref, wt_ref, b_ref, real_ref, idx_ref, gate_ref, rank_ref, cnt_ref, hp_ref, base_sc):
    tm = h_ref.shape[0]
    h = h_ref[...]
    hp_ref[...] = _pack_bf16_pairs(h)

    @pl.when(pl.program_id(0) == 0)
    def _():
        base_sc[...] = jnp.zeros_like(base_sc)

    wt = wt_ref[...]
    h_hi = h.astype(BF16)
    h_lo = (h - h_hi.astype(F32)).astype(BF16)
    w_hi = wt.astype(BF16)
    w_lo = (wt - w_hi.astype(F32)).astype(BF16)
    contract_last = (((1,), (1,)), ((), ()))
    mm = lambda a, b: lax.dot_general(a, b, contract_last, preferred_element_type=F32)
    logits = mm(w_hi, h_hi) + (mm(w_hi, h_lo) + mm(w_lo, h_hi)) + b_ref[...]

    expert = lax.broadcasted_iota(I32, (N_EXPERTS, tm), 0).astype(F32)
    vals = logits
    idx_rows, val_rows, onehots = [], [], []
    for _ in range(TOP_K):
        m = jnp.max(vals, axis=0, keepdims=True)
        idx = jnp.min(jnp.where(vals == m, expert, float(N_EXPERTS)), axis=0, keepdims=True)
        hit = expert == idx
        onehots.append(jnp.where(hit, real_ref[...], 0.0))
        idx_rows.append(idx)
        val_rows.append(m)
        vals = jnp.where(hit, -jnp.inf, vals)

    onehot_all = jnp.concatenate(onehots, axis=0)
    t_from = lax.broadcasted_iota(I32, (tm, tm), 0)
    t_to = lax.broadcasted_iota(I32, (tm, tm), 1)
    earlier = jnp.where(t_from < t_to, 1.0, 0.0).astype(BF16)
    before = jnp.dot(onehot_all.astype(BF16), earlier, preferred_element_type=F32)
    base = base_sc[...]
    starts = []
    for k in range(TOP_K):
        starts.append(base)
        base = base + jnp.sum(onehots[k], axis=1, keepdims=True)
    contrib = onehot_all * (jnp.concatenate(starts, axis=0) + before)
    rank_rows = [jnp.sum(contrib[k * N_EXPERTS:(k + 1) * N_EXPERTS], axis=0, keepdims=True)
                 for k in range(TOP_K)]

    top_vals = jnp.concatenate(val_rows, axis=0)
    e = jnp.exp(top_vals - top_vals[0:1])
    idx_ref[...] = jnp.concatenate(idx_rows, axis=0).astype(I32)
    gate_ref[...] = e / jnp.sum(e, axis=0, keepdims=True)
    rank_ref[...] = jnp.concatenate(rank_rows, axis=0).astype(I32)
    base_sc[...] = base
    cnt_ref[...] = base.astype(I32)


def _router(h, w, b, real, part):
    d = h.shape[1]
    n = h.shape[0] // MOE_PARTS
    tm = ROUTE_TILE
    first = part * (n // tm)
    out4 = lambda dt: jax.ShapeDtypeStruct((TOP_K, n), dt)
    spec4 = pl.BlockSpec((TOP_K, tm), lambda i: (0, i))
    return pl.pallas_call(
        _router_kernel,
        out_shape=(out4(I32), out4(F32), out4(I32), jax.ShapeDtypeStruct((N_EXPERTS, 1), I32),
                   jax.ShapeDtypeStruct((n, d // 2), I32)),
        grid=(n // tm,),
        in_specs=[pl.BlockSpec((tm, d), lambda i: (first + i, 0)),
                  pl.BlockSpec((N_EXPERTS, d), lambda i: (0, 0)),
                  pl.BlockSpec((N_EXPERTS, 1), lambda i: (0, 0)),
                  pl.BlockSpec((1, tm), lambda i: (0, first + i))],
        out_specs=(spec4, spec4, spec4, pl.BlockSpec((N_EXPERTS, 1), lambda i: (0, 0)),
                   pl.BlockSpec((tm, d // 2), lambda i: (i, 0))),
        scratch_shapes=[pltpu.VMEM((N_EXPERTS, 1), F32)],
        compiler_params=_params("arbitrary"),
        name="router",
    )(h, w.T, b[:, None], real)


def _sc_gather_rows(table, idx, name):
    rows, width = table.shape
    total = idx.shape[0]
    per_worker = total // SC_WORKERS
    max_chunk = min(SC_MAX_INDICES, SC_ROW_BUFFER_BYTES // (width * table.dtype.itemsize))
    chunk = max(c for c in range(SUBLANES, max_chunk + 1, SUBLANES) if per_worker % (2 * c) == 0)
    n_chunks = per_worker // chunk
    assert total == SC_WORKERS * n_chunks * chunk and n_chunks % 2 == 0
    mesh = plsc.VectorSubcoreMesh(core_axis_name="c", subcore_axis_name="s",
                                  num_cores=SC_CORES, num_subcores=SC_SUBCORES)

    def body(table_hbm, idx_hbm, out_hbm, idx_v, rows0, rows1, gsem0, gsem1, psem0, psem1):
        worker = lax.axis_index("s") * SC_CORES + lax.axis_index("c")
        base = worker * per_worker
        pltpu.sync_copy(idx_hbm.at[worker], idx_v)
        bufs = (rows0, rows1)
        gsems = (gsem0, gsem1)
        psems = (psem0, psem1)

        def gather(c, slot):
            return pltpu.make_async_copy(table_hbm.at[idx_v.at[c]], bufs[slot], gsems[slot])

        def put(c, slot):
            return pltpu.make_async_copy(bufs[slot], out_hbm.at[pl.ds(base + c * chunk, chunk)], psems[slot])

        gather(0, 0).start()

        @pl.loop(0, n_chunks, step=2)
        def _(c0):
            for slot in range(2):
                c = c0 + slot

                @pl.when(c + 1 < n_chunks)
                def _():
                    @pl.when(c >= 1)
                    def _():
                        put(c - 1, 1 - slot).wait()
                    gather(c + 1, 1 - slot).start()

                gather(c, slot).wait()
                put(c, slot).start()

        put(n_chunks - 2, 0).wait()
        put(n_chunks - 1, 1).wait()

    return pl.kernel(
        body,
        out_type=jax.ShapeDtypeStruct((total, width), table.dtype),
        mesh=mesh,
        scratch_types=[pltpu.VMEM((n_chunks, chunk), I32),
                       pltpu.VMEM((chunk, width), table.dtype), pltpu.VMEM((chunk, width), table.dtype),
                       pltpu.SemaphoreType.DMA, pltpu.SemaphoreType.DMA,
                       pltpu.SemaphoreType.DMA, pltpu.SemaphoreType.DMA],
        name=name,
    )(table, idx.reshape(SC_WORKERS, n_chunks, chunk))


def _sc_scatter_rows(src, dest, out_rows, name):
    n, width = src.shape
    fan = dest.shape[0]
    per_worker = n // SC_WORKERS
    chunk = max(c for c in range(SUBLANES, SC_MAX_INDICES + 1, SUBLANES) if per_worker % (2 * c) == 0)
    n_chunks = per_worker // chunk
    assert n == SC_WORKERS * n_chunks * chunk and n_chunks % 2 == 0
    mesh = plsc.VectorSubcoreMesh(core_axis_name="c", subcore_axis_name="s",
                                  num_cores=SC_CORES, num_subcores=SC_SUBCORES)

    def body(src_hbm, idx_hbm, out_hbm, idx_v, rows0, rows1, lsem0, lsem1, ssem0, ssem1):
        worker = lax.axis_index("s") * SC_CORES + lax.axis_index("c")
        base = worker * per_worker
        pltpu.sync_copy(idx_hbm.at[worker], idx_v)
        bufs = (rows0, rows1)
        lsems = (lsem0, lsem1)
        ssems = (ssem0, ssem1)

        def load(c, slot):
            return pltpu.make_async_copy(src_hbm.at[pl.ds(base + c * chunk, chunk)], bufs[slot], lsems[slot])

        def scatter(c, k, slot):
            return pltpu.make_async_copy(bufs[slot], out_hbm.at[idx_v.at[k * n_chunks + c]], ssems[slot])

        load(0, 0).start()

        @pl.loop(0, n_chunks, step=2)
        def _(c0):
            for slot in range(2):
                c = c0 + slot

                @pl.when(c + 1 < n_chunks)
                def _():
                    @pl.when(c >= 1)
                    def _():
                        for k in range(fan):
                            scatter(c - 1, k, 1 - slot).wait()
                    load(c + 1, 1 - slot).start()

                load(c, slot).wait()
                for k in range(fan):
                    scatter(c, k, slot).start()

        for k in range(fan):
            scatter(n_chunks - 2, k, 0).wait()
        for k in range(fan):
            scatter(n_chunks - 1, k, 1).wait()

    idx = dest.reshape(fan, SC_WORKERS, n_chunks, chunk).transpose(1, 0, 2, 3)
    return pl.kernel(
        body,
        out_type=jax.ShapeDtypeStruct((out_rows, width), src.dtype),
        mesh=mesh,
        scratch_types=[pltpu.VMEM((fan * n_chunks, chunk), I32),
                       pltpu.VMEM((chunk, width), src.dtype), pltpu.VMEM((chunk, width), src.dtype),
                       pltpu.SemaphoreType.DMA, pltpu.SemaphoreType.DMA,
                       pltpu.SemaphoreType.DMA, pltpu.SemaphoreType.DMA],
        name=name,
    )(src, idx.reshape(SC_WORKERS, fan * n_chunks, chunk))


def _expert_kernel(be_ref, valid_ref, xs_ref, wgu_ref, bgu_ref, wd_ref, bd_ref, y_ref, wgu_sc, wd_sc):
    i = pl.program_id(0)
    e = be_ref[i]
    e_prev = be_ref[jnp.maximum(i - 1, 0)]
    d = wd_sc.shape[0]

    @pl.when((i == 0) | (e != e_prev))
    def _():
        for r in range(0, d, LANES):
            wgu_sc[r:r + LANES, :] = wgu_ref[0, r:r + LANES, :].astype(BF16)
            wd_sc[r:r + LANES, :] = wd_ref[0, r:r + LANES, :].astype(BF16)

    @pl.when(valid_ref[i] > 0)
    def _():
        x = _unpack_bf16_pairs(xs_ref[...])
        h = jnp.dot(x, wgu_sc[...], preferred_element_type=F32) + bgu_ref[0]
        glu = jnp.minimum(h[:, :d], SWIGLU_LIMIT)
        lin = jnp.clip(h[:, d:], -SWIGLU_LIMIT, SWIGLU_LIMIT)
        act = glu * _sigmoid(SWIGLU_ALPHA * glu) * (lin + 1.0)
        y = jnp.dot(act.astype(BF16), wd_sc[...], preferred_element_type=F32) + bd_ref[0]
        y_ref[...] = _pack_bf16_pairs(y)

    @pl.when(valid_ref[i] == 0)
    def _():
        y_ref[...] = jnp.zeros_like(y_ref)


def _experts(block_e, valid_rows, xs, w_gu, b_gu, w_down, b_down, layer):
    cap = xs.shape[0]
    d = D_MODEL
    tm = EXPERT_TILE
    n_blocks = cap // tm
    grid_spec = pltpu.PrefetchScalarGridSpec(
        num_scalar_prefetch=2,
        grid=(n_blocks,),
        in_specs=[
            pl.BlockSpec((tm, d // 2), lambda i, be, nu: (i, 0)),
            pl.BlockSpec((None, 1, d, 2 * d), lambda i, be, nu: (layer, be[i], 0, 0)),
            pl.BlockSpec((1, 1, 2 * d), lambda i, be, nu: (be[i], 0, 0)),
            pl.BlockSpec((None, 1, d, d), lambda i, be, nu: (layer, be[i], 0, 0)),
            pl.BlockSpec((1, 1, d), lambda i, be, nu: (be[i], 0, 0)),
        ],
        out_specs=pl.BlockSpec((tm, d // 2), lambda i, be, nu: (i, 0)),
        scratch_shapes=[pltpu.VMEM((d, 2 * d), BF16), pltpu.VMEM((d, d), BF16)],
    )
    return pl.pallas_call(
        _expert_kernel,
        out_shape=jax.ShapeDtypeStruct((cap, d // 2), I32),
        grid_spec=grid_spec,
        compiler_params=_params("arbitrary"),
        name="moe_experts",
    )(block_e, valid_rows, xs, w_gu, b_gu[layer][:, None, :], w_down, b_down[layer][:, None, :])


def _combine_kernel(g_ref, b_ref, *refs, n_sub):
    o_ref = refs[-1]
    rows = o_ref.shape[0] // n_sub
    for s in range(n_sub):
        gate_ref, res_ref, yk_ref = refs[3 * s:3 * s + 3]
        gates = gate_ref[...]
        ffn = gates[:, 0:1] * _unpack_bf16_pairs(yk_ref[0], F32)
        for k in range(1, TOP_K):
            ffn = ffn + gates[:, k:k + 1] * _unpack_bf16_pairs(yk_ref[k], F32)
        o_ref[s * rows:(s + 1) * rows, :] = _layer_norm(DN_ALPHA * res_ref[...] + ffn, g_ref[...], b_ref[...])


def _combine(gates, res, yk, g, b, part, acc, drop_front=None):
    n, d = res.shape
    if drop_front is None:
        tt, n_sub = MOVE_TILE, 1
        steps = n // MOE_PARTS // tt
        grid = (steps,)
        local = lambda s, i: i
        src = lambda s, i: part * steps + i
        dst = lambda i: part * steps + i
        out_rows = n
    else:
        batch, lp = drop_front
        tt, n_sub = LANES, 2
        tiles = lp // tt
        part_batch = batch // MOE_PARTS
        pairs = (tiles - 1) // n_sub
        assert pairs * n_sub == tiles - 1
        grid = (part_batch, pairs)
        local = lambda s, b_, j: b_ * tiles + 1 + n_sub * j + s
        src = lambda s, b_, j: (part * part_batch + b_) * tiles + 1 + n_sub * j + s
        dst = lambda b_, j: (part * part_batch + b_) * pairs + j
        out_rows = n - batch * tt
    vec = pl.BlockSpec((1, d), lambda *_: (0, 0))
    in_specs = [vec, vec]
    args = (g, b)
    for s in range(n_sub):
        in_specs += [pl.BlockSpec((tt, TOP_K), lambda *i, s=s: (local(s, *i), 0)),
                     pl.BlockSpec((tt, d), lambda *i, s=s: (src(s, *i), 0)),
                     pl.BlockSpec((TOP_K, tt, d // 2), lambda *i, s=s: (0, local(s, *i), 0))]
        args += (gates, res, yk)
    aliases = {}
    if acc is not None:
        in_specs.append(pl.BlockSpec(memory_space=pl.ANY))
        args += (acc,)
        aliases = {len(args) - 1: 0}
    return pl.pallas_call(
        functools.partial(_combine_kernel, n_sub=n_sub),
        out_shape=jax.ShapeDtypeStruct((out_rows, d), F32),
        grid=grid,
        in_specs=in_specs,
        out_specs=pl.BlockSpec((n_sub * tt, d), lambda *i: (dst(*i), 0)),
        input_output_aliases=aliases,
        compiler_params=_params(*(["parallel"] * len(grid))),
        name="moe_combine",
    )(*args)


def _moe_layer(h, router_w, router_b, w_gu, b_gu, w_down, b_down, ln_g, ln_b, layer, lp, drop_front=None):
    n, d = h.shape
    tm = EXPERT_TILE
    n_part = n // MOE_PARTS
    assert n_part % lp == 0
    real_all = ((jnp.arange(n, dtype=I32) % lp) >= FRONT_PAD).astype(F32)[None, :]
    token = jnp.arange(n_part, dtype=I32)
    real = (token % lp) >= FRONT_PAD
    n_real = n_part - (n_part // lp) * FRONT_PAD
    n_blocks = -(-(n_real * TOP_K + N_EXPERTS * (tm - 1)) // tm) + 1
    block_start = jnp.arange(n_blocks, dtype=I32) * tm
    out = None
    for part in range(MOE_PARTS):
        idx, gates, rank, counts, h_packed = _router(h, router_w, router_b, real_all, part)
        counts = counts[:, 0]
        padded = (counts + tm - 1) // tm * tm
        pad_end = jnp.cumsum(padded)
        pad_start = pad_end - padded
        group_start = jnp.sum(jnp.where(idx[:, :, None] == jnp.arange(N_EXPERTS, dtype=I32), pad_start, 0), axis=-1)
        dest = jnp.where(real, group_start + rank, (n_blocks - 1) * tm + token % tm)
        block_e = jnp.minimum(jnp.sum((pad_end[None, :] <= block_start[:, None]).astype(I32), axis=1),
                              N_EXPERTS - 1)
        group_end = (pad_start + counts)[block_e]
        valid_rows = jnp.where(block_start < pad_end[-1], jnp.clip(group_end - block_start, 0, tm), 0).astype(I32)

        xs = _sc_scatter_rows(h_packed, dest, n_blocks * tm, "moe_dispatch")
        ys = _experts(block_e, valid_rows, xs, w_gu, b_gu, w_down, b_down, layer)
        yk = _sc_gather_rows(ys, dest.reshape(TOP_K * n_part), "moe_collect").reshape(TOP_K, n_part, d // 2)
        out = _combine(gates.T, h, yk, ln_g[None, :], ln_b[None, :], part, out, drop_front)
    return out


def _qkv_kernel(x_ref, w_ref, o_ref):
    xb = x_ref[...].astype(BF16)
    d = x_ref.shape[1]
    for part in range(3):
        acc = jnp.dot(xb, w_ref[:, part * d:(part + 1) * d], preferred_element_type=F32)
        if part == 0:
            acc = acc * (NA_HEAD_DIM ** -0.5)
        for p in range(HEAD_PAIRS):
            o_ref[part * HEAD_PAIRS + p] = acc[:, p * LANES:(p + 1) * LANES].astype(BF16)


def _qkv(x, w_bf16):
    n, d = x.shape
    return pl.pallas_call(
        _qkv_kernel,
        out_shape=jax.ShapeDtypeStruct((3 * HEAD_PAIRS, n, LANES), BF16),
        grid=(n // ROW_TILE,),
        in_specs=[pl.BlockSpec((ROW_TILE, d), lambda i: (i, 0)),
                  pl.BlockSpec((d, 3 * d), lambda i: (0, 0))],
        out_specs=pl.BlockSpec((3 * HEAD_PAIRS, ROW_TILE, LANES), lambda i: (0, i, 0)),
        compiler_params=_params("parallel"),
        name="na_qkv",
    )(x, w_bf16)


def _na_kernel(q_ref, k_ref, v_ref, bias_a_ref, bias_b_ref, mb_ref, o_ref, *, rows):
    s = pl.program_id(1)
    w = GRID_W
    low = lax.broadcasted_iota(I32, (w, LANES), 1) < NA_HEAD_DIM
    contract_last = (((1,), (1,)), ((), ()))
    meta0 = FRONT_PAD

    def stacked_q(p, q0):
        qp = q_ref[p, q0:q0 + w, :]
        zero = jnp.zeros_like(qp)
        return jnp.concatenate([jnp.where(low, qp, zero), jnp.where(low, zero, qp)], axis=0)

    def attend_all(q0, k0, bias_ref):
        scores = []
        for p in range(HEAD_PAIRS):
            q2 = stacked_q(p, q0)
            s_meta = lax.dot_general(q2, k_ref[p, meta0:meta0 + N_META, :], contract_last,
                                     preferred_element_type=F32) + mb_ref[p]
            s_win = None
            if k0 is not None:
                s_win = lax.dot_general(q2, k_ref[p, pl.ds(k0, NA_KH * w), :], contract_last,
                                        preferred_element_type=F32) + bias_ref[0, p]
            scores.append((s_meta, s_win))
        outs = []
        for p in range(HEAD_PAIRS):
            s_meta, s_win = scores[p]
            m = jnp.max(s_meta, axis=-1, keepdims=True)
            if s_win is not None:
                m = jnp.maximum(m, jnp.max(s_win, axis=-1, keepdims=True))
            p_meta = jnp.exp(s_meta - m)
            denom = jnp.sum(p_meta, axis=-1, keepdims=True)
            o = jnp.dot(p_meta.astype(BF16), v_ref[p, meta0:meta0 + N_META, :], preferred_element_type=F32)
            if s_win is not None:
                p_win = jnp.exp(s_win - m)
                denom = denom + jnp.sum(p_win, axis=-1, keepdims=True)
                o = o + jnp.dot(p_win.astype(BF16), v_ref[p, pl.ds(k0, NA_KH * w), :],
                                preferred_element_type=F32)
            o = o / denom
            outs.append(jnp.where(low, o[:w], o[w:]))
        return outs

    @pl.when(s == 0)
    def _():
        qrow = lax.broadcasted_iota(I32, (w, LANES), 0)
        for p, o in enumerate(attend_all(LANES - w, None, None)):
            o_ref[p, 0:LANES - w, :] = jnp.zeros((LANES - w, LANES), o_ref.dtype)
            o_ref[p, LANES - w:LANES, :] = jnp.where(qrow >= w - N_META, o, 0.0).astype(o_ref.dtype)

    @pl.when(s >= 1)
    def _():
        for half, bias_ref in enumerate((bias_a_ref, bias_b_ref)):
            r = 2 * (s - 1) + half
            rs = jnp.clip(r - NA_KH // 2, 0, rows - NA_KH)
            k0 = pl.multiple_of(LANES + rs * w, w)
            for p, o in enumerate(attend_all(half * w, k0, bias_ref)):
                o_ref[p, half * w:(half + 1) * w, :] = o.astype(o_ref.dtype)


def _na_bias_table(rpb, rows):
    del rows
    w = GRID_W
    q = jnp.arange(w)
    col_start = jnp.clip(q - NA_KW // 2, 0, w - NA_KW)
    c = jnp.arange(w)
    in_win = (c[None, :] >= col_start[:, None]) & (c[None, :] < col_start[:, None] + NA_KW)
    pad = w - NA_KW
    rp = jnp.pad(rpb.astype(F32), ((0, 0), (0, 0), (pad, pad)))
    toeplitz = jnp.stack([rp[:, :, w - 1 - qq:2 * w - 1 - qq] for qq in range(w)], axis=2)
    toeplitz = jnp.where(in_win[None, None], toeplitz, NEG_BIG)
    tabs = [jnp.concatenate([toeplitz[:, NA_KH - 1 - v + j] for j in range(NA_KH)], axis=-1) for v in range(NA_KH)]
    return jnp.stack(tabs, axis=0)


def _na_attention(qkv, bias_tab, meta_bias, *, batch, lp):
    n = qkv.shape[1]
    w = GRID_W
    rows = (lp - LANES) // w
    tiles = lp // LANES

    def variant(half):
        def index(b, s):
            r = jnp.clip(2 * (s - 1) + half, 0, rows - 1)
            return (r - jnp.clip(r - NA_KH // 2, 0, rows - NA_KH), 0, 0, 0)
        return index

    kern = functools.partial(_na_kernel, rows=rows)
    bias_block = (1, HEAD_PAIRS, 2 * w, NA_KH * w)
    bias_tab = bias_tab.reshape(NA_KH, HEAD_PAIRS, 2 * w, NA_KH * w)
    return pl.pallas_call(
        kern,
        out_shape=jax.ShapeDtypeStruct((HEAD_PAIRS, n, LANES), BF16),
        grid=(batch, tiles),
        in_specs=[
            pl.BlockSpec((HEAD_PAIRS, LANES, LANES), lambda b, s: (0, b * tiles + s, 0)),
            pl.BlockSpec((HEAD_PAIRS, lp, LANES), lambda b, s: (1, b, 0)),
            pl.BlockSpec((HEAD_PAIRS, lp, LANES), lambda b, s: (2, b, 0)),
            pl.BlockSpec(bias_block, variant(0)),
            pl.BlockSpec(bias_block, variant(1)),
            pl.BlockSpec((HEAD_PAIRS, 2 * w, N_META), lambda b, s: (0, 0, 0)),
        ],
        out_specs=pl.BlockSpec((HEAD_PAIRS, LANES, LANES), lambda b, s: (0, b * tiles + s, 0)),
        compiler_params=_params("parallel", "arbitrary"),
        name="na_attention",
    )(qkv, qkv, qkv, bias_tab, bias_tab,
      jnp.repeat(meta_bias, w, axis=0).reshape(HEAD_PAIRS, 2 * w, N_META))


def _na_out_kernel(o_ref, res_ref, w_ref, g_ref, b_ref, out_ref):
    att = jnp.concatenate([o_ref[p] for p in range(HEAD_PAIRS)], axis=-1)
    mix = jnp.dot(att, w_ref[...], preferred_element_type=F32)
    out_ref[...] = _layer_norm(DN_ALPHA * res_ref[...] + mix, g_ref[...], b_ref[...])


def _na_out(o, res, w_bf16, g, b):
    n, d = res.shape
    vec = pl.BlockSpec((1, d), lambda i: (0, 0))
    return pl.pallas_call(
        _na_out_kernel,
        out_shape=jax.ShapeDtypeStruct((n, d), F32),
        grid=(n // ROW_TILE,),
        in_specs=[pl.BlockSpec((HEAD_PAIRS, ROW_TILE, LANES), lambda i: (0, i, 0)),
                  pl.BlockSpec((ROW_TILE, d), lambda i: (i, 0)),
                  pl.BlockSpec((d, d), lambda i: (0, 0)), vec, vec],
        out_specs=pl.BlockSpec((ROW_TILE, d), lambda i: (i, 0)),
        compiler_params=_params("parallel"),
        name="na_out",
    )(o, res, w_bf16, g, b)


def kernel(x, meta_tokens, lru_w_in, lru_conv_w, lru_conv_b, lru_wa, lru_ba, lru_wx, lru_bx, lru_lambda, lru_w_out, na_w_qkv, na_rpb, na_meta_bias, na_w_out, ln_mix_g, ln_mix_b, router_w, router_b, moe_w_gu, moe_b_gu, moe_w_down, moe_b_down, ln_ffn_g, ln_ffn_b):
    batch, seq, d = x.shape
    lp = LANES + seq
    assert d == D_MODEL and seq % (2 * GRID_W) == 0 and seq // GRID_W >= NA_KH
    assert lp % SCAN_CHUNK == 0 and (batch * lp) % ROW_TILE == 0
    assert batch % MOE_PARTS == 0 and (batch // MOE_PARTS * lp) % ROUTE_TILE == 0
    n = batch * lp

    front = jnp.zeros((batch, FRONT_PAD, d), x.dtype)
    meta = jnp.broadcast_to(meta_tokens[None].astype(x.dtype), (batch, N_META, d))
    h = jnp.concatenate([front, meta, x], axis=1).reshape(n, d)

    u = _matmul(h, lru_w_in[0].astype(BF16))
    row = lambda v: v[None, :]
    def scan(direction, reverse, fuse_out=None):
        return _lru_scan(
            u, lru_conv_w[0], row(lru_conv_b[0]),
            (0.5 * lru_wa[0, direction]).astype(BF16), row(0.5 * lru_ba[0, direction]),
            (0.5 * lru_wx[0, direction]).astype(BF16), row(0.5 * lru_bx[0, direction]),
            row(lru_lambda[0, direction]), reverse=reverse, lp=lp, fuse_out=fuse_out)

    h_fwd = scan(0, False)
    h = scan(1, True, fuse_out=(h_fwd, h, lru_w_out[0].astype(BF16), row(ln_mix_g[0]), row(ln_mix_b[0])))
    h = _moe_layer(h, router_w[0], router_b[0], moe_w_gu, moe_b_gu, moe_w_down, moe_b_down,
                   ln_ffn_g[0], ln_ffn_b[0], 0, lp)

    qkv = _qkv(h, na_w_qkv[0].astype(BF16))
    att = _na_attention(qkv, _na_bias_table(na_rpb[0], seq // GRID_W), na_meta_bias[0].astype(F32),
                        batch=batch, lp=lp)
    h = _na_out(att, h, na_w_out[0].astype(BF16), row(ln_mix_g[1]), row(ln_mix_b[1]))
    h = _moe_layer(h, router_w[1], router_b[1], moe_w_gu, moe_b_gu, moe_w_down, moe_b_down,
                   ln_ffn_g[1], ln_ffn_b[1], 1, lp, drop_front=(batch, lp))

    return h.reshape(batch, seq, d)
```

```python
import functools

import jax
import jax.numpy as jnp
from jax import lax
from jax.experimental import pallas as pl
from jax.experimental.pallas import tpu as pltpu
from jax.experimental.pallas import tpu_sc as plsc

F32 = jnp.float32
BF16 = jnp.bfloat16
I32 = jnp.int32
U32 = jnp.uint32

D_MODEL = 1024
N_META = 16
GRID_W = 64
LRU_BLOCKS = 4
LRU_BLOCK = D_MODEL // LRU_BLOCKS
CONV_W = 4
LRU_C = 8.0
NA_HEADS = 16
NA_HEAD_DIM = D_MODEL // NA_HEADS
NA_KH = 8
NA_KW = 16
N_EXPERTS = 32
TOP_K = 4
SWIGLU_LIMIT = 7.0
SWIGLU_ALPHA = 1.702
DEPTH = 2
DN_ALPHA = (2.0 * DEPTH) ** 0.25
LN_EPS = 1e-5

LANES = 128
SUBLANES = 8
FRONT_PAD = LANES - N_META
HEAD_PAIRS = D_MODEL // LANES
NEG_BIG = -1e30

ROW_TILE = 512
SCAN_CHUNK = 352
ROUTE_TILE = 512
MOVE_TILE = 256
EXPERT_TILE = 512
MOE_PARTS = 2
VMEM_LIMIT = 56 << 20

SC_CORES = 2
SC_SUBCORES = 16
SC_WORKERS = SC_CORES * SC_SUBCORES
SC_MAX_INDICES = 64
SC_ROW_BUFFER_BYTES = 128 << 10


def _params(*sem):
    return pltpu.CompilerParams(dimension_semantics=sem, vmem_limit_bytes=VMEM_LIMIT)


def _sigmoid(x):
    return 0.5 * jnp.tanh(0.5 * x) + 0.5


def _layer_norm(x, g, b):
    mu = jnp.mean(x, axis=-1, keepdims=True)
    xc = x - mu
    var = jnp.mean(xc * xc, axis=-1, keepdims=True)
    return xc * lax.rsqrt(var + LN_EPS) * g + b


def _matmul_kernel(x_ref, w_ref, o_ref):
    o_ref[...] = jnp.dot(x_ref[...].astype(BF16), w_ref[...], preferred_element_type=F32).astype(o_ref.dtype)


def _matmul(x, w_bf16):
    n, k = x.shape
    m = w_bf16.shape[1]
    return pl.pallas_call(
        _matmul_kernel,
        out_shape=jax.ShapeDtypeStruct((n, m), BF16),
        grid=(n // ROW_TILE,),
        in_specs=[pl.BlockSpec((ROW_TILE, k), lambda i: (i, 0)),
                  pl.BlockSpec((k, m), lambda i: (0, 0))],
        out_specs=pl.BlockSpec((ROW_TILE, m), lambda i: (i, 0)),
        compiler_params=_params("parallel"),
        name="in_proj",
    )(x, w_bf16)


def _lru_scan_kernel(xr_ref, xp_ref, xn_ref, cw_ref, cb_ref, wa_ref, ba_ref, wx_ref, bx_ref,
                     lam_ref, *rest, reverse, chunks_per_batch, fused):
    if fused:
        hf_ref, y_ref, res_ref, wout_ref, g_ref, beta_ref, h_ref, a_sc, b_sc, h_sc, carry_sc = rest
    else:
        h_ref, a_sc, b_sc, h_sc, carry_sc = rest
    t_rows = xr_ref.shape[0]
    seg = t_rows // SUBLANES
    halo = xp_ref.shape[0]
    lane_tiles = xr_ref.shape[1] // LANES
    step = pl.program_id(0)
    chunk = (pl.num_programs(0) - 1 - step) if reverse else step
    cib = chunk % chunks_per_batch
    first_chunk = cib == 0
    last_chunk = cib == chunks_per_batch - 1
    row8 = lax.broadcasted_iota(I32, (SUBLANES, 1), 0)
    cat = lambda parts: jnp.concatenate(parts, axis=0)

    head_pos = cib * t_rows + lax.broadcasted_iota(I32, (LANES, 1), 0)
    head_real = head_pos >= FRONT_PAD
    xr = xr_ref[...].astype(F32)
    xr = cat([jnp.where(head_real, xr[:LANES], 0.0), xr[LANES:]])
    prev_pos = cib * t_rows - halo + lax.broadcasted_iota(I32, (halo, 1), 0)
    prev = jnp.where(prev_pos >= FRONT_PAD, xp_ref[...].astype(F32), 0.0)
    nxt = jnp.where(last_chunk, 0.0, xn_ref[...].astype(F32))

    xm1 = pltpu.roll(xr, 1, 0)
    xm2 = pltpu.roll(xr, 2, 0)
    xp1 = pltpu.roll(xr, t_rows - 1, 0)
    xm1 = cat([jnp.where(row8 == 0, prev[halo - 1:halo], xm1[:SUBLANES]), xm1[SUBLANES:]])
    xm2_head = jnp.where(row8 == 0, prev[halo - 2:halo - 1],
                         jnp.where(row8 == 1, prev[halo - 1:halo], xm2[:SUBLANES]))
    xm2 = cat([xm2_head, xm2[SUBLANES:]])
    xp1 = cat([xp1[:t_rows - SUBLANES], jnp.where(row8 == SUBLANES - 1, nxt[0:1], xp1[t_rows - SUBLANES:])])
    cw = cw_ref[...]
    xc = cw[0:1, :] * xm2 + cw[1:2, :] * xm1 + cw[2:3, :] * xr + cw[3:4, :] * xp1 + cb_ref[...]

    xcb = xc.astype(BF16)
    za, zx = [], []
    for blk in range(LRU_BLOCKS):
        xblk = xcb[:, blk * LRU_BLOCK:(blk + 1) * LRU_BLOCK]
        za.append(jnp.dot(xblk, wa_ref[blk], preferred_element_type=F32))
        zx.append(jnp.dot(xblk, wx_ref[blk], preferred_element_type=F32))
    tanh_a = jnp.tanh(jnp.concatenate(za, axis=-1) + ba_ref[...])
    tanh_x = jnp.tanh(jnp.concatenate(zx, axis=-1) + bx_ref[...])

    z = -lam_ref[...]
    softplus = jnp.maximum(z, 0.0) + jnp.log1p(jnp.exp(-jnp.abs(z)))
    half_rate = (-0.5 * LRU_C) * softplus
    log_a = tanh_a * half_rate + half_rate
    a = jnp.exp(log_a)
    gap = jnp.tanh(log_a) * (-1.0 - a * a)
    mult = jnp.where(gap > 0.0, gap * lax.rsqrt(gap), 0.0)
    if reverse:
        tail = jnp.where(jnp.logical_and(last_chunk, row8 == SUBLANES - 1), 1.0, mult[t_rows - SUBLANES:])
        mult = cat([mult[:t_rows - SUBLANES], tail])
    else:
        mult = cat([jnp.where(head_pos == FRONT_PAD, 1.0, mult[:LANES]), mult[LANES:]])
    b = mult * ((0.5 * tanh_x + 0.5) * xc)
    b = cat([jnp.where(head_real, b[:LANES], 0.0), b[LANES:]])
    for c in range(lane_tiles):
        a_sc[c] = a[:, c * LANES:(c + 1) * LANES]
        b_sc[c] = b[:, c * LANES:(c + 1) * LANES]

    @pl.when(last_chunk if reverse else first_chunk)
    def _():
        carry_sc[...] = jnp.zeros_like(carry_sc)

    order = range(seg - 1, -1, -1) if reverse else range(seg)
    rows_of = lambda j: pl.ds(j, SUBLANES, stride=seg)

    local = [jnp.zeros((SUBLANES, LANES), F32)] * lane_tiles
    prod = [jnp.ones((SUBLANES, LANES), F32)] * lane_tiles
    for j in order:
        for c in range(lane_tiles):
            av = a_sc[c, rows_of(j), :]
            local[c] = av * local[c] + b_sc[c, rows_of(j), :]
            prod[c] = av * prod[c]
            h_sc[c, rows_of(j), :] = local[c]
            a_sc[c, rows_of(j), :] = prod[c]

    carry_in = carry_sc[...]
    carry_out = []
    seg_carry = []
    for c in range(lane_tiles):
        state = carry_in[:, c * LANES:(c + 1) * LANES]
        rows = [None] * SUBLANES
        for s in (range(SUBLANES - 1, -1, -1) if reverse else range(SUBLANES)):
            rows[s] = state
            state = local[c][s:s + 1] + prod[c][s:s + 1] * state
        seg_carry.append(cat(rows))
        carry_out.append(state)
    carry_sc[...] = jnp.concatenate(carry_out, axis=-1)

    for j in order:
        for c in range(lane_tiles):
            h_sc[c, rows_of(j), :] = h_sc[c, rows_of(j), :] + a_sc[c, rows_of(j), :] * seg_carry[c]
    states = jnp.concatenate([h_sc[c] for c in range(lane_tiles)], axis=-1)
    if fused:
        gated = (hf_ref[...] + states) * _gelu_tanh(y_ref[...].astype(F32))
        mix = jnp.dot(gated.astype(BF16), wout_ref[...], preferred_element_type=F32)
        h_ref[...] = _layer_norm(DN_ALPHA * res_ref[...] + mix, g_ref[...], beta_ref[...])
    else:
        h_ref[...] = states


def _lru_scan(u, cw, cb, wa_half, ba_half, wx_half, bx_half, lam, *, reverse, lp, fuse_out=None):
    n = u.shape[0]
    d = D_MODEL
    t = SCAN_CHUNK
    halo = 2 * SUBLANES
    n_chunks = n // t
    cpb = lp // t
    t_h = t // halo
    n_h = n // halo

    def chunk_of(i):
        return (n_chunks - 1 - i) if reverse else i

    kern = functools.partial(_lru_scan_kernel, reverse=reverse, chunks_per_batch=cpb, fused=fuse_out is not None)
    chunk_rows = lambda col: pl.BlockSpec((t, d), lambda i: (chunk_of(i), col))
    extra_specs, extra_args = [], ()
    if fuse_out is not None:
        h_other, res, w_out, ln_g, ln_b = fuse_out
        extra_specs = [chunk_rows(0), chunk_rows(1), chunk_rows(0),
                       pl.BlockSpec((d, d), lambda i: (0, 0)),
                       pl.BlockSpec((1, d), lambda i: (0, 0)), pl.BlockSpec((1, d), lambda i: (0, 0))]
        extra_args = (h_other, u, res, w_out, ln_g, ln_b)
    full2 = lambda shape: pl.BlockSpec(shape, lambda i: (0, 0))
    full3 = lambda shape: pl.BlockSpec(shape, lambda i: (0, 0, 0))
    tile_major = pltpu.VMEM((d // LANES, t, LANES), F32)
    return pl.pallas_call(
        kern,
        out_shape=jax.ShapeDtypeStruct((n, d), F32),
        grid=(n_chunks,),
        in_specs=[
            pl.BlockSpec((t, d), lambda i: (chunk_of(i), 0)),
            pl.BlockSpec((halo, d), lambda i: (jnp.maximum(chunk_of(i) * t_h - 1, 0), 0)),
            pl.BlockSpec((halo, d), lambda i: (jnp.minimum((chunk_of(i) + 1) * t_h, n_h - 1), 0)),
            full2((CONV_W, d)), full2((1, d)),
            full3((LRU_BLOCKS, LRU_BLOCK, LRU_BLOCK)), full2((1, d)),
            full3((LRU_BLOCKS, LRU_BLOCK, LRU_BLOCK)), full2((1, d)),
            full2((1, d)),
        ] + extra_specs,
        out_specs=pl.BlockSpec((t, d), lambda i: (chunk_of(i), 0)),
        scratch_shapes=[tile_major, tile_major, tile_major, pltpu.VMEM((1, d), F32)],
        compiler_params=_params("arbitrary"),
        name=("lru_scan_bwd" if reverse else "lru_scan_fwd") + ("_out" if fuse_out is not None else ""),
    )(u, u, u, cw, cb, wa_half, ba_half, wx_half, bx_half, lam, *extra_args)


def _gelu_tanh(y):
    c = 0.7978845608028654
    return y * (0.5 * (1.0 + jnp.tanh(c * (y + 0.044715 * (y * y * y)))))


def _pack_bf16_pairs(x):
    half = x.shape[1] // 2
    lo = pltpu.bitcast(x[:, :half].astype(BF16).astype(F32), U32)
    hi = pltpu.bitcast(x[:, half:].astype(BF16).astype(F32), U32)
    return pltpu.bitcast((lo >> 16) | (hi & jnp.uint32(0xFFFF0000)), I32)


def _unpack_bf16_pairs(packed, dtype=BF16):
    u = pltpu.bitcast(packed, U32)
    lo = pltpu.bitcast(u << 16, F32).astype(dtype)
    hi = pltpu.bitcast(u & jnp.uint32(0xFFFF0000), F32).astype(dtype)
    return jnp.concatenate([lo, hi], axis=-1)


def _router_kernel(h_ref, wt_ref, b_ref, real_ref, idx_ref, gate_ref, rank_ref, cnt_ref, hp_ref, base_sc):
    tm = h_ref.shape[0]
    h = h_ref[...]
    hp_ref[...] = _pack_bf16_pairs(h)

    @pl.when(pl.program_id(0) == 0)
    def _():
        base_sc[...] = jnp.zeros_like(base_sc)

    wt = wt_ref[...]
    h_hi = h.astype(BF16)
    h_lo = (h - h_hi.astype(F32)).astype(BF16)
    w_hi = wt.astype(BF16)
    w_lo = (wt - w_hi.astype(F32)).astype(BF16)
    contract_last = (((1,), (1,)), ((), ()))
    mm = lambda a, b: lax.dot_general(a, b, contract_last, preferred_element_type=F32)
    logits = mm(w_hi, h_hi) + (mm(w_hi, h_lo) + mm(w_lo, h_hi)) + b_ref[...]

    expert = lax.broadcasted_iota(I32, (N_EXPERTS, tm), 0).astype(F32)
    vals = logits
    idx_rows, val_rows, onehots = [], [], []
    for _ in range(TOP_K):
        m = jnp.max(vals, axis=0, keepdims=True)
        idx = jnp.min(jnp.where(vals == m, expert, float(N_EXPERTS)), axis=0, keepdims=True)
        hit = expert == idx
        onehots.append(jnp.where(hit, real_ref[...], 0.0))
        idx_rows.append(idx)
        val_rows.append(m)
        vals = jnp.where(hit, -jnp.inf, vals)

    onehot_all = jnp.concatenate(onehots, axis=0)
    t_from = lax.broadcasted_iota(I32, (tm, tm), 0)
    t_to = lax.broadcasted_iota(I32, (tm, tm), 1)
    earlier = jnp.where(t_from < t_to, 1.0, 0.0).astype(BF16)
    before = jnp.dot(onehot_all.astype(BF16), earlier, preferred_element_type=F32)
    base = base_sc[...]
    starts = []
    for k in range(TOP_K):
        starts.append(base)
        base = base + jnp.sum(onehots[k], axis=1, keepdims=True)
    contrib = onehot_all * (jnp.concatenate(starts, axis=0) + before)
    rank_rows = [jnp.sum(contrib[k * N_EXPERTS:(k + 1) * N_EXPERTS], axis=0, keepdims=True)
                 for k in range(TOP_K)]

    top_vals = jnp.concatenate(val_rows, axis=0)
    e = jnp.exp(top_vals - top_vals[0:1])
    idx_ref[...] = jnp.concatenate(idx_rows, axis=0).astype(I32)
    gate_ref[...] = e / jnp.sum(e, axis=0, keepdims=True)
    rank_ref[...] = jnp.concatenate(rank_rows, axis=0).astype(I32)
    base_sc[...] = base
    cnt_ref[...] = base.astype(I32)


def _router(h, w, b, real, part):
    d = h.shape[1]
    n = h.shape[0] // MOE_PARTS
    tm = ROUTE_TILE
    first = part * (n // tm)
    out4 = lambda dt: jax.ShapeDtypeStruct((TOP_K, n), dt)
    spec4 = pl.BlockSpec((TOP_K, tm), lambda i: (0, i))
    return pl.pallas_call(
        _router_kernel,
        out_shape=(out4(I32), out4(F32), out4(I32), jax.ShapeDtypeStruct((N_EXPERTS, 1), I32),
                   jax.ShapeDtypeStruct((n, d // 2), I32)),
        grid=(n // tm,),
        in_specs=[pl.BlockSpec((tm, d), lambda i: (first + i, 0)),
                  pl.BlockSpec((N_EXPERTS, d), lambda i: (0, 0)),
                  pl.BlockSpec((N_EXPERTS, 1), lambda i: (0, 0)),
                  pl.BlockSpec((1, tm), lambda i: (0, first + i))],
        out_specs=(spec4, spec4, spec4, pl.BlockSpec((N_EXPERTS, 1), lambda i: (0, 0)),
                   pl.BlockSpec((tm, d // 2), lambda i: (i, 0))),
        scratch_shapes=[pltpu.VMEM((N_EXPERTS, 1), F32)],
        compiler_params=_params("arbitrary"),
        name="router",
    )(h, w.T, b[:, None], real)


def _sc_gather_rows(table, idx, name):
    rows, width = table.shape
    total = idx.shape[0]
    per_worker = total // SC_WORKERS
    max_chunk = min(SC_MAX_INDICES, SC_ROW_BUFFER_BYTES // (width * table.dtype.itemsize))
    chunk = max(c for c in range(SUBLANES, max_chunk + 1, SUBLANES) if per_worker % (2 * c) == 0)
    n_chunks = per_worker // chunk
    assert total == SC_WORKERS * n_chunks * chunk and n_chunks % 2 == 0
    mesh = plsc.VectorSubcoreMesh(core_axis_name="c", subcore_axis_name="s",
                                  num_cores=SC_CORES, num_subcores=SC_SUBCORES)

    def body(table_hbm, idx_hbm, out_hbm, idx_v, rows0, rows1, gsem0, gsem1, psem0, psem1):
        worker = lax.axis_index("s") * SC_CORES + lax.axis_index("c")
        base = worker * per_worker
        pltpu.sync_copy(idx_hbm.at[worker], idx_v)
        bufs = (rows0, rows1)
        gsems = (gsem0, gsem1)
        psems = (psem0, psem1)

        def gather(c, slot):
            return pltpu.make_async_copy(table_hbm.at[idx_v.at[c]], bufs[slot], gsems[slot])

        def put(c, slot):
            return pltpu.make_async_copy(bufs[slot], out_hbm.at[pl.ds(base + c * chunk, chunk)], psems[slot])

        gather(0, 0).start()

        @pl.loop(0, n_chunks, step=2)
        def _(c0):
            for slot in range(2):
                c = c0 + slot

                @pl.when(c + 1 < n_chunks)
                def _():
                    @pl.when(c >= 1)
                    def _():
                        put(c - 1, 1 - slot).wait()
                    gather(c + 1, 1 - slot).start()

                gather(c, slot).wait()
                put(c, slot).start()

        put(n_chunks - 2, 0).wait()
        put(n_chunks - 1, 1).wait()

    return pl.kernel(
        body,
        out_type=jax.ShapeDtypeStruct((total, width), table.dtype),
        mesh=mesh,
        scratch_types=[pltpu.VMEM((n_chunks, chunk), I32),
                       pltpu.VMEM((chunk, width), table.dtype), pltpu.VMEM((chunk, width), table.dtype),
                       pltpu.SemaphoreType.DMA, pltpu.SemaphoreType.DMA,
                       pltpu.SemaphoreType.DMA, pltpu.SemaphoreType.DMA],
        name=name,
    )(table, idx.reshape(SC_WORKERS, n_chunks, chunk))


def _sc_scatter_rows(src, dest, out_rows, name):
    n, width = src.shape
    fan = dest.shape[0]
    per_worker = n // SC_WORKERS
    chunk = max(c for c in range(SUBLANES, SC_MAX_INDICES + 1, SUBLANES) if per_worker % (2 * c) == 0)
    n_chunks = per_worker // chunk
    assert n == SC_WORKERS * n_chunks * chunk and n_chunks % 2 == 0
    mesh = plsc.VectorSubcoreMesh(core_axis_name="c", subcore_axis_name="s",
                                  num_cores=SC_CORES, num_subcores=SC_SUBCORES)

    def body(src_hbm, idx_hbm, out_hbm, idx_v, rows0, rows1, lsem0, lsem1, ssem0, ssem1):
        worker = lax.axis_index("s") * SC_CORES + lax.axis_index("c")
        base = worker * per_worker
        pltpu.sync_copy(idx_hbm.at[worker], idx_v)
        bufs = (rows0, rows1)
        lsems = (lsem0, lsem1)
        ssems = (ssem0, ssem1)

        def load(c, slot):
            return pltpu.make_async_copy(src_hbm.at[pl.ds(base + c * chunk, chunk)], bufs[slot], lsems[slot])

        def scatter(c, k, slot):
            return pltpu.make_async_copy(bufs[slot], out_hbm.at[idx_v.at[k * n_chunks + c]], ssems[slot])

        load(0, 0).start()

        @pl.loop(0, n_chunks, step=2)
        def _(c0):
            for slot in range(2):
                c = c0 + slot

                @pl.when(c + 1 < n_chunks)
                def _():
                    @pl.when(c >= 1)
                    def _():
                        for k in range(fan):
                            scatter(c - 1, k, 1 - slot).wait()
                    load(c + 1, 1 - slot).start()

                load(c, slot).wait()
                for k in range(fan):
                    scatter(c, k, slot).start()

        for k in range(fan):
            scatter(n_chunks - 2, k, 0).wait()
        for k in range(fan):
            scatter(n_chunks - 1, k, 1).wait()

    idx = dest.reshape(fan, SC_WORKERS, n_chunks, chunk).transpose(1, 0, 2, 3)
    return pl.kernel(
        body,
        out_type=jax.ShapeDtypeStruct((out_rows, width), src.dtype),
        mesh=mesh,
        scratch_types=[pltpu.VMEM((fan * n_chunks, chunk), I32),
                       pltpu.VMEM((chunk, width), src.dtype), pltpu.VMEM((chunk, width), src.dtype),
                       pltpu.SemaphoreType.DMA, pltpu.SemaphoreType.DMA,
                       pltpu.SemaphoreType.DMA, pltpu.SemaphoreType.DMA],
        name=name,
    )(src, idx.reshape(SC_WORKERS, fan * n_chunks, chunk))


def _expert_kernel(be_ref, valid_ref, xs_ref, wgu_ref, bgu_ref, wd_ref, bd_ref, y_ref, wgu_sc, wd_sc):
    i = pl.program_id(0)
    e = be_ref[i]
    e_prev = be_ref[jnp.maximum(i - 1, 0)]
    d = wd_sc.shape[0]

    @pl.when((i == 0) | (e != e_prev))
    def _():
        for r in range(0, d, LANES):
            wgu_sc[r:r + LANES, :] = wgu_ref[0, r:r + LANES, :].astype(BF16)
            wd_sc[r:r + LANES, :] = wd_ref[0, r:r + LANES, :].astype(BF16)

    @pl.when(valid_ref[i] > 0)
    def _():
        x = _unpack_bf16_pairs(xs_ref[...])
        h = jnp.dot(x, wgu_sc[...], preferred_element_type=F32) + bgu_ref[0]
        glu = jnp.minimum(h[:, :d], SWIGLU_LIMIT)
        lin = jnp.clip(h[:, d:], -SWIGLU_LIMIT, SWIGLU_LIMIT)
        act = glu * _sigmoid(SWIGLU_ALPHA * glu) * (lin + 1.0)
        y = jnp.dot(act.astype(BF16), wd_sc[...], preferred_element_type=F32) + bd_ref[0]
        y_ref[...] = _pack_bf16_pairs(y)

    @pl.when(valid_ref[i] == 0)
    def _():
        y_ref[...] = jnp.zeros_like(y_ref)


def _experts(block_e, valid_rows, xs, w_gu, b_gu, w_down, b_down, layer):
    cap = xs.shape[0]
    d = D_MODEL
    tm = EXPERT_TILE
    n_blocks = cap // tm
    grid_spec = pltpu.PrefetchScalarGridSpec(
        num_scalar_prefetch=2,
        grid=(n_blocks,),
        in_specs=[
            pl.BlockSpec((tm, d // 2), lambda i, be, nu: (i, 0)),
            pl.BlockSpec((None, 1, d, 2 * d), lambda i, be, nu: (layer, be[i], 0, 0)),
            pl.BlockSpec((1, 1, 2 * d), lambda i, be, nu: (be[i], 0, 0)),
            pl.BlockSpec((None, 1, d, d), lambda i, be, nu: (layer, be[i], 0, 0)),
            pl.BlockSpec((1, 1, d), lambda i, be, nu: (be[i], 0, 0)),
        ],
        out_specs=pl.BlockSpec((tm, d // 2), lambda i, be, nu: (i, 0)),
        scratch_shapes=[pltpu.VMEM((d, 2 * d), BF16), pltpu.VMEM((d, d), BF16)],
    )
    return pl.pallas_call(
        _expert_kernel,
        out_shape=jax.ShapeDtypeStruct((cap, d // 2), I32),
        grid_spec=grid_spec,
        compiler_params=_params("arbitrary"),
        name="moe_experts",
    )(block_e, valid_rows, xs, w_gu, b_gu[layer][:, None, :], w_down, b_down[layer][:, None, :])


def _combine_kernel(g_ref, b_ref, *refs, n_sub):
    o_ref = refs[-1]
    rows = o_ref.shape[0] // n_sub
    for s in range(n_sub):
        gate_ref, res_ref, yk_ref = refs[3 * s:3 * s + 3]
        gates = gate_ref[...]
        ffn = gates[:, 0:1] * _unpack_bf16_pairs(yk_ref[0], F32)
        for k in range(1, TOP_K):
            ffn = ffn + gates[:, k:k + 1] * _unpack_bf16_pairs(yk_ref[k], F32)
        o_ref[s * rows:(s + 1) * rows, :] = _layer_norm(DN_ALPHA * res_ref[...] + ffn, g_ref[...], b_ref[...])


def _combine(gates, res, yk, g, b, part, acc, drop_front=None):
    n, d = res.shape
    if drop_front is None:
        tt, n_sub = MOVE_TILE, 1
        steps = n // MOE_PARTS // tt
        grid = (steps,)
        local = lambda s, i: i
        src = lambda s, i: part * steps + i
        dst = lambda i: part * steps + i
        out_rows = n
    else:
        batch, lp = drop_front
        tt, n_sub = LANES, 2
        tiles = lp // tt
        part_batch = batch // MOE_PARTS
        pairs = (tiles - 1) // n_sub
        assert pairs * n_sub == tiles - 1
        grid = (part_batch, pairs)
        local = lambda s, b_, j: b_ * tiles + 1 + n_sub * j + s
        src = lambda s, b_, j: (part * part_batch + b_) * tiles + 1 + n_sub * j + s
        dst = lambda b_, j: (part * part_batch + b_) * pairs + j
        out_rows = n - batch * tt
    vec = pl.BlockSpec((1, d), lambda *_: (0, 0))
    in_specs = [vec, vec]
    args = (g, b)
    for s in range(n_sub):
        in_specs += [pl.BlockSpec((tt, TOP_K), lambda *i, s=s: (local(s, *i), 0)),
                     pl.BlockSpec((tt, d), lambda *i, s=s: (src(s, *i), 0)),
                     pl.BlockSpec((TOP_K, tt, d // 2), lambda *i, s=s: (0, local(s, *i), 0))]
        args += (gates, res, yk)
    aliases = {}
    if acc is not None:
        in_specs.append(pl.BlockSpec(memory_space=pl.ANY))
        args += (acc,)
        aliases = {len(args) - 1: 0}
    return pl.pallas_call(
        functools.partial(_combine_kernel, n_sub=n_sub),
        out_shape=jax.ShapeDtypeStruct((out_rows, d), F32),
        grid=grid,
        in_specs=in_specs,
        out_specs=pl.BlockSpec((n_sub * tt, d), lambda *i: (dst(*i), 0)),
        input_output_aliases=aliases,
        compiler_params=_params(*(["parallel"] * len(grid))),
        name="moe_combine",
    )(*args)


def _moe_layer(h, router_w, router_b, w_gu, b_gu, w_down, b_down, ln_g, ln_b, layer, lp, drop_front=None):
    n, d = h.shape
    tm = EXPERT_TILE
    n_part = n // MOE_PARTS
    assert n_part % lp == 0
    real_all = ((jnp.arange(n, dtype=I32) % lp) >= FRONT_PAD).astype(F32)[None, :]
    token = jnp.arange(n_part, dtype=I32)
    real = (token % lp) >= FRONT_PAD
    n_real = n_part - (n_part // lp) * FRONT_PAD
    n_blocks = -(-(n_real * TOP_K + N_EXPERTS * (tm - 1)) // tm) + 1
    block_start = jnp.arange(n_blocks, dtype=I32) * tm
    out = None
    for part in range(MOE_PARTS):
        idx, gates, rank, counts, h_packed = _router(h, router_w, router_b, real_all, part)
        counts = counts[:, 0]
        padded = (counts + tm - 1) // tm * tm
        pad_end = jnp.cumsum(padded)
        pad_start = pad_end - padded
        group_start = jnp.sum(jnp.where(idx[:, :, None] == jnp.arange(N_EXPERTS, dtype=I32), pad_start, 0), axis=-1)
        dest = jnp.where(real, group_start + rank, (n_blocks - 1) * tm + token % tm)
        block_e = jnp.minimum(jnp.sum((pad_end[None, :] <= block_start[:, None]).astype(I32), axis=1),
                              N_EXPERTS - 1)
        group_end = (pad_start + counts)[block_e]
        valid_rows = jnp.where(block_start < pad_end[-1], jnp.clip(group_end - block_start, 0, tm), 0).astype(I32)

        xs = _sc_scatter_rows(h_packed, dest, n_blocks * tm, "moe_dispatch")
        ys = _experts(block_e, valid_rows, xs, w_gu, b_gu, w_down, b_down, layer)
        yk = _sc_gather_rows(ys, dest.reshape(TOP_K * n_part), "moe_collect").reshape(TOP_K, n_part, d // 2)
        out = _combine(gates.T, h, yk, ln_g[None, :], ln_b[None, :], part, out, drop_front)
    return out


def _qkv_kernel(x_ref, w_ref, o_ref):
    xb = x_ref[...].astype(BF16)
    d = x_ref.shape[1]
    for part in range(3):
        acc = jnp.dot(xb, w_ref[:, part * d:(part + 1) * d], preferred_element_type=F32)
        if part == 0:
            acc = acc * (NA_HEAD_DIM ** -0.5)
        for p in range(HEAD_PAIRS):
            o_ref[part * HEAD_PAIRS + p] = acc[:, p * LANES:(p + 1) * LANES].astype(BF16)


def _qkv(x, w_bf16):
    n, d = x.shape
    return pl.pallas_call(
        _qkv_kernel,
        out_shape=jax.ShapeDtypeStruct((3 * HEAD_PAIRS, n, LANES), BF16),
        grid=(n // ROW_TILE,),
        in_specs=[pl.BlockSpec((ROW_TILE, d), lambda i: (i, 0)),
                  pl.BlockSpec((d, 3 * d), lambda i: (0, 0))],
        out_specs=pl.BlockSpec((3 * HEAD_PAIRS, ROW_TILE, LANES), lambda i: (0, i, 0)),
        compiler_params=_params("parallel"),
        name="na_qkv",
    )(x, w_bf16)


def _na_kernel(q_ref, k_ref, v_ref, bias_a_ref, bias_b_ref, mb_ref, o_ref, *, rows):
    s = pl.program_id(1)
    w = GRID_W
    low = lax.broadcasted_iota(I32, (w, LANES), 1) < NA_HEAD_DIM
    contract_last = (((1,), (1,)), ((), ()))
    meta0 = FRONT_PAD

    def stacked_q(p, q0):
        qp = q_ref[p, q0:q0 + w, :]
        zero = jnp.zeros_like(qp)
        return jnp.concatenate([jnp.where(low, qp, zero), jnp.where(low, zero, qp)], axis=0)

    def attend_all(q0, k0, bias_ref):
        scores = []
        for p in range(HEAD_PAIRS):
            q2 = stacked_q(p, q0)
            s_meta = lax.dot_general(q2, k_ref[p, meta0:meta0 + N_META, :], contract_last,
                                     preferred_element_type=F32) + mb_ref[p]
            s_win = None
            if k0 is not None:
                s_win = lax.dot_general(q2, k_ref[p, pl.ds(k0, NA_KH * w), :], contract_last,
                                        preferred_element_type=F32) + bias_ref[0, p]
            scores.append((s_meta, s_win))
        outs = []
        for p in range(HEAD_PAIRS):
            s_meta, s_win = scores[p]
            m = jnp.max(s_meta, axis=-1, keepdims=True)
            if s_win is not None:
                m = jnp.maximum(m, jnp.max(s_win, axis=-1, keepdims=True))
            p_meta = jnp.exp(s_meta - m)
            denom = jnp.sum(p_meta, axis=-1, keepdims=True)
            o = jnp.dot(p_meta.astype(BF16), v_ref[p, meta0:meta0 + N_META, :], preferred_element_type=F32)
            if s_win is not None:
                p_win = jnp.exp(s_win - m)
                denom = denom + jnp.sum(p_win, axis=-1, keepdims=True)
                o = o + jnp.dot(p_win.astype(BF16), v_ref[p, pl.ds(k0, NA_KH * w), :],
                                preferred_element_type=F32)
            o = o / denom
            outs.append(jnp.where(low, o[:w], o[w:]))
        return outs

    @pl.when(s == 0)
    def _():
        qrow = lax.broadcasted_iota(I32, (w, LANES), 0)
        for p, o in enumerate(attend_all(LANES - w, None, None)):
            o_ref[p, 0:LANES - w, :] = jnp.zeros((LANES - w, LANES), o_ref.dtype)
            o_ref[p, LANES - w:LANES, :] = jnp.where(qrow >= w - N_META, o, 0.0).astype(o_ref.dtype)

    @pl.when(s >= 1)
    def _():
        for half, bias_ref in enumerate((bias_a_ref, bias_b_ref)):
            r = 2 * (s - 1) + half
            rs = jnp.clip(r - NA_KH // 2, 0, rows - NA_KH)
            k0 = pl.multiple_of(LANES + rs * w, w)
            for p, o in enumerate(attend_all(half * w, k0, bias_ref)):
                o_ref[p, half * w:(half + 1) * w, :] = o.astype(o_ref.dtype)


def _na_bias_table(rpb, rows):
    del rows
    w = GRID_W
    q = jnp.arange(w)
    col_start = jnp.clip(q - NA_KW // 2, 0, w - NA_KW)
    c = jnp.arange(w)
    in_win = (c[None, :] >= col_start[:, None]) & (c[None, :] < col_start[:, None] + NA_KW)
    pad = w - NA_KW
    rp = jnp.pad(rpb.astype(F32), ((0, 0), (0, 0), (pad, pad)))
    toeplitz = jnp.stack([rp[:, :, w - 1 - qq:2 * w - 1 - qq] for qq in range(w)], axis=2)
    toeplitz = jnp.where(in_win[None, None], toeplitz, NEG_BIG)
    tabs = [jnp.concatenate([toeplitz[:, NA_KH - 1 - v + j] for j in range(NA_KH)], axis=-1) for v in range(NA_KH)]
    return jnp.stack(tabs, axis=0)


def _na_attention(qkv, bias_tab, meta_bias, *, batch, lp):
    n = qkv.shape[1]
    w = GRID_W
    rows = (lp - LANES) // w
    tiles = lp // LANES

    def variant(half):
        def index(b, s):
            r = jnp.clip(2 * (s - 1) + half, 0, rows - 1)
            return (r - jnp.clip(r - NA_KH // 2, 0, rows - NA_KH), 0, 0, 0)
        return index

    kern = functools.partial(_na_kernel, rows=rows)
    bias_block = (1, HEAD_PAIRS, 2 * w, NA_KH * w)
    bias_tab = bias_tab.reshape(NA_KH, HEAD_PAIRS, 2 * w, NA_KH * w)
    return pl.pallas_call(
        kern,
        out_shape=jax.ShapeDtypeStruct((HEAD_PAIRS, n, LANES), BF16),
        grid=(batch, tiles),
        in_specs=[
            pl.BlockSpec((HEAD_PAIRS, LANES, LANES), lambda b, s: (0, b * tiles + s, 0)),
            pl.BlockSpec((HEAD_PAIRS, lp, LANES), lambda b, s: (1, b, 0)),
            pl.BlockSpec((HEAD_PAIRS, lp, LANES), lambda b, s: (2, b, 0)),
            pl.BlockSpec(bias_block, variant(0)),
            pl.BlockSpec(bias_block, variant(1)),
            pl.BlockSpec((HEAD_PAIRS, 2 * w, N_META), lambda b, s: (0, 0, 0)),
        ],
        out_specs=pl.BlockSpec((HEAD_PAIRS, LANES, LANES), lambda b, s: (0, b * tiles + s, 0)),
        compiler_params=_params("parallel", "arbitrary"),
        name="na_attention",
    )(qkv, qkv, qkv, bias_tab, bias_tab,
      jnp.repeat(meta_bias, w, axis=0).reshape(HEAD_PAIRS, 2 * w, N_META))


def _na_out_kernel(o_ref, res_ref, w_ref, g_ref, b_ref, out_ref):
    att = jnp.concatenate([o_ref[p] for p in range(HEAD_PAIRS)], axis=-1)
    mix = jnp.dot(att, w_ref[...], preferred_element_type=F32)
    out_ref[...] = _layer_norm(DN_ALPHA * res_ref[...] + mix, g_ref[...], b_ref[...])


def _na_out(o, res, w_bf16, g, b):
    n, d = res.shape
    vec = pl.BlockSpec((1, d), lambda i: (0, 0))
    return pl.pallas_call(
        _na_out_kernel,
        out_shape=jax.ShapeDtypeStruct((n, d), F32),
        grid=(n // ROW_TILE,),
        in_specs=[pl.BlockSpec((HEAD_PAIRS, ROW_TILE, LANES), lambda i: (0, i, 0)),
                  pl.BlockSpec((ROW_TILE, d), lambda i: (i, 0)),
                  pl.BlockSpec((d, d), lambda i: (0, 0)), vec, vec],
        out_specs=pl.BlockSpec((ROW_TILE, d), lambda i: (i, 0)),
        compiler_params=_params("parallel"),
        name="na_out",
    )(o, res, w_bf16, g, b)


def kernel(x, meta_tokens, lru_w_in, lru_conv_w, lru_conv_b, lru_wa, lru_ba, lru_wx, lru_bx, lru_lambda, lru_w_out, na_w_qkv, na_rpb, na_meta_bias, na_w_out, ln_mix_g, ln_mix_b, router_w, router_b, moe_w_gu, moe_b_gu, moe_w_down, moe_b_down, ln_ffn_g, ln_ffn_b):
    batch, seq, d = x.shape
    lp = LANES + seq
    assert d == D_MODEL and seq % (2 * GRID_W) == 0 and seq // GRID_W >= NA_KH
    assert lp % SCAN_CHUNK == 0 and (batch * lp) % ROW_TILE == 0
    assert batch % MOE_PARTS == 0 and (batch // MOE_PARTS * lp) % ROUTE_TILE == 0
    n = batch * lp

    front = jnp.zeros((batch, FRONT_PAD, d), x.dtype)
    meta = jnp.broadcast_to(meta_tokens[None].astype(x.dtype), (batch, N_META, d))
    h = jnp.concatenate([front, meta, x], axis=1).reshape(n, d)

    u = _matmul(h, lru_w_in[0].astype(BF16))
    row = lambda v: v[None, :]
    def scan(direction, reverse, fuse_out=None):
        return _lru_scan(
            u, lru_conv_w[0], row(lru_conv_b[0]),
            (0.5 * lru_wa[0, direction]).astype(BF16), row(0.5 * lru_ba[0, direction]),
            (0.5 * lru_wx[0, direction]).astype(BF16), row(0.5 * lru_bx[0, direction]),
            row(lru_lambda[0, direction]), reverse=reverse, lp=lp, fuse_out=fuse_out)

    h_fwd = scan(0, False)
    h = scan(1, True, fuse_out=(h_fwd, h, lru_w_out[0].astype(BF16), row(ln_mix_g[0]), row(ln_mix_b[0])))
    h = _moe_layer(h, router_w[0], router_b[0], moe_w_gu, moe_b_gu, moe_w_down, moe_b_down,
                   ln_ffn_g[0], ln_ffn_b[0], 0, lp)

    qkv = _qkv(h, na_w_qkv[0].astype(BF16))
    att = _na_attention(qkv, _na_bias_table(na_rpb[0], seq // GRID_W), na_meta_bias[0].astype(F32),
                        batch=batch, lp=lp)
    h = _na_out(att, h, na_w_out[0].astype(BF16), row(ln_mix_g[1]), row(ln_mix_b[1]))
    h = _moe_layer(h, router_w[1], router_b[1], moe_w_gu, moe_b_gu, moe_w_down, moe_b_down,
                   ln_ffn_g[1], ln_ffn_b[1], 1, lp, drop_front=(batch, lp))

    return h.reshape(batch, seq, d)
```

```python
import functools

import jax
import jax.numpy as jnp
from jax import lax
from jax.experimental import pallas as pl
from jax.experimental.pallas import tpu as pltpu
from jax.experimental.pallas import tpu_sc as plsc

F32 = jnp.float32
BF16 = jnp.bfloat16
I32 = jnp.int32
U32 = jnp.uint32

D_MODEL = 1024
N_META = 16
GRID_W = 64
LRU_BLOCKS = 4
LRU_BLOCK = D_MODEL // LRU_BLOCKS
CONV_W = 4
LRU_C = 8.0
NA_HEADS = 16
NA_HEAD_DIM = D_MODEL // NA_HEADS
NA_KH = 8
NA_KW = 16
N_EXPERTS = 32
TOP_K = 4
SWIGLU_LIMIT = 7.0
SWIGLU_ALPHA = 1.702
DEPTH = 2
DN_ALPHA = (2.0 * DEPTH) ** 0.25
LN_EPS = 1e-5

LANES = 128
SUBLANES = 8
FRONT_PAD = LANES - N_META
HEAD_PAIRS = D_MODEL // LANES
NEG_BIG = -1e30

ROW_TILE = 512
SCAN_CHUNK = 352
ROUTE_TILE = 512
MOVE_TILE = 256
EXPERT_TILE = 512
VMEM_LIMIT = 56 << 20

SC_CORES = 2
SC_SUBCORES = 16
SC_WORKERS = SC_CORES * SC_SUBCORES
SC_MAX_INDICES = 64
SC_ROW_BUFFER_BYTES = 128 << 10


def _params(*sem):
    return pltpu.CompilerParams(dimension_semantics=sem, vmem_limit_bytes=VMEM_LIMIT)


def _sigmoid(x):
    return 0.5 * jnp.tanh(0.5 * x) + 0.5


def _layer_norm(x, g, b):
    mu = jnp.mean(x, axis=-1, keepdims=True)
    xc = x - mu
    var = jnp.mean(xc * xc, axis=-1, keepdims=True)
    return xc * lax.rsqrt(var + LN_EPS) * g + b


def _matmul_kernel(x_ref, w_ref, o_ref):
    o_ref[...] = jnp.dot(x_ref[...].astype(BF16), w_ref[...], preferred_element_type=F32).astype(o_ref.dtype)


def _matmul(x, w_bf16):
    n, k = x.shape
    m = w_bf16.shape[1]
    return pl.pallas_call(
        _matmul_kernel,
        out_shape=jax.ShapeDtypeStruct((n, m), BF16),
        grid=(n // ROW_TILE,),
        in_specs=[pl.BlockSpec((ROW_TILE, k), lambda i: (i, 0)),
                  pl.BlockSpec((k, m), lambda i: (0, 0))],
        out_specs=pl.BlockSpec((ROW_TILE, m), lambda i: (i, 0)),
        compiler_params=_params("parallel"),
        name="in_proj",
    )(x, w_bf16)


def _lru_scan_kernel(xr_ref, xp_ref, xn_ref, cw_ref, cb_ref, wa_ref, ba_ref, wx_ref, bx_ref,
                     lam_ref, *rest, reverse, chunks_per_batch, fused):
    if fused:
        hf_ref, y_ref, res_ref, wout_ref, g_ref, beta_ref, h_ref, a_sc, b_sc, h_sc, carry_sc = rest
    else:
        h_ref, a_sc, b_sc, h_sc, carry_sc = rest
    t_rows = xr_ref.shape[0]
    seg = t_rows // SUBLANES
    halo = xp_ref.shape[0]
    lane_tiles = xr_ref.shape[1] // LANES
    step = pl.program_id(0)
    chunk = (pl.num_programs(0) - 1 - step) if reverse else step
    cib = chunk % chunks_per_batch
    first_chunk = cib == 0
    last_chunk = cib == chunks_per_batch - 1
    row8 = lax.broadcasted_iota(I32, (SUBLANES, 1), 0)
    cat = lambda parts: jnp.concatenate(parts, axis=0)

    head_pos = cib * t_rows + lax.broadcasted_iota(I32, (LANES, 1), 0)
    head_real = head_pos >= FRONT_PAD
    xr = xr_ref[...].astype(F32)
    xr = cat([jnp.where(head_real, xr[:LANES], 0.0), xr[LANES:]])
    prev_pos = cib * t_rows - halo + lax.broadcasted_iota(I32, (halo, 1), 0)
    prev = jnp.where(prev_pos >= FRONT_PAD, xp_ref[...].astype(F32), 0.0)
    nxt = jnp.where(last_chunk, 0.0, xn_ref[...].astype(F32))

    xm1 = pltpu.roll(xr, 1, 0)
    xm2 = pltpu.roll(xr, 2, 0)
    xp1 = pltpu.roll(xr, t_rows - 1, 0)
    xm1 = cat([jnp.where(row8 == 0, prev[halo - 1:halo], xm1[:SUBLANES]), xm1[SUBLANES:]])
    xm2_head = jnp.where(row8 == 0, prev[halo - 2:halo - 1],
                         jnp.where(row8 == 1, prev[halo - 1:halo], xm2[:SUBLANES]))
    xm2 = cat([xm2_head, xm2[SUBLANES:]])
    xp1 = cat([xp1[:t_rows - SUBLANES], jnp.where(row8 == SUBLANES - 1, nxt[0:1], xp1[t_rows - SUBLANES:])])
    cw = cw_ref[...]
    xc = cw[0:1, :] * xm2 + cw[1:2, :] * xm1 + cw[2:3, :] * xr + cw[3:4, :] * xp1 + cb_ref[...]

    xcb = xc.astype(BF16)
    za, zx = [], []
    for blk in range(LRU_BLOCKS):
        xblk = xcb[:, blk * LRU_BLOCK:(blk + 1) * LRU_BLOCK]
        za.append(jnp.dot(xblk, wa_ref[blk], preferred_element_type=F32))
        zx.append(jnp.dot(xblk, wx_ref[blk], preferred_element_type=F32))
    tanh_a = jnp.tanh(jnp.concatenate(za, axis=-1) + ba_ref[...])
    tanh_x = jnp.tanh(jnp.concatenate(zx, axis=-1) + bx_ref[...])

    z = -lam_ref[...]
    softplus = jnp.maximum(z, 0.0) + jnp.log1p(jnp.exp(-jnp.abs(z)))
    half_rate = (-0.5 * LRU_C) * softplus
    log_a = tanh_a * half_rate + half_rate
    a = jnp.exp(log_a)
    gap = jnp.tanh(log_a) * (-1.0 - a * a)
    mult = jnp.where(gap > 0.0, gap * lax.rsqrt(gap), 0.0)
    if reverse:
        tail = jnp.where(jnp.logical_and(last_chunk, row8 == SUBLANES - 1), 1.0, mult[t_rows - SUBLANES:])
        mult = cat([mult[:t_rows - SUBLANES], tail])
    else:
        mult = cat([jnp.where(head_pos == FRONT_PAD, 1.0, mult[:LANES]), mult[LANES:]])
    b = mult * ((0.5 * tanh_x + 0.5) * xc)
    b = cat([jnp.where(head_real, b[:LANES], 0.0), b[LANES:]])
    for c in range(lane_tiles):
        a_sc[c] = a[:, c * LANES:(c + 1) * LANES]
        b_sc[c] = b[:, c * LANES:(c + 1) * LANES]

    @pl.when(last_chunk if reverse else first_chunk)
    def _():
        carry_sc[...] = jnp.zeros_like(carry_sc)

    order = range(seg - 1, -1, -1) if reverse else range(seg)
    rows_of = lambda j: pl.ds(j, SUBLANES, stride=seg)

    local = [jnp.zeros((SUBLANES, LANES), F32)] * lane_tiles
    prod = [jnp.ones((SUBLANES, LANES), F32)] * lane_tiles
    for j in order:
        for c in range(lane_tiles):
            av = a_sc[c, rows_of(j), :]
            local[c] = av * local[c] + b_sc[c, rows_of(j), :]
            prod[c] = av * prod[c]
            h_sc[c, rows_of(j), :] = local[c]
            a_sc[c, rows_of(j), :] = prod[c]

    carry_in = carry_sc[...]
    carry_out = []
    seg_carry = []
    for c in range(lane_tiles):
        state = carry_in[:, c * LANES:(c + 1) * LANES]
        rows = [None] * SUBLANES
        for s in (range(SUBLANES - 1, -1, -1) if reverse else range(SUBLANES)):
            rows[s] = state
            state = local[c][s:s + 1] + prod[c][s:s + 1] * state
        seg_carry.append(cat(rows))
        carry_out.append(state)
    carry_sc[...] = jnp.concatenate(carry_out, axis=-1)

    for j in order:
        for c in range(lane_tiles):
            h_sc[c, rows_of(j), :] = h_sc[c, rows_of(j), :] + a_sc[c, rows_of(j), :] * seg_carry[c]
    states = jnp.concatenate([h_sc[c] for c in range(lane_tiles)], axis=-1)
    if fused:
        gated = (hf_ref[...] + states) * _gelu_tanh(y_ref[...].astype(F32))
        mix = jnp.dot(gated.astype(BF16), wout_ref[...], preferred_element_type=F32)
        h_ref[...] = _layer_norm(DN_ALPHA * res_ref[...] + mix, g_ref[...], beta_ref[...])
    else:
        h_ref[...] = states


def _lru_scan(u, cw, cb, wa_half, ba_half, wx_half, bx_half, lam, *, reverse, lp, fuse_out=None):
    n = u.shape[0]
    d = D_MODEL
    t = SCAN_CHUNK
    halo = 2 * SUBLANES
    n_chunks = n // t
    cpb = lp // t
    t_h = t // halo
    n_h = n // halo

    def chunk_of(i):
        return (n_chunks - 1 - i) if reverse else i

    kern = functools.partial(_lru_scan_kernel, reverse=reverse, chunks_per_batch=cpb, fused=fuse_out is not None)
    chunk_rows = lambda col: pl.BlockSpec((t, d), lambda i: (chunk_of(i), col))
    extra_specs, extra_args = [], ()
    if fuse_out is not None:
        h_other, res, w_out, ln_g, ln_b = fuse_out
        extra_specs = [chunk_rows(0), chunk_rows(1), chunk_rows(0),
                       pl.BlockSpec((d, d), lambda i: (0, 0)),
                       pl.BlockSpec((1, d), lambda i: (0, 0)), pl.BlockSpec((1, d), lambda i: (0, 0))]
        extra_args = (h_other, u, res, w_out, ln_g, ln_b)
    full2 = lambda shape: pl.BlockSpec(shape, lambda i: (0, 0))
    full3 = lambda shape: pl.BlockSpec(shape, lambda i: (0, 0, 0))
    tile_major = pltpu.VMEM((d // LANES, t, LANES), F32)
    return pl.pallas_call(
        kern,
        out_shape=jax.ShapeDtypeStruct((n, d), F32),
        grid=(n_chunks,),
        in_specs=[
            pl.BlockSpec((t, d), lambda i: (chunk_of(i), 0)),
            pl.BlockSpec((halo, d), lambda i: (jnp.maximum(chunk_of(i) * t_h - 1, 0), 0)),
            pl.BlockSpec((halo, d), lambda i: (jnp.minimum((chunk_of(i) + 1) * t_h, n_h - 1), 0)),
            full2((CONV_W, d)), full2((1, d)),
            full3((LRU_BLOCKS, LRU_BLOCK, LRU_BLOCK)), full2((1, d)),
            full3((LRU_BLOCKS, LRU_BLOCK, LRU_BLOCK)), full2((1, d)),
            full2((1, d)),
        ] + extra_specs,
        out_specs=pl.BlockSpec((t, d), lambda i: (chunk_of(i), 0)),
        scratch_shapes=[tile_major, tile_major, tile_major, pltpu.VMEM((1, d), F32)],
        compiler_params=_params("arbitrary"),
        name=("lru_scan_bwd" if reverse else "lru_scan_fwd") + ("_out" if fuse_out is not None else ""),
    )(u, u, u, cw, cb, wa_half, ba_half, wx_half, bx_half, lam, *extra_args)


def _gelu_tanh(y):
    c = 0.7978845608028654
    return y * (0.5 * (1.0 + jnp.tanh(c * (y + 0.044715 * (y * y * y)))))


def _pack_bf16_pairs(x):
    half = x.shape[1] // 2
    lo = pltpu.bitcast(x[:, :half].astype(BF16).astype(F32), U32)
    hi = pltpu.bitcast(x[:, half:].astype(BF16).astype(F32), U32)
    return pltpu.bitcast((lo >> 16) | (hi & jnp.uint32(0xFFFF0000)), I32)


def _unpack_bf16_pairs(packed, dtype=BF16):
    u = pltpu.bitcast(packed, U32)
    lo = pltpu.bitcast(u << 16, F32).astype(dtype)
    hi = pltpu.bitcast(u & jnp.uint32(0xFFFF0000), F32).astype(dtype)
    return jnp.concatenate([lo, hi], axis=-1)


def _router_kernel(h_ref, wt_ref, b_ref, real_ref, idx_ref, gate_ref, rank_ref, cnt_ref, hp_ref, base_sc):
    tm = h_ref.shape[0]
    h = h_ref[...]
    hp_ref[...] = _pack_bf16_pairs(h)

    @pl.when(pl.program_id(0) == 0)
    def _():
        base_sc[...] = jnp.zeros_like(base_sc)

    wt = wt_ref[...]
    h_hi = h.astype(BF16)
    h_lo = (h - h_hi.astype(F32)).astype(BF16)
    w_hi = wt.astype(BF16)
    w_lo = (wt - w_hi.astype(F32)).astype(BF16)
    contract_last = (((1,), (1,)), ((), ()))
    mm = lambda a, b: lax.dot_general(a, b, contract_last, preferred_element_type=F32)
    logits = mm(w_hi, h_hi) + (mm(w_hi, h_lo) + mm(w_lo, h_hi)) + b_ref[...]

    expert = lax.broadcasted_iota(I32, (N_EXPERTS, tm), 0).astype(F32)
    vals = logits
    idx_rows, val_rows, onehots = [], [], []
    for _ in range(TOP_K):
        m = jnp.max(vals, axis=0, keepdims=True)
        idx = jnp.min(jnp.where(vals == m, expert, float(N_EXPERTS)), axis=0, keepdims=True)
        hit = expert == idx
        onehots.append(jnp.where(hit, real_ref[...], 0.0))
        idx_rows.append(idx)
        val_rows.append(m)
        vals = jnp.where(hit, -jnp.inf, vals)

    onehot_all = jnp.concatenate(onehots, axis=0)
    t_from = lax.broadcasted_iota(I32, (tm, tm), 0)
    t_to = lax.broadcasted_iota(I32, (tm, tm), 1)
    earlier = jnp.where(t_from < t_to, 1.0, 0.0).astype(BF16)
    before = jnp.dot(onehot_all.astype(BF16), earlier, preferred_element_type=F32)
    base = base_sc[...]
    starts = []
    for k in range(TOP_K):
        starts.append(base)
        base = base + jnp.sum(onehots[k], axis=1, keepdims=True)
    contrib = onehot_all * (jnp.concatenate(starts, axis=0) + before)
    rank_rows = [jnp.sum(contrib[k * N_EXPERTS:(k + 1) * N_EXPERTS], axis=0, keepdims=True)
                 for k in range(TOP_K)]

    top_vals = jnp.concatenate(val_rows, axis=0)
    e = jnp.exp(top_vals - top_vals[0:1])
    idx_ref[...] = jnp.concatenate(idx_rows, axis=0).astype(I32)
    gate_ref[...] = e / jnp.sum(e, axis=0, keepdims=True)
    rank_ref[...] = jnp.concatenate(rank_rows, axis=0).astype(I32)
    base_sc[...] = base
    cnt_ref[...] = base.astype(I32)


def _router(h, w, b, real):
    n, d = h.shape
    tm = ROUTE_TILE
    out4 = lambda dt: jax.ShapeDtypeStruct((TOP_K, n), dt)
    spec4 = pl.BlockSpec((TOP_K, tm), lambda i: (0, i))
    return pl.pallas_call(
        _router_kernel,
        out_shape=(out4(I32), out4(F32), out4(I32), jax.ShapeDtypeStruct((N_EXPERTS, 1), I32),
                   jax.ShapeDtypeStruct((n, d // 2), I32)),
        grid=(n // tm,),
        in_specs=[pl.BlockSpec((tm, d), lambda i: (i, 0)),
                  pl.BlockSpec((N_EXPERTS, d), lambda i: (0, 0)),
                  pl.BlockSpec((N_EXPERTS, 1), lambda i: (0, 0)),
                  pl.BlockSpec((1, tm), lambda i: (0, i))],
        out_specs=(spec4, spec4, spec4, pl.BlockSpec((N_EXPERTS, 1), lambda i: (0, 0)),
                   pl.BlockSpec((tm, d // 2), lambda i: (i, 0))),
        scratch_shapes=[pltpu.VMEM((N_EXPERTS, 1), F32)],
        compiler_params=_params("arbitrary"),
        name="router",
    )(h, w.T, b[:, None], real)


def _sc_gather_rows(table, idx, name):
    rows, width = table.shape
    total = idx.shape[0]
    per_worker = total // SC_WORKERS
    max_chunk = min(SC_MAX_INDICES, SC_ROW_BUFFER_BYTES // (width * table.dtype.itemsize))
    chunk = max(c for c in range(SUBLANES, max_chunk + 1, SUBLANES) if per_worker % (2 * c) == 0)
    n_chunks = per_worker // chunk
    assert total == SC_WORKERS * n_chunks * chunk and n_chunks % 2 == 0
    mesh = plsc.VectorSubcoreMesh(core_axis_name="c", subcore_axis_name="s",
                                  num_cores=SC_CORES, num_subcores=SC_SUBCORES)

    def body(table_hbm, idx_hbm, out_hbm, idx_v, rows0, rows1, gsem0, gsem1, psem0, psem1):
        worker = lax.axis_index("s") * SC_CORES + lax.axis_index("c")
        base = worker * per_worker
        pltpu.sync_copy(idx_hbm.at[worker], idx_v)
        bufs = (rows0, rows1)
        gsems = (gsem0, gsem1)
        psems = (psem0, psem1)

        def gather(c, slot):
            return pltpu.make_async_copy(table_hbm.at[idx_v.at[c]], bufs[slot], gsems[slot])

        def put(c, slot):
            return pltpu.make_async_copy(bufs[slot], out_hbm.at[pl.ds(base + c * chunk, chunk)], psems[slot])

        gather(0, 0).start()

        @pl.loop(0, n_chunks, step=2)
        def _(c0):
            for slot in range(2):
                c = c0 + slot

                @pl.when(c + 1 < n_chunks)
                def _():
                    @pl.when(c >= 1)
                    def _():
                        put(c - 1, 1 - slot).wait()
                    gather(c + 1, 1 - slot).start()

                gather(c, slot).wait()
                put(c, slot).start()

        put(n_chunks - 2, 0).wait()
        put(n_chunks - 1, 1).wait()

    return pl.kernel(
        body,
        out_type=jax.ShapeDtypeStruct((total, width), table.dtype),
        mesh=mesh,
        scratch_types=[pltpu.VMEM((n_chunks, chunk), I32),
                       pltpu.VMEM((chunk, width), table.dtype), pltpu.VMEM((chunk, width), table.dtype),
                       pltpu.SemaphoreType.DMA, pltpu.SemaphoreType.DMA,
                       pltpu.SemaphoreType.DMA, pltpu.SemaphoreType.DMA],
        name=name,
    )(table, idx.reshape(SC_WORKERS, n_chunks, chunk))


def _sc_scatter_rows(src, dest, out_rows, name):
    n, width = src.shape
    fan = dest.shape[0]
    per_worker = n // SC_WORKERS
    chunk = max(c for c in range(SUBLANES, SC_MAX_INDICES + 1, SUBLANES) if per_worker % (2 * c) == 0)
    n_chunks = per_worker // chunk
    assert n == SC_WORKERS * n_chunks * chunk and n_chunks % 2 == 0
    mesh = plsc.VectorSubcoreMesh(core_axis_name="c", subcore_axis_name="s",
                                  num_cores=SC_CORES, num_subcores=SC_SUBCORES)

    def body(src_hbm, idx_hbm, out_hbm, idx_v, rows0, rows1, lsem0, lsem1, ssem0, ssem1):
        worker = lax.axis_index("s") * SC_CORES + lax.axis_index("c")
        base = worker * per_worker
        pltpu.sync_copy(idx_hbm.at[worker], idx_v)
        bufs = (rows0, rows1)
        lsems = (lsem0, lsem1)
        ssems = (ssem0, ssem1)

        def load(c, slot):
            return pltpu.make_async_copy(src_hbm.at[pl.ds(base + c * chunk, chunk)], bufs[slot], lsems[slot])

        def scatter(c, k, slot):
            return pltpu.make_async_copy(bufs[slot], out_hbm.at[idx_v.at[k * n_chunks + c]], ssems[slot])

        load(0, 0).start()

        @pl.loop(0, n_chunks, step=2)
        def _(c0):
            for slot in range(2):
                c = c0 + slot

                @pl.when(c + 1 < n_chunks)
                def _():
                    @pl.when(c >= 1)
                    def _():
                        for k in range(fan):
                            scatter(c - 1, k, 1 - slot).wait()
                    load(c + 1, 1 - slot).start()

                load(c, slot).wait()
                for k in range(fan):
                    scatter(c, k, slot).start()

        for k in range(fan):
            scatter(n_chunks - 2, k, 0).wait()
        for k in range(fan):
            scatter(n_chunks - 1, k, 1).wait()

    idx = dest.reshape(fan, SC_WORKERS, n_chunks, chunk).transpose(1, 0, 2, 3)
    return pl.kernel(
        body,
        out_type=jax.ShapeDtypeStruct((out_rows, width), src.dtype),
        mesh=mesh,
        scratch_types=[pltpu.VMEM((fan * n_chunks, chunk), I32),
                       pltpu.VMEM((chunk, width), src.dtype), pltpu.VMEM((chunk, width), src.dtype),
                       pltpu.SemaphoreType.DMA, pltpu.SemaphoreType.DMA,
                       pltpu.SemaphoreType.DMA, pltpu.SemaphoreType.DMA],
        name=name,
    )(src, idx.reshape(SC_WORKERS, fan * n_chunks, chunk))


def _expert_kernel(be_ref, valid_ref, xs_ref, wgu_ref, bgu_ref, wd_ref, bd_ref, y_ref, wgu_sc, wd_sc):
    i = pl.program_id(0)
    e = be_ref[i]
    e_prev = be_ref[jnp.maximum(i - 1, 0)]
    d = wd_sc.shape[0]

    @pl.when((i == 0) | (e != e_prev))
    def _():
        for r in range(0, d, LANES):
            wgu_sc[r:r + LANES, :] = wgu_ref[0, r:r + LANES, :].astype(BF16)
            wd_sc[r:r + LANES, :] = wd_ref[0, r:r + LANES, :].astype(BF16)

    valid = valid_ref[i]

    @pl.when(valid > 0)
    def _():
        row = lax.broadcasted_iota(I32, xs_ref.shape, 0)
        x = _unpack_bf16_pairs(jnp.where(row < valid, xs_ref[...], 0))
        h = jnp.dot(x, wgu_sc[...], preferred_element_type=F32) + bgu_ref[0]
        glu = jnp.minimum(h[:, :d], SWIGLU_LIMIT)
        lin = jnp.clip(h[:, d:], -SWIGLU_LIMIT, SWIGLU_LIMIT)
        act = glu * _sigmoid(SWIGLU_ALPHA * glu) * (lin + 1.0)
        y = jnp.dot(act.astype(BF16), wd_sc[...], preferred_element_type=F32) + bd_ref[0]
        y_ref[...] = _pack_bf16_pairs(y)

    @pl.when(valid == 0)
    def _():
        y_ref[...] = jnp.zeros_like(y_ref)


def _experts(block_e, valid_rows, xs, w_gu, b_gu, w_down, b_down, layer):
    cap = xs.shape[0]
    d = D_MODEL
    tm = EXPERT_TILE
    n_blocks = cap // tm
    grid_spec = pltpu.PrefetchScalarGridSpec(
        num_scalar_prefetch=2,
        grid=(n_blocks,),
        in_specs=[
            pl.BlockSpec((tm, d // 2), lambda i, be, nu: (i, 0)),
            pl.BlockSpec((None, 1, d, 2 * d), lambda i, be, nu: (layer, be[i], 0, 0)),
            pl.BlockSpec((1, 1, 2 * d), lambda i, be, nu: (be[i], 0, 0)),
            pl.BlockSpec((None, 1, d, d), lambda i, be, nu: (layer, be[i], 0, 0)),
            pl.BlockSpec((1, 1, d), lambda i, be, nu: (be[i], 0, 0)),
        ],
        out_specs=pl.BlockSpec((tm, d // 2), lambda i, be, nu: (i, 0)),
        scratch_shapes=[pltpu.VMEM((d, 2 * d), BF16), pltpu.VMEM((d, d), BF16)],
    )
    return pl.pallas_call(
        _expert_kernel,
        out_shape=jax.ShapeDtypeStruct((cap, d // 2), I32),
        grid_spec=grid_spec,
        compiler_params=_params("arbitrary"),
        name="moe_experts",
    )(block_e, valid_rows, xs, w_gu, b_gu[layer][:, None, :], w_down, b_down[layer][:, None, :])


def _combine_kernel(g_ref, b_ref, *refs, n_sub):
    o_ref = refs[-1]
    rows = o_ref.shape[0] // n_sub
    for s in range(n_sub):
        gate_ref, res_ref, yk_ref = refs[3 * s:3 * s + 3]
        gates = gate_ref[...]
        ffn = gates[:, 0:1] * _unpack_bf16_pairs(yk_ref[0], F32)
        for k in range(1, TOP_K):
            ffn = ffn + gates[:, k:k + 1] * _unpack_bf16_pairs(yk_ref[k], F32)
        o_ref[s * rows:(s + 1) * rows, :] = _layer_norm(DN_ALPHA * res_ref[...] + ffn, g_ref[...], b_ref[...])


def _combine(gates, res, yk, g, b, drop_front=None):
    n, d = res.shape
    if drop_front is None:
        tt, n_sub = MOVE_TILE, 1
        grid = (n // tt,)
        src = lambda s, i: i
        dst = lambda i: i
        out_rows = n
    else:
        batch, lp = drop_front
        tt, n_sub = LANES, 2
        tiles = lp // tt
        pairs = (tiles - 1) // n_sub
        assert pairs * n_sub == tiles - 1
        grid = (batch, pairs)
        src = lambda s, b_, j: b_ * tiles + 1 + n_sub * j + s
        dst = lambda b_, j: b_ * pairs + j
        out_rows = n - batch * tt
    vec = pl.BlockSpec((1, d), lambda *_: (0, 0))
    in_specs = [vec, vec]
    args = (g, b)
    for s in range(n_sub):
        in_specs += [pl.BlockSpec((tt, TOP_K), lambda *i, s=s: (src(s, *i), 0)),
                     pl.BlockSpec((tt, d), lambda *i, s=s: (src(s, *i), 0)),
                     pl.BlockSpec((TOP_K, tt, d // 2), lambda *i, s=s: (0, src(s, *i), 0))]
        args += (gates, res, yk)
    return pl.pallas_call(
        functools.partial(_combine_kernel, n_sub=n_sub),
        out_shape=jax.ShapeDtypeStruct((out_rows, d), F32),
        grid=grid,
        in_specs=in_specs,
        out_specs=pl.BlockSpec((n_sub * tt, d), lambda *i: (dst(*i), 0)),
        compiler_params=_params(*(["parallel"] * len(grid))),
        name="moe_combine",
    )(*args)


def _moe_layer(h, router_w, router_b, w_gu, b_gu, w_down, b_down, ln_g, ln_b, layer, lp, drop_front=None):
    n, d = h.shape
    tm = EXPERT_TILE
    token = jnp.arange(n, dtype=I32)
    real = (token % lp) >= FRONT_PAD
    idx, gates, rank, counts, h_packed = _router(h, router_w, router_b, real.astype(F32)[None, :])
    counts = counts[:, 0]
    padded = (counts + tm - 1) // tm * tm
    pad_end = jnp.cumsum(padded)
    pad_start = pad_end - padded
    n_real = n - (n // lp) * FRONT_PAD
    n_blocks = -(-(n_real * TOP_K + N_EXPERTS * (tm - 1)) // tm) + 1
    group_start = jnp.sum(jnp.where(idx[:, :, None] == jnp.arange(N_EXPERTS, dtype=I32), pad_start, 0), axis=-1)
    dest = jnp.where(real, group_start + rank, (n_blocks - 1) * tm + token % tm)
    block_start = jnp.arange(n_blocks, dtype=I32) * tm
    block_e = jnp.minimum(jnp.sum((pad_end[None, :] <= block_start[:, None]).astype(I32), axis=1),
                          N_EXPERTS - 1)
    group_end = (pad_start + counts)[block_e]
    valid_rows = jnp.where(block_start < pad_end[-1], jnp.clip(group_end - block_start, 0, tm), 0).astype(I32)

    xs = _sc_scatter_rows(h_packed, dest, n_blocks * tm, "moe_dispatch")
    ys = _experts(block_e, valid_rows, xs, w_gu, b_gu, w_down, b_down, layer)
    yk = _sc_gather_rows(ys, dest.reshape(TOP_K * n), "moe_collect").reshape(TOP_K, n, d // 2)
    return _combine(gates.T, h, yk, ln_g[None, :], ln_b[None, :], drop_front)


def _qkv_kernel(x_ref, w_ref, o_ref):
    xb = x_ref[...].astype(BF16)
    d = x_ref.shape[1]
    for part in range(3):
        acc = jnp.dot(xb, w_ref[:, part * d:(part + 1) * d], preferred_element_type=F32)
        if part == 0:
            acc = acc * (NA_HEAD_DIM ** -0.5)
        for p in range(HEAD_PAIRS):
            o_ref[part * HEAD_PAIRS + p] = acc[:, p * LANES:(p + 1) * LANES].astype(BF16)


def _qkv(x, w_bf16):
    n, d = x.shape
    return pl.pallas_call(
        _qkv_kernel,
        out_shape=jax.ShapeDtypeStruct((3 * HEAD_PAIRS, n, LANES), BF16),
        grid=(n // ROW_TILE,),
        in_specs=[pl.BlockSpec((ROW_TILE, d), lambda i: (i, 0)),
                  pl.BlockSpec((d, 3 * d), lambda i: (0, 0))],
        out_specs=pl.BlockSpec((3 * HEAD_PAIRS, ROW_TILE, LANES), lambda i: (0, i, 0)),
        compiler_params=_params("parallel"),
        name="na_qkv",
    )(x, w_bf16)


def _na_kernel(q_ref, k_ref, v_ref, bias_a_ref, bias_b_ref, mb_ref, o_ref, *, rows):
    s = pl.program_id(1)
    w = GRID_W
    low = lax.broadcasted_iota(I32, (w, LANES), 1) < NA_HEAD_DIM
    contract_last = (((1,), (1,)), ((), ()))
    meta0 = FRONT_PAD

    def stacked_q(p, q0):
        qp = q_ref[p, q0:q0 + w, :]
        zero = jnp.zeros_like(qp)
        return jnp.concatenate([jnp.where(low, qp, zero), jnp.where(low, zero, qp)], axis=0)

    def attend_all(q0, k0, bias_ref):
        scores = []
        for p in range(HEAD_PAIRS):
            q2 = stacked_q(p, q0)
            s_meta = lax.dot_general(q2, k_ref[p, meta0:meta0 + N_META, :], contract_last,
                                     preferred_element_type=F32) + mb_ref[p]
            s_win = None
            if k0 is not None:
                s_win = lax.dot_general(q2, k_ref[p, pl.ds(k0, NA_KH * w), :], contract_last,
                                        preferred_element_type=F32) + bias_ref[0, p]
            scores.append((s_meta, s_win))
        outs = []
        for p in range(HEAD_PAIRS):
            s_meta, s_win = scores[p]
            m = jnp.max(s_meta, axis=-1, keepdims=True)
            if s_win is not None:
                m = jnp.maximum(m, jnp.max(s_win, axis=-1, keepdims=True))
            p_meta = jnp.exp(s_meta - m)
            denom = jnp.sum(p_meta, axis=-1, keepdims=True)
            o = jnp.dot(p_meta.astype(BF16), v_ref[p, meta0:meta0 + N_META, :], preferred_element_type=F32)
            if s_win is not None:
                p_win = jnp.exp(s_win - m)
                denom = denom + jnp.sum(p_win, axis=-1, keepdims=True)
                o = o + jnp.dot(p_win.astype(BF16), v_ref[p, pl.ds(k0, NA_KH * w), :],
                                preferred_element_type=F32)
            o = o / denom
            outs.append(jnp.where(low, o[:w], o[w:]))
        return outs

    @pl.when(s == 0)
    def _():
        qrow = lax.broadcasted_iota(I32, (w, LANES), 0)
        for p, o in enumerate(attend_all(LANES - w, None, None)):
            o_ref[p, 0:LANES - w, :] = jnp.zeros((LANES - w, LANES), o_ref.dtype)
            o_ref[p, LANES - w:LANES, :] = jnp.where(qrow >= w - N_META, o, 0.0).astype(o_ref.dtype)

    @pl.when(s >= 1)
    def _():
        for half, bias_ref in enumerate((bias_a_ref, bias_b_ref)):
            r = 2 * (s - 1) + half
            rs = jnp.clip(r - NA_KH // 2, 0, rows - NA_KH)
            k0 = pl.multiple_of(LANES + rs * w, w)
            for p, o in enumerate(attend_all(half * w, k0, bias_ref)):
                o_ref[p, half * w:(half + 1) * w, :] = o.astype(o_ref.dtype)


def _na_bias_table(rpb, rows):
    del rows
    w = GRID_W
    q = jnp.arange(w)
    col_start = jnp.clip(q - NA_KW // 2, 0, w - NA_KW)
    c = jnp.arange(w)
    in_win = (c[None, :] >= col_start[:, None]) & (c[None, :] < col_start[:, None] + NA_KW)
    pad = w - NA_KW
    rp = jnp.pad(rpb.astype(F32), ((0, 0), (0, 0), (pad, pad)))
    toeplitz = jnp.stack([rp[:, :, w - 1 - qq:2 * w - 1 - qq] for qq in range(w)], axis=2)
    toeplitz = jnp.where(in_win[None, None], toeplitz, NEG_BIG)
    tabs = [jnp.concatenate([toeplitz[:, NA_KH - 1 - v + j] for j in range(NA_KH)], axis=-1) for v in range(NA_KH)]
    return jnp.stack(tabs, axis=0)


def _na_attention(qkv, bias_tab, meta_bias, *, batch, lp):
    n = qkv.shape[1]
    w = GRID_W
    rows = (lp - LANES) // w
    tiles = lp // LANES

    def variant(half):
        def index(b, s):
            r = jnp.clip(2 * (s - 1) + half, 0, rows - 1)
            return (r - jnp.clip(r - NA_KH // 2, 0, rows - NA_KH), 0, 0, 0)
        return index

    kern = functools.partial(_na_kernel, rows=rows)
    bias_block = (1, HEAD_PAIRS, 2 * w, NA_KH * w)
    bias_tab = bias_tab.reshape(NA_KH, HEAD_PAIRS, 2 * w, NA_KH * w)
    return pl.pallas_call(
        kern,
        out_shape=jax.ShapeDtypeStruct((HEAD_PAIRS, n, LANES), BF16),
        grid=(batch, tiles),
        in_specs=[
            pl.BlockSpec((HEAD_PAIRS, LANES, LANES), lambda b, s: (0, b * tiles + s, 0)),
            pl.BlockSpec((HEAD_PAIRS, lp, LANES), lambda b, s: (1, b, 0)),
            pl.BlockSpec((HEAD_PAIRS, lp, LANES), lambda b, s: (2, b, 0)),
            pl.BlockSpec(bias_block, variant(0)),
            pl.BlockSpec(bias_block, variant(1)),
            pl.BlockSpec((HEAD_PAIRS, 2 * w, N_META), lambda b, s: (0, 0, 0)),
        ],
        out_specs=pl.BlockSpec((HEAD_PAIRS, LANES, LANES), lambda b, s: (0, b * tiles + s, 0)),
        compiler_params=_params("parallel", "arbitrary"),
        name="na_attention",
    )(qkv, qkv, qkv, bias_tab, bias_tab,
      jnp.repeat(meta_bias, w, axis=0).reshape(HEAD_PAIRS, 2 * w, N_META))


def _na_out_kernel(o_ref, res_ref, w_ref, g_ref, b_ref, out_ref):
    att = jnp.concatenate([o_ref[p] for p in range(HEAD_PAIRS)], axis=-1)
    mix = jnp.dot(att, w_ref[...], preferred_element_type=F32)
    out_ref[...] = _layer_norm(DN_ALPHA * res_ref[...] + mix, g_ref[...], b_ref[...])


def _na_out(o, res, w_bf16, g, b):
    n, d = res.shape
    vec = pl.BlockSpec((1, d), lambda i: (0, 0))
    return pl.pallas_call(
        _na_out_kernel,
        out_shape=jax.ShapeDtypeStruct((n, d), F32),
        grid=(n // ROW_TILE,),
        in_specs=[pl.BlockSpec((HEAD_PAIRS, ROW_TILE, LANES), lambda i: (0, i, 0)),
                  pl.BlockSpec((ROW_TILE, d), lambda i: (i, 0)),
                  pl.BlockSpec((d, d), lambda i: (0, 0)), vec, vec],
        out_specs=pl.BlockSpec((ROW_TILE, d), lambda i: (i, 0)),
        compiler_params=_params("parallel"),
        name="na_out",
    )(o, res, w_bf16, g, b)


def kernel(x, meta_tokens, lru_w_in, lru_conv_w, lru_conv_b, lru_wa, lru_ba, lru_wx, lru_bx, lru_lambda, lru_w_out, na_w_qkv, na_rpb, na_meta_bias, na_w_out, ln_mix_g, ln_mix_b, router_w, router_b, moe_w_gu, moe_b_gu, moe_w_down, moe_b_down, ln_ffn_g, ln_ffn_b):
    batch, seq, d = x.shape
    lp = LANES + seq
    assert d == D_MODEL and seq % (2 * GRID_W) == 0 and seq // GRID_W >= NA_KH
    assert lp % SCAN_CHUNK == 0 and (batch * lp) % ROW_TILE == 0
    n = batch * lp

    front = jnp.zeros((batch, FRONT_PAD, d), x.dtype)
    meta = jnp.broadcast_to(meta_tokens[None].astype(x.dtype), (batch, N_META, d))
    h = jnp.concatenate([front, meta, x], axis=1).reshape(n, d)

    u = _matmul(h, lru_w_in[0].astype(BF16))
    row = lambda v: v[None, :]
    def scan(direction, reverse, fuse_out=None):
        return _lru_scan(
            u, lru_conv_w[0], row(lru_conv_b[0]),
            (0.5 * lru_wa[0, direction]).astype(BF16), row(0.5 * lru_ba[0, direction]),
            (0.5 * lru_wx[0, direction]).astype(BF16), row(0.5 * lru_bx[0, direction]),
            row(lru_lambda[0, direction]), reverse=reverse, lp=lp, fuse_out=fuse_out)

    h_fwd = scan(0, False)
    h = scan(1, True, fuse_out=(h_fwd, h, lru_w_out[0].astype(BF16), row(ln_mix_g[0]), row(ln_mix_b[0])))
    h = _moe_layer(h, router_w[0], router_b[0], moe_w_gu, moe_b_gu, moe_w_down, moe_b_down,
                   ln_ffn_g[0], ln_ffn_b[0], 0, lp)

    qkv = _qkv(h, na_w_qkv[0].astype(BF16))
    att = _na_attention(qkv, _na_bias_table(na_rpb[0], seq // GRID_W), na_meta_bias[0].astype(F32),
                        batch=batch, lp=lp)
    h = _na_out(att, h, na_w_out[0].astype(BF16), row(ln_mix_g[1]), row(ln_mix_b[1]))
    h = _moe_layer(h, router_w[1], router_b[1], moe_w_gu, moe_b_gu, moe_w_down, moe_b_down,
                   ln_ffn_g[1], ln_ffn_b[1], 1, lp, drop_front=(batch, lp))

    return h.reshape(batch, seq, d)
```

```python
import functools

import jax
import jax.numpy as jnp
from jax import lax
from jax.experimental import pallas as pl
from jax.experimental.pallas import tpu as pltpu
from jax.experimental.pallas import tpu_sc as plsc

F32 = jnp.float32
BF16 = jnp.bfloat16
I32 = jnp.int32
U32 = jnp.uint32

D_MODEL = 1024
N_META = 16
GRID_W = 64
LRU_BLOCKS = 4
LRU_BLOCK = D_MODEL // LRU_BLOCKS
CONV_W = 4
LRU_C = 8.0
NA_HEADS = 16
NA_HEAD_DIM = D_MODEL // NA_HEADS
NA_KH = 8
NA_KW = 16
N_EXPERTS = 32
TOP_K = 4
SWIGLU_LIMIT = 7.0
SWIGLU_ALPHA = 1.702
DEPTH = 2
DN_ALPHA = (2.0 * DEPTH) ** 0.25
LN_EPS = 1e-5

LANES = 128
SUBLANES = 8
FRONT_PAD = LANES - N_META
HEAD_PAIRS = D_MODEL // LANES
NEG_BIG = -1e30

ROW_TILE = 512
SCAN_CHUNK = 352
ROUTE_TILE = 512
MOVE_TILE = 256
EXPERT_TILE = 512
COLLECT_PARTS = 2
VMEM_LIMIT = 56 << 20

SC_CORES = 2
SC_SUBCORES = 16
SC_WORKERS = SC_CORES * SC_SUBCORES
SC_MAX_INDICES = 64
SC_ROW_BUFFER_BYTES = 128 << 10


def _params(*sem):
    return pltpu.CompilerParams(dimension_semantics=sem, vmem_limit_bytes=VMEM_LIMIT)


def _sigmoid(x):
    return 0.5 * jnp.tanh(0.5 * x) + 0.5


def _layer_norm(x, g, b):
    mu = jnp.mean(x, axis=-1, keepdims=True)
    xc = x - mu
    var = jnp.mean(xc * xc, axis=-1, keepdims=True)
    return xc * lax.rsqrt(var + LN_EPS) * g + b


def _matmul_kernel(x_ref, w_ref, o_ref):
    o_ref[...] = jnp.dot(x_ref[...].astype(BF16), w_ref[...], preferred_element_type=F32).astype(o_ref.dtype)


def _matmul(x, w_bf16):
    n, k = x.shape
    m = w_bf16.shape[1]
    return pl.pallas_call(
        _matmul_kernel,
        out_shape=jax.ShapeDtypeStruct((n, m), BF16),
        grid=(n // ROW_TILE,),
        in_specs=[pl.BlockSpec((ROW_TILE, k), lambda i: (i, 0)),
                  pl.BlockSpec((k, m), lambda i: (0, 0))],
        out_specs=pl.BlockSpec((ROW_TILE, m), lambda i: (i, 0)),
        compiler_params=_params("parallel"),
        name="in_proj",
    )(x, w_bf16)


def _lru_scan_kernel(xr_ref, xp_ref, xn_ref, cw_ref, cb_ref, wa_ref, ba_ref, wx_ref, bx_ref,
                     lam_ref, *rest, reverse, chunks_per_batch, fused):
    if fused:
        hf_ref, y_ref, res_ref, wout_ref, g_ref, beta_ref, h_ref, a_sc, b_sc, h_sc, carry_sc = rest
    else:
        h_ref, a_sc, b_sc, h_sc, carry_sc = rest
    t_rows = xr_ref.shape[0]
    seg = t_rows // SUBLANES
    halo = xp_ref.shape[0]
    lane_tiles = xr_ref.shape[1] // LANES
    step = pl.program_id(0)
    chunk = (pl.num_programs(0) - 1 - step) if reverse else step
    cib = chunk % chunks_per_batch
    first_chunk = cib == 0
    last_chunk = cib == chunks_per_batch - 1
    row8 = lax.broadcasted_iota(I32, (SUBLANES, 1), 0)
    cat = lambda parts: jnp.concatenate(parts, axis=0)

    head_pos = cib * t_rows + lax.broadcasted_iota(I32, (LANES, 1), 0)
    head_real = head_pos >= FRONT_PAD
    xr = xr_ref[...].astype(F32)
    xr = cat([jnp.where(head_real, xr[:LANES], 0.0), xr[LANES:]])
    prev_pos = cib * t_rows - halo + lax.broadcasted_iota(I32, (halo, 1), 0)
    prev = jnp.where(prev_pos >= FRONT_PAD, xp_ref[...].astype(F32), 0.0)
    nxt = jnp.where(last_chunk, 0.0, xn_ref[...].astype(F32))

    xm1 = pltpu.roll(xr, 1, 0)
    xm2 = pltpu.roll(xr, 2, 0)
    xp1 = pltpu.roll(xr, t_rows - 1, 0)
    xm1 = cat([jnp.where(row8 == 0, prev[halo - 1:halo], xm1[:SUBLANES]), xm1[SUBLANES:]])
    xm2_head = jnp.where(row8 == 0, prev[halo - 2:halo - 1],
                         jnp.where(row8 == 1, prev[halo - 1:halo], xm2[:SUBLANES]))
    xm2 = cat([xm2_head, xm2[SUBLANES:]])
    xp1 = cat([xp1[:t_rows - SUBLANES], jnp.where(row8 == SUBLANES - 1, nxt[0:1], xp1[t_rows - SUBLANES:])])
    cw = cw_ref[...]
    xc = cw[0:1, :] * xm2 + cw[1:2, :] * xm1 + cw[2:3, :] * xr + cw[3:4, :] * xp1 + cb_ref[...]

    xcb = xc.astype(BF16)
    za, zx = [], []
    for blk in range(LRU_BLOCKS):
        xblk = xcb[:, blk * LRU_BLOCK:(blk + 1) * LRU_BLOCK]
        za.append(jnp.dot(xblk, wa_ref[blk], preferred_element_type=F32))
        zx.append(jnp.dot(xblk, wx_ref[blk], preferred_element_type=F32))
    tanh_a = jnp.tanh(jnp.concatenate(za, axis=-1) + ba_ref[...])
    tanh_x = jnp.tanh(jnp.concatenate(zx, axis=-1) + bx_ref[...])

    z = -lam_ref[...]
    softplus = jnp.maximum(z, 0.0) + jnp.log1p(jnp.exp(-jnp.abs(z)))
    half_rate = (-0.5 * LRU_C) * softplus
    log_a = tanh_a * half_rate + half_rate
    a = jnp.exp(log_a)
    gap = jnp.tanh(log_a) * (-1.0 - a * a)
    mult = jnp.where(gap > 0.0, gap * lax.rsqrt(gap), 0.0)
    if reverse:
        tail = jnp.where(jnp.logical_and(last_chunk, row8 == SUBLANES - 1), 1.0, mult[t_rows - SUBLANES:])
        mult = cat([mult[:t_rows - SUBLANES], tail])
    else:
        mult = cat([jnp.where(head_pos == FRONT_PAD, 1.0, mult[:LANES]), mult[LANES:]])
    b = mult * ((0.5 * tanh_x + 0.5) * xc)
    b = cat([jnp.where(head_real, b[:LANES], 0.0), b[LANES:]])
    for c in range(lane_tiles):
        a_sc[c] = a[:, c * LANES:(c + 1) * LANES]
        b_sc[c] = b[:, c * LANES:(c + 1) * LANES]

    @pl.when(last_chunk if reverse else first_chunk)
    def _():
        carry_sc[...] = jnp.zeros_like(carry_sc)

    order = range(seg - 1, -1, -1) if reverse else range(seg)
    rows_of = lambda j: pl.ds(j, SUBLANES, stride=seg)

    local = [jnp.zeros((SUBLANES, LANES), F32)] * lane_tiles
    prod = [jnp.ones((SUBLANES, LANES), F32)] * lane_tiles
    for j in order:
        for c in range(lane_tiles):
            av = a_sc[c, rows_of(j), :]
            local[c] = av * local[c] + b_sc[c, rows_of(j), :]
            prod[c] = av * prod[c]
            h_sc[c, rows_of(j), :] = local[c]
            a_sc[c, rows_of(j), :] = prod[c]

    carry_in = carry_sc[...]
    carry_out = []
    seg_carry = []
    for c in range(lane_tiles):
        state = carry_in[:, c * LANES:(c + 1) * LANES]
        rows = [None] * SUBLANES
        for s in (range(SUBLANES - 1, -1, -1) if reverse else range(SUBLANES)):
            rows[s] = state
            state = local[c][s:s + 1] + prod[c][s:s + 1] * state
        seg_carry.append(cat(rows))
        carry_out.append(state)
    carry_sc[...] = jnp.concatenate(carry_out, axis=-1)

    for j in order:
        for c in range(lane_tiles):
            h_sc[c, rows_of(j), :] = h_sc[c, rows_of(j), :] + a_sc[c, rows_of(j), :] * seg_carry[c]
    states = jnp.concatenate([h_sc[c] for c in range(lane_tiles)], axis=-1)
    if fused:
        gated = (hf_ref[...] + states) * _gelu_tanh(y_ref[...].astype(F32))
        mix = jnp.dot(gated.astype(BF16), wout_ref[...], preferred_element_type=F32)
        h_ref[...] = _layer_norm(DN_ALPHA * res_ref[...] + mix, g_ref[...], beta_ref[...])
    else:
        h_ref[...] = states


def _lru_scan(u, cw, cb, wa_half, ba_half, wx_half, bx_half, lam, *, reverse, lp, fuse_out=None):
    n = u.shape[0]
    d = D_MODEL
    t = SCAN_CHUNK
    halo = 2 * SUBLANES
    n_chunks = n // t
    cpb = lp // t
    t_h = t // halo
    n_h = n // halo

    def chunk_of(i):
        return (n_chunks - 1 - i) if reverse else i

    kern = functools.partial(_lru_scan_kernel, reverse=reverse, chunks_per_batch=cpb, fused=fuse_out is not None)
    chunk_rows = lambda col: pl.BlockSpec((t, d), lambda i: (chunk_of(i), col))
    extra_specs, extra_args = [], ()
    if fuse_out is not None:
        h_other, res, w_out, ln_g, ln_b = fuse_out
        extra_specs = [chunk_rows(0), chunk_rows(1), chunk_rows(0),
                       pl.BlockSpec((d, d), lambda i: (0, 0)),
                       pl.BlockSpec((1, d), lambda i: (0, 0)), pl.BlockSpec((1, d), lambda i: (0, 0))]
        extra_args = (h_other, u, res, w_out, ln_g, ln_b)
    full2 = lambda shape: pl.BlockSpec(shape, lambda i: (0, 0))
    full3 = lambda shape: pl.BlockSpec(shape, lambda i: (0, 0, 0))
    tile_major = pltpu.VMEM((d // LANES, t, LANES), F32)
    return pl.pallas_call(
        kern,
        out_shape=jax.ShapeDtypeStruct((n, d), F32),
        grid=(n_chunks,),
        in_specs=[
            pl.BlockSpec((t, d), lambda i: (chunk_of(i), 0)),
            pl.BlockSpec((halo, d), lambda i: (jnp.maximum(chunk_of(i) * t_h - 1, 0), 0)),
            pl.BlockSpec((halo, d), lambda i: (jnp.minimum((chunk_of(i) + 1) * t_h, n_h - 1), 0)),
            full2((CONV_W, d)), full2((1, d)),
            full3((LRU_BLOCKS, LRU_BLOCK, LRU_BLOCK)), full2((1, d)),
            full3((LRU_BLOCKS, LRU_BLOCK, LRU_BLOCK)), full2((1, d)),
            full2((1, d)),
        ] + extra_specs,
        out_specs=pl.BlockSpec((t, d), lambda i: (chunk_of(i), 0)),
        scratch_shapes=[tile_major, tile_major, tile_major, pltpu.VMEM((1, d), F32)],
        compiler_params=_params("arbitrary"),
        name=("lru_scan_bwd" if reverse else "lru_scan_fwd") + ("_out" if fuse_out is not None else ""),
    )(u, u, u, cw, cb, wa_half, ba_half, wx_half, bx_half, lam, *extra_args)


def _gelu_tanh(y):
    c = 0.7978845608028654
    return y * (0.5 * (1.0 + jnp.tanh(c * (y + 0.044715 * (y * y * y)))))


def _pack_bf16_pairs(x):
    half = x.shape[1] // 2
    lo = pltpu.bitcast(x[:, :half].astype(BF16).astype(F32), U32)
    hi = pltpu.bitcast(x[:, half:].astype(BF16).astype(F32), U32)
    return pltpu.bitcast((lo >> 16) | (hi & jnp.uint32(0xFFFF0000)), I32)


def _unpack_bf16_pairs(packed, dtype=BF16):
    u = pltpu.bitcast(packed, U32)
    lo = pltpu.bitcast(u << 16, F32).astype(dtype)
    hi = pltpu.bitcast(u & jnp.uint32(0xFFFF0000), F32).astype(dtype)
    return jnp.concatenate([lo, hi], axis=-1)


def _router_kernel(h_ref, wt_ref, b_ref, real_ref, idx_ref, gate_ref, rank_ref, cnt_ref, hp_ref, base_sc):
    tm = h_ref.shape[0]
    h = h_ref[...]
    hp_ref[...] = _pack_bf16_pairs(h)

    @pl.when(pl.program_id(0) == 0)
    def _():
        base_sc[...] = jnp.zeros_like(base_sc)

    wt = wt_ref[...]
    h_hi = h.astype(BF16)
    h_lo = (h - h_hi.astype(F32)).astype(BF16)
    w_hi = wt.astype(BF16)
    w_lo = (wt - w_hi.astype(F32)).astype(BF16)
    contract_last = (((1,), (1,)), ((), ()))
    mm = lambda a, b: lax.dot_general(a, b, contract_last, preferred_element_type=F32)
    logits = mm(w_hi, h_hi) + (mm(w_hi, h_lo) + mm(w_lo, h_hi)) + b_ref[...]

    expert = lax.broadcasted_iota(I32, (N_EXPERTS, tm), 0).astype(F32)
    vals = logits
    idx_rows, val_rows, onehots = [], [], []
    for _ in range(TOP_K):
        m = jnp.max(vals, axis=0, keepdims=True)
        idx = jnp.min(jnp.where(vals == m, expert, float(N_EXPERTS)), axis=0, keepdims=True)
        hit = expert == idx
        onehots.append(jnp.where(hit, real_ref[...], 0.0))
        idx_rows.append(idx)
        val_rows.append(m)
        vals = jnp.where(hit, -jnp.inf, vals)

    onehot_all = jnp.concatenate(onehots, axis=0)
    t_from = lax.broadcasted_iota(I32, (tm, tm), 0)
    t_to = lax.broadcasted_iota(I32, (tm, tm), 1)
    earlier = jnp.where(t_from < t_to, 1.0, 0.0).astype(BF16)
    before = jnp.dot(onehot_all.astype(BF16), earlier, preferred_element_type=F32)
    base = base_sc[...]
    starts = []
    for k in range(TOP_K):
        starts.append(base)
        base = base + jnp.sum(onehots[k], axis=1, keepdims=True)
    contrib = onehot_all * (jnp.concatenate(starts, axis=0) + before)
    rank_rows = [jnp.sum(contrib[k * N_EXPERTS:(k + 1) * N_EXPERTS], axis=0, keepdims=True)
                 for k in range(TOP_K)]

    top_vals = jnp.concatenate(val_rows, axis=0)
    e = jnp.exp(top_vals - top_vals[0:1])
    idx_ref[...] = jnp.concatenate(idx_rows, axis=0).astype(I32)
    gate_ref[...] = e / jnp.sum(e, axis=0, keepdims=True)
    rank_ref[...] = jnp.concatenate(rank_rows, axis=0).astype(I32)
    base_sc[...] = base
    cnt_ref[...] = base.astype(I32)


def _router(h, w, b, real):
    n, d = h.shape
    tm = ROUTE_TILE
    out4 = lambda dt: jax.ShapeDtypeStruct((TOP_K, n), dt)
    spec4 = pl.BlockSpec((TOP_K, tm), lambda i: (0, i))
    return pl.pallas_call(
        _router_kernel,
        out_shape=(out4(I32), out4(F32), out4(I32), jax.ShapeDtypeStruct((N_EXPERTS, 1), I32),
                   jax.ShapeDtypeStruct((n, d // 2), I32)),
        grid=(n // tm,),
        in_specs=[pl.BlockSpec((tm, d), lambda i: (i, 0)),
                  pl.BlockSpec((N_EXPERTS, d), lambda i: (0, 0)),
                  pl.BlockSpec((N_EXPERTS, 1), lambda i: (0, 0)),
                  pl.BlockSpec((1, tm), lambda i: (0, i))],
        out_specs=(spec4, spec4, spec4, pl.BlockSpec((N_EXPERTS, 1), lambda i: (0, 0)),
                   pl.BlockSpec((tm, d // 2), lambda i: (i, 0))),
        scratch_shapes=[pltpu.VMEM((N_EXPERTS, 1), F32)],
        compiler_params=_params("arbitrary"),
        name="router",
    )(h, w.T, b[:, None], real)


def _sc_gather_rows(table, idx, name):
    rows, width = table.shape
    total = idx.shape[0]
    per_worker = total // SC_WORKERS
    max_chunk = min(SC_MAX_INDICES, SC_ROW_BUFFER_BYTES // (width * table.dtype.itemsize))
    chunk = max(c for c in range(SUBLANES, max_chunk + 1, SUBLANES) if per_worker % (2 * c) == 0)
    n_chunks = per_worker // chunk
    assert total == SC_WORKERS * n_chunks * chunk and n_chunks % 2 == 0
    mesh = plsc.VectorSubcoreMesh(core_axis_name="c", subcore_axis_name="s",
                                  num_cores=SC_CORES, num_subcores=SC_SUBCORES)

    def body(table_hbm, idx_hbm, out_hbm, idx_v, rows0, rows1, gsem0, gsem1, psem0, psem1):
        worker = lax.axis_index("s") * SC_CORES + lax.axis_index("c")
        base = worker * per_worker
        pltpu.sync_copy(idx_hbm.at[worker], idx_v)
        bufs = (rows0, rows1)
        gsems = (gsem0, gsem1)
        psems = (psem0, psem1)

        def gather(c, slot):
            return pltpu.make_async_copy(table_hbm.at[idx_v.at[c]], bufs[slot], gsems[slot])

        def put(c, slot):
            return pltpu.make_async_copy(bufs[slot], out_hbm.at[pl.ds(base + c * chunk, chunk)], psems[slot])

        gather(0, 0).start()

        @pl.loop(0, n_chunks, step=2)
        def _(c0):
            for slot in range(2):
                c = c0 + slot

                @pl.when(c + 1 < n_chunks)
                def _():
                    @pl.when(c >= 1)
                    def _():
                        put(c - 1, 1 - slot).wait()
                    gather(c + 1, 1 - slot).start()

                gather(c, slot).wait()
                put(c, slot).start()

        put(n_chunks - 2, 0).wait()
        put(n_chunks - 1, 1).wait()

    return pl.kernel(
        body,
        out_type=jax.ShapeDtypeStruct((total, width), table.dtype),
        mesh=mesh,
        scratch_types=[pltpu.VMEM((n_chunks, chunk), I32),
                       pltpu.VMEM((chunk, width), table.dtype), pltpu.VMEM((chunk, width), table.dtype),
                       pltpu.SemaphoreType.DMA, pltpu.SemaphoreType.DMA,
                       pltpu.SemaphoreType.DMA, pltpu.SemaphoreType.DMA],
        name=name,
    )(table, idx.reshape(SC_WORKERS, n_chunks, chunk))


def _sc_scatter_rows(src, dest, out_rows, name):
    n, width = src.shape
    fan = dest.shape[0]
    per_worker = n // SC_WORKERS
    chunk = max(c for c in range(SUBLANES, SC_MAX_INDICES + 1, SUBLANES) if per_worker % (2 * c) == 0)
    n_chunks = per_worker // chunk
    assert n == SC_WORKERS * n_chunks * chunk and n_chunks % 2 == 0
    mesh = plsc.VectorSubcoreMesh(core_axis_name="c", subcore_axis_name="s",
                                  num_cores=SC_CORES, num_subcores=SC_SUBCORES)

    def body(src_hbm, idx_hbm, out_hbm, idx_v, rows0, rows1, lsem0, lsem1, ssem0, ssem1):
        worker = lax.axis_index("s") * SC_CORES + lax.axis_index("c")
        base = worker * per_worker
        pltpu.sync_copy(idx_hbm.at[worker], idx_v)
        bufs = (rows0, rows1)
        lsems = (lsem0, lsem1)
        ssems = (ssem0, ssem1)

        def load(c, slot):
            return pltpu.make_async_copy(src_hbm.at[pl.ds(base + c * chunk, chunk)], bufs[slot], lsems[slot])

        def scatter(c, k, slot):
            return pltpu.make_async_copy(bufs[slot], out_hbm.at[idx_v.at[k * n_chunks + c]], ssems[slot])

        load(0, 0).start()

        @pl.loop(0, n_chunks, step=2)
        def _(c0):
            for slot in range(2):
                c = c0 + slot

                @pl.when(c + 1 < n_chunks)
                def _():
                    @pl.when(c >= 1)
                    def _():
                        for k in range(fan):
                            scatter(c - 1, k, 1 - slot).wait()
                    load(c + 1, 1 - slot).start()

                load(c, slot).wait()
                for k in range(fan):
                    scatter(c, k, slot).start()

        for k in range(fan):
            scatter(n_chunks - 2, k, 0).wait()
        for k in range(fan):
            scatter(n_chunks - 1, k, 1).wait()

    idx = dest.reshape(fan, SC_WORKERS, n_chunks, chunk).transpose(1, 0, 2, 3)
    return pl.kernel(
        body,
        out_type=jax.ShapeDtypeStruct((out_rows, width), src.dtype),
        mesh=mesh,
        scratch_types=[pltpu.VMEM((fan * n_chunks, chunk), I32),
                       pltpu.VMEM((chunk, width), src.dtype), pltpu.VMEM((chunk, width), src.dtype),
                       pltpu.SemaphoreType.DMA, pltpu.SemaphoreType.DMA,
                       pltpu.SemaphoreType.DMA, pltpu.SemaphoreType.DMA],
        name=name,
    )(src, idx.reshape(SC_WORKERS, fan * n_chunks, chunk))


def _expert_kernel(be_ref, valid_ref, xs_ref, wgu_ref, bgu_ref, wd_ref, bd_ref, y_ref, wgu_sc, wd_sc):
    i = pl.program_id(0)
    e = be_ref[i]
    e_prev = be_ref[jnp.maximum(i - 1, 0)]
    d = wd_sc.shape[0]

    @pl.when((i == 0) | (e != e_prev))
    def _():
        for r in range(0, d, LANES):
            wgu_sc[r:r + LANES, :] = wgu_ref[0, r:r + LANES, :].astype(BF16)
            wd_sc[r:r + LANES, :] = wd_ref[0, r:r + LANES, :].astype(BF16)

    valid = valid_ref[i]

    @pl.when(valid > 0)
    def _():
        row = lax.broadcasted_iota(I32, xs_ref.shape, 0)
        x = _unpack_bf16_pairs(jnp.where(row < valid, xs_ref[...], 0))
        h = jnp.dot(x, wgu_sc[...], preferred_element_type=F32) + bgu_ref[0]
        glu = jnp.minimum(h[:, :d], SWIGLU_LIMIT)
        lin = jnp.clip(h[:, d:], -SWIGLU_LIMIT, SWIGLU_LIMIT)
        act = glu * _sigmoid(SWIGLU_ALPHA * glu) * (lin + 1.0)
        y = jnp.dot(act.astype(BF16), wd_sc[...], preferred_element_type=F32) + bd_ref[0]
        y_ref[...] = _pack_bf16_pairs(y)

    @pl.when(valid == 0)
    def _():
        y_ref[...] = jnp.zeros_like(y_ref)


def _experts(block_e, valid_rows, xs, w_gu, b_gu, w_down, b_down, layer):
    cap = xs.shape[0]
    d = D_MODEL
    tm = EXPERT_TILE
    n_blocks = cap // tm
    grid_spec = pltpu.PrefetchScalarGridSpec(
        num_scalar_prefetch=2,
        grid=(n_blocks,),
        in_specs=[
            pl.BlockSpec((tm, d // 2), lambda i, be, nu: (i, 0)),
            pl.BlockSpec((None, 1, d, 2 * d), lambda i, be, nu: (layer, be[i], 0, 0)),
            pl.BlockSpec((1, 1, 2 * d), lambda i, be, nu: (be[i], 0, 0)),
            pl.BlockSpec((None, 1, d, d), lambda i, be, nu: (layer, be[i], 0, 0)),
            pl.BlockSpec((1, 1, d), lambda i, be, nu: (be[i], 0, 0)),
        ],
        out_specs=pl.BlockSpec((tm, d // 2), lambda i, be, nu: (i, 0)),
        scratch_shapes=[pltpu.VMEM((d, 2 * d), BF16), pltpu.VMEM((d, d), BF16)],
    )
    return pl.pallas_call(
        _expert_kernel,
        out_shape=jax.ShapeDtypeStruct((cap, d // 2), I32),
        grid_spec=grid_spec,
        compiler_params=_params("arbitrary"),
        name="moe_experts",
    )(block_e, valid_rows, xs, w_gu, b_gu[layer][:, None, :], w_down, b_down[layer][:, None, :])


def _combine_kernel(g_ref, b_ref, *refs, n_sub):
    o_ref = refs[-1]
    rows = o_ref.shape[0] // n_sub
    for s in range(n_sub):
        gate_ref, res_ref, yk_ref = refs[3 * s:3 * s + 3]
        gates = gate_ref[...]
        ffn = gates[:, 0:1] * _unpack_bf16_pairs(yk_ref[0], F32)
        for k in range(1, TOP_K):
            ffn = ffn + gates[:, k:k + 1] * _unpack_bf16_pairs(yk_ref[k], F32)
        o_ref[s * rows:(s + 1) * rows, :] = _layer_norm(DN_ALPHA * res_ref[...] + ffn, g_ref[...], b_ref[...])


def _combine(gates, res, yk, g, b, part, acc, drop_front=None):
    n, d = res.shape
    if drop_front is None:
        tt, n_sub = MOVE_TILE, 1
        steps = n // COLLECT_PARTS // tt
        grid = (steps,)
        local = lambda s, i: i
        src = lambda s, i: part * steps + i
        dst = lambda i: part * steps + i
        out_rows = n
    else:
        batch, lp = drop_front
        tt, n_sub = LANES, 2
        tiles = lp // tt
        part_batch = batch // COLLECT_PARTS
        pairs = (tiles - 1) // n_sub
        assert pairs * n_sub == tiles - 1
        grid = (part_batch, pairs)
        local = lambda s, b_, j: b_ * tiles + 1 + n_sub * j + s
        src = lambda s, b_, j: (part * part_batch + b_) * tiles + 1 + n_sub * j + s
        dst = lambda b_, j: (part * part_batch + b_) * pairs + j
        out_rows = n - batch * tt
    vec = pl.BlockSpec((1, d), lambda *_: (0, 0))
    in_specs = [vec, vec]
    args = (g, b)
    for s in range(n_sub):
        in_specs += [pl.BlockSpec((tt, TOP_K), lambda *i, s=s: (local(s, *i), 0)),
                     pl.BlockSpec((tt, d), lambda *i, s=s: (src(s, *i), 0)),
                     pl.BlockSpec((TOP_K, tt, d // 2), lambda *i, s=s: (0, local(s, *i), 0))]
        args += (gates, res, yk)
    aliases = {}
    if acc is not None:
        in_specs.append(pl.BlockSpec(memory_space=pl.ANY))
        args += (acc,)
        aliases = {len(args) - 1: 0}
    return pl.pallas_call(
        functools.partial(_combine_kernel, n_sub=n_sub),
        out_shape=jax.ShapeDtypeStruct((out_rows, d), F32),
        grid=grid,
        in_specs=in_specs,
        out_specs=pl.BlockSpec((n_sub * tt, d), lambda *i: (dst(*i), 0)),
        input_output_aliases=aliases,
        compiler_params=_params(*(["parallel"] * len(grid))),
        name="moe_combine",
    )(*args)


def _moe_layer(h, router_w, router_b, w_gu, b_gu, w_down, b_down, ln_g, ln_b, layer, lp, drop_front=None):
    n, d = h.shape
    tm = EXPERT_TILE
    token = jnp.arange(n, dtype=I32)
    real = (token % lp) >= FRONT_PAD
    idx, gates, rank, counts, h_packed = _router(h, router_w, router_b, real.astype(F32)[None, :])
    counts = counts[:, 0]
    padded = (counts + tm - 1) // tm * tm
    pad_end = jnp.cumsum(padded)
    pad_start = pad_end - padded
    n_real = n - (n // lp) * FRONT_PAD
    n_blocks = -(-(n_real * TOP_K + N_EXPERTS * (tm - 1)) // tm) + 1
    group_start = jnp.sum(jnp.where(idx[:, :, None] == jnp.arange(N_EXPERTS, dtype=I32), pad_start, 0), axis=-1)
    dest = jnp.where(real, group_start + rank, (n_blocks - 1) * tm + token % tm)
    block_start = jnp.arange(n_blocks, dtype=I32) * tm
    block_e = jnp.minimum(jnp.sum((pad_end[None, :] <= block_start[:, None]).astype(I32), axis=1),
                          N_EXPERTS - 1)
    group_end = (pad_start + counts)[block_e]
    valid_rows = jnp.where(block_start < pad_end[-1], jnp.clip(group_end - block_start, 0, tm), 0).astype(I32)

    xs = _sc_scatter_rows(h_packed, dest, n_blocks * tm, "moe_dispatch")
    ys = _experts(block_e, valid_rows, xs, w_gu, b_gu, w_down, b_down, layer)
    n_part = n // COLLECT_PARTS
    assert n_part % lp == 0
    out = None
    for part in range(COLLECT_PARTS):
        rows = slice(part * n_part, (part + 1) * n_part)
        yk = _sc_gather_rows(ys, dest[:, rows].reshape(TOP_K * n_part), "moe_collect")
        out = _combine(gates[:, rows].T, h, yk.reshape(TOP_K, n_part, d // 2), ln_g[None, :], ln_b[None, :],
                       part, out, drop_front)
    return out


def _qkv_kernel(x_ref, w_ref, o_ref):
    xb = x_ref[...].astype(BF16)
    d = x_ref.shape[1]
    for part in range(3):
        acc = jnp.dot(xb, w_ref[:, part * d:(part + 1) * d], preferred_element_type=F32)
        if part == 0:
            acc = acc * (NA_HEAD_DIM ** -0.5)
        for p in range(HEAD_PAIRS):
            o_ref[part * HEAD_PAIRS + p] = acc[:, p * LANES:(p + 1) * LANES].astype(BF16)


def _qkv(x, w_bf16):
    n, d = x.shape
    return pl.pallas_call(
        _qkv_kernel,
        out_shape=jax.ShapeDtypeStruct((3 * HEAD_PAIRS, n, LANES), BF16),
        grid=(n // ROW_TILE,),
        in_specs=[pl.BlockSpec((ROW_TILE, d), lambda i: (i, 0)),
                  pl.BlockSpec((d, 3 * d), lambda i: (0, 0))],
        out_specs=pl.BlockSpec((3 * HEAD_PAIRS, ROW_TILE, LANES), lambda i: (0, i, 0)),
        compiler_params=_params("parallel"),
        name="na_qkv",
    )(x, w_bf16)


def _na_kernel(q_ref, k_ref, v_ref, bias_a_ref, bias_b_ref, mb_ref, o_ref, *, rows):
    s = pl.program_id(1)
    w = GRID_W
    low = lax.broadcasted_iota(I32, (w, LANES), 1) < NA_HEAD_DIM
    contract_last = (((1,), (1,)), ((), ()))
    meta0 = FRONT_PAD

    def stacked_q(p, q0):
        qp = q_ref[p, q0:q0 + w, :]
        zero = jnp.zeros_like(qp)
        return jnp.concatenate([jnp.where(low, qp, zero), jnp.where(low, zero, qp)], axis=0)

    def attend_all(q0, k0, bias_ref):
        scores = []
        for p in range(HEAD_PAIRS):
            q2 = stacked_q(p, q0)
            s_meta = lax.dot_general(q2, k_ref[p, meta0:meta0 + N_META, :], contract_last,
                                     preferred_element_type=F32) + mb_ref[p]
            s_win = None
            if k0 is not None:
                s_win = lax.dot_general(q2, k_ref[p, pl.ds(k0, NA_KH * w), :], contract_last,
                                        preferred_element_type=F32) + bias_ref[0, p]
            scores.append((s_meta, s_win))
        outs = []
        for p in range(HEAD_PAIRS):
            s_meta, s_win = scores[p]
            m = jnp.max(s_meta, axis=-1, keepdims=True)
            if s_win is not None:
                m = jnp.maximum(m, jnp.max(s_win, axis=-1, keepdims=True))
            p_meta = jnp.exp(s_meta - m)
            denom = jnp.sum(p_meta, axis=-1, keepdims=True)
            o = jnp.dot(p_meta.astype(BF16), v_ref[p, meta0:meta0 + N_META, :], preferred_element_type=F32)
            if s_win is not None:
                p_win = jnp.exp(s_win - m)
                denom = denom + jnp.sum(p_win, axis=-1, keepdims=True)
                o = o + jnp.dot(p_win.astype(BF16), v_ref[p, pl.ds(k0, NA_KH * w), :],
                                preferred_element_type=F32)
            o = o / denom
            outs.append(jnp.where(low, o[:w], o[w:]))
        return outs

    @pl.when(s == 0)
    def _():
        qrow = lax.broadcasted_iota(I32, (w, LANES), 0)
        for p, o in enumerate(attend_all(LANES - w, None, None)):
            o_ref[p, 0:LANES - w, :] = jnp.zeros((LANES - w, LANES), o_ref.dtype)
            o_ref[p, LANES - w:LANES, :] = jnp.where(qrow >= w - N_META, o, 0.0).astype(o_ref.dtype)

    @pl.when(s >= 1)
    def _():
        for half, bias_ref in enumerate((bias_a_ref, bias_b_ref)):
            r = 2 * (s - 1) + half
            rs = jnp.clip(r - NA_KH // 2, 0, rows - NA_KH)
            k0 = pl.multiple_of(LANES + rs * w, w)
            for p, o in enumerate(attend_all(half * w, k0, bias_ref)):
                o_ref[p, half * w:(half + 1) * w, :] = o.astype(o_ref.dtype)


def _na_bias_table(rpb, rows):
    del rows
    w = GRID_W
    q = jnp.arange(w)
    col_start = jnp.clip(q - NA_KW // 2, 0, w - NA_KW)
    c = jnp.arange(w)
    in_win = (c[None, :] >= col_start[:, None]) & (c[None, :] < col_start[:, None] + NA_KW)
    pad = w - NA_KW
    rp = jnp.pad(rpb.astype(F32), ((0, 0), (0, 0), (pad, pad)))
    toeplitz = jnp.stack([rp[:, :, w - 1 - qq:2 * w - 1 - qq] for qq in range(w)], axis=2)
    toeplitz = jnp.where(in_win[None, None], toeplitz, NEG_BIG)
    tabs = [jnp.concatenate([toeplitz[:, NA_KH - 1 - v + j] for j in range(NA_KH)], axis=-1) for v in range(NA_KH)]
    return jnp.stack(tabs, axis=0)


def _na_attention(qkv, bias_tab, meta_bias, *, batch, lp):
    n = qkv.shape[1]
    w = GRID_W
    rows = (lp - LANES) // w
    tiles = lp // LANES

    def variant(half):
        def index(b, s):
            r = jnp.clip(2 * (s - 1) + half, 0, rows - 1)
            return (r - jnp.clip(r - NA_KH // 2, 0, rows - NA_KH), 0, 0, 0)
        return index

    kern = functools.partial(_na_kernel, rows=rows)
    bias_block = (1, HEAD_PAIRS, 2 * w, NA_KH * w)
    bias_tab = bias_tab.reshape(NA_KH, HEAD_PAIRS, 2 * w, NA_KH * w)
    return pl.pallas_call(
        kern,
        out_shape=jax.ShapeDtypeStruct((HEAD_PAIRS, n, LANES), BF16),
        grid=(batch, tiles),
        in_specs=[
            pl.BlockSpec((HEAD_PAIRS, LANES, LANES), lambda b, s: (0, b * tiles + s, 0)),
            pl.BlockSpec((HEAD_PAIRS, lp, LANES), lambda b, s: (1, b, 0)),
            pl.BlockSpec((HEAD_PAIRS, lp, LANES), lambda b, s: (2, b, 0)),
            pl.BlockSpec(bias_block, variant(0)),
            pl.BlockSpec(bias_block, variant(1)),
            pl.BlockSpec((HEAD_PAIRS, 2 * w, N_META), lambda b, s: (0, 0, 0)),
        ],
        out_specs=pl.BlockSpec((HEAD_PAIRS, LANES, LANES), lambda b, s: (0, b * tiles + s, 0)),
        compiler_params=_params("parallel", "arbitrary"),
        name="na_attention",
    )(qkv, qkv, qkv, bias_tab, bias_tab,
      jnp.repeat(meta_bias, w, axis=0).reshape(HEAD_PAIRS, 2 * w, N_META))


def _na_out_kernel(o_ref, res_ref, w_ref, g_ref, b_ref, out_ref):
    att = jnp.concatenate([o_ref[p] for p in range(HEAD_PAIRS)], axis=-1)
    mix = jnp.dot(att, w_ref[...], preferred_element_type=F32)
    out_ref[...] = _layer_norm(DN_ALPHA * res_ref[...] + mix, g_ref[...], b_ref[...])


def _na_out(o, res, w_bf16, g, b):
    n, d = res.shape
    vec = pl.BlockSpec((1, d), lambda i: (0, 0))
    return pl.pallas_call(
        _na_out_kernel,
        out_shape=jax.ShapeDtypeStruct((n, d), F32),
        grid=(n // ROW_TILE,),
        in_specs=[pl.BlockSpec((HEAD_PAIRS, ROW_TILE, LANES), lambda i: (0, i, 0)),
                  pl.BlockSpec((ROW_TILE, d), lambda i: (i, 0)),
                  pl.BlockSpec((d, d), lambda i: (0, 0)), vec, vec],
        out_specs=pl.BlockSpec((ROW_TILE, d), lambda i: (i, 0)),
        compiler_params=_params("parallel"),
        name="na_out",
    )(o, res, w_bf16, g, b)


def kernel(x, meta_tokens, lru_w_in, lru_conv_w, lru_conv_b, lru_wa, lru_ba, lru_wx, lru_bx, lru_lambda, lru_w_out, na_w_qkv, na_rpb, na_meta_bias, na_w_out, ln_mix_g, ln_mix_b, router_w, router_b, moe_w_gu, moe_b_gu, moe_w_down, moe_b_down, ln_ffn_g, ln_ffn_b):
    batch, seq, d = x.shape
    lp = LANES + seq
    assert d == D_MODEL and seq % (2 * GRID_W) == 0 and seq // GRID_W >= NA_KH
    assert lp % SCAN_CHUNK == 0 and (batch * lp) % ROW_TILE == 0
    n = batch * lp

    front = jnp.zeros((batch, FRONT_PAD, d), x.dtype)
    meta = jnp.broadcast_to(meta_tokens[None].astype(x.dtype), (batch, N_META, d))
    h = jnp.concatenate([front, meta, x], axis=1).reshape(n, d)

    u = _matmul(h, lru_w_in[0].astype(BF16))
    row = lambda v: v[None, :]
    def scan(direction, reverse, fuse_out=None):
        return _lru_scan(
            u, lru_conv_w[0], row(lru_conv_b[0]),
            (0.5 * lru_wa[0, direction]).astype(BF16), row(0.5 * lru_ba[0, direction]),
            (0.5 * lru_wx[0, direction]).astype(BF16), row(0.5 * lru_bx[0, direction]),
            row(lru_lambda[0, direction]), reverse=reverse, lp=lp, fuse_out=fuse_out)

    h_fwd = scan(0, False)
    h = scan(1, True, fuse_out=(h_fwd, h, lru_w_out[0].astype(BF16), row(ln_mix_g[0]), row(ln_mix_b[0])))
    h = _moe_layer(h, router_w[0], router_b[0], moe_w_gu, moe_b_gu, moe_w_down, moe_b_down,
                   ln_ffn_g[0], ln_ffn_b[0], 0, lp)

    qkv = _qkv(h, na_w_qkv[0].astype(BF16))
    att = _na_attention(qkv, _na_bias_table(na_rpb[0], seq // GRID_W), na_meta_bias[0].astype(F32),
                        batch=batch, lp=lp)
    h = _na_out(att, h, na_w_out[0].astype(BF16), row(ln_mix_g[1]), row(ln_mix_b[1]))
    h = _moe_layer(h, router_w[1], router_b[1], moe_w_gu, moe_b_gu, moe_w_down, moe_b_down,
                   ln_ffn_g[1], ln_ffn_b[1], 1, lp, drop_front=(batch, lp))

    return h.reshape(batch, seq, d)
```

```python
import functools

import jax
import jax.numpy as jnp
from jax import lax
from jax.experimental import pallas as pl
from jax.experimental.pallas import tpu as pltpu
from jax.experimental.pallas import tpu_sc as plsc

F32 = jnp.float32
BF16 = jnp.bfloat16
I32 = jnp.int32
U32 = jnp.uint32

D_MODEL = 1024
N_META = 16
GRID_W = 64
LRU_BLOCKS = 4
LRU_BLOCK = D_MODEL // LRU_BLOCKS
CONV_W = 4
LRU_C = 8.0
NA_HEADS = 16
NA_HEAD_DIM = D_MODEL // NA_HEADS
NA_KH = 8
NA_KW = 16
N_EXPERTS = 32
TOP_K = 4
SWIGLU_LIMIT = 7.0
SWIGLU_ALPHA = 1.702
DEPTH = 2
DN_ALPHA = (2.0 * DEPTH) ** 0.25
LN_EPS = 1e-5

LANES = 128
SUBLANES = 8
FRONT_PAD = LANES - N_META
HEAD_PAIRS = D_MODEL // LANES
NEG_BIG = -1e30

ROW_TILE = 512
SCAN_CHUNK = 352
ROUTE_TILE = 512
MOVE_TILE = 256
EXPERT_TILE = 512
COLLECT_PARTS = 4
VMEM_LIMIT = 56 << 20

SC_CORES = 2
SC_SUBCORES = 16
SC_WORKERS = SC_CORES * SC_SUBCORES
SC_MAX_INDICES = 64
SC_ROW_BUFFER_BYTES = 128 << 10


def _params(*sem):
    return pltpu.CompilerParams(dimension_semantics=sem, vmem_limit_bytes=VMEM_LIMIT)


def _sigmoid(x):
    return 0.5 * jnp.tanh(0.5 * x) + 0.5


def _layer_norm(x, g, b):
    mu = jnp.mean(x, axis=-1, keepdims=True)
    xc = x - mu
    var = jnp.mean(xc * xc, axis=-1, keepdims=True)
    return xc * lax.rsqrt(var + LN_EPS) * g + b


def _matmul_kernel(x_ref, w_ref, o_ref):
    o_ref[...] = jnp.dot(x_ref[...].astype(BF16), w_ref[...], preferred_element_type=F32).astype(o_ref.dtype)


def _matmul(x, w_bf16):
    n, k = x.shape
    m = w_bf16.shape[1]
    return pl.pallas_call(
        _matmul_kernel,
        out_shape=jax.ShapeDtypeStruct((n, m), BF16),
        grid=(n // ROW_TILE,),
        in_specs=[pl.BlockSpec((ROW_TILE, k), lambda i: (i, 0)),
                  pl.BlockSpec((k, m), lambda i: (0, 0))],
        out_specs=pl.BlockSpec((ROW_TILE, m), lambda i: (i, 0)),
        compiler_params=_params("parallel"),
        name="in_proj",
    )(x, w_bf16)


def _lru_scan_kernel(xr_ref, xp_ref, xn_ref, cw_ref, cb_ref, wa_ref, ba_ref, wx_ref, bx_ref,
                     lam_ref, *rest, reverse, chunks_per_batch, fused):
    if fused:
        hf_ref, y_ref, res_ref, wout_ref, g_ref, beta_ref, h_ref, a_sc, b_sc, h_sc, carry_sc = rest
    else:
        h_ref, a_sc, b_sc, h_sc, carry_sc = rest
    t_rows = xr_ref.shape[0]
    seg = t_rows // SUBLANES
    halo = xp_ref.shape[0]
    lane_tiles = xr_ref.shape[1] // LANES
    step = pl.program_id(0)
    chunk = (pl.num_programs(0) - 1 - step) if reverse else step
    cib = chunk % chunks_per_batch
    first_chunk = cib == 0
    last_chunk = cib == chunks_per_batch - 1
    row8 = lax.broadcasted_iota(I32, (SUBLANES, 1), 0)
    cat = lambda parts: jnp.concatenate(parts, axis=0)

    head_pos = cib * t_rows + lax.broadcasted_iota(I32, (LANES, 1), 0)
    head_real = head_pos >= FRONT_PAD
    xr = xr_ref[...].astype(F32)
    xr = cat([jnp.where(head_real, xr[:LANES], 0.0), xr[LANES:]])
    prev_pos = cib * t_rows - halo + lax.broadcasted_iota(I32, (halo, 1), 0)
    prev = jnp.where(prev_pos >= FRONT_PAD, xp_ref[...].astype(F32), 0.0)
    nxt = jnp.where(last_chunk, 0.0, xn_ref[...].astype(F32))

    xm1 = pltpu.roll(xr, 1, 0)
    xm2 = pltpu.roll(xr, 2, 0)
    xp1 = pltpu.roll(xr, t_rows - 1, 0)
    xm1 = cat([jnp.where(row8 == 0, prev[halo - 1:halo], xm1[:SUBLANES]), xm1[SUBLANES:]])
    xm2_head = jnp.where(row8 == 0, prev[halo - 2:halo - 1],
                         jnp.where(row8 == 1, prev[halo - 1:halo], xm2[:SUBLANES]))
    xm2 = cat([xm2_head, xm2[SUBLANES:]])
    xp1 = cat([xp1[:t_rows - SUBLANES], jnp.where(row8 == SUBLANES - 1, nxt[0:1], xp1[t_rows - SUBLANES:])])
    cw = cw_ref[...]
    xc = cw[0:1, :] * xm2 + cw[1:2, :] * xm1 + cw[2:3, :] * xr + cw[3:4, :] * xp1 + cb_ref[...]

    xcb = xc.astype(BF16)
    za, zx = [], []
    for blk in range(LRU_BLOCKS):
        xblk = xcb[:, blk * LRU_BLOCK:(blk + 1) * LRU_BLOCK]
        za.append(jnp.dot(xblk, wa_ref[blk], preferred_element_type=F32))
        zx.append(jnp.dot(xblk, wx_ref[blk], preferred_element_type=F32))
    tanh_a = jnp.tanh(jnp.concatenate(za, axis=-1) + ba_ref[...])
    tanh_x = jnp.tanh(jnp.concatenate(zx, axis=-1) + bx_ref[...])

    z = -lam_ref[...]
    softplus = jnp.maximum(z, 0.0) + jnp.log1p(jnp.exp(-jnp.abs(z)))
    half_rate = (-0.5 * LRU_C) * softplus
    log_a = tanh_a * half_rate + half_rate
    a = jnp.exp(log_a)
    gap = jnp.tanh(log_a) * (-1.0 - a * a)
    mult = jnp.where(gap > 0.0, gap * lax.rsqrt(gap), 0.0)
    if reverse:
        tail = jnp.where(jnp.logical_and(last_chunk, row8 == SUBLANES - 1), 1.0, mult[t_rows - SUBLANES:])
        mult = cat([mult[:t_rows - SUBLANES], tail])
    else:
        mult = cat([jnp.where(head_pos == FRONT_PAD, 1.0, mult[:LANES]), mult[LANES:]])
    b = mult * ((0.5 * tanh_x + 0.5) * xc)
    b = cat([jnp.where(head_real, b[:LANES], 0.0), b[LANES:]])
    for c in range(lane_tiles):
        a_sc[c] = a[:, c * LANES:(c + 1) * LANES]
        b_sc[c] = b[:, c * LANES:(c + 1) * LANES]

    @pl.when(last_chunk if reverse else first_chunk)
    def _():
        carry_sc[...] = jnp.zeros_like(carry_sc)

    order = range(seg - 1, -1, -1) if reverse else range(seg)
    rows_of = lambda j: pl.ds(j, SUBLANES, stride=seg)

    local = [jnp.zeros((SUBLANES, LANES), F32)] * lane_tiles
    prod = [jnp.ones((SUBLANES, LANES), F32)] * lane_tiles
    for j in order:
        for c in range(lane_tiles):
            av = a_sc[c, rows_of(j), :]
            local[c] = av * local[c] + b_sc[c, rows_of(j), :]
            prod[c] = av * prod[c]
            h_sc[c, rows_of(j), :] = local[c]
            a_sc[c, rows_of(j), :] = prod[c]

    carry_in = carry_sc[...]
    carry_out = []
    seg_carry = []
    for c in range(lane_tiles):
        state = carry_in[:, c * LANES:(c + 1) * LANES]
        rows = [None] * SUBLANES
        for s in (range(SUBLANES - 1, -1, -1) if reverse else range(SUBLANES)):
            rows[s] = state
            state = local[c][s:s + 1] + prod[c][s:s + 1] * state
        seg_carry.append(cat(rows))
        carry_out.append(state)
    carry_sc[...] = jnp.concatenate(carry_out, axis=-1)

    for j in order:
        for c in range(lane_tiles):
            h_sc[c, rows_of(j), :] = h_sc[c, rows_of(j), :] + a_sc[c, rows_of(j), :] * seg_carry[c]
    states = jnp.concatenate([h_sc[c] for c in range(lane_tiles)], axis=-1)
    if fused:
        gated = (hf_ref[...] + states) * _gelu_tanh(y_ref[...].astype(F32))
        mix = jnp.dot(gated.astype(BF16), wout_ref[...], preferred_element_type=F32)
        h_ref[...] = _layer_norm(DN_ALPHA * res_ref[...] + mix, g_ref[...], beta_ref[...])
    else:
        h_ref[...] = states


def _lru_scan(u, cw, cb, wa_half, ba_half, wx_half, bx_half, lam, *, reverse, lp, fuse_out=None):
    n = u.shape[0]
    d = D_MODEL
    t = SCAN_CHUNK
    halo = 2 * SUBLANES
    n_chunks = n // t
    cpb = lp // t
    t_h = t // halo
    n_h = n // halo

    def chunk_of(i):
        return (n_chunks - 1 - i) if reverse else i

    kern = functools.partial(_lru_scan_kernel, reverse=reverse, chunks_per_batch=cpb, fused=fuse_out is not None)
    chunk_rows = lambda col: pl.BlockSpec((t, d), lambda i: (chunk_of(i), col))
    extra_specs, extra_args = [], ()
    if fuse_out is not None:
        h_other, res, w_out, ln_g, ln_b = fuse_out
        extra_specs = [chunk_rows(0), chunk_rows(1), chunk_rows(0),
                       pl.BlockSpec((d, d), lambda i: (0, 0)),
                       pl.BlockSpec((1, d), lambda i: (0, 0)), pl.BlockSpec((1, d), lambda i: (0, 0))]
        extra_args = (h_other, u, res, w_out, ln_g, ln_b)
    full2 = lambda shape: pl.BlockSpec(shape, lambda i: (0, 0))
    full3 = lambda shape: pl.BlockSpec(shape, lambda i: (0, 0, 0))
    tile_major = pltpu.VMEM((d // LANES, t, LANES), F32)
    return pl.pallas_call(
        kern,
        out_shape=jax.ShapeDtypeStruct((n, d), F32),
        grid=(n_chunks,),
        in_specs=[
            pl.BlockSpec((t, d), lambda i: (chunk_of(i), 0)),
            pl.BlockSpec((halo, d), lambda i: (jnp.maximum(chunk_of(i) * t_h - 1, 0), 0)),
            pl.BlockSpec((halo, d), lambda i: (jnp.minimum((chunk_of(i) + 1) * t_h, n_h - 1), 0)),
            full2((CONV_W, d)), full2((1, d)),
            full3((LRU_BLOCKS, LRU_BLOCK, LRU_BLOCK)), full2((1, d)),
            full3((LRU_BLOCKS, LRU_BLOCK, LRU_BLOCK)), full2((1, d)),
            full2((1, d)),
        ] + extra_specs,
        out_specs=pl.BlockSpec((t, d), lambda i: (chunk_of(i), 0)),
        scratch_shapes=[tile_major, tile_major, tile_major, pltpu.VMEM((1, d), F32)],
        compiler_params=_params("arbitrary"),
        name=("lru_scan_bwd" if reverse else "lru_scan_fwd") + ("_out" if fuse_out is not None else ""),
    )(u, u, u, cw, cb, wa_half, ba_half, wx_half, bx_half, lam, *extra_args)


def _gelu_tanh(y):
    c = 0.7978845608028654
    return y * (0.5 * (1.0 + jnp.tanh(c * (y + 0.044715 * (y * y * y)))))


def _pack_bf16_pairs(x):
    half = x.shape[1] // 2
    lo = pltpu.bitcast(x[:, :half].astype(BF16).astype(F32), U32)
    hi = pltpu.bitcast(x[:, half:].astype(BF16).astype(F32), U32)
    return pltpu.bitcast((lo >> 16) | (hi & jnp.uint32(0xFFFF0000)), I32)


def _unpack_bf16_pairs(packed, dtype=BF16):
    u = pltpu.bitcast(packed, U32)
    lo = pltpu.bitcast(u << 16, F32).astype(dtype)
    hi = pltpu.bitcast(u & jnp.uint32(0xFFFF0000), F32).astype(dtype)
    return jnp.concatenate([lo, hi], axis=-1)


def _router_kernel(h_ref, wt_ref, b_ref, real_ref, idx_ref, gate_ref, rank_ref, cnt_ref, hp_ref, base_sc):
    tm = h_ref.shape[0]
    h = h_ref[...]
    hp_ref[...] = _pack_bf16_pairs(h)

    @pl.when(pl.program_id(0) == 0)
    def _():
        base_sc[...] = jnp.zeros_like(base_sc)

    wt = wt_ref[...]
    h_hi = h.astype(BF16)
    h_lo = (h - h_hi.astype(F32)).astype(BF16)
    w_hi = wt.astype(BF16)
    w_lo = (wt - w_hi.astype(F32)).astype(BF16)
    contract_last = (((1,), (1,)), ((), ()))
    mm = lambda a, b: lax.dot_general(a, b, contract_last, preferred_element_type=F32)
    logits = mm(w_hi, h_hi) + (mm(w_hi, h_lo) + mm(w_lo, h_hi)) + b_ref[...]

    expert = lax.broadcasted_iota(I32, (N_EXPERTS, tm), 0).astype(F32)
    vals = logits
    idx_rows, val_rows, onehots = [], [], []
    for _ in range(TOP_K):
        m = jnp.max(vals, axis=0, keepdims=True)
        idx = jnp.min(jnp.where(vals == m, expert, float(N_EXPERTS)), axis=0, keepdims=True)
        hit = expert == idx
        onehots.append(jnp.where(hit, real_ref[...], 0.0))
        idx_rows.append(idx)
        val_rows.append(m)
        vals = jnp.where(hit, -jnp.inf, vals)

    onehot_all = jnp.concatenate(onehots, axis=0)
    t_from = lax.broadcasted_iota(I32, (tm, tm), 0)
    t_to = lax.broadcasted_iota(I32, (tm, tm), 1)
    earlier = jnp.where(t_from < t_to, 1.0, 0.0).astype(BF16)
    before = jnp.dot(onehot_all.astype(BF16), earlier, preferred_element_type=F32)
    base = base_sc[...]
    starts = []
    for k in range(TOP_K):
        starts.append(base)
        base = base + jnp.sum(onehots[k], axis=1, keepdims=True)
    contrib = onehot_all * (jnp.concatenate(starts, axis=0) + before)
    rank_rows = [jnp.sum(contrib[k * N_EXPERTS:(k + 1) * N_EXPERTS], axis=0, keepdims=True)
                 for k in range(TOP_K)]

    top_vals = jnp.concatenate(val_rows, axis=0)
    e = jnp.exp(top_vals - top_vals[0:1])
    idx_ref[...] = jnp.concatenate(idx_rows, axis=0).astype(I32)
    gate_ref[...] = e / jnp.sum(e, axis=0, keepdims=True)
    rank_ref[...] = jnp.concatenate(rank_rows, axis=0).astype(I32)
    base_sc[...] = base
    cnt_ref[...] = base.astype(I32)


def _router(h, w, b, real):
    n, d = h.shape
    tm = ROUTE_TILE
    out4 = lambda dt: jax.ShapeDtypeStruct((TOP_K, n), dt)
    spec4 = pl.BlockSpec((TOP_K, tm), lambda i: (0, i))
    return pl.pallas_call(
        _router_kernel,
        out_shape=(out4(I32), out4(F32), out4(I32), jax.ShapeDtypeStruct((N_EXPERTS, 1), I32),
                   jax.ShapeDtypeStruct((n, d // 2), I32)),
        grid=(n // tm,),
        in_specs=[pl.BlockSpec((tm, d), lambda i: (i, 0)),
                  pl.BlockSpec((N_EXPERTS, d), lambda i: (0, 0)),
                  pl.BlockSpec((N_EXPERTS, 1), lambda i: (0, 0)),
                  pl.BlockSpec((1, tm), lambda i: (0, i))],
        out_specs=(spec4, spec4, spec4, pl.BlockSpec((N_EXPERTS, 1), lambda i: (0, 0)),
                   pl.BlockSpec((tm, d // 2), lambda i: (i, 0))),
        scratch_shapes=[pltpu.VMEM((N_EXPERTS, 1), F32)],
        compiler_params=_params("arbitrary"),
        name="router",
    )(h, w.T, b[:, None], real)


def _sc_gather_rows(table, idx, name):
    rows, width = table.shape
    total = idx.shape[0]
    per_worker = total // SC_WORKERS
    max_chunk = min(SC_MAX_INDICES, SC_ROW_BUFFER_BYTES // (width * table.dtype.itemsize))
    chunk = max(c for c in range(SUBLANES, max_chunk + 1, SUBLANES) if per_worker % (2 * c) == 0)
    n_chunks = per_worker // chunk
    assert total == SC_WORKERS * n_chunks * chunk and n_chunks % 2 == 0
    mesh = plsc.VectorSubcoreMesh(core_axis_name="c", subcore_axis_name="s",
                                  num_cores=SC_CORES, num_subcores=SC_SUBCORES)

    def body(table_hbm, idx_hbm, out_hbm, idx_v, rows0, rows1, gsem0, gsem1, psem0, psem1):
        worker = lax.axis_index("s") * SC_CORES + lax.axis_index("c")
        base = worker * per_worker
        pltpu.sync_copy(idx_hbm.at[worker], idx_v)
        bufs = (rows0, rows1)
        gsems = (gsem0, gsem1)
        psems = (psem0, psem1)

        def gather(c, slot):
            return pltpu.make_async_copy(table_hbm.at[idx_v.at[c]], bufs[slot], gsems[slot])

        def put(c, slot):
            return pltpu.make_async_copy(bufs[slot], out_hbm.at[pl.ds(base + c * chunk, chunk)], psems[slot])

        gather(0, 0).start()

        @pl.loop(0, n_chunks, step=2)
        def _(c0):
            for slot in range(2):
                c = c0 + slot

                @pl.when(c + 1 < n_chunks)
                def _():
                    @pl.when(c >= 1)
                    def _():
                        put(c - 1, 1 - slot).wait()
                    gather(c + 1, 1 - slot).start()

                gather(c, slot).wait()
                put(c, slot).start()

        put(n_chunks - 2, 0).wait()
        put(n_chunks - 1, 1).wait()

    return pl.kernel(
        body,
        out_type=jax.ShapeDtypeStruct((total, width), table.dtype),
        mesh=mesh,
        scratch_types=[pltpu.VMEM((n_chunks, chunk), I32),
                       pltpu.VMEM((chunk, width), table.dtype), pltpu.VMEM((chunk, width), table.dtype),
                       pltpu.SemaphoreType.DMA, pltpu.SemaphoreType.DMA,
                       pltpu.SemaphoreType.DMA, pltpu.SemaphoreType.DMA],
        name=name,
    )(table, idx.reshape(SC_WORKERS, n_chunks, chunk))


def _sc_scatter_rows(src, dest, out_rows, name):
    n, width = src.shape
    fan = dest.shape[0]
    per_worker = n // SC_WORKERS
    chunk = max(c for c in range(SUBLANES, SC_MAX_INDICES + 1, SUBLANES) if per_worker % (2 * c) == 0)
    n_chunks = per_worker // chunk
    assert n == SC_WORKERS * n_chunks * chunk and n_chunks % 2 == 0
    mesh = plsc.VectorSubcoreMesh(core_axis_name="c", subcore_axis_name="s",
                                  num_cores=SC_CORES, num_subcores=SC_SUBCORES)

    def body(src_hbm, idx_hbm, out_hbm, idx_v, rows0, rows1, lsem0, lsem1, ssem0, ssem1):
        worker = lax.axis_index("s") * SC_CORES + lax.axis_index("c")
        base = worker * per_worker
        pltpu.sync_copy(idx_hbm.at[worker], idx_v)
        bufs = (rows0, rows1)
        lsems = (lsem0, lsem1)
        ssems = (ssem0, ssem1)

        def load(c, slot):
            return pltpu.make_async_copy(src_hbm.at[pl.ds(base + c * chunk, chunk)], bufs[slot], lsems[slot])

        def scatter(c, k, slot):
            return pltpu.make_async_copy(bufs[slot], out_hbm.at[idx_v.at[k * n_chunks + c]], ssems[slot])

        load(0, 0).start()

        @pl.loop(0, n_chunks, step=2)
        def _(c0):
            for slot in range(2):
                c = c0 + slot

                @pl.when(c + 1 < n_chunks)
                def _():
                    @pl.when(c >= 1)
                    def _():
                        for k in range(fan):
                            scatter(c - 1, k, 1 - slot).wait()
                    load(c + 1, 1 - slot).start()

                load(c, slot).wait()
                for k in range(fan):
                    scatter(c, k, slot).start()

        for k in range(fan):
            scatter(n_chunks - 2, k, 0).wait()
        for k in range(fan):
            scatter(n_chunks - 1, k, 1).wait()

    idx = dest.reshape(fan, SC_WORKERS, n_chunks, chunk).transpose(1, 0, 2, 3)
    return pl.kernel(
        body,
        out_type=jax.ShapeDtypeStruct((out_rows, width), src.dtype),
        mesh=mesh,
        scratch_types=[pltpu.VMEM((fan * n_chunks, chunk), I32),
                       pltpu.VMEM((chunk, width), src.dtype), pltpu.VMEM((chunk, width), src.dtype),
                       pltpu.SemaphoreType.DMA, pltpu.SemaphoreType.DMA,
                       pltpu.SemaphoreType.DMA, pltpu.SemaphoreType.DMA],
        name=name,
    )(src, idx.reshape(SC_WORKERS, fan * n_chunks, chunk))


def _expert_kernel(be_ref, valid_ref, xs_ref, wgu_ref, bgu_ref, wd_ref, bd_ref, y_ref, wgu_sc, wd_sc):
    i = pl.program_id(0)
    e = be_ref[i]
    e_prev = be_ref[jnp.maximum(i - 1, 0)]
    d = wd_sc.shape[0]

    @pl.when((i == 0) | (e != e_prev))
    def _():
        for r in range(0, d, LANES):
            wgu_sc[r:r + LANES, :] = wgu_ref[0, r:r + LANES, :].astype(BF16)
            wd_sc[r:r + LANES, :] = wd_ref[0, r:r + LANES, :].astype(BF16)

    valid = valid_ref[i]

    @pl.when(valid > 0)
    def _():
        row = lax.broadcasted_iota(I32, xs_ref.shape, 0)
        x = _unpack_bf16_pairs(jnp.where(row < valid, xs_ref[...], 0))
        h = jnp.dot(x, wgu_sc[...], preferred_element_type=F32) + bgu_ref[0]
        glu = jnp.minimum(h[:, :d], SWIGLU_LIMIT)
        lin = jnp.clip(h[:, d:], -SWIGLU_LIMIT, SWIGLU_LIMIT)
        act = glu * _sigmoid(SWIGLU_ALPHA * glu) * (lin + 1.0)
        y = jnp.dot(act.astype(BF16), wd_sc[...], preferred_element_type=F32) + bd_ref[0]
        y_ref[...] = _pack_bf16_pairs(y)

    @pl.when(valid == 0)
    def _():
        y_ref[...] = jnp.zeros_like(y_ref)


def _experts(block_e, valid_rows, xs, w_gu, b_gu, w_down, b_down, layer):
    cap = xs.shape[0]
    d = D_MODEL
    tm = EXPERT_TILE
    n_blocks = cap // tm
    grid_spec = pltpu.PrefetchScalarGridSpec(
        num_scalar_prefetch=2,
        grid=(n_blocks,),
        in_specs=[
            pl.BlockSpec((tm, d // 2), lambda i, be, nu: (i, 0)),
            pl.BlockSpec((None, 1, d, 2 * d), lambda i, be, nu: (layer, be[i], 0, 0)),
            pl.BlockSpec((1, 1, 2 * d), lambda i, be, nu: (be[i], 0, 0)),
            pl.BlockSpec((None, 1, d, d), lambda i, be, nu: (layer, be[i], 0, 0)),
            pl.BlockSpec((1, 1, d), lambda i, be, nu: (be[i], 0, 0)),
        ],
        out_specs=pl.BlockSpec((tm, d // 2), lambda i, be, nu: (i, 0)),
        scratch_shapes=[pltpu.VMEM((d, 2 * d), BF16), pltpu.VMEM((d, d), BF16)],
    )
    return pl.pallas_call(
        _expert_kernel,
        out_shape=jax.ShapeDtypeStruct((cap, d // 2), I32),
        grid_spec=grid_spec,
        compiler_params=_params("arbitrary"),
        name="moe_experts",
    )(block_e, valid_rows, xs, w_gu, b_gu[layer][:, None, :], w_down, b_down[layer][:, None, :])


def _combine_kernel(g_ref, b_ref, *refs, n_sub):
    o_ref = refs[-1]
    rows = o_ref.shape[0] // n_sub
    for s in range(n_sub):
        gate_ref, res_ref, yk_ref = refs[3 * s:3 * s + 3]
        gates = gate_ref[...]
        ffn = gates[:, 0:1] * _unpack_bf16_pairs(yk_ref[0], F32)
        for k in range(1, TOP_K):
            ffn = ffn + gates[:, k:k + 1] * _unpack_bf16_pairs(yk_ref[k], F32)
        o_ref[s * rows:(s + 1) * rows, :] = _layer_norm(DN_ALPHA * res_ref[...] + ffn, g_ref[...], b_ref[...])


def _combine(gates, res, yk, g, b, part, acc, drop_front=None):
    n, d = res.shape
    if drop_front is None:
        tt, n_sub = MOVE_TILE, 1
        steps = n // COLLECT_PARTS // tt
        grid = (steps,)
        local = lambda s, i: i
        src = lambda s, i: part * steps + i
        dst = lambda i: part * steps + i
        out_rows = n
    else:
        batch, lp = drop_front
        tt, n_sub = LANES, 2
        tiles = lp // tt
        part_batch = batch // COLLECT_PARTS
        pairs = (tiles - 1) // n_sub
        assert pairs * n_sub == tiles - 1
        grid = (part_batch, pairs)
        local = lambda s, b_, j: b_ * tiles + 1 + n_sub * j + s
        src = lambda s, b_, j: (part * part_batch + b_) * tiles + 1 + n_sub * j + s
        dst = lambda b_, j: (part * part_batch + b_) * pairs + j
        out_rows = n - batch * tt
    vec = pl.BlockSpec((1, d), lambda *_: (0, 0))
    in_specs = [vec, vec]
    args = (g, b)
    for s in range(n_sub):
        in_specs += [pl.BlockSpec((tt, TOP_K), lambda *i, s=s: (local(s, *i), 0)),
                     pl.BlockSpec((tt, d), lambda *i, s=s: (src(s, *i), 0)),
                     pl.BlockSpec((TOP_K, tt, d // 2), lambda *i, s=s: (0, local(s, *i), 0))]
        args += (gates, res, yk)
    aliases = {}
    if acc is not None:
        in_specs.append(pl.BlockSpec(memory_space=pl.ANY))
        args += (acc,)
        aliases = {len(args) - 1: 0}
    return pl.pallas_call(
        functools.partial(_combine_kernel, n_sub=n_sub),
        out_shape=jax.ShapeDtypeStruct((out_rows, d), F32),
        grid=grid,
        in_specs=in_specs,
        out_specs=pl.BlockSpec((n_sub * tt, d), lambda *i: (dst(*i), 0)),
        input_output_aliases=aliases,
        compiler_params=_params(*(["parallel"] * len(grid))),
        name="moe_combine",
    )(*args)


def _moe_layer(h, router_w, router_b, w_gu, b_gu, w_down, b_down, ln_g, ln_b, layer, lp, drop_front=None):
    n, d = h.shape
    tm = EXPERT_TILE
    token = jnp.arange(n, dtype=I32)
    real = (token % lp) >= FRONT_PAD
    idx, gates, rank, counts, h_packed = _router(h, router_w, router_b, real.astype(F32)[None, :])
    counts = counts[:, 0]
    padded = (counts + tm - 1) // tm * tm
    pad_end = jnp.cumsum(padded)
    pad_start = pad_end - padded
    n_real = n - (n // lp) * FRONT_PAD
    n_blocks = -(-(n_real * TOP_K + N_EXPERTS * (tm - 1)) // tm) + 1
    group_start = jnp.sum(jnp.where(idx[:, :, None] == jnp.arange(N_EXPERTS, dtype=I32), pad_start, 0), axis=-1)
    dest = jnp.where(real, group_start + rank, (n_blocks - 1) * tm + token % tm)
    block_start = jnp.arange(n_blocks, dtype=I32) * tm
    block_e = jnp.minimum(jnp.sum((pad_end[None, :] <= block_start[:, None]).astype(I32), axis=1),
                          N_EXPERTS - 1)
    group_end = (pad_start + counts)[block_e]
    valid_rows = jnp.where(block_start < pad_end[-1], jnp.clip(group_end - block_start, 0, tm), 0).astype(I32)

    xs = _sc_scatter_rows(h_packed, dest, n_blocks * tm, "moe_dispatch")
    ys = _experts(block_e, valid_rows, xs, w_gu, b_gu, w_down, b_down, layer)
    n_part = n // COLLECT_PARTS
    assert n_part % lp == 0
    out = None
    for part in range(COLLECT_PARTS):
        rows = slice(part * n_part, (part + 1) * n_part)
        yk = _sc_gather_rows(ys, dest[:, rows].reshape(TOP_K * n_part), "moe_collect")
        out = _combine(gates[:, rows].T, h, yk.reshape(TOP_K, n_part, d // 2), ln_g[None, :], ln_b[None, :],
                       part, out, drop_front)
    return out


def _qkv_kernel(x_ref, w_ref, o_ref):
    xb = x_ref[...].astype(BF16)
    d = x_ref.shape[1]
    for part in range(3):
        acc = jnp.dot(xb, w_ref[:, part * d:(part + 1) * d], preferred_element_type=F32)
        if part == 0:
            acc = acc * (NA_HEAD_DIM ** -0.5)
        for p in range(HEAD_PAIRS):
            o_ref[part * HEAD_PAIRS + p] = acc[:, p * LANES:(p + 1) * LANES].astype(BF16)


def _qkv(x, w_bf16):
    n, d = x.shape
    return pl.pallas_call(
        _qkv_kernel,
        out_shape=jax.ShapeDtypeStruct((3 * HEAD_PAIRS, n, LANES), BF16),
        grid=(n // ROW_TILE,),
        in_specs=[pl.BlockSpec((ROW_TILE, d), lambda i: (i, 0)),
                  pl.BlockSpec((d, 3 * d), lambda i: (0, 0))],
        out_specs=pl.BlockSpec((3 * HEAD_PAIRS, ROW_TILE, LANES), lambda i: (0, i, 0)),
        compiler_params=_params("parallel"),
        name="na_qkv",
    )(x, w_bf16)


def _na_kernel(q_ref, k_ref, v_ref, bias_a_ref, bias_b_ref, mb_ref, o_ref, *, rows):
    s = pl.program_id(1)
    w = GRID_W
    low = lax.broadcasted_iota(I32, (w, LANES), 1) < NA_HEAD_DIM
    contract_last = (((1,), (1,)), ((), ()))
    meta0 = FRONT_PAD

    def stacked_q(p, q0):
        qp = q_ref[p, q0:q0 + w, :]
        zero = jnp.zeros_like(qp)
        return jnp.concatenate([jnp.where(low, qp, zero), jnp.where(low, zero, qp)], axis=0)

    def attend_all(q0, k0, bias_ref):
        scores = []
        for p in range(HEAD_PAIRS):
            q2 = stacked_q(p, q0)
            s_meta = lax.dot_general(q2, k_ref[p, meta0:meta0 + N_META, :], contract_last,
                                     preferred_element_type=F32) + mb_ref[p]
            s_win = None
            if k0 is not None:
                s_win = lax.dot_general(q2, k_ref[p, pl.ds(k0, NA_KH * w), :], contract_last,
                                        preferred_element_type=F32) + bias_ref[0, p]
            scores.append((s_meta, s_win))
        outs = []
        for p in range(HEAD_PAIRS):
            s_meta, s_win = scores[p]
            m = jnp.max(s_meta, axis=-1, keepdims=True)
            if s_win is not None:
                m = jnp.maximum(m, jnp.max(s_win, axis=-1, keepdims=True))
            p_meta = jnp.exp(s_meta - m)
            denom = jnp.sum(p_meta, axis=-1, keepdims=True)
            o = jnp.dot(p_meta.astype(BF16), v_ref[p, meta0:meta0 + N_META, :], preferred_element_type=F32)
            if s_win is not None:
                p_win = jnp.exp(s_win - m)
                denom = denom + jnp.sum(p_win, axis=-1, keepdims=True)
                o = o + jnp.dot(p_win.astype(BF16), v_ref[p, pl.ds(k0, NA_KH * w), :],
                                preferred_element_type=F32)
            o = o / denom
            outs.append(jnp.where(low, o[:w], o[w:]))
        return outs

    @pl.when(s == 0)
    def _():
        qrow = lax.broadcasted_iota(I32, (w, LANES), 0)
        for p, o in enumerate(attend_all(LANES - w, None, None)):
            o_ref[p, 0:LANES - w, :] = jnp.zeros((LANES - w, LANES), o_ref.dtype)
            o_ref[p, LANES - w:LANES, :] = jnp.where(qrow >= w - N_META, o, 0.0).astype(o_ref.dtype)

    @pl.when(s >= 1)
    def _():
        for half, bias_ref in enumerate((bias_a_ref, bias_b_ref)):
            r = 2 * (s - 1) + half
            rs = jnp.clip(r - NA_KH // 2, 0, rows - NA_KH)
            k0 = pl.multiple_of(LANES + rs * w, w)
            for p, o in enumerate(attend_all(half * w, k0, bias_ref)):
                o_ref[p, half * w:(half + 1) * w, :] = o.astype(o_ref.dtype)


def _na_bias_table(rpb, rows):
    del rows
    w = GRID_W
    q = jnp.arange(w)
    col_start = jnp.clip(q - NA_KW // 2, 0, w - NA_KW)
    c = jnp.arange(w)
    in_win = (c[None, :] >= col_start[:, None]) & (c[None, :] < col_start[:, None] + NA_KW)
    pad = w - NA_KW
    rp = jnp.pad(rpb.astype(F32), ((0, 0), (0, 0), (pad, pad)))
    toeplitz = jnp.stack([rp[:, :, w - 1 - qq:2 * w - 1 - qq] for qq in range(w)], axis=2)
    toeplitz = jnp.where(in_win[None, None], toeplitz, NEG_BIG)
    tabs = [jnp.concatenate([toeplitz[:, NA_KH - 1 - v + j] for j in range(NA_KH)], axis=-1) for v in range(NA_KH)]
    return jnp.stack(tabs, axis=0)


def _na_attention(qkv, bias_tab, meta_bias, *, batch, lp):
    n = qkv.shape[1]
    w = GRID_W
    rows = (lp - LANES) // w
    tiles = lp // LANES

    def variant(half):
        def index(b, s):
            r = jnp.clip(2 * (s - 1) + half, 0, rows - 1)
            return (r - jnp.clip(r - NA_KH // 2, 0, rows - NA_KH), 0, 0, 0)
        return index

    kern = functools.partial(_na_kernel, rows=rows)
    bias_block = (1, HEAD_PAIRS, 2 * w, NA_KH * w)
    bias_tab = bias_tab.reshape(NA_KH, HEAD_PAIRS, 2 * w, NA_KH * w)
    return pl.pallas_call(
        kern,
        out_shape=jax.ShapeDtypeStruct((HEAD_PAIRS, n, LANES), BF16),
        grid=(batch, tiles),
        in_specs=[
            pl.BlockSpec((HEAD_PAIRS, LANES, LANES), lambda b, s: (0, b * tiles + s, 0)),
            pl.BlockSpec((HEAD_PAIRS, lp, LANES), lambda b, s: (1, b, 0)),
            pl.BlockSpec((HEAD_PAIRS, lp, LANES), lambda b, s: (2, b, 0)),
            pl.BlockSpec(bias_block, variant(0)),
            pl.BlockSpec(bias_block, variant(1)),
            pl.BlockSpec((HEAD_PAIRS, 2 * w, N_META), lambda b, s: (0, 0, 0)),
        ],
        out_specs=pl.BlockSpec((HEAD_PAIRS, LANES, LANES), lambda b, s: (0, b * tiles + s, 0)),
        compiler_params=_params("parallel", "arbitrary"),
        name="na_attention",
    )(qkv, qkv, qkv, bias_tab, bias_tab,
      jnp.repeat(meta_bias, w, axis=0).reshape(HEAD_PAIRS, 2 * w, N_META))


def _na_out_kernel(o_ref, res_ref, w_ref, g_ref, b_ref, out_ref):
    att = jnp.concatenate([o_ref[p] for p in range(HEAD_PAIRS)], axis=-1)
    mix = jnp.dot(att, w_ref[...], preferred_element_type=F32)
    out_ref[...] = _layer_norm(DN_ALPHA * res_ref[...] + mix, g_ref[...], b_ref[...])


def _na_out(o, res, w_bf16, g, b):
    n, d = res.shape
    vec = pl.BlockSpec((1, d), lambda i: (0, 0))
    return pl.pallas_call(
        _na_out_kernel,
        out_shape=jax.ShapeDtypeStruct((n, d), F32),
        grid=(n // ROW_TILE,),
        in_specs=[pl.BlockSpec((HEAD_PAIRS, ROW_TILE, LANES), lambda i: (0, i, 0)),
                  pl.BlockSpec((ROW_TILE, d), lambda i: (i, 0)),
                  pl.BlockSpec((d, d), lambda i: (0, 0)), vec, vec],
        out_specs=pl.BlockSpec((ROW_TILE, d), lambda i: (i, 0)),
        compiler_params=_params("parallel"),
        name="na_out",
    )(o, res, w_bf16, g, b)


def kernel(x, meta_tokens, lru_w_in, lru_conv_w, lru_conv_b, lru_wa, lru_ba, lru_wx, lru_bx, lru_lambda, lru_w_out, na_w_qkv, na_rpb, na_meta_bias, na_w_out, ln_mix_g, ln_mix_b, router_w, router_b, moe_w_gu, moe_b_gu, moe_w_down, moe_b_down, ln_ffn_g, ln_ffn_b):
    batch, seq, d = x.shape
    lp = LANES + seq
    assert d == D_MODEL and seq % (2 * GRID_W) == 0 and seq // GRID_W >= NA_KH
    assert lp % SCAN_CHUNK == 0 and (batch * lp) % ROW_TILE == 0
    n = batch * lp

    front = jnp.zeros((batch, FRONT_PAD, d), x.dtype)
    meta = jnp.broadcast_to(meta_tokens[None].astype(x.dtype), (batch, N_META, d))
    h = jnp.concatenate([front, meta, x], axis=1).reshape(n, d)

    u = _matmul(h, lru_w_in[0].astype(BF16))
    row = lambda v: v[None, :]
    def scan(direction, reverse, fuse_out=None):
        return _lru_scan(
            u, lru_conv_w[0], row(lru_conv_b[0]),
            (0.5 * lru_wa[0, direction]).astype(BF16), row(0.5 * lru_ba[0, direction]),
            (0.5 * lru_wx[0, direction]).astype(BF16), row(0.5 * lru_bx[0, direction]),
            row(lru_lambda[0, direction]), reverse=reverse, lp=lp, fuse_out=fuse_out)

    h_fwd = scan(0, False)
    h = scan(1, True, fuse_out=(h_fwd, h, lru_w_out[0].astype(BF16), row(ln_mix_g[0]), row(ln_mix_b[0])))
    h = _moe_layer(h, router_w[0], router_b[0], moe_w_gu, moe_b_gu, moe_w_down, moe_b_down,
                   ln_ffn_g[0], ln_ffn_b[0], 0, lp)

    qkv = _qkv(h, na_w_qkv[0].astype(BF16))
    att = _na_attention(qkv, _na_bias_table(na_rpb[0], seq // GRID_W), na_meta_bias[0].astype(F32),
                        batch=batch, lp=lp)
    h = _na_out(att, h, na_w_out[0].astype(BF16), row(ln_mix_g[1]), row(ln_mix_b[1]))
    h = _moe_layer(h, router_w[1], router_b[1], moe_w_gu, moe_b_gu, moe_w_down, moe_b_down,
                   ln_ffn_g[1], ln_ffn_b[1], 1, lp, drop_front=(batch, lp))

    return h.reshape(batch, seq, d)
```

```python
import functools

import jax
import jax.numpy as jnp
from jax import lax
from jax.experimental import pallas as pl
from jax.experimental.pallas import tpu as pltpu
from jax.experimental.pallas import tpu_sc as plsc

F32 = jnp.float32
BF16 = jnp.bfloat16
I32 = jnp.int32
U32 = jnp.uint32

D_MODEL = 1024
N_META = 16
GRID_W = 64
LRU_BLOCKS = 4
LRU_BLOCK = D_MODEL // LRU_BLOCKS
CONV_W = 4
LRU_C = 8.0
NA_HEADS = 16
NA_HEAD_DIM = D_MODEL // NA_HEADS
NA_KH = 8
NA_KW = 16
N_EXPERTS = 32
TOP_K = 4
SWIGLU_LIMIT = 7.0
SWIGLU_ALPHA = 1.702
DEPTH = 2
DN_ALPHA = (2.0 * DEPTH) ** 0.25
LN_EPS = 1e-5

LANES = 128
SUBLANES = 8
FRONT_PAD = LANES - N_META
HEAD_PAIRS = D_MODEL // LANES
NEG_BIG = -1e30

ROW_TILE = 1024
SCAN_CHUNK = 352
ROUTE_TILE = 512
MOVE_TILE = 256
EXPERT_TILE = 512
COLLECT_PARTS = 2
VMEM_LIMIT = 56 << 20

SC_CORES = 2
SC_SUBCORES = 16
SC_WORKERS = SC_CORES * SC_SUBCORES
SC_MAX_INDICES = 64
SC_ROW_BUFFER_BYTES = 128 << 10


def _params(*sem):
    return pltpu.CompilerParams(dimension_semantics=sem, vmem_limit_bytes=VMEM_LIMIT)


def _sigmoid(x):
    return 0.5 * jnp.tanh(0.5 * x) + 0.5


def _layer_norm(x, g, b):
    mu = jnp.mean(x, axis=-1, keepdims=True)
    xc = x - mu
    var = jnp.mean(xc * xc, axis=-1, keepdims=True)
    return xc * lax.rsqrt(var + LN_EPS) * g + b


def _matmul_kernel(x_ref, w_ref, o_ref):
    o_ref[...] = jnp.dot(x_ref[...].astype(BF16), w_ref[...], preferred_element_type=F32).astype(o_ref.dtype)


def _matmul(x, w_bf16):
    n, k = x.shape
    m = w_bf16.shape[1]
    return pl.pallas_call(
        _matmul_kernel,
        out_shape=jax.ShapeDtypeStruct((n, m), BF16),
        grid=(n // ROW_TILE,),
        in_specs=[pl.BlockSpec((ROW_TILE, k), lambda i: (i, 0)),
                  pl.BlockSpec((k, m), lambda i: (0, 0))],
        out_specs=pl.BlockSpec((ROW_TILE, m), lambda i: (i, 0)),
        compiler_params=_params("parallel"),
        name="in_proj",
    )(x, w_bf16)


def _lru_scan_kernel(xr_ref, xp_ref, xn_ref, cw_ref, cb_ref, wa_ref, ba_ref, wx_ref, bx_ref,
                     lam_ref, *rest, reverse, chunks_per_batch, fused):
    if fused:
        hf_ref, y_ref, res_ref, wout_ref, g_ref, beta_ref, h_ref, a_sc, b_sc, h_sc, carry_sc = rest
    else:
        h_ref, a_sc, b_sc, h_sc, carry_sc = rest
    t_rows = xr_ref.shape[0]
    seg = t_rows // SUBLANES
    halo = xp_ref.shape[0]
    lane_tiles = xr_ref.shape[1] // LANES
    step = pl.program_id(0)
    chunk = (pl.num_programs(0) - 1 - step) if reverse else step
    cib = chunk % chunks_per_batch
    first_chunk = cib == 0
    last_chunk = cib == chunks_per_batch - 1
    row8 = lax.broadcasted_iota(I32, (SUBLANES, 1), 0)
    cat = lambda parts: jnp.concatenate(parts, axis=0)

    head_pos = cib * t_rows + lax.broadcasted_iota(I32, (LANES, 1), 0)
    head_real = head_pos >= FRONT_PAD
    xr = xr_ref[...].astype(F32)
    xr = cat([jnp.where(head_real, xr[:LANES], 0.0), xr[LANES:]])
    prev_pos = cib * t_rows - halo + lax.broadcasted_iota(I32, (halo, 1), 0)
    prev = jnp.where(prev_pos >= FRONT_PAD, xp_ref[...].astype(F32), 0.0)
    nxt = jnp.where(last_chunk, 0.0, xn_ref[...].astype(F32))

    xm1 = pltpu.roll(xr, 1, 0)
    xm2 = pltpu.roll(xr, 2, 0)
    xp1 = pltpu.roll(xr, t_rows - 1, 0)
    xm1 = cat([jnp.where(row8 == 0, prev[halo - 1:halo], xm1[:SUBLANES]), xm1[SUBLANES:]])
    xm2_head = jnp.where(row8 == 0, prev[halo - 2:halo - 1],
                         jnp.where(row8 == 1, prev[halo - 1:halo], xm2[:SUBLANES]))
    xm2 = cat([xm2_head, xm2[SUBLANES:]])
    xp1 = cat([xp1[:t_rows - SUBLANES], jnp.where(row8 == SUBLANES - 1, nxt[0:1], xp1[t_rows - SUBLANES:])])
    cw = cw_ref[...]
    xc = cw[0:1, :] * xm2 + cw[1:2, :] * xm1 + cw[2:3, :] * xr + cw[3:4, :] * xp1 + cb_ref[...]

    xcb = xc.astype(BF16)
    za, zx = [], []
    for blk in range(LRU_BLOCKS):
        xblk = xcb[:, blk * LRU_BLOCK:(blk + 1) * LRU_BLOCK]
        za.append(jnp.dot(xblk, wa_ref[blk], preferred_element_type=F32))
        zx.append(jnp.dot(xblk, wx_ref[blk], preferred_element_type=F32))
    tanh_a = jnp.tanh(jnp.concatenate(za, axis=-1) + ba_ref[...])
    tanh_x = jnp.tanh(jnp.concatenate(zx, axis=-1) + bx_ref[...])

    z = -lam_ref[...]
    softplus = jnp.maximum(z, 0.0) + jnp.log1p(jnp.exp(-jnp.abs(z)))
    half_rate = (-0.5 * LRU_C) * softplus
    log_a = tanh_a * half_rate + half_rate
    a = jnp.exp(log_a)
    gap = jnp.tanh(log_a) * (-1.0 - a * a)
    mult = jnp.where(gap > 0.0, gap * lax.rsqrt(gap), 0.0)
    if reverse:
        tail = jnp.where(jnp.logical_and(last_chunk, row8 == SUBLANES - 1), 1.0, mult[t_rows - SUBLANES:])
        mult = cat([mult[:t_rows - SUBLANES], tail])
    else:
        mult = cat([jnp.where(head_pos == FRONT_PAD, 1.0, mult[:LANES]), mult[LANES:]])
    b = mult * ((0.5 * tanh_x + 0.5) * xc)
    b = cat([jnp.where(head_real, b[:LANES], 0.0), b[LANES:]])
    for c in range(lane_tiles):
        a_sc[c] = a[:, c * LANES:(c + 1) * LANES]
        b_sc[c] = b[:, c * LANES:(c + 1) * LANES]

    @pl.when(last_chunk if reverse else first_chunk)
    def _():
        carry_sc[...] = jnp.zeros_like(carry_sc)

    order = range(seg - 1, -1, -1) if reverse else range(seg)
    rows_of = lambda j: pl.ds(j, SUBLANES, stride=seg)

    local = [jnp.zeros((SUBLANES, LANES), F32)] * lane_tiles
    prod = [jnp.ones((SUBLANES, LANES), F32)] * lane_tiles
    for j in order:
        for c in range(lane_tiles):
            av = a_sc[c, rows_of(j), :]
            local[c] = av * local[c] + b_sc[c, rows_of(j), :]
            prod[c] = av * prod[c]
            h_sc[c, rows_of(j), :] = local[c]
            a_sc[c, rows_of(j), :] = prod[c]

    carry_in = carry_sc[...]
    carry_out = []
    seg_carry = []
    for c in range(lane_tiles):
        state = carry_in[:, c * LANES:(c + 1) * LANES]
        rows = [None] * SUBLANES
        for s in (range(SUBLANES - 1, -1, -1) if reverse else range(SUBLANES)):
            rows[s] = state
            state = local[c][s:s + 1] + prod[c][s:s + 1] * state
        seg_carry.append(cat(rows))
        carry_out.append(state)
    carry_sc[...] = jnp.concatenate(carry_out, axis=-1)

    for j in order:
        for c in range(lane_tiles):
            h_sc[c, rows_of(j), :] = h_sc[c, rows_of(j), :] + a_sc[c, rows_of(j), :] * seg_carry[c]
    states = jnp.concatenate([h_sc[c] for c in range(lane_tiles)], axis=-1)
    if fused:
        gated = (hf_ref[...] + states) * _gelu_tanh(y_ref[...].astype(F32))
        mix = jnp.dot(gated.astype(BF16), wout_ref[...], preferred_element_type=F32)
        h_ref[...] = _layer_norm(DN_ALPHA * res_ref[...] + mix, g_ref[...], beta_ref[...])
    else:
        h_ref[...] = states


def _lru_scan(u, cw, cb, wa_half, ba_half, wx_half, bx_half, lam, *, reverse, lp, fuse_out=None):
    n = u.shape[0]
    d = D_MODEL
    t = SCAN_CHUNK
    halo = 2 * SUBLANES
    n_chunks = n // t
    cpb = lp // t
    t_h = t // halo
    n_h = n // halo

    def chunk_of(i):
        return (n_chunks - 1 - i) if reverse else i

    kern = functools.partial(_lru_scan_kernel, reverse=reverse, chunks_per_batch=cpb, fused=fuse_out is not None)
    chunk_rows = lambda col: pl.BlockSpec((t, d), lambda i: (chunk_of(i), col))
    extra_specs, extra_args = [], ()
    if fuse_out is not None:
        h_other, res, w_out, ln_g, ln_b = fuse_out
        extra_specs = [chunk_rows(0), chunk_rows(1), chunk_rows(0),
                       pl.BlockSpec((d, d), lambda i: (0, 0)),
                       pl.BlockSpec((1, d), lambda i: (0, 0)), pl.BlockSpec((1, d), lambda i: (0, 0))]
        extra_args = (h_other, u, res, w_out, ln_g, ln_b)
    full2 = lambda shape: pl.BlockSpec(shape, lambda i: (0, 0))
    full3 = lambda shape: pl.BlockSpec(shape, lambda i: (0, 0, 0))
    tile_major = pltpu.VMEM((d // LANES, t, LANES), F32)
    return pl.pallas_call(
        kern,
        out_shape=jax.ShapeDtypeStruct((n, d), F32),
        grid=(n_chunks,),
        in_specs=[
            pl.BlockSpec((t, d), lambda i: (chunk_of(i), 0)),
            pl.BlockSpec((halo, d), lambda i: (jnp.maximum(chunk_of(i) * t_h - 1, 0), 0)),
            pl.BlockSpec((halo, d), lambda i: (jnp.minimum((chunk_of(i) + 1) * t_h, n_h - 1), 0)),
            full2((CONV_W, d)), full2((1, d)),
            full3((LRU_BLOCKS, LRU_BLOCK, LRU_BLOCK)), full2((1, d)),
            full3((LRU_BLOCKS, LRU_BLOCK, LRU_BLOCK)), full2((1, d)),
            full2((1, d)),
        ] + extra_specs,
        out_specs=pl.BlockSpec((t, d), lambda i: (chunk_of(i), 0)),
        scratch_shapes=[tile_major, tile_major, tile_major, pltpu.VMEM((1, d), F32)],
        compiler_params=_params("arbitrary"),
        name=("lru_scan_bwd" if reverse else "lru_scan_fwd") + ("_out" if fuse_out is not None else ""),
    )(u, u, u, cw, cb, wa_half, ba_half, wx_half, bx_half, lam, *extra_args)


def _gelu_tanh(y):
    c = 0.7978845608028654
    return y * (0.5 * (1.0 + jnp.tanh(c * (y + 0.044715 * (y * y * y)))))


def _pack_bf16_pairs(x):
    half = x.shape[1] // 2
    lo = pltpu.bitcast(x[:, :half].astype(BF16).astype(F32), U32)
    hi = pltpu.bitcast(x[:, half:].astype(BF16).astype(F32), U32)
    return pltpu.bitcast((lo >> 16) | (hi & jnp.uint32(0xFFFF0000)), I32)


def _unpack_bf16_pairs(packed, dtype=BF16):
    u = pltpu.bitcast(packed, U32)
    lo = pltpu.bitcast(u << 16, F32).astype(dtype)
    hi = pltpu.bitcast(u & jnp.uint32(0xFFFF0000), F32).astype(dtype)
    return jnp.concatenate([lo, hi], axis=-1)


def _router_kernel(h_ref, wt_ref, b_ref, real_ref, idx_ref, gate_ref, rank_ref, cnt_ref, hp_ref, base_sc):
    tm = h_ref.shape[0]
    h = h_ref[...]
    hp_ref[...] = _pack_bf16_pairs(h)

    @pl.when(pl.program_id(0) == 0)
    def _():
        base_sc[...] = jnp.zeros_like(base_sc)

    wt = wt_ref[...]
    h_hi = h.astype(BF16)
    h_lo = (h - h_hi.astype(F32)).astype(BF16)
    w_hi = wt.astype(BF16)
    w_lo = (wt - w_hi.astype(F32)).astype(BF16)
    contract_last = (((1,), (1,)), ((), ()))
    mm = lambda a, b: lax.dot_general(a, b, contract_last, preferred_element_type=F32)
    logits = mm(w_hi, h_hi) + (mm(w_hi, h_lo) + mm(w_lo, h_hi)) + b_ref[...]

    expert = lax.broadcasted_iota(I32, (N_EXPERTS, tm), 0).astype(F32)
    vals = logits
    idx_rows, val_rows, onehots = [], [], []
    for _ in range(TOP_K):
        m = jnp.max(vals, axis=0, keepdims=True)
        idx = jnp.min(jnp.where(vals == m, expert, float(N_EXPERTS)), axis=0, keepdims=True)
        hit = expert == idx
        onehots.append(jnp.where(hit, real_ref[...], 0.0))
        idx_rows.append(idx)
        val_rows.append(m)
        vals = jnp.where(hit, -jnp.inf, vals)

    onehot_all = jnp.concatenate(onehots, axis=0)
    t_from = lax.broadcasted_iota(I32, (tm, tm), 0)
    t_to = lax.broadcasted_iota(I32, (tm, tm), 1)
    earlier = jnp.where(t_from < t_to, 1.0, 0.0).astype(BF16)
    before = jnp.dot(onehot_all.astype(BF16), earlier, preferred_element_type=F32)
    base = base_sc[...]
    starts = []
    for k in range(TOP_K):
        starts.append(base)
        base = base + jnp.sum(onehots[k], axis=1, keepdims=True)
    contrib = onehot_all * (jnp.concatenate(starts, axis=0) + before)
    rank_rows = [jnp.sum(contrib[k * N_EXPERTS:(k + 1) * N_EXPERTS], axis=0, keepdims=True)
                 for k in range(TOP_K)]

    top_vals = jnp.concatenate(val_rows, axis=0)
    e = jnp.exp(top_vals - top_vals[0:1])
    idx_ref[...] = jnp.concatenate(idx_rows, axis=0).astype(I32)
    gate_ref[...] = e / jnp.sum(e, axis=0, keepdims=True)
    rank_ref[...] = jnp.concatenate(rank_rows, axis=0).astype(I32)
    base_sc[...] = base
    cnt_ref[...] = base.astype(I32)


def _router(h, w, b, real):
    n, d = h.shape
    tm = ROUTE_TILE
    out4 = lambda dt: jax.ShapeDtypeStruct((TOP_K, n), dt)
    spec4 = pl.BlockSpec((TOP_K, tm), lambda i: (0, i))
    return pl.pallas_call(
        _router_kernel,
        out_shape=(out4(I32), out4(F32), out4(I32), jax.ShapeDtypeStruct((N_EXPERTS, 1), I32),
                   jax.ShapeDtypeStruct((n, d // 2), I32)),
        grid=(n // tm,),
        in_specs=[pl.BlockSpec((tm, d), lambda i: (i, 0)),
                  pl.BlockSpec((N_EXPERTS, d), lambda i: (0, 0)),
                  pl.BlockSpec((N_EXPERTS, 1), lambda i: (0, 0)),
                  pl.BlockSpec((1, tm), lambda i: (0, i))],
        out_specs=(spec4, spec4, spec4, pl.BlockSpec((N_EXPERTS, 1), lambda i: (0, 0)),
                   pl.BlockSpec((tm, d // 2), lambda i: (i, 0))),
        scratch_shapes=[pltpu.VMEM((N_EXPERTS, 1), F32)],
        compiler_params=_params("arbitrary"),
        name="router",
    )(h, w.T, b[:, None], real)


def _sc_gather_rows(table, idx, name):
    rows, width = table.shape
    total = idx.shape[0]
    per_worker = total // SC_WORKERS
    max_chunk = min(SC_MAX_INDICES, SC_ROW_BUFFER_BYTES // (width * table.dtype.itemsize))
    chunk = max(c for c in range(SUBLANES, max_chunk + 1, SUBLANES) if per_worker % (2 * c) == 0)
    n_chunks = per_worker // chunk
    assert total == SC_WORKERS * n_chunks * chunk and n_chunks % 2 == 0
    mesh = plsc.VectorSubcoreMesh(core_axis_name="c", subcore_axis_name="s",
                                  num_cores=SC_CORES, num_subcores=SC_SUBCORES)

    def body(table_hbm, idx_hbm, out_hbm, idx_v, rows0, rows1, gsem0, gsem1, psem0, psem1):
        worker = lax.axis_index("s") * SC_CORES + lax.axis_index("c")
        base = worker * per_worker
        pltpu.sync_copy(idx_hbm.at[worker], idx_v)
        bufs = (rows0, rows1)
        gsems = (gsem0, gsem1)
        psems = (psem0, psem1)

        def gather(c, slot):
            return pltpu.make_async_copy(table_hbm.at[idx_v.at[c]], bufs[slot], gsems[slot])

        def put(c, slot):
            return pltpu.make_async_copy(bufs[slot], out_hbm.at[pl.ds(base + c * chunk, chunk)], psems[slot])

        gather(0, 0).start()

        @pl.loop(0, n_chunks, step=2)
        def _(c0):
            for slot in range(2):
                c = c0 + slot

                @pl.when(c + 1 < n_chunks)
                def _():
                    @pl.when(c >= 1)
                    def _():
                        put(c - 1, 1 - slot).wait()
                    gather(c + 1, 1 - slot).start()

                gather(c, slot).wait()
                put(c, slot).start()

        put(n_chunks - 2, 0).wait()
        put(n_chunks - 1, 1).wait()

    return pl.kernel(
        body,
        out_type=jax.ShapeDtypeStruct((total, width), table.dtype),
        mesh=mesh,
        scratch_types=[pltpu.VMEM((n_chunks, chunk), I32),
                       pltpu.VMEM((chunk, width), table.dtype), pltpu.VMEM((chunk, width), table.dtype),
                       pltpu.SemaphoreType.DMA, pltpu.SemaphoreType.DMA,
                       pltpu.SemaphoreType.DMA, pltpu.SemaphoreType.DMA],
        name=name,
    )(table, idx.reshape(SC_WORKERS, n_chunks, chunk))


def _sc_scatter_rows(src, dest, out_rows, name):
    n, width = src.shape
    fan = dest.shape[0]
    per_worker = n // SC_WORKERS
    chunk = max(c for c in range(SUBLANES, SC_MAX_INDICES + 1, SUBLANES) if per_worker % (2 * c) == 0)
    n_chunks = per_worker // chunk
    assert n == SC_WORKERS * n_chunks * chunk and n_chunks % 2 == 0
    mesh = plsc.VectorSubcoreMesh(core_axis_name="c", subcore_axis_name="s",
                                  num_cores=SC_CORES, num_subcores=SC_SUBCORES)

    def body(src_hbm, idx_hbm, out_hbm, idx_v, rows0, rows1, lsem0, lsem1, ssem0, ssem1):
        worker = lax.axis_index("s") * SC_CORES + lax.axis_index("c")
        base = worker * per_worker
        pltpu.sync_copy(idx_hbm.at[worker], idx_v)
        bufs = (rows0, rows1)
        lsems = (lsem0, lsem1)
        ssems = (ssem0, ssem1)

        def load(c, slot):
            return pltpu.make_async_copy(src_hbm.at[pl.ds(base + c * chunk, chunk)], bufs[slot], lsems[slot])

        def scatter(c, k, slot):
            return pltpu.make_async_copy(bufs[slot], out_hbm.at[idx_v.at[k * n_chunks + c]], ssems[slot])

        load(0, 0).start()

        @pl.loop(0, n_chunks, step=2)
        def _(c0):
            for slot in range(2):
                c = c0 + slot

                @pl.when(c + 1 < n_chunks)
                def _():
                    @pl.when(c >= 1)
                    def _():
                        for k in range(fan):
                            scatter(c - 1, k, 1 - slot).wait()
                    load(c + 1, 1 - slot).start()

                load(c, slot).wait()
                for k in range(fan):
                    scatter(c, k, slot).start()

        for k in range(fan):
            scatter(n_chunks - 2, k, 0).wait()
        for k in range(fan):
            scatter(n_chunks - 1, k, 1).wait()

    idx = dest.reshape(fan, SC_WORKERS, n_chunks, chunk).transpose(1, 0, 2, 3)
    return pl.kernel(
        body,
        out_type=jax.ShapeDtypeStruct((out_rows, width), src.dtype),
        mesh=mesh,
        scratch_types=[pltpu.VMEM((fan * n_chunks, chunk), I32),
                       pltpu.VMEM((chunk, width), src.dtype), pltpu.VMEM((chunk, width), src.dtype),
                       pltpu.SemaphoreType.DMA, pltpu.SemaphoreType.DMA,
                       pltpu.SemaphoreType.DMA, pltpu.SemaphoreType.DMA],
        name=name,
    )(src, idx.reshape(SC_WORKERS, fan * n_chunks, chunk))


def _expert_kernel(be_ref, valid_ref, xs_ref, wgu_ref, bgu_ref, wd_ref, bd_ref, y_ref, wgu_sc, wd_sc):
    i = pl.program_id(0)
    e = be_ref[i]
    e_prev = be_ref[jnp.maximum(i - 1, 0)]
    d = wd_sc.shape[0]

    @pl.when((i == 0) | (e != e_prev))
    def _():
        for r in range(0, d, LANES):
            wgu_sc[r:r + LANES, :] = wgu_ref[0, r:r + LANES, :].astype(BF16)
            wd_sc[r:r + LANES, :] = wd_ref[0, r:r + LANES, :].astype(BF16)

    valid = valid_ref[i]

    @pl.when(valid > 0)
    def _():
        row = lax.broadcasted_iota(I32, xs_ref.shape, 0)
        x = _unpack_bf16_pairs(jnp.where(row < valid, xs_ref[...], 0))
        h = jnp.dot(x, wgu_sc[...], preferred_element_type=F32) + bgu_ref[0]
        glu = jnp.minimum(h[:, :d], SWIGLU_LIMIT)
        lin = jnp.clip(h[:, d:], -SWIGLU_LIMIT, SWIGLU_LIMIT)
        act = glu * _sigmoid(SWIGLU_ALPHA * glu) * (lin + 1.0)
        y = jnp.dot(act.astype(BF16), wd_sc[...], preferred_element_type=F32) + bd_ref[0]
        y_ref[...] = _pack_bf16_pairs(y)

    @pl.when(valid == 0)
    def _():
        y_ref[...] = jnp.zeros_like(y_ref)


def _experts(block_e, valid_rows, xs, w_gu, b_gu, w_down, b_down, layer):
    cap = xs.shape[0]
    d = D_MODEL
    tm = EXPERT_TILE
    n_blocks = cap // tm
    grid_spec = pltpu.PrefetchScalarGridSpec(
        num_scalar_prefetch=2,
        grid=(n_blocks,),
        in_specs=[
            pl.BlockSpec((tm, d // 2), lambda i, be, nu: (i, 0)),
            pl.BlockSpec((None, 1, d, 2 * d), lambda i, be, nu: (layer, be[i], 0, 0)),
            pl.BlockSpec((1, 1, 2 * d), lambda i, be, nu: (be[i], 0, 0)),
            pl.BlockSpec((None, 1, d, d), lambda i, be, nu: (layer, be[i], 0, 0)),
            pl.BlockSpec((1, 1, d), lambda i, be, nu: (be[i], 0, 0)),
        ],
        out_specs=pl.BlockSpec((tm, d // 2), lambda i, be, nu: (i, 0)),
        scratch_shapes=[pltpu.VMEM((d, 2 * d), BF16), pltpu.VMEM((d, d), BF16)],
    )
    return pl.pallas_call(
        _expert_kernel,
        out_shape=jax.ShapeDtypeStruct((cap, d // 2), I32),
        grid_spec=grid_spec,
        compiler_params=_params("arbitrary"),
        name="moe_experts",
    )(block_e, valid_rows, xs, w_gu, b_gu[layer][:, None, :], w_down, b_down[layer][:, None, :])


def _combine_kernel(g_ref, b_ref, *refs, n_sub):
    o_ref = refs[-1]
    rows = o_ref.shape[0] // n_sub
    for s in range(n_sub):
        gate_ref, res_ref, yk_ref = refs[3 * s:3 * s + 3]
        gates = gate_ref[...]
        ffn = gates[:, 0:1] * _unpack_bf16_pairs(yk_ref[0], F32)
        for k in range(1, TOP_K):
            ffn = ffn + gates[:, k:k + 1] * _unpack_bf16_pairs(yk_ref[k], F32)
        o_ref[s * rows:(s + 1) * rows, :] = _layer_norm(DN_ALPHA * res_ref[...] + ffn, g_ref[...], b_ref[...])


def _combine(gates, res, yk, g, b, part, acc, drop_front=None):
    n, d = res.shape
    if drop_front is None:
        tt, n_sub = MOVE_TILE, 1
        steps = n // COLLECT_PARTS // tt
        grid = (steps,)
        local = lambda s, i: i
        src = lambda s, i: part * steps + i
        dst = lambda i: part * steps + i
        out_rows = n
    else:
        batch, lp = drop_front
        tt, n_sub = LANES, 2
        tiles = lp // tt
        part_batch = batch // COLLECT_PARTS
        pairs = (tiles - 1) // n_sub
        assert pairs * n_sub == tiles - 1
        grid = (part_batch, pairs)
        local = lambda s, b_, j: b_ * tiles + 1 + n_sub * j + s
        src = lambda s, b_, j: (part * part_batch + b_) * tiles + 1 + n_sub * j + s
        dst = lambda b_, j: (part * part_batch + b_) * pairs + j
        out_rows = n - batch * tt
    vec = pl.BlockSpec((1, d), lambda *_: (0, 0))
    in_specs = [vec, vec]
    args = (g, b)
    for s in range(n_sub):
        in_specs += [pl.BlockSpec((tt, TOP_K), lambda *i, s=s: (local(s, *i), 0)),
                     pl.BlockSpec((tt, d), lambda *i, s=s: (src(s, *i), 0)),
                     pl.BlockSpec((TOP_K, tt, d // 2), lambda *i, s=s: (0, local(s, *i), 0))]
        args += (gates, res, yk)
    aliases = {}
    if acc is not None:
        in_specs.append(pl.BlockSpec(memory_space=pl.ANY))
        args += (acc,)
        aliases = {len(args) - 1: 0}
    return pl.pallas_call(
        functools.partial(_combine_kernel, n_sub=n_sub),
        out_shape=jax.ShapeDtypeStruct((out_rows, d), F32),
        grid=grid,
        in_specs=in_specs,
        out_specs=pl.BlockSpec((n_sub * tt, d), lambda *i: (dst(*i), 0)),
        input_output_aliases=aliases,
        compiler_params=_params(*(["parallel"] * len(grid))),
        name="moe_combine",
    )(*args)


def _moe_layer(h, router_w, router_b, w_gu, b_gu, w_down, b_down, ln_g, ln_b, layer, lp, drop_front=None):
    n, d = h.shape
    tm = EXPERT_TILE
    token = jnp.arange(n, dtype=I32)
    real = (token % lp) >= FRONT_PAD
    idx, gates, rank, counts, h_packed = _router(h, router_w, router_b, real.astype(F32)[None, :])
    counts = counts[:, 0]
    padded = (counts + tm - 1) // tm * tm
    pad_end = jnp.cumsum(padded)
    pad_start = pad_end - padded
    n_real = n - (n // lp) * FRONT_PAD
    n_blocks = -(-(n_real * TOP_K + N_EXPERTS * (tm - 1)) // tm) + 1
    group_start = jnp.sum(jnp.where(idx[:, :, None] == jnp.arange(N_EXPERTS, dtype=I32), pad_start, 0), axis=-1)
    dest = jnp.where(real, group_start + rank, (n_blocks - 1) * tm + token % tm)
    block_start = jnp.arange(n_blocks, dtype=I32) * tm
    block_e = jnp.minimum(jnp.sum((pad_end[None, :] <= block_start[:, None]).astype(I32), axis=1),
                          N_EXPERTS - 1)
    group_end = (pad_start + counts)[block_e]
    valid_rows = jnp.where(block_start < pad_end[-1], jnp.clip(group_end - block_start, 0, tm), 0).astype(I32)

    xs = _sc_scatter_rows(h_packed, dest, n_blocks * tm, "moe_dispatch")
    ys = _experts(block_e, valid_rows, xs, w_gu, b_gu, w_down, b_down, layer)
    n_part = n // COLLECT_PARTS
    assert n_part % lp == 0
    out = None
    for part in range(COLLECT_PARTS):
        rows = slice(part * n_part, (part + 1) * n_part)
        yk = _sc_gather_rows(ys, dest[:, rows].reshape(TOP_K * n_part), "moe_collect")
        out = _combine(gates[:, rows].T, h, yk.reshape(TOP_K, n_part, d // 2), ln_g[None, :], ln_b[None, :],
                       part, out, drop_front)
    return out


def _qkv_kernel(x_ref, w_ref, o_ref):
    xb = x_ref[...].astype(BF16)
    d = x_ref.shape[1]
    for part in range(3):
        acc = jnp.dot(xb, w_ref[:, part * d:(part + 1) * d], preferred_element_type=F32)
        if part == 0:
            acc = acc * (NA_HEAD_DIM ** -0.5)
        for p in range(HEAD_PAIRS):
            o_ref[part * HEAD_PAIRS + p] = acc[:, p * LANES:(p + 1) * LANES].astype(BF16)


def _qkv(x, w_bf16):
    n, d = x.shape
    return pl.pallas_call(
        _qkv_kernel,
        out_shape=jax.ShapeDtypeStruct((3 * HEAD_PAIRS, n, LANES), BF16),
        grid=(n // ROW_TILE,),
        in_specs=[pl.BlockSpec((ROW_TILE, d), lambda i: (i, 0)),
                  pl.BlockSpec((d, 3 * d), lambda i: (0, 0))],
        out_specs=pl.BlockSpec((3 * HEAD_PAIRS, ROW_TILE, LANES), lambda i: (0, i, 0)),
        compiler_params=_params("parallel"),
        name="na_qkv",
    )(x, w_bf16)


def _na_kernel(q_ref, k_ref, v_ref, bias_a_ref, bias_b_ref, mb_ref, o_ref, *, rows):
    s = pl.program_id(1)
    w = GRID_W
    low = lax.broadcasted_iota(I32, (w, LANES), 1) < NA_HEAD_DIM
    contract_last = (((1,), (1,)), ((), ()))
    meta0 = FRONT_PAD

    def stacked_q(p, q0):
        qp = q_ref[p, q0:q0 + w, :]
        zero = jnp.zeros_like(qp)
        return jnp.concatenate([jnp.where(low, qp, zero), jnp.where(low, zero, qp)], axis=0)

    def attend_all(q0, k0, bias_ref):
        scores = []
        for p in range(HEAD_PAIRS):
            q2 = stacked_q(p, q0)
            s_meta = lax.dot_general(q2, k_ref[p, meta0:meta0 + N_META, :], contract_last,
                                     preferred_element_type=F32) + mb_ref[p]
            s_win = None
            if k0 is not None:
                s_win = lax.dot_general(q2, k_ref[p, pl.ds(k0, NA_KH * w), :], contract_last,
                                        preferred_element_type=F32) + bias_ref[0, p]
            scores.append((s_meta, s_win))
        outs = []
        for p in range(HEAD_PAIRS):
            s_meta, s_win = scores[p]
            m = jnp.max(s_meta, axis=-1, keepdims=True)
            if s_win is not None:
                m = jnp.maximum(m, jnp.max(s_win, axis=-1, keepdims=True))
            p_meta = jnp.exp(s_meta - m)
            denom = jnp.sum(p_meta, axis=-1, keepdims=True)
            o = jnp.dot(p_meta.astype(BF16), v_ref[p, meta0:meta0 + N_META, :], preferred_element_type=F32)
            if s_win is not None:
                p_win = jnp.exp(s_win - m)
                denom = denom + jnp.sum(p_win, axis=-1, keepdims=True)
                o = o + jnp.dot(p_win.astype(BF16), v_ref[p, pl.ds(k0, NA_KH * w), :],
                                preferred_element_type=F32)
            o = o / denom
            outs.append(jnp.where(low, o[:w], o[w:]))
        return outs

    @pl.when(s == 0)
    def _():
        qrow = lax.broadcasted_iota(I32, (w, LANES), 0)
        for p, o in enumerate(attend_all(LANES - w, None, None)):
            o_ref[p, 0:LANES - w, :] = jnp.zeros((LANES - w, LANES), o_ref.dtype)
            o_ref[p, LANES - w:LANES, :] = jnp.where(qrow >= w - N_META, o, 0.0).astype(o_ref.dtype)

    @pl.when(s >= 1)
    def _():
        for half, bias_ref in enumerate((bias_a_ref, bias_b_ref)):
            r = 2 * (s - 1) + half
            rs = jnp.clip(r - NA_KH // 2, 0, rows - NA_KH)
            k0 = pl.multiple_of(LANES + rs * w, w)
            for p, o in enumerate(attend_all(half * w, k0, bias_ref)):
                o_ref[p, half * w:(half + 1) * w, :] = o.astype(o_ref.dtype)


def _na_bias_table(rpb, rows):
    del rows
    w = GRID_W
    q = jnp.arange(w)
    col_start = jnp.clip(q - NA_KW // 2, 0, w - NA_KW)
    c = jnp.arange(w)
    in_win = (c[None, :] >= col_start[:, None]) & (c[None, :] < col_start[:, None] + NA_KW)
    pad = w - NA_KW
    rp = jnp.pad(rpb.astype(F32), ((0, 0), (0, 0), (pad, pad)))
    toeplitz = jnp.stack([rp[:, :, w - 1 - qq:2 * w - 1 - qq] for qq in range(w)], axis=2)
    toeplitz = jnp.where(in_win[None, None], toeplitz, NEG_BIG)
    tabs = [jnp.concatenate([toeplitz[:, NA_KH - 1 - v + j] for j in range(NA_KH)], axis=-1) for v in range(NA_KH)]
    return jnp.stack(tabs, axis=0)


def _na_attention(qkv, bias_tab, meta_bias, *, batch, lp):
    n = qkv.shape[1]
    w = GRID_W
    rows = (lp - LANES) // w
    tiles = lp // LANES

    def variant(half):
        def index(b, s):
            r = jnp.clip(2 * (s - 1) + half, 0, rows - 1)
            return (r - jnp.clip(r - NA_KH // 2, 0, rows - NA_KH), 0, 0, 0)
        return index

    kern = functools.partial(_na_kernel, rows=rows)
    bias_block = (1, HEAD_PAIRS, 2 * w, NA_KH * w)
    bias_tab = bias_tab.reshape(NA_KH, HEAD_PAIRS, 2 * w, NA_KH * w)
    return pl.pallas_call(
        kern,
        out_shape=jax.ShapeDtypeStruct((HEAD_PAIRS, n, LANES), BF16),
        grid=(batch, tiles),
        in_specs=[
            pl.BlockSpec((HEAD_PAIRS, LANES, LANES), lambda b, s: (0, b * tiles + s, 0)),
            pl.BlockSpec((HEAD_PAIRS, lp, LANES), lambda b, s: (1, b, 0)),
            pl.BlockSpec((HEAD_PAIRS, lp, LANES), lambda b, s: (2, b, 0)),
            pl.BlockSpec(bias_block, variant(0)),
            pl.BlockSpec(bias_block, variant(1)),
            pl.BlockSpec((HEAD_PAIRS, 2 * w, N_META), lambda b, s: (0, 0, 0)),
        ],
        out_specs=pl.BlockSpec((HEAD_PAIRS, LANES, LANES), lambda b, s: (0, b * tiles + s, 0)),
        compiler_params=_params("parallel", "arbitrary"),
        name="na_attention",
    )(qkv, qkv, qkv, bias_tab, bias_tab,
      jnp.repeat(meta_bias, w, axis=0).reshape(HEAD_PAIRS, 2 * w, N_META))


def _na_out_kernel(o_ref, res_ref, w_ref, g_ref, b_ref, out_ref):
    att = jnp.concatenate([o_ref[p] for p in range(HEAD_PAIRS)], axis=-1)
    mix = jnp.dot(att, w_ref[...], preferred_element_type=F32)
    out_ref[...] = _layer_norm(DN_ALPHA * res_ref[...] + mix, g_ref[...], b_ref[...])


def _na_out(o, res, w_bf16, g, b):
    n, d = res.shape
    vec = pl.BlockSpec((1, d), lambda i: (0, 0))
    return pl.pallas_call(
        _na_out_kernel,
        out_shape=jax.ShapeDtypeStruct((n, d), F32),
        grid=(n // ROW_TILE,),
        in_specs=[pl.BlockSpec((HEAD_PAIRS, ROW_TILE, LANES), lambda i: (0, i, 0)),
                  pl.BlockSpec((ROW_TILE, d), lambda i: (i, 0)),
                  pl.BlockSpec((d, d), lambda i: (0, 0)), vec, vec],
        out_specs=pl.BlockSpec((ROW_TILE, d), lambda i: (i, 0)),
        compiler_params=_params("parallel"),
        name="na_out",
    )(o, res, w_bf16, g, b)


def kernel(x, meta_tokens, lru_w_in, lru_conv_w, lru_conv_b, lru_wa, lru_ba, lru_wx, lru_bx, lru_lambda, lru_w_out, na_w_qkv, na_rpb, na_meta_bias, na_w_out, ln_mix_g, ln_mix_b, router_w, router_b, moe_w_gu, moe_b_gu, moe_w_down, moe_b_down, ln_ffn_g, ln_ffn_b):
    batch, seq, d = x.shape
    lp = LANES + seq
    assert d == D_MODEL and seq % (2 * GRID_W) == 0 and seq // GRID_W >= NA_KH
    assert lp % SCAN_CHUNK == 0 and (batch * lp) % ROW_TILE == 0
    n = batch * lp

    front = jnp.zeros((batch, FRONT_PAD, d), x.dtype)
    meta = jnp.broadcast_to(meta_tokens[None].astype(x.dtype), (batch, N_META, d))
    h = jnp.concatenate([front, meta, x], axis=1).reshape(n, d)

    u = _matmul(h, lru_w_in[0].astype(BF16))
    row = lambda v: v[None, :]
    def scan(direction, reverse, fuse_out=None):
        return _lru_scan(
            u, lru_conv_w[0], row(lru_conv_b[0]),
            (0.5 * lru_wa[0, direction]).astype(BF16), row(0.5 * lru_ba[0, direction]),
            (0.5 * lru_wx[0, direction]).astype(BF16), row(0.5 * lru_bx[0, direction]),
            row(lru_lambda[0, direction]), reverse=reverse, lp=lp, fuse_out=fuse_out)

    h_fwd = scan(0, False)
    h = scan(1, True, fuse_out=(h_fwd, h, lru_w_out[0].astype(BF16), row(ln_mix_g[0]), row(ln_mix_b[0])))
    h = _moe_layer(h, router_w[0], router_b[0], moe_w_gu, moe_b_gu, moe_w_down, moe_b_down,
                   ln_ffn_g[0], ln_ffn_b[0], 0, lp)

    qkv = _qkv(h, na_w_qkv[0].astype(BF16))
    att = _na_attention(qkv, _na_bias_table(na_rpb[0], seq // GRID_W), na_meta_bias[0].astype(F32),
                        batch=batch, lp=lp)
    h = _na_out(att, h, na_w_out[0].astype(BF16), row(ln_mix_g[1]), row(ln_mix_b[1]))
    h = _moe_layer(h, router_w[1], router_b[1], moe_w_gu, moe_b_gu, moe_w_down, moe_b_down,
                   ln_ffn_g[1], ln_ffn_b[1], 1, lp, drop_front=(batch, lp))

    return h.reshape(batch, seq, d)
```

```python
import functools

import jax
import jax.numpy as jnp
from jax import lax
from jax.experimental import pallas as pl
from jax.experimental.pallas import tpu as pltpu
from jax.experimental.pallas import tpu_sc as plsc

F32 = jnp.float32
BF16 = jnp.bfloat16
I32 = jnp.int32
U32 = jnp.uint32

D_MODEL = 1024
N_META = 16
GRID_W = 64
LRU_BLOCKS = 4
LRU_BLOCK = D_MODEL // LRU_BLOCKS
CONV_W = 4
LRU_C = 8.0
NA_HEADS = 16
NA_HEAD_DIM = D_MODEL // NA_HEADS
NA_KH = 8
NA_KW = 16
N_EXPERTS = 32
TOP_K = 4
SWIGLU_LIMIT = 7.0
SWIGLU_ALPHA = 1.702
DEPTH = 2
DN_ALPHA = (2.0 * DEPTH) ** 0.25
LN_EPS = 1e-5

LANES = 128
SUBLANES = 8
FRONT_PAD = LANES - N_META
HEAD_PAIRS = D_MODEL // LANES
NEG_BIG = -1e30

ROW_TILE = 1024
SCAN_CHUNK = 352
ROUTE_TILE = 512
MOVE_TILE = 512
EXPERT_TILE = 512
COLLECT_PARTS = 2
VMEM_LIMIT = 56 << 20

SC_CORES = 2
SC_SUBCORES = 16
SC_WORKERS = SC_CORES * SC_SUBCORES
SC_MAX_INDICES = 64
SC_ROW_BUFFER_BYTES = 128 << 10


def _params(*sem):
    return pltpu.CompilerParams(dimension_semantics=sem, vmem_limit_bytes=VMEM_LIMIT)


def _sigmoid(x):
    return 0.5 * jnp.tanh(0.5 * x) + 0.5


def _layer_norm(x, g, b):
    mu = jnp.mean(x, axis=-1, keepdims=True)
    xc = x - mu
    var = jnp.mean(xc * xc, axis=-1, keepdims=True)
    return xc * lax.rsqrt(var + LN_EPS) * g + b


def _matmul_kernel(x_ref, w_ref, o_ref):
    o_ref[...] = jnp.dot(x_ref[...].astype(BF16), w_ref[...], preferred_element_type=F32).astype(o_ref.dtype)


def _matmul(x, w_bf16):
    n, k = x.shape
    m = w_bf16.shape[1]
    return pl.pallas_call(
        _matmul_kernel,
        out_shape=jax.ShapeDtypeStruct((n, m), BF16),
        grid=(n // ROW_TILE,),
        in_specs=[pl.BlockSpec((ROW_TILE, k), lambda i: (i, 0)),
                  pl.BlockSpec((k, m), lambda i: (0, 0))],
        out_specs=pl.BlockSpec((ROW_TILE, m), lambda i: (i, 0)),
        compiler_params=_params("parallel"),
        name="in_proj",
    )(x, w_bf16)


def _lru_scan_kernel(xr_ref, xp_ref, xn_ref, cw_ref, cb_ref, wa_ref, ba_ref, wx_ref, bx_ref,
                     lam_ref, *rest, reverse, chunks_per_batch, fused):
    if fused:
        hf_ref, y_ref, res_ref, wout_ref, g_ref, beta_ref, h_ref, a_sc, b_sc, h_sc, carry_sc = rest
    else:
        h_ref, a_sc, b_sc, h_sc, carry_sc = rest
    t_rows = xr_ref.shape[0]
    seg = t_rows // SUBLANES
    halo = xp_ref.shape[0]
    lane_tiles = xr_ref.shape[1] // LANES
    step = pl.program_id(0)
    chunk = (pl.num_programs(0) - 1 - step) if reverse else step
    cib = chunk % chunks_per_batch
    first_chunk = cib == 0
    last_chunk = cib == chunks_per_batch - 1
    row8 = lax.broadcasted_iota(I32, (SUBLANES, 1), 0)
    cat = lambda parts: jnp.concatenate(parts, axis=0)

    head_pos = cib * t_rows + lax.broadcasted_iota(I32, (LANES, 1), 0)
    head_real = head_pos >= FRONT_PAD
    xr = xr_ref[...].astype(F32)
    xr = cat([jnp.where(head_real, xr[:LANES], 0.0), xr[LANES:]])
    prev_pos = cib * t_rows - halo + lax.broadcasted_iota(I32, (halo, 1), 0)
    prev = jnp.where(prev_pos >= FRONT_PAD, xp_ref[...].astype(F32), 0.0)
    nxt = jnp.where(last_chunk, 0.0, xn_ref[...].astype(F32))

    xm1 = pltpu.roll(xr, 1, 0)
    xm2 = pltpu.roll(xr, 2, 0)
    xp1 = pltpu.roll(xr, t_rows - 1, 0)
    xm1 = cat([jnp.where(row8 == 0, prev[halo - 1:halo], xm1[:SUBLANES]), xm1[SUBLANES:]])
    xm2_head = jnp.where(row8 == 0, prev[halo - 2:halo - 1],
                         jnp.where(row8 == 1, prev[halo - 1:halo], xm2[:SUBLANES]))
    xm2 = cat([xm2_head, xm2[SUBLANES:]])
    xp1 = cat([xp1[:t_rows - SUBLANES], jnp.where(row8 == SUBLANES - 1, nxt[0:1], xp1[t_rows - SUBLANES:])])
    cw = cw_ref[...]
    xc = cw[0:1, :] * xm2 + cw[1:2, :] * xm1 + cw[2:3, :] * xr + cw[3:4, :] * xp1 + cb_ref[...]

    xcb = xc.astype(BF16)
    za, zx = [], []
    for blk in range(LRU_BLOCKS):
        xblk = xcb[:, blk * LRU_BLOCK:(blk + 1) * LRU_BLOCK]
        za.append(jnp.dot(xblk, wa_ref[blk], preferred_element_type=F32))
        zx.append(jnp.dot(xblk, wx_ref[blk], preferred_element_type=F32))
    tanh_a = jnp.tanh(jnp.concatenate(za, axis=-1) + ba_ref[...])
    tanh_x = jnp.tanh(jnp.concatenate(zx, axis=-1) + bx_ref[...])

    z = -lam_ref[...]
    softplus = jnp.maximum(z, 0.0) + jnp.log1p(jnp.exp(-jnp.abs(z)))
    half_rate = (-0.5 * LRU_C) * softplus
    log_a = tanh_a * half_rate + half_rate
    a = jnp.exp(log_a)
    gap = jnp.tanh(log_a) * (-1.0 - a * a)
    mult = jnp.where(gap > 0.0, gap * lax.rsqrt(gap), 0.0)
    if reverse:
        tail = jnp.where(jnp.logical_and(last_chunk, row8 == SUBLANES - 1), 1.0, mult[t_rows - SUBLANES:])
        mult = cat([mult[:t_rows - SUBLANES], tail])
    else:
        mult = cat([jnp.where(head_pos == FRONT_PAD, 1.0, mult[:LANES]), mult[LANES:]])
    b = mult * ((0.5 * tanh_x + 0.5) * xc)
    b = cat([jnp.where(head_real, b[:LANES], 0.0), b[LANES:]])
    for c in range(lane_tiles):
        a_sc[c] = a[:, c * LANES:(c + 1) * LANES]
        b_sc[c] = b[:, c * LANES:(c + 1) * LANES]

    @pl.when(last_chunk if reverse else first_chunk)
    def _():
        carry_sc[...] = jnp.zeros_like(carry_sc)

    order = range(seg - 1, -1, -1) if reverse else range(seg)
    rows_of = lambda j: pl.ds(j, SUBLANES, stride=seg)

    local = [jnp.zeros((SUBLANES, LANES), F32)] * lane_tiles
    prod = [jnp.ones((SUBLANES, LANES), F32)] * lane_tiles
    for j in order:
        for c in range(lane_tiles):
            av = a_sc[c, rows_of(j), :]
            local[c] = av * local[c] + b_sc[c, rows_of(j), :]
            prod[c] = av * prod[c]
            h_sc[c, rows_of(j), :] = local[c]
            a_sc[c, rows_of(j), :] = prod[c]

    carry_in = carry_sc[...]
    carry_out = []
    seg_carry = []
    for c in range(lane_tiles):
        state = carry_in[:, c * LANES:(c + 1) * LANES]
        rows = [None] * SUBLANES
        for s in (range(SUBLANES - 1, -1, -1) if reverse else range(SUBLANES)):
            rows[s] = state
            state = local[c][s:s + 1] + prod[c][s:s + 1] * state
        seg_carry.append(cat(rows))
        carry_out.append(state)
    carry_sc[...] = jnp.concatenate(carry_out, axis=-1)

    for j in order:
        for c in range(lane_tiles):
            h_sc[c, rows_of(j), :] = h_sc[c, rows_of(j), :] + a_sc[c, rows_of(j), :] * seg_carry[c]
    states = jnp.concatenate([h_sc[c] for c in range(lane_tiles)], axis=-1)
    if fused:
        gated = (hf_ref[...] + states) * _gelu_tanh(y_ref[...].astype(F32))
        mix = jnp.dot(gated.astype(BF16), wout_ref[...], preferred_element_type=F32)
        h_ref[...] = _layer_norm(DN_ALPHA * res_ref[...] + mix, g_ref[...], beta_ref[...])
    else:
        h_ref[...] = states


def _lru_scan(u, cw, cb, wa_half, ba_half, wx_half, bx_half, lam, *, reverse, lp, fuse_out=None):
    n = u.shape[0]
    d = D_MODEL
    t = SCAN_CHUNK
    halo = 2 * SUBLANES
    n_chunks = n // t
    cpb = lp // t
    t_h = t // halo
    n_h = n // halo

    def chunk_of(i):
        return (n_chunks - 1 - i) if reverse else i

    kern = functools.partial(_lru_scan_kernel, reverse=reverse, chunks_per_batch=cpb, fused=fuse_out is not None)
    chunk_rows = lambda col: pl.BlockSpec((t, d), lambda i: (chunk_of(i), col))
    extra_specs, extra_args = [], ()
    if fuse_out is not None:
        h_other, res, w_out, ln_g, ln_b = fuse_out
        extra_specs = [chunk_rows(0), chunk_rows(1), chunk_rows(0),
                       pl.BlockSpec((d, d), lambda i: (0, 0)),
                       pl.BlockSpec((1, d), lambda i: (0, 0)), pl.BlockSpec((1, d), lambda i: (0, 0))]
        extra_args = (h_other, u, res, w_out, ln_g, ln_b)
    full2 = lambda shape: pl.BlockSpec(shape, lambda i: (0, 0))
    full3 = lambda shape: pl.BlockSpec(shape, lambda i: (0, 0, 0))
    tile_major = pltpu.VMEM((d // LANES, t, LANES), F32)
    return pl.pallas_call(
        kern,
        out_shape=jax.ShapeDtypeStruct((n, d), F32),
        grid=(n_chunks,),
        in_specs=[
            pl.BlockSpec((t, d), lambda i: (chunk_of(i), 0)),
            pl.BlockSpec((halo, d), lambda i: (jnp.maximum(chunk_of(i) * t_h - 1, 0), 0)),
            pl.BlockSpec((halo, d), lambda i: (jnp.minimum((chunk_of(i) + 1) * t_h, n_h - 1), 0)),
            full2((CONV_W, d)), full2((1, d)),
            full3((LRU_BLOCKS, LRU_BLOCK, LRU_BLOCK)), full2((1, d)),
            full3((LRU_BLOCKS, LRU_BLOCK, LRU_BLOCK)), full2((1, d)),
            full2((1, d)),
        ] + extra_specs,
        out_specs=pl.BlockSpec((t, d), lambda i: (chunk_of(i), 0)),
        scratch_shapes=[tile_major, tile_major, tile_major, pltpu.VMEM((1, d), F32)],
        compiler_params=_params("arbitrary"),
        name=("lru_scan_bwd" if reverse else "lru_scan_fwd") + ("_out" if fuse_out is not None else ""),
    )(u, u, u, cw, cb, wa_half, ba_half, wx_half, bx_half, lam, *extra_args)


def _gelu_tanh(y):
    c = 0.7978845608028654
    return y * (0.5 * (1.0 + jnp.tanh(c * (y + 0.044715 * (y * y * y)))))


def _pack_bf16_pairs(x):
    half = x.shape[1] // 2
    lo = pltpu.bitcast(x[:, :half].astype(BF16).astype(F32), U32)
    hi = pltpu.bitcast(x[:, half:].astype(BF16).astype(F32), U32)
    return pltpu.bitcast((lo >> 16) | (hi & jnp.uint32(0xFFFF0000)), I32)


def _unpack_bf16_pairs(packed, dtype=BF16):
    u = pltpu.bitcast(packed, U32)
    lo = pltpu.bitcast(u << 16, F32).astype(dtype)
    hi = pltpu.bitcast(u & jnp.uint32(0xFFFF0000), F32).astype(dtype)
    return jnp.concatenate([lo, hi], axis=-1)


def _router_kernel(h_ref, wt_ref, b_ref, real_ref, idx_ref, gate_ref, rank_ref, cnt_ref, hp_ref, base_sc):
    tm = h_ref.shape[0]
    h = h_ref[...]
    hp_ref[...] = _pack_bf16_pairs(h)

    @pl.when(pl.program_id(0) == 0)
    def _():
        base_sc[...] = jnp.zeros_like(base_sc)

    wt = wt_ref[...]
    h_hi = h.astype(BF16)
    h_lo = (h - h_hi.astype(F32)).astype(BF16)
    w_hi = wt.astype(BF16)
    w_lo = (wt - w_hi.astype(F32)).astype(BF16)
    contract_last = (((1,), (1,)), ((), ()))
    mm = lambda a, b: lax.dot_general(a, b, contract_last, preferred_element_type=F32)
    logits = mm(w_hi, h_hi) + (mm(w_hi, h_lo) + mm(w_lo, h_hi)) + b_ref[...]

    expert = lax.broadcasted_iota(I32, (N_EXPERTS, tm), 0).astype(F32)
    vals = logits
    idx_rows, val_rows, onehots = [], [], []
    for _ in range(TOP_K):
        m = jnp.max(vals, axis=0, keepdims=True)
        idx = jnp.min(jnp.where(vals == m, expert, float(N_EXPERTS)), axis=0, keepdims=True)
        hit = expert == idx
        onehots.append(jnp.where(hit, real_ref[...], 0.0))
        idx_rows.append(idx)
        val_rows.append(m)
        vals = jnp.where(hit, -jnp.inf, vals)

    onehot_all = jnp.concatenate(onehots, axis=0)
    t_from = lax.broadcasted_iota(I32, (tm, tm), 0)
    t_to = lax.broadcasted_iota(I32, (tm, tm), 1)
    earlier = jnp.where(t_from < t_to, 1.0, 0.0).astype(BF16)
    before = jnp.dot(onehot_all.astype(BF16), earlier, preferred_element_type=F32)
    base = base_sc[...]
    starts = []
    for k in range(TOP_K):
        starts.append(base)
        base = base + jnp.sum(onehots[k], axis=1, keepdims=True)
    contrib = onehot_all * (jnp.concatenate(starts, axis=0) + before)
    rank_rows = [jnp.sum(contrib[k * N_EXPERTS:(k + 1) * N_EXPERTS], axis=0, keepdims=True)
                 for k in range(TOP_K)]

    top_vals = jnp.concatenate(val_rows, axis=0)
    e = jnp.exp(top_vals - top_vals[0:1])
    idx_ref[...] = jnp.concatenate(idx_rows, axis=0).astype(I32)
    gate_ref[...] = e / jnp.sum(e, axis=0, keepdims=True)
    rank_ref[...] = jnp.concatenate(rank_rows, axis=0).astype(I32)
    base_sc[...] = base
    cnt_ref[...] = base.astype(I32)


def _router(h, w, b, real):
    n, d = h.shape
    tm = ROUTE_TILE
    out4 = lambda dt: jax.ShapeDtypeStruct((TOP_K, n), dt)
    spec4 = pl.BlockSpec((TOP_K, tm), lambda i: (0, i))
    return pl.pallas_call(
        _router_kernel,
        out_shape=(out4(I32), out4(F32), out4(I32), jax.ShapeDtypeStruct((N_EXPERTS, 1), I32),
                   jax.ShapeDtypeStruct((n, d // 2), I32)),
        grid=(n // tm,),
        in_specs=[pl.BlockSpec((tm, d), lambda i: (i, 0)),
                  pl.BlockSpec((N_EXPERTS, d), lambda i: (0, 0)),
                  pl.BlockSpec((N_EXPERTS, 1), lambda i: (0, 0)),
                  pl.BlockSpec((1, tm), lambda i: (0, i))],
        out_specs=(spec4, spec4, spec4, pl.BlockSpec((N_EXPERTS, 1), lambda i: (0, 0)),
                   pl.BlockSpec((tm, d // 2), lambda i: (i, 0))),
        scratch_shapes=[pltpu.VMEM((N_EXPERTS, 1), F32)],
        compiler_params=_params("arbitrary"),
        name="router",
    )(h, w.T, b[:, None], real)


def _sc_gather_rows(table, idx, name):
    rows, width = table.shape
    total = idx.shape[0]
    per_worker = total // SC_WORKERS
    max_chunk = min(SC_MAX_INDICES, SC_ROW_BUFFER_BYTES // (width * table.dtype.itemsize))
    chunk = max(c for c in range(SUBLANES, max_chunk + 1, SUBLANES) if per_worker % (2 * c) == 0)
    n_chunks = per_worker // chunk
    assert total == SC_WORKERS * n_chunks * chunk and n_chunks % 2 == 0
    mesh = plsc.VectorSubcoreMesh(core_axis_name="c", subcore_axis_name="s",
                                  num_cores=SC_CORES, num_subcores=SC_SUBCORES)

    def body(table_hbm, idx_hbm, out_hbm, idx_v, rows0, rows1, gsem0, gsem1, psem0, psem1):
        worker = lax.axis_index("s") * SC_CORES + lax.axis_index("c")
        base = worker * per_worker
        pltpu.sync_copy(idx_hbm.at[worker], idx_v)
        bufs = (rows0, rows1)
        gsems = (gsem0, gsem1)
        psems = (psem0, psem1)

        def gather(c, slot):
            return pltpu.make_async_copy(table_hbm.at[idx_v.at[c]], bufs[slot], gsems[slot])

        def put(c, slot):
            return pltpu.make_async_copy(bufs[slot], out_hbm.at[pl.ds(base + c * chunk, chunk)], psems[slot])

        gather(0, 0).start()

        @pl.loop(0, n_chunks, step=2)
        def _(c0):
            for slot in range(2):
                c = c0 + slot

                @pl.when(c + 1 < n_chunks)
                def _():
                    @pl.when(c >= 1)
                    def _():
                        put(c - 1, 1 - slot).wait()
                    gather(c + 1, 1 - slot).start()

                gather(c, slot).wait()
                put(c, slot).start()

        put(n_chunks - 2, 0).wait()
        put(n_chunks - 1, 1).wait()

    return pl.kernel(
        body,
        out_type=jax.ShapeDtypeStruct((total, width), table.dtype),
        mesh=mesh,
        scratch_types=[pltpu.VMEM((n_chunks, chunk), I32),
                       pltpu.VMEM((chunk, width), table.dtype), pltpu.VMEM((chunk, width), table.dtype),
                       pltpu.SemaphoreType.DMA, pltpu.SemaphoreType.DMA,
                       pltpu.SemaphoreType.DMA, pltpu.SemaphoreType.DMA],
        name=name,
    )(table, idx.reshape(SC_WORKERS, n_chunks, chunk))


def _sc_scatter_rows(src, dest, out_rows, name):
    n, width = src.shape
    fan = dest.shape[0]
    per_worker = n // SC_WORKERS
    chunk = max(c for c in range(SUBLANES, SC_MAX_INDICES + 1, SUBLANES) if per_worker % (2 * c) == 0)
    n_chunks = per_worker // chunk
    assert n == SC_WORKERS * n_chunks * chunk and n_chunks % 2 == 0
    mesh = plsc.VectorSubcoreMesh(core_axis_name="c", subcore_axis_name="s",
                                  num_cores=SC_CORES, num_subcores=SC_SUBCORES)

    def body(src_hbm, idx_hbm, out_hbm, idx_v, rows0, rows1, lsem0, lsem1, ssem0, ssem1):
        worker = lax.axis_index("s") * SC_CORES + lax.axis_index("c")
        base = worker * per_worker
        pltpu.sync_copy(idx_hbm.at[worker], idx_v)
        bufs = (rows0, rows1)
        lsems = (lsem0, lsem1)
        ssems = (ssem0, ssem1)

        def load(c, slot):
            return pltpu.make_async_copy(src_hbm.at[pl.ds(base + c * chunk, chunk)], bufs[slot], lsems[slot])

        def scatter(c, k, slot):
            return pltpu.make_async_copy(bufs[slot], out_hbm.at[idx_v.at[k * n_chunks + c]], ssems[slot])

        load(0, 0).start()

        @pl.loop(0, n_chunks, step=2)
        def _(c0):
            for slot in range(2):
                c = c0 + slot

                @pl.when(c + 1 < n_chunks)
                def _():
                    @pl.when(c >= 1)
                    def _():
                        for k in range(fan):
                            scatter(c - 1, k, 1 - slot).wait()
                    load(c + 1, 1 - slot).start()

                load(c, slot).wait()
                for k in range(fan):
                    scatter(c, k, slot).start()

        for k in range(fan):
            scatter(n_chunks - 2, k, 0).wait()
        for k in range(fan):
            scatter(n_chunks - 1, k, 1).wait()

    idx = dest.reshape(fan, SC_WORKERS, n_chunks, chunk).transpose(1, 0, 2, 3)
    return pl.kernel(
        body,
        out_type=jax.ShapeDtypeStruct((out_rows, width), src.dtype),
        mesh=mesh,
        scratch_types=[pltpu.VMEM((fan * n_chunks, chunk), I32),
                       pltpu.VMEM((chunk, width), src.dtype), pltpu.VMEM((chunk, width), src.dtype),
                       pltpu.SemaphoreType.DMA, pltpu.SemaphoreType.DMA,
                       pltpu.SemaphoreType.DMA, pltpu.SemaphoreType.DMA],
        name=name,
    )(src, idx.reshape(SC_WORKERS, fan * n_chunks, chunk))


def _expert_kernel(be_ref, valid_ref, xs_ref, wgu_ref, bgu_ref, wd_ref, bd_ref, y_ref, wgu_sc, wd_sc):
    i = pl.program_id(0)
    e = be_ref[i]
    e_prev = be_ref[jnp.maximum(i - 1, 0)]
    d = wd_sc.shape[0]

    @pl.when((i == 0) | (e != e_prev))
    def _():
        for r in range(0, d, LANES):
            wgu_sc[r:r + LANES, :] = wgu_ref[0, r:r + LANES, :].astype(BF16)
            wd_sc[r:r + LANES, :] = wd_ref[0, r:r + LANES, :].astype(BF16)

    valid = valid_ref[i]

    @pl.when(valid > 0)
    def _():
        row = lax.broadcasted_iota(I32, xs_ref.shape, 0)
        x = _unpack_bf16_pairs(jnp.where(row < valid, xs_ref[...], 0))
        h = jnp.dot(x, wgu_sc[...], preferred_element_type=F32) + bgu_ref[0]
        glu = jnp.minimum(h[:, :d], SWIGLU_LIMIT)
        lin = jnp.clip(h[:, d:], -SWIGLU_LIMIT, SWIGLU_LIMIT)
        act = glu * _sigmoid(SWIGLU_ALPHA * glu) * (lin + 1.0)
        y = jnp.dot(act.astype(BF16), wd_sc[...], preferred_element_type=F32) + bd_ref[0]
        y_ref[...] = _pack_bf16_pairs(y)

    @pl.when(valid == 0)
    def _():
        y_ref[...] = jnp.zeros_like(y_ref)


def _experts(block_e, valid_rows, xs, w_gu, b_gu, w_down, b_down, layer):
    cap = xs.shape[0]
    d = D_MODEL
    tm = EXPERT_TILE
    n_blocks = cap // tm
    grid_spec = pltpu.PrefetchScalarGridSpec(
        num_scalar_prefetch=2,
        grid=(n_blocks,),
        in_specs=[
            pl.BlockSpec((tm, d // 2), lambda i, be, nu: (i, 0)),
            pl.BlockSpec((None, 1, d, 2 * d), lambda i, be, nu: (layer, be[i], 0, 0)),
            pl.BlockSpec((1, 1, 2 * d), lambda i, be, nu: (be[i], 0, 0)),
            pl.BlockSpec((None, 1, d, d), lambda i, be, nu: (layer, be[i], 0, 0)),
            pl.BlockSpec((1, 1, d), lambda i, be, nu: (be[i], 0, 0)),
        ],
        out_specs=pl.BlockSpec((tm, d // 2), lambda i, be, nu: (i, 0)),
        scratch_shapes=[pltpu.VMEM((d, 2 * d), BF16), pltpu.VMEM((d, d), BF16)],
    )
    return pl.pallas_call(
        _expert_kernel,
        out_shape=jax.ShapeDtypeStruct((cap, d // 2), I32),
        grid_spec=grid_spec,
        compiler_params=_params("arbitrary"),
        name="moe_experts",
    )(block_e, valid_rows, xs, w_gu, b_gu[layer][:, None, :], w_down, b_down[layer][:, None, :])


def _combine_kernel(g_ref, b_ref, *refs, n_sub):
    o_ref = refs[-1]
    rows = o_ref.shape[0] // n_sub
    for s in range(n_sub):
        gate_ref, res_ref, yk_ref = refs[3 * s:3 * s + 3]
        gates = gate_ref[...]
        ffn = gates[:, 0:1] * _unpack_bf16_pairs(yk_ref[0], F32)
        for k in range(1, TOP_K):
            ffn = ffn + gates[:, k:k + 1] * _unpack_bf16_pairs(yk_ref[k], F32)
        o_ref[s * rows:(s + 1) * rows, :] = _layer_norm(DN_ALPHA * res_ref[...] + ffn, g_ref[...], b_ref[...])


def _combine(gates, res, yk, g, b, part, acc, drop_front=None):
    n, d = res.shape
    if drop_front is None:
        tt, n_sub = MOVE_TILE, 1
        steps = n // COLLECT_PARTS // tt
        grid = (steps,)
        local = lambda s, i: i
        src = lambda s, i: part * steps + i
        dst = lambda i: part * steps + i
        out_rows = n
    else:
        batch, lp = drop_front
        tt, n_sub = LANES, 2
        tiles = lp // tt
        part_batch = batch // COLLECT_PARTS
        pairs = (tiles - 1) // n_sub
        assert pairs * n_sub == tiles - 1
        grid = (part_batch, pairs)
        local = lambda s, b_, j: b_ * tiles + 1 + n_sub * j + s
        src = lambda s, b_, j: (part * part_batch + b_) * tiles + 1 + n_sub * j + s
        dst = lambda b_, j: (part * part_batch + b_) * pairs + j
        out_rows = n - batch * tt
    vec = pl.BlockSpec((1, d), lambda *_: (0, 0))
    in_specs = [vec, vec]
    args = (g, b)
    for s in range(n_sub):
        in_specs += [pl.BlockSpec((tt, TOP_K), lambda *i, s=s: (local(s, *i), 0)),
                     pl.BlockSpec((tt, d), lambda *i, s=s: (src(s, *i), 0)),
                     pl.BlockSpec((TOP_K, tt, d // 2), lambda *i, s=s: (0, local(s, *i), 0))]
        args += (gates, res, yk)
    aliases = {}
    if acc is not None:
        in_specs.append(pl.BlockSpec(memory_space=pl.ANY))
        args += (acc,)
        aliases = {len(args) - 1: 0}
    return pl.pallas_call(
        functools.partial(_combine_kernel, n_sub=n_sub),
        out_shape=jax.ShapeDtypeStruct((out_rows, d), F32),
        grid=grid,
        in_specs=in_specs,
        out_specs=pl.BlockSpec((n_sub * tt, d), lambda *i: (dst(*i), 0)),
        input_output_aliases=aliases,
        compiler_params=_params(*(["parallel"] * len(grid))),
        name="moe_combine",
    )(*args)


def _moe_layer(h, router_w, router_b, w_gu, b_gu, w_down, b_down, ln_g, ln_b, layer, lp, drop_front=None):
    n, d = h.shape
    tm = EXPERT_TILE
    token = jnp.arange(n, dtype=I32)
    real = (token % lp) >= FRONT_PAD
    idx, gates, rank, counts, h_packed = _router(h, router_w, router_b, real.astype(F32)[None, :])
    counts = counts[:, 0]
    padded = (counts + tm - 1) // tm * tm
    pad_end = jnp.cumsum(padded)
    pad_start = pad_end - padded
    n_real = n - (n // lp) * FRONT_PAD
    n_blocks = -(-(n_real * TOP_K + N_EXPERTS * (tm - 1)) // tm) + 1
    group_start = jnp.sum(jnp.where(idx[:, :, None] == jnp.arange(N_EXPERTS, dtype=I32), pad_start, 0), axis=-1)
    dest = jnp.where(real, group_start + rank, (n_blocks - 1) * tm + token % tm)
    block_start = jnp.arange(n_blocks, dtype=I32) * tm
    block_e = jnp.minimum(jnp.sum((pad_end[None, :] <= block_start[:, None]).astype(I32), axis=1),
                          N_EXPERTS - 1)
    group_end = (pad_start + counts)[block_e]
    valid_rows = jnp.where(block_start < pad_end[-1], jnp.clip(group_end - block_start, 0, tm), 0).astype(I32)

    xs = _sc_scatter_rows(h_packed, dest, n_blocks * tm, "moe_dispatch")
    ys = _experts(block_e, valid_rows, xs, w_gu, b_gu, w_down, b_down, layer)
    n_part = n // COLLECT_PARTS
    assert n_part % lp == 0
    out = None
    for part in range(COLLECT_PARTS):
        rows = slice(part * n_part, (part + 1) * n_part)
        yk = _sc_gather_rows(ys, dest[:, rows].reshape(TOP_K * n_part), "moe_collect")
        out = _combine(gates[:, rows].T, h, yk.reshape(TOP_K, n_part, d // 2), ln_g[None, :], ln_b[None, :],
                       part, out, drop_front)
    return out


def _qkv_kernel(x_ref, w_ref, o_ref):
    xb = x_ref[...].astype(BF16)
    d = x_ref.shape[1]
    for part in range(3):
        acc = jnp.dot(xb, w_ref[:, part * d:(part + 1) * d], preferred_element_type=F32)
        if part == 0:
            acc = acc * (NA_HEAD_DIM ** -0.5)
        for p in range(HEAD_PAIRS):
            o_ref[part * HEAD_PAIRS + p] = acc[:, p * LANES:(p + 1) * LANES].astype(BF16)


def _qkv(x, w_bf16):
    n, d = x.shape
    return pl.pallas_call(
        _qkv_kernel,
        out_shape=jax.ShapeDtypeStruct((3 * HEAD_PAIRS, n, LANES), BF16),
        grid=(n // ROW_TILE,),
        in_specs=[pl.BlockSpec((ROW_TILE, d), lambda i: (i, 0)),
                  pl.BlockSpec((d, 3 * d), lambda i: (0, 0))],
        out_specs=pl.BlockSpec((3 * HEAD_PAIRS, ROW_TILE, LANES), lambda i: (0, i, 0)),
        compiler_params=_params("parallel"),
        name="na_qkv",
    )(x, w_bf16)


def _na_kernel(q_ref, k_ref, v_ref, bias_a_ref, bias_b_ref, mb_ref, o_ref, *, rows):
    s = pl.program_id(1)
    w = GRID_W
    low = lax.broadcasted_iota(I32, (w, LANES), 1) < NA_HEAD_DIM
    contract_last = (((1,), (1,)), ((), ()))
    meta0 = FRONT_PAD

    def stacked_q(p, q0):
        qp = q_ref[p, q0:q0 + w, :]
        zero = jnp.zeros_like(qp)
        return jnp.concatenate([jnp.where(low, qp, zero), jnp.where(low, zero, qp)], axis=0)

    def attend_all(q0, k0, bias_ref):
        scores = []
        for p in range(HEAD_PAIRS):
            q2 = stacked_q(p, q0)
            s_meta = lax.dot_general(q2, k_ref[p, meta0:meta0 + N_META, :], contract_last,
                                     preferred_element_type=F32) + mb_ref[p]
            s_win = None
            if k0 is not None:
                s_win = lax.dot_general(q2, k_ref[p, pl.ds(k0, NA_KH * w), :], contract_last,
                                        preferred_element_type=F32) + bias_ref[0, p]
            scores.append((s_meta, s_win))
        outs = []
        for p in range(HEAD_PAIRS):
            s_meta, s_win = scores[p]
            m = jnp.max(s_meta, axis=-1, keepdims=True)
            if s_win is not None:
                m = jnp.maximum(m, jnp.max(s_win, axis=-1, keepdims=True))
            p_meta = jnp.exp(s_meta - m)
            denom = jnp.sum(p_meta, axis=-1, keepdims=True)
            o = jnp.dot(p_meta.astype(BF16), v_ref[p, meta0:meta0 + N_META, :], preferred_element_type=F32)
            if s_win is not None:
                p_win = jnp.exp(s_win - m)
                denom = denom + jnp.sum(p_win, axis=-1, keepdims=True)
                o = o + jnp.dot(p_win.astype(BF16), v_ref[p, pl.ds(k0, NA_KH * w), :],
                                preferred_element_type=F32)
            o = o / denom
            outs.append(jnp.where(low, o[:w], o[w:]))
        return outs

    @pl.when(s == 0)
    def _():
        qrow = lax.broadcasted_iota(I32, (w, LANES), 0)
        for p, o in enumerate(attend_all(LANES - w, None, None)):
            o_ref[p, 0:LANES - w, :] = jnp.zeros((LANES - w, LANES), o_ref.dtype)
            o_ref[p, LANES - w:LANES, :] = jnp.where(qrow >= w - N_META, o, 0.0).astype(o_ref.dtype)

    @pl.when(s >= 1)
    def _():
        for half, bias_ref in enumerate((bias_a_ref, bias_b_ref)):
            r = 2 * (s - 1) + half
            rs = jnp.clip(r - NA_KH // 2, 0, rows - NA_KH)
            k0 = pl.multiple_of(LANES + rs * w, w)
            for p, o in enumerate(attend_all(half * w, k0, bias_ref)):
                o_ref[p, half * w:(half + 1) * w, :] = o.astype(o_ref.dtype)


def _na_bias_table(rpb, rows):
    del rows
    w = GRID_W
    q = jnp.arange(w)
    col_start = jnp.clip(q - NA_KW // 2, 0, w - NA_KW)
    c = jnp.arange(w)
    in_win = (c[None, :] >= col_start[:, None]) & (c[None, :] < col_start[:, None] + NA_KW)
    pad = w - NA_KW
    rp = jnp.pad(rpb.astype(F32), ((0, 0), (0, 0), (pad, pad)))
    toeplitz = jnp.stack([rp[:, :, w - 1 - qq:2 * w - 1 - qq] for qq in range(w)], axis=2)
    toeplitz = jnp.where(in_win[None, None], toeplitz, NEG_BIG)
    tabs = [jnp.concatenate([toeplitz[:, NA_KH - 1 - v + j] for j in range(NA_KH)], axis=-1) for v in range(NA_KH)]
    return jnp.stack(tabs, axis=0)


def _na_attention(qkv, bias_tab, meta_bias, *, batch, lp):
    n = qkv.shape[1]
    w = GRID_W
    rows = (lp - LANES) // w
    tiles = lp // LANES

    def variant(half):
        def index(b, s):
            r = jnp.clip(2 * (s - 1) + half, 0, rows - 1)
            return (r - jnp.clip(r - NA_KH // 2, 0, rows - NA_KH), 0, 0, 0)
        return index

    kern = functools.partial(_na_kernel, rows=rows)
    bias_block = (1, HEAD_PAIRS, 2 * w, NA_KH * w)
    bias_tab = bias_tab.reshape(NA_KH, HEAD_PAIRS, 2 * w, NA_KH * w)
    return pl.pallas_call(
        kern,
        out_shape=jax.ShapeDtypeStruct((HEAD_PAIRS, n, LANES), BF16),
        grid=(batch, tiles),
        in_specs=[
            pl.BlockSpec((HEAD_PAIRS, LANES, LANES), lambda b, s: (0, b * tiles + s, 0)),
            pl.BlockSpec((HEAD_PAIRS, lp, LANES), lambda b, s: (1, b, 0)),
            pl.BlockSpec((HEAD_PAIRS, lp, LANES), lambda b, s: (2, b, 0)),
            pl.BlockSpec(bias_block, variant(0)),
            pl.BlockSpec(bias_block, variant(1)),
            pl.BlockSpec((HEAD_PAIRS, 2 * w, N_META), lambda b, s: (0, 0, 0)),
        ],
        out_specs=pl.BlockSpec((HEAD_PAIRS, LANES, LANES), lambda b, s: (0, b * tiles + s, 0)),
        compiler_params=_params("parallel", "arbitrary"),
        name="na_attention",
    )(qkv, qkv, qkv, bias_tab, bias_tab,
      jnp.repeat(meta_bias, w, axis=0).reshape(HEAD_PAIRS, 2 * w, N_META))


def _na_out_kernel(o_ref, res_ref, w_ref, g_ref, b_ref, out_ref):
    att = jnp.concatenate([o_ref[p] for p in range(HEAD_PAIRS)], axis=-1)
    mix = jnp.dot(att, w_ref[...], preferred_element_type=F32)
    out_ref[...] = _layer_norm(DN_ALPHA * res_ref[...] + mix, g_ref[...], b_ref[...])


def _na_out(o, res, w_bf16, g, b):
    n, d = res.shape
    vec = pl.BlockSpec((1, d), lambda i: (0, 0))
    return pl.pallas_call(
        _na_out_kernel,
        out_shape=jax.ShapeDtypeStruct((n, d), F32),
        grid=(n // ROW_TILE,),
        in_specs=[pl.BlockSpec((HEAD_PAIRS, ROW_TILE, LANES), lambda i: (0, i, 0)),
                  pl.BlockSpec((ROW_TILE, d), lambda i: (i, 0)),
                  pl.BlockSpec((d, d), lambda i: (0, 0)), vec, vec],
        out_specs=pl.BlockSpec((ROW_TILE, d), lambda i: (i, 0)),
        compiler_params=_params("parallel"),
        name="na_out",
    )(o, res, w_bf16, g, b)


def kernel(x, meta_tokens, lru_w_in, lru_conv_w, lru_conv_b, lru_wa, lru_ba, lru_wx, lru_bx, lru_lambda, lru_w_out, na_w_qkv, na_rpb, na_meta_bias, na_w_out, ln_mix_g, ln_mix_b, router_w, router_b, moe_w_gu, moe_b_gu, moe_w_down, moe_b_down, ln_ffn_g, ln_ffn_b):
    batch, seq, d = x.shape
    lp = LANES + seq
    assert d == D_MODEL and seq % (2 * GRID_W) == 0 and seq // GRID_W >= NA_KH
    assert lp % SCAN_CHUNK == 0 and (batch * lp) % ROW_TILE == 0
    n = batch * lp

    front = jnp.zeros((batch, FRONT_PAD, d), x.dtype)
    meta = jnp.broadcast_to(meta_tokens[None].astype(x.dtype), (batch, N_META, d))
    h = jnp.concatenate([front, meta, x], axis=1).reshape(n, d)

    u = _matmul(h, lru_w_in[0].astype(BF16))
    row = lambda v: v[None, :]
    def scan(direction, reverse, fuse_out=None):
        return _lru_scan(
            u, lru_conv_w[0], row(lru_conv_b[0]),
            (0.5 * lru_wa[0, direction]).astype(BF16), row(0.5 * lru_ba[0, direction]),
            (0.5 * lru_wx[0, direction]).astype(BF16), row(0.5 * lru_bx[0, direction]),
            row(lru_lambda[0, direction]), reverse=reverse, lp=lp, fuse_out=fuse_out)

    h_fwd = scan(0, False)
    h = scan(1, True, fuse_out=(h_fwd, h, lru_w_out[0].astype(BF16), row(ln_mix_g[0]), row(ln_mix_b[0])))
    h = _moe_layer(h, router_w[0], router_b[0], moe_w_gu, moe_b_gu, moe_w_down, moe_b_down,
                   ln_ffn_g[0], ln_ffn_b[0], 0, lp)

    qkv = _qkv(h, na_w_qkv[0].astype(BF16))
    att = _na_attention(qkv, _na_bias_table(na_rpb[0], seq // GRID_W), na_meta_bias[0].astype(F32),
                        batch=batch, lp=lp)
    h = _na_out(att, h, na_w_out[0].astype(BF16), row(ln_mix_g[1]), row(ln_mix_b[1]))
    h = _moe_layer(h, router_w[1], router_b[1], moe_w_gu, moe_b_gu, moe_w_down, moe_b_down,
                   ln_ffn_g[1], ln_ffn_b[1], 1, lp, drop_front=(batch, lp))

    return h.reshape(batch, seq, d)
```

```python
import functools

import jax
import jax.numpy as jnp
from jax import lax
from jax.experimental import pallas as pl
from jax.experimental.pallas import tpu as pltpu
from jax.experimental.pallas import tpu_sc as plsc

F32 = jnp.float32
BF16 = jnp.bfloat16
I32 = jnp.int32
U32 = jnp.uint32

D_MODEL = 1024
N_META = 16
GRID_W = 64
LRU_BLOCKS = 4
LRU_BLOCK = D_MODEL // LRU_BLOCKS
CONV_W = 4
LRU_C = 8.0
NA_HEADS = 16
NA_HEAD_DIM = D_MODEL // NA_HEADS
NA_KH = 8
NA_KW = 16
N_EXPERTS = 32
TOP_K = 4
SWIGLU_LIMIT = 7.0
SWIGLU_ALPHA = 1.702
DEPTH = 2
DN_ALPHA = (2.0 * DEPTH) ** 0.25
LN_EPS = 1e-5

LANES = 128
SUBLANES = 8
FRONT_PAD = LANES - N_META
HEAD_PAIRS = D_MODEL // LANES
NEG_BIG = -1e30

ROW_TILE = 1024
SCAN_CHUNK = 352
ROUTE_TILE = 512
MOVE_TILE = 512
EXPERT_TILE = 512
COLLECT_PARTS = 2
VMEM_LIMIT = 56 << 20

SC_CORES = 2
SC_SUBCORES = 16
SC_WORKERS = SC_CORES * SC_SUBCORES
SC_MAX_INDICES = 64
SC_ROW_BUFFER_BYTES = 128 << 10


def _params(*sem):
    return pltpu.CompilerParams(dimension_semantics=sem, vmem_limit_bytes=VMEM_LIMIT)


def _sigmoid(x):
    return 0.5 * jnp.tanh(0.5 * x) + 0.5


def _layer_norm(x, g, b):
    mu = jnp.mean(x, axis=-1, keepdims=True)
    xc = x - mu
    var = jnp.mean(xc * xc, axis=-1, keepdims=True)
    return xc * lax.rsqrt(var + LN_EPS) * g + b


def _matmul_kernel(x_ref, w_ref, o_ref):
    o_ref[...] = jnp.dot(x_ref[...].astype(BF16), w_ref[...], preferred_element_type=F32).astype(o_ref.dtype)


def _matmul(x, w_bf16):
    n, k = x.shape
    m = w_bf16.shape[1]
    return pl.pallas_call(
        _matmul_kernel,
        out_shape=jax.ShapeDtypeStruct((n, m), BF16),
        grid=(n // ROW_TILE,),
        in_specs=[pl.BlockSpec((ROW_TILE, k), lambda i: (i, 0)),
                  pl.BlockSpec((k, m), lambda i: (0, 0))],
        out_specs=pl.BlockSpec((ROW_TILE, m), lambda i: (i, 0)),
        compiler_params=_params("parallel"),
        name="in_proj",
    )(x, w_bf16)


def _lru_scan_kernel(xr_ref, xp_ref, xn_ref, cw_ref, cb_ref, wa_ref, ba_ref, wx_ref, bx_ref,
                     lam_ref, *rest, reverse, chunks_per_batch, fused):
    if fused:
        hf_ref, y_ref, res_ref, wout_ref, g_ref, beta_ref, h_ref, a_sc, b_sc, h_sc, carry_sc = rest
    else:
        h_ref, a_sc, b_sc, h_sc, carry_sc = rest
    t_rows = xr_ref.shape[0]
    seg = t_rows // SUBLANES
    halo = xp_ref.shape[0]
    lane_tiles = xr_ref.shape[1] // LANES
    step = pl.program_id(0)
    chunk = (pl.num_programs(0) - 1 - step) if reverse else step
    cib = chunk % chunks_per_batch
    first_chunk = cib == 0
    last_chunk = cib == chunks_per_batch - 1
    row8 = lax.broadcasted_iota(I32, (SUBLANES, 1), 0)
    cat = lambda parts: jnp.concatenate(parts, axis=0)

    head_pos = cib * t_rows + lax.broadcasted_iota(I32, (LANES, 1), 0)
    head_real = head_pos >= FRONT_PAD
    xr = xr_ref[...].astype(F32)
    xr = cat([jnp.where(head_real, xr[:LANES], 0.0), xr[LANES:]])
    prev_pos = cib * t_rows - halo + lax.broadcasted_iota(I32, (halo, 1), 0)
    prev = jnp.where(prev_pos >= FRONT_PAD, xp_ref[...].astype(F32), 0.0)
    nxt = jnp.where(last_chunk, 0.0, xn_ref[...].astype(F32))

    xm1 = pltpu.roll(xr, 1, 0)
    xm2 = pltpu.roll(xr, 2, 0)
    xp1 = pltpu.roll(xr, t_rows - 1, 0)
    xm1 = cat([jnp.where(row8 == 0, prev[halo - 1:halo], xm1[:SUBLANES]), xm1[SUBLANES:]])
    xm2_head = jnp.where(row8 == 0, prev[halo - 2:halo - 1],
                         jnp.where(row8 == 1, prev[halo - 1:halo], xm2[:SUBLANES]))
    xm2 = cat([xm2_head, xm2[SUBLANES:]])
    xp1 = cat([xp1[:t_rows - SUBLANES], jnp.where(row8 == SUBLANES - 1, nxt[0:1], xp1[t_rows - SUBLANES:])])
    cw = cw_ref[...]
    xc = cw[0:1, :] * xm2 + cw[1:2, :] * xm1 + cw[2:3, :] * xr + cw[3:4, :] * xp1 + cb_ref[...]

    xcb = xc.astype(BF16)
    za, zx = [], []
    for blk in range(LRU_BLOCKS):
        xblk = xcb[:, blk * LRU_BLOCK:(blk + 1) * LRU_BLOCK]
        za.append(jnp.dot(xblk, wa_ref[blk], preferred_element_type=F32))
        zx.append(jnp.dot(xblk, wx_ref[blk], preferred_element_type=F32))
    tanh_a = jnp.tanh(jnp.concatenate(za, axis=-1) + ba_ref[...])
    tanh_x = jnp.tanh(jnp.concatenate(zx, axis=-1) + bx_ref[...])

    z = -lam_ref[...]
    softplus = jnp.maximum(z, 0.0) + jnp.log1p(jnp.exp(-jnp.abs(z)))
    half_rate = (-0.5 * LRU_C) * softplus
    log_a = tanh_a * half_rate + half_rate
    a = jnp.exp(log_a)
    gap = jnp.tanh(log_a) * (-1.0 - a * a)
    mult = jnp.where(gap > 0.0, gap * lax.rsqrt(gap), 0.0)
    if reverse:
        tail = jnp.where(jnp.logical_and(last_chunk, row8 == SUBLANES - 1), 1.0, mult[t_rows - SUBLANES:])
        mult = cat([mult[:t_rows - SUBLANES], tail])
    else:
        mult = cat([jnp.where(head_pos == FRONT_PAD, 1.0, mult[:LANES]), mult[LANES:]])
    b = mult * ((0.5 * tanh_x + 0.5) * xc)
    b = cat([jnp.where(head_real, b[:LANES], 0.0), b[LANES:]])
    for c in range(lane_tiles):
        a_sc[c] = a[:, c * LANES:(c + 1) * LANES]
        b_sc[c] = b[:, c * LANES:(c + 1) * LANES]

    @pl.when(last_chunk if reverse else first_chunk)
    def _():
        carry_sc[...] = jnp.zeros_like(carry_sc)

    order = range(seg - 1, -1, -1) if reverse else range(seg)
    rows_of = lambda j: pl.ds(j, SUBLANES, stride=seg)

    local = [jnp.zeros((SUBLANES, LANES), F32)] * lane_tiles
    prod = [jnp.ones((SUBLANES, LANES), F32)] * lane_tiles
    for j in order:
        for c in range(lane_tiles):
            av = a_sc[c, rows_of(j), :]
            local[c] = av * local[c] + b_sc[c, rows_of(j), :]
            prod[c] = av * prod[c]
            h_sc[c, rows_of(j), :] = local[c]
            a_sc[c, rows_of(j), :] = prod[c]

    carry_in = carry_sc[...]
    carry_out = []
    seg_carry = []
    for c in range(lane_tiles):
        state = carry_in[:, c * LANES:(c + 1) * LANES]
        rows = [None] * SUBLANES
        for s in (range(SUBLANES - 1, -1, -1) if reverse else range(SUBLANES)):
            rows[s] = state
            state = local[c][s:s + 1] + prod[c][s:s + 1] * state
        seg_carry.append(cat(rows))
        carry_out.append(state)
    carry_sc[...] = jnp.concatenate(carry_out, axis=-1)

    for j in order:
        for c in range(lane_tiles):
            h_sc[c, rows_of(j), :] = h_sc[c, rows_of(j), :] + a_sc[c, rows_of(j), :] * seg_carry[c]
    states = jnp.concatenate([h_sc[c] for c in range(lane_tiles)], axis=-1)
    if fused:
        gated = (hf_ref[...] + states) * _gelu_tanh(y_ref[...].astype(F32))
        mix = jnp.dot(gated.astype(BF16), wout_ref[...], preferred_element_type=F32)
        h_ref[...] = _layer_norm(DN_ALPHA * res_ref[...] + mix, g_ref[...], beta_ref[...])
    else:
        h_ref[...] = states


def _lru_scan(u, cw, cb, wa_half, ba_half, wx_half, bx_half, lam, *, reverse, lp, fuse_out=None):
    n = u.shape[0]
    d = D_MODEL
    t = SCAN_CHUNK
    halo = 2 * SUBLANES
    n_chunks = n // t
    cpb = lp // t
    t_h = t // halo
    n_h = n // halo

    def chunk_of(i):
        return (n_chunks - 1 - i) if reverse else i

    kern = functools.partial(_lru_scan_kernel, reverse=reverse, chunks_per_batch=cpb, fused=fuse_out is not None)
    chunk_rows = lambda col: pl.BlockSpec((t, d), lambda i: (chunk_of(i), col))
    extra_specs, extra_args = [], ()
    if fuse_out is not None:
        h_other, res, w_out, ln_g, ln_b = fuse_out
        extra_specs = [chunk_rows(0), chunk_rows(1), chunk_rows(0),
                       pl.BlockSpec((d, d), lambda i: (0, 0)),
                       pl.BlockSpec((1, d), lambda i: (0, 0)), pl.BlockSpec((1, d), lambda i: (0, 0))]
        extra_args = (h_other, u, res, w_out, ln_g, ln_b)
    full2 = lambda shape: pl.BlockSpec(shape, lambda i: (0, 0))
    full3 = lambda shape: pl.BlockSpec(shape, lambda i: (0, 0, 0))
    tile_major = pltpu.VMEM((d // LANES, t, LANES), F32)
    return pl.pallas_call(
        kern,
        out_shape=jax.ShapeDtypeStruct((n, d), F32),
        grid=(n_chunks,),
        in_specs=[
            pl.BlockSpec((t, d), lambda i: (chunk_of(i), 0)),
            pl.BlockSpec((halo, d), lambda i: (jnp.maximum(chunk_of(i) * t_h - 1, 0), 0)),
            pl.BlockSpec((halo, d), lambda i: (jnp.minimum((chunk_of(i) + 1) * t_h, n_h - 1), 0)),
            full2((CONV_W, d)), full2((1, d)),
            full3((LRU_BLOCKS, LRU_BLOCK, LRU_BLOCK)), full2((1, d)),
            full3((LRU_BLOCKS, LRU_BLOCK, LRU_BLOCK)), full2((1, d)),
            full2((1, d)),
        ] + extra_specs,
        out_specs=pl.BlockSpec((t, d), lambda i: (chunk_of(i), 0)),
        scratch_shapes=[tile_major, tile_major, tile_major, pltpu.VMEM((1, d), F32)],
        compiler_params=_params("arbitrary"),
        name=("lru_scan_bwd" if reverse else "lru_scan_fwd") + ("_out" if fuse_out is not None else ""),
    )(u, u, u, cw, cb, wa_half, ba_half, wx_half, bx_half, lam, *extra_args)


def _gelu_tanh(y):
    c = 0.7978845608028654
    return y * (0.5 * (1.0 + jnp.tanh(c * (y + 0.044715 * (y * y * y)))))


def _pack_bf16_pairs(x):
    half = x.shape[1] // 2
    lo = pltpu.bitcast(x[:, :half].astype(BF16).astype(F32), U32)
    hi = pltpu.bitcast(x[:, half:].astype(BF16).astype(F32), U32)
    return pltpu.bitcast((lo >> 16) | (hi & jnp.uint32(0xFFFF0000)), I32)


def _unpack_bf16_pairs(packed, dtype=BF16):
    u = pltpu.bitcast(packed, U32)
    lo = pltpu.bitcast(u << 16, F32).astype(dtype)
    hi = pltpu.bitcast(u & jnp.uint32(0xFFFF0000), F32).astype(dtype)
    return jnp.concatenate([lo, hi], axis=-1)


def _router_kernel(h_ref, wt_ref, b_ref, real_ref, idx_ref, gate_ref, rank_ref, cnt_ref, hp_ref, base_sc):
    tm = h_ref.shape[0]
    h = h_ref[...]
    hp_ref[...] = _pack_bf16_pairs(h)

    @pl.when(pl.program_id(0) == 0)
    def _():
        base_sc[...] = jnp.zeros_like(base_sc)

    wt = wt_ref[...]
    h_hi = h.astype(BF16)
    h_lo = (h - h_hi.astype(F32)).astype(BF16)
    w_hi = wt.astype(BF16)
    w_lo = (wt - w_hi.astype(F32)).astype(BF16)
    contract_last = (((1,), (1,)), ((), ()))
    mm = lambda a, b: lax.dot_general(a, b, contract_last, preferred_element_type=F32)
    logits = mm(w_hi, h_hi) + (mm(w_hi, h_lo) + mm(w_lo, h_hi)) + b_ref[...]

    expert = lax.broadcasted_iota(I32, (N_EXPERTS, tm), 0).astype(F32)
    vals = logits
    idx_rows, val_rows, onehots = [], [], []
    for _ in range(TOP_K):
        m = jnp.max(vals, axis=0, keepdims=True)
        idx = jnp.min(jnp.where(vals == m, expert, float(N_EXPERTS)), axis=0, keepdims=True)
        hit = expert == idx
        onehots.append(jnp.where(hit, real_ref[...], 0.0))
        idx_rows.append(idx)
        val_rows.append(m)
        vals = jnp.where(hit, -jnp.inf, vals)

    onehot_all = jnp.concatenate(onehots, axis=0)
    t_from = lax.broadcasted_iota(I32, (tm, tm), 0)
    t_to = lax.broadcasted_iota(I32, (tm, tm), 1)
    earlier = jnp.where(t_from < t_to, 1.0, 0.0).astype(BF16)
    before = jnp.dot(onehot_all.astype(BF16), earlier, preferred_element_type=F32)
    base = base_sc[...]
    starts = []
    for k in range(TOP_K):
        starts.append(base)
        base = base + jnp.sum(onehots[k], axis=1, keepdims=True)
    contrib = onehot_all * (jnp.concatenate(starts, axis=0) + before)
    rank_rows = [jnp.sum(contrib[k * N_EXPERTS:(k + 1) * N_EXPERTS], axis=0, keepdims=True)
                 for k in range(TOP_K)]

    top_vals = jnp.concatenate(val_rows, axis=0)
    e = jnp.exp(top_vals - top_vals[0:1])
    idx_ref[...] = jnp.concatenate(idx_rows, axis=0).astype(I32)
    gate_ref[...] = e / jnp.sum(e, axis=0, keepdims=True)
    rank_ref[...] = jnp.concatenate(rank_rows, axis=0).astype(I32)
    base_sc[...] = base
    cnt_ref[...] = base.astype(I32)


def _router(h, w, b, real):
    n, d = h.shape
    tm = ROUTE_TILE
    out4 = lambda dt: jax.ShapeDtypeStruct((TOP_K, n), dt)
    spec4 = pl.BlockSpec((TOP_K, tm), lambda i: (0, i))
    return pl.pallas_call(
        _router_kernel,
        out_shape=(out4(I32), out4(F32), out4(I32), jax.ShapeDtypeStruct((N_EXPERTS, 1), I32),
                   jax.ShapeDtypeStruct((n, d // 2), I32)),
        grid=(n // tm,),
        in_specs=[pl.BlockSpec((tm, d), lambda i: (i, 0)),
                  pl.BlockSpec((N_EXPERTS, d), lambda i: (0, 0)),
                  pl.BlockSpec((N_EXPERTS, 1), lambda i: (0, 0)),
                  pl.BlockSpec((1, tm), lambda i: (0, i))],
        out_specs=(spec4, spec4, spec4, pl.BlockSpec((N_EXPERTS, 1), lambda i: (0, 0)),
                   pl.BlockSpec((tm, d // 2), lambda i: (i, 0))),
        scratch_shapes=[pltpu.VMEM((N_EXPERTS, 1), F32)],
        compiler_params=_params("arbitrary"),
        name="router",
    )(h, w.T, b[:, None], real)


def _sc_gather_rows(table, idx, name):
    rows, width = table.shape
    total = idx.shape[0]
    per_worker = total // SC_WORKERS
    max_chunk = min(SC_MAX_INDICES, SC_ROW_BUFFER_BYTES // (width * table.dtype.itemsize))
    chunk = max(c for c in range(SUBLANES, max_chunk + 1, SUBLANES) if per_worker % (2 * c) == 0)
    n_chunks = per_worker // chunk
    assert total == SC_WORKERS * n_chunks * chunk and n_chunks % 2 == 0
    mesh = plsc.VectorSubcoreMesh(core_axis_name="c", subcore_axis_name="s",
                                  num_cores=SC_CORES, num_subcores=SC_SUBCORES)

    def body(table_hbm, idx_hbm, out_hbm, idx_v, rows0, rows1, gsem0, gsem1, psem0, psem1):
        worker = lax.axis_index("s") * SC_CORES + lax.axis_index("c")
        base = worker * per_worker
        pltpu.sync_copy(idx_hbm.at[worker], idx_v)
        bufs = (rows0, rows1)
        gsems = (gsem0, gsem1)
        psems = (psem0, psem1)

        def gather(c, slot):
            return pltpu.make_async_copy(table_hbm.at[idx_v.at[c]], bufs[slot], gsems[slot])

        def put(c, slot):
            return pltpu.make_async_copy(bufs[slot], out_hbm.at[pl.ds(base + c * chunk, chunk)], psems[slot])

        gather(0, 0).start()

        @pl.loop(0, n_chunks, step=2)
        def _(c0):
            for slot in range(2):
                c = c0 + slot

                @pl.when(c + 1 < n_chunks)
                def _():
                    @pl.when(c >= 1)
                    def _():
                        put(c - 1, 1 - slot).wait()
                    gather(c + 1, 1 - slot).start()

                gather(c, slot).wait()
                put(c, slot).start()

        put(n_chunks - 2, 0).wait()
        put(n_chunks - 1, 1).wait()

    return pl.kernel(
        body,
        out_type=jax.ShapeDtypeStruct((total, width), table.dtype),
        mesh=mesh,
        scratch_types=[pltpu.VMEM((n_chunks, chunk), I32),
                       pltpu.VMEM((chunk, width), table.dtype), pltpu.VMEM((chunk, width), table.dtype),
                       pltpu.SemaphoreType.DMA, pltpu.SemaphoreType.DMA,
                       pltpu.SemaphoreType.DMA, pltpu.SemaphoreType.DMA],
        name=name,
    )(table, idx.reshape(SC_WORKERS, n_chunks, chunk))


def _sc_scatter_rows(src, dest, out_rows, name):
    n, width = src.shape
    fan = dest.shape[0]
    per_worker = n // SC_WORKERS
    chunk = max(c for c in range(SUBLANES, SC_MAX_INDICES + 1, SUBLANES) if per_worker % (2 * c) == 0)
    n_chunks = per_worker // chunk
    assert n == SC_WORKERS * n_chunks * chunk and n_chunks % 2 == 0
    mesh = plsc.VectorSubcoreMesh(core_axis_name="c", subcore_axis_name="s",
                                  num_cores=SC_CORES, num_subcores=SC_SUBCORES)

    def body(src_hbm, idx_hbm, out_hbm, idx_v, rows0, rows1, lsem0, lsem1, ssem0, ssem1):
        worker = lax.axis_index("s") * SC_CORES + lax.axis_index("c")
        base = worker * per_worker
        pltpu.sync_copy(idx_hbm.at[worker], idx_v)
        bufs = (rows0, rows1)
        lsems = (lsem0, lsem1)
        ssems = (ssem0, ssem1)

        def load(c, slot):
            return pltpu.make_async_copy(src_hbm.at[pl.ds(base + c * chunk, chunk)], bufs[slot], lsems[slot])

        def scatter(c, k, slot):
            return pltpu.make_async_copy(bufs[slot], out_hbm.at[idx_v.at[k * n_chunks + c]], ssems[slot])

        load(0, 0).start()

        @pl.loop(0, n_chunks, step=2)
        def _(c0):
            for slot in range(2):
                c = c0 + slot

                @pl.when(c + 1 < n_chunks)
                def _():
                    @pl.when(c >= 1)
                    def _():
                        for k in range(fan):
                            scatter(c - 1, k, 1 - slot).wait()
                    load(c + 1, 1 - slot).start()

                load(c, slot).wait()
                for k in range(fan):
                    scatter(c, k, slot).start()

        for k in range(fan):
            scatter(n_chunks - 2, k, 0).wait()
        for k in range(fan):
            scatter(n_chunks - 1, k, 1).wait()

    idx = dest.reshape(fan, SC_WORKERS, n_chunks, chunk).transpose(1, 0, 2, 3)
    return pl.kernel(
        body,
        out_type=jax.ShapeDtypeStruct((out_rows, width), src.dtype),
        mesh=mesh,
        scratch_types=[pltpu.VMEM((fan * n_chunks, chunk), I32),
                       pltpu.VMEM((chunk, width), src.dtype), pltpu.VMEM((chunk, width), src.dtype),
                       pltpu.SemaphoreType.DMA, pltpu.SemaphoreType.DMA,
                       pltpu.SemaphoreType.DMA, pltpu.SemaphoreType.DMA],
        name=name,
    )(src, idx.reshape(SC_WORKERS, fan * n_chunks, chunk))


def _expert_kernel(be_ref, valid_ref, xs_ref, wgu_ref, bgu_ref, wd_ref, bd_ref, y_ref, wgu_sc, wd_sc):
    i = pl.program_id(0)
    e = be_ref[i]
    e_prev = be_ref[jnp.maximum(i - 1, 0)]
    d = wd_sc.shape[0]

    @pl.when((i == 0) | (e != e_prev))
    def _():
        for r in range(0, d, LANES):
            wgu_sc[r:r + LANES, :] = wgu_ref[0, r:r + LANES, :].astype(BF16)
            wd_sc[r:r + LANES, :] = wd_ref[0, r:r + LANES, :].astype(BF16)

    valid = valid_ref[i]

    @pl.when(valid > 0)
    def _():
        row = lax.broadcasted_iota(I32, xs_ref.shape, 0)
        x = _unpack_bf16_pairs(jnp.where(row < valid, xs_ref[...], 0))
        h = jnp.dot(x, wgu_sc[...], preferred_element_type=F32) + bgu_ref[0]
        glu = jnp.minimum(h[:, :d], SWIGLU_LIMIT)
        lin = jnp.clip(h[:, d:], -SWIGLU_LIMIT, SWIGLU_LIMIT)
        act = glu * _sigmoid(SWIGLU_ALPHA * glu) * (lin + 1.0)
        y = jnp.dot(act.astype(BF16), wd_sc[...], preferred_element_type=F32) + bd_ref[0]
        y_ref[...] = _pack_bf16_pairs(y)

    @pl.when(valid == 0)
    def _():
        y_ref[...] = jnp.zeros_like(y_ref)


def _experts(block_e, valid_rows, xs, w_gu, b_gu, w_down, b_down, layer):
    cap = xs.shape[0]
    d = D_MODEL
    tm = EXPERT_TILE
    n_blocks = cap // tm
    grid_spec = pltpu.PrefetchScalarGridSpec(
        num_scalar_prefetch=2,
        grid=(n_blocks,),
        in_specs=[
            pl.BlockSpec((tm, d // 2), lambda i, be, nu: (i, 0)),
            pl.BlockSpec((None, 1, d, 2 * d), lambda i, be, nu: (layer, be[i], 0, 0)),
            pl.BlockSpec((1, 1, 2 * d), lambda i, be, nu: (be[i], 0, 0)),
            pl.BlockSpec((None, 1, d, d), lambda i, be, nu: (layer, be[i], 0, 0)),
            pl.BlockSpec((1, 1, d), lambda i, be, nu: (be[i], 0, 0)),
        ],
        out_specs=pl.BlockSpec((tm, d // 2), lambda i, be, nu: (i, 0)),
        scratch_shapes=[pltpu.VMEM((d, 2 * d), BF16), pltpu.VMEM((d, d), BF16)],
    )
    return pl.pallas_call(
        _expert_kernel,
        out_shape=jax.ShapeDtypeStruct((cap, d // 2), I32),
        grid_spec=grid_spec,
        compiler_params=_params("arbitrary"),
        name="moe_experts",
    )(block_e, valid_rows, xs, w_gu, b_gu[layer][:, None, :], w_down, b_down[layer][:, None, :])


def _combine_kernel(g_ref, b_ref, *refs, n_sub):
    o_ref = refs[-1]
    rows = o_ref.shape[0] // n_sub
    for s in range(n_sub):
        gate_ref, res_ref, yk_ref = refs[3 * s:3 * s + 3]
        gates = gate_ref[...]
        ffn = gates[:, 0:1] * _unpack_bf16_pairs(yk_ref[0], F32)
        for k in range(1, TOP_K):
            ffn = ffn + gates[:, k:k + 1] * _unpack_bf16_pairs(yk_ref[k], F32)
        o_ref[s * rows:(s + 1) * rows, :] = _layer_norm(DN_ALPHA * res_ref[...] + ffn, g_ref[...], b_ref[...])


def _combine(gates, res, yk, g, b, part, acc, drop_front=None):
    n, d = res.shape
    if drop_front is None:
        tt, n_sub = MOVE_TILE, 1
        steps = n // COLLECT_PARTS // tt
        grid = (steps,)
        local = lambda s, i: i
        src = lambda s, i: part * steps + i
        dst = lambda i: part * steps + i
        out_rows = n
    else:
        batch, lp = drop_front
        tt, n_sub = LANES, 4
        tiles = lp // tt
        part_batch = batch // COLLECT_PARTS
        pairs = (tiles - 1) // n_sub
        assert pairs * n_sub == tiles - 1
        grid = (part_batch, pairs)
        local = lambda s, b_, j: b_ * tiles + 1 + n_sub * j + s
        src = lambda s, b_, j: (part * part_batch + b_) * tiles + 1 + n_sub * j + s
        dst = lambda b_, j: (part * part_batch + b_) * pairs + j
        out_rows = n - batch * tt
    vec = pl.BlockSpec((1, d), lambda *_: (0, 0))
    in_specs = [vec, vec]
    args = (g, b)
    for s in range(n_sub):
        in_specs += [pl.BlockSpec((tt, TOP_K), lambda *i, s=s: (local(s, *i), 0)),
                     pl.BlockSpec((tt, d), lambda *i, s=s: (src(s, *i), 0)),
                     pl.BlockSpec((TOP_K, tt, d // 2), lambda *i, s=s: (0, local(s, *i), 0))]
        args += (gates, res, yk)
    aliases = {}
    if acc is not None:
        in_specs.append(pl.BlockSpec(memory_space=pl.ANY))
        args += (acc,)
        aliases = {len(args) - 1: 0}
    return pl.pallas_call(
        functools.partial(_combine_kernel, n_sub=n_sub),
        out_shape=jax.ShapeDtypeStruct((out_rows, d), F32),
        grid=grid,
        in_specs=in_specs,
        out_specs=pl.BlockSpec((n_sub * tt, d), lambda *i: (dst(*i), 0)),
        input_output_aliases=aliases,
        compiler_params=_params(*(["parallel"] * len(grid))),
        name="moe_combine",
    )(*args)


def _moe_layer(h, router_w, router_b, w_gu, b_gu, w_down, b_down, ln_g, ln_b, layer, lp, drop_front=None):
    n, d = h.shape
    tm = EXPERT_TILE
    token = jnp.arange(n, dtype=I32)
    real = (token % lp) >= FRONT_PAD
    idx, gates, rank, counts, h_packed = _router(h, router_w, router_b, real.astype(F32)[None, :])
    counts = counts[:, 0]
    padded = (counts + tm - 1) // tm * tm
    pad_end = jnp.cumsum(padded)
    pad_start = pad_end - padded
    n_real = n - (n // lp) * FRONT_PAD
    n_blocks = -(-(n_real * TOP_K + N_EXPERTS * (tm - 1)) // tm) + 1
    group_start = jnp.sum(jnp.where(idx[:, :, None] == jnp.arange(N_EXPERTS, dtype=I32), pad_start, 0), axis=-1)
    dest = jnp.where(real, group_start + rank, (n_blocks - 1) * tm + token % tm)
    block_start = jnp.arange(n_blocks, dtype=I32) * tm
    block_e = jnp.minimum(jnp.sum((pad_end[None, :] <= block_start[:, None]).astype(I32), axis=1),
                          N_EXPERTS - 1)
    group_end = (pad_start + counts)[block_e]
    valid_rows = jnp.where(block_start < pad_end[-1], jnp.clip(group_end - block_start, 0, tm), 0).astype(I32)

    xs = _sc_scatter_rows(h_packed, dest, n_blocks * tm, "moe_dispatch")
    ys = _experts(block_e, valid_rows, xs, w_gu, b_gu, w_down, b_down, layer)
    n_part = n // COLLECT_PARTS
    assert n_part % lp == 0
    out = None
    for part in range(COLLECT_PARTS):
        rows = slice(part * n_part, (part + 1) * n_part)
        yk = _sc_gather_rows(ys, dest[:, rows].reshape(TOP_K * n_part), "moe_collect")
        out = _combine(gates[:, rows].T, h, yk.reshape(TOP_K, n_part, d // 2), ln_g[None, :], ln_b[None, :],
                       part, out, drop_front)
    return out


def _qkv_kernel(x_ref, w_ref, o_ref):
    xb = x_ref[...].astype(BF16)
    d = x_ref.shape[1]
    for part in range(3):
        acc = jnp.dot(xb, w_ref[:, part * d:(part + 1) * d], preferred_element_type=F32)
        if part == 0:
            acc = acc * (NA_HEAD_DIM ** -0.5)
        for p in range(HEAD_PAIRS):
            o_ref[part * HEAD_PAIRS + p] = acc[:, p * LANES:(p + 1) * LANES].astype(BF16)


def _qkv(x, w_bf16):
    n, d = x.shape
    return pl.pallas_call(
        _qkv_kernel,
        out_shape=jax.ShapeDtypeStruct((3 * HEAD_PAIRS, n, LANES), BF16),
        grid=(n // ROW_TILE,),
        in_specs=[pl.BlockSpec((ROW_TILE, d), lambda i: (i, 0)),
                  pl.BlockSpec((d, 3 * d), lambda i: (0, 0))],
        out_specs=pl.BlockSpec((3 * HEAD_PAIRS, ROW_TILE, LANES), lambda i: (0, i, 0)),
        compiler_params=_params("parallel"),
        name="na_qkv",
    )(x, w_bf16)


def _na_kernel(q_ref, k_ref, v_ref, bias_a_ref, bias_b_ref, mb_ref, o_ref, *, rows):
    s = pl.program_id(1)
    w = GRID_W
    low = lax.broadcasted_iota(I32, (w, LANES), 1) < NA_HEAD_DIM
    contract_last = (((1,), (1,)), ((), ()))
    meta0 = FRONT_PAD

    def stacked_q(p, q0):
        qp = q_ref[p, q0:q0 + w, :]
        zero = jnp.zeros_like(qp)
        return jnp.concatenate([jnp.where(low, qp, zero), jnp.where(low, zero, qp)], axis=0)

    def attend_all(q0, k0, bias_ref):
        scores = []
        for p in range(HEAD_PAIRS):
            q2 = stacked_q(p, q0)
            s_meta = lax.dot_general(q2, k_ref[p, meta0:meta0 + N_META, :], contract_last,
                                     preferred_element_type=F32) + mb_ref[p]
            s_win = None
            if k0 is not None:
                s_win = lax.dot_general(q2, k_ref[p, pl.ds(k0, NA_KH * w), :], contract_last,
                                        preferred_element_type=F32) + bias_ref[0, p]
            scores.append((s_meta, s_win))
        outs = []
        for p in range(HEAD_PAIRS):
            s_meta, s_win = scores[p]
            m = jnp.max(s_meta, axis=-1, keepdims=True)
            if s_win is not None:
                m = jnp.maximum(m, jnp.max(s_win, axis=-1, keepdims=True))
            p_meta = jnp.exp(s_meta - m)
            denom = jnp.sum(p_meta, axis=-1, keepdims=True)
            o = jnp.dot(p_meta.astype(BF16), v_ref[p, meta0:meta0 + N_META, :], preferred_element_type=F32)
            if s_win is not None:
                p_win = jnp.exp(s_win - m)
                denom = denom + jnp.sum(p_win, axis=-1, keepdims=True)
                o = o + jnp.dot(p_win.astype(BF16), v_ref[p, pl.ds(k0, NA_KH * w), :],
                                preferred_element_type=F32)
            o = o / denom
            outs.append(jnp.where(low, o[:w], o[w:]))
        return outs

    @pl.when(s == 0)
    def _():
        qrow = lax.broadcasted_iota(I32, (w, LANES), 0)
        for p, o in enumerate(attend_all(LANES - w, None, None)):
            o_ref[p, 0:LANES - w, :] = jnp.zeros((LANES - w, LANES), o_ref.dtype)
            o_ref[p, LANES - w:LANES, :] = jnp.where(qrow >= w - N_META, o, 0.0).astype(o_ref.dtype)

    @pl.when(s >= 1)
    def _():
        for half, bias_ref in enumerate((bias_a_ref, bias_b_ref)):
            r = 2 * (s - 1) + half
            rs = jnp.clip(r - NA_KH // 2, 0, rows - NA_KH)
            k0 = pl.multiple_of(LANES + rs * w, w)
            for p, o in enumerate(attend_all(half * w, k0, bias_ref)):
                o_ref[p, half * w:(half + 1) * w, :] = o.astype(o_ref.dtype)


def _na_bias_table(rpb, rows):
    del rows
    w = GRID_W
    q = jnp.arange(w)
    col_start = jnp.clip(q - NA_KW // 2, 0, w - NA_KW)
    c = jnp.arange(w)
    in_win = (c[None, :] >= col_start[:, None]) & (c[None, :] < col_start[:, None] + NA_KW)
    pad = w - NA_KW
    rp = jnp.pad(rpb.astype(F32), ((0, 0), (0, 0), (pad, pad)))
    toeplitz = jnp.stack([rp[:, :, w - 1 - qq:2 * w - 1 - qq] for qq in range(w)], axis=2)
    toeplitz = jnp.where(in_win[None, None], toeplitz, NEG_BIG)
    tabs = [jnp.concatenate([toeplitz[:, NA_KH - 1 - v + j] for j in range(NA_KH)], axis=-1) for v in range(NA_KH)]
    return jnp.stack(tabs, axis=0)


def _na_attention(qkv, bias_tab, meta_bias, *, batch, lp):
    n = qkv.shape[1]
    w = GRID_W
    rows = (lp - LANES) // w
    tiles = lp // LANES

    def variant(half):
        def index(b, s):
            r = jnp.clip(2 * (s - 1) + half, 0, rows - 1)
            return (r - jnp.clip(r - NA_KH // 2, 0, rows - NA_KH), 0, 0, 0)
        return index

    kern = functools.partial(_na_kernel, rows=rows)
    bias_block = (1, HEAD_PAIRS, 2 * w, NA_KH * w)
    bias_tab = bias_tab.reshape(NA_KH, HEAD_PAIRS, 2 * w, NA_KH * w)
    return pl.pallas_call(
        kern,
        out_shape=jax.ShapeDtypeStruct((HEAD_PAIRS, n, LANES), BF16),
        grid=(batch, tiles),
        in_specs=[
            pl.BlockSpec((HEAD_PAIRS, LANES, LANES), lambda b, s: (0, b * tiles + s, 0)),
            pl.BlockSpec((HEAD_PAIRS, lp, LANES), lambda b, s: (1, b, 0)),
            pl.BlockSpec((HEAD_PAIRS, lp, LANES), lambda b, s: (2, b, 0)),
            pl.BlockSpec(bias_block, variant(0)),
            pl.BlockSpec(bias_block, variant(1)),
            pl.BlockSpec((HEAD_PAIRS, 2 * w, N_META), lambda b, s: (0, 0, 0)),
        ],
        out_specs=pl.BlockSpec((HEAD_PAIRS, LANES, LANES), lambda b, s: (0, b * tiles + s, 0)),
        compiler_params=_params("parallel", "arbitrary"),
        name="na_attention",
    )(qkv, qkv, qkv, bias_tab, bias_tab,
      jnp.repeat(meta_bias, w, axis=0).reshape(HEAD_PAIRS, 2 * w, N_META))


def _na_out_kernel(o_ref, res_ref, w_ref, g_ref, b_ref, out_ref):
    att = jnp.concatenate([o_ref[p] for p in range(HEAD_PAIRS)], axis=-1)
    mix = jnp.dot(att, w_ref[...], preferred_element_type=F32)
    out_ref[...] = _layer_norm(DN_ALPHA * res_ref[...] + mix, g_ref[...], b_ref[...])


def _na_out(o, res, w_bf16, g, b):
    n, d = res.shape
    vec = pl.BlockSpec((1, d), lambda i: (0, 0))
    return pl.pallas_call(
        _na_out_kernel,
        out_shape=jax.ShapeDtypeStruct((n, d), F32),
        grid=(n // ROW_TILE,),
        in_specs=[pl.BlockSpec((HEAD_PAIRS, ROW_TILE, LANES), lambda i: (0, i, 0)),
                  pl.BlockSpec((ROW_TILE, d), lambda i: (i, 0)),
                  pl.BlockSpec((d, d), lambda i: (0, 0)), vec, vec],
        out_specs=pl.BlockSpec((ROW_TILE, d), lambda i: (i, 0)),
        compiler_params=_params("parallel"),
        name="na_out",
    )(o, res, w_bf16, g, b)


def kernel(x, meta_tokens, lru_w_in, lru_conv_w, lru_conv_b, lru_wa, lru_ba, lru_wx, lru_bx, lru_lambda, lru_w_out, na_w_qkv, na_rpb, na_meta_bias, na_w_out, ln_mix_g, ln_mix_b, router_w, router_b, moe_w_gu, moe_b_gu, moe_w_down, moe_b_down, ln_ffn_g, ln_ffn_b):
    batch, seq, d = x.shape
    lp = LANES + seq
    assert d == D_MODEL and seq % (2 * GRID_W) == 0 and seq // GRID_W >= NA_KH
    assert lp % SCAN_CHUNK == 0 and (batch * lp) % ROW_TILE == 0
    n = batch * lp

    front = jnp.zeros((batch, FRONT_PAD, d), x.dtype)
    meta = jnp.broadcast_to(meta_tokens[None].astype(x.dtype), (batch, N_META, d))
    h = jnp.concatenate([front, meta, x], axis=1).reshape(n, d)

    u = _matmul(h, lru_w_in[0].astype(BF16))
    row = lambda v: v[None, :]
    def scan(direction, reverse, fuse_out=None):
        return _lru_scan(
            u, lru_conv_w[0], row(lru_conv_b[0]),
            (0.5 * lru_wa[0, direction]).astype(BF16), row(0.5 * lru_ba[0, direction]),
            (0.5 * lru_wx[0, direction]).astype(BF16), row(0.5 * lru_bx[0, direction]),
            row(lru_lambda[0, direction]), reverse=reverse, lp=lp, fuse_out=fuse_out)

    h_fwd = scan(0, False)
    h = scan(1, True, fuse_out=(h_fwd, h, lru_w_out[0].astype(BF16), row(ln_mix_g[0]), row(ln_mix_b[0])))
    h = _moe_layer(h, router_w[0], router_b[0], moe_w_gu, moe_b_gu, moe_w_down, moe_b_down,
                   ln_ffn_g[0], ln_ffn_b[0], 0, lp)

    qkv = _qkv(h, na_w_qkv[0].astype(BF16))
    att = _na_attention(qkv, _na_bias_table(na_rpb[0], seq // GRID_W), na_meta_bias[0].astype(F32),
                        batch=batch, lp=lp)
    h = _na_out(att, h, na_w_out[0].astype(BF16), row(ln_mix_g[1]), row(ln_mix_b[1]))
    h = _moe_layer(h, router_w[1], router_b[1], moe_w_gu, moe_b_gu, moe_w_down, moe_b_down,
                   ln_ffn_g[1], ln_ffn_b[1], 1, lp, drop_front=(batch, lp))

    return h.reshape(batch, seq, d)
```
